```python
import jax, jax.numpy as jnp
from jax import lax
import numpy as np

D_MODEL = 1024
BATCH = 8
SEQ = 16384
DEPTH = 4

D_MIX = D_MODEL
GLA_HEADS = 4
GLA_DV = 128
GLA_DK = 64
GLA_GATE_RANK = 16
GLA_GATE_NORMALIZER = 16.0
GLA_CHUNK = 64
GLA_WIDTH = GLA_HEADS * GLA_DV
LRU_WIDTH = D_MIX - GLA_WIDTH
LRU_BLOCKS = 8
LRU_BLOCK = LRU_WIDTH // LRU_BLOCKS
LRU_CONV = 4
LRU_C = 8.0
FFN_HIDDEN = 3 * D_MODEL
FFN_CONV = 3
EPS = 1e-6
Q_COLS = GLA_HEADS * GLA_DK
K_COLS = GLA_HEADS * GLA_DK
V_COLS = GLA_WIDTH
G_COLS = GLA_WIDTH
A_COLS = GLA_GATE_RANK
X_COLS = LRU_WIDTH
Y_COLS = LRU_WIDTH
D_IN = Q_COLS + K_COLS + V_COLS + G_COLS + A_COLS + X_COLS + Y_COLS
SPLITS = [Q_COLS, Q_COLS + K_COLS, Q_COLS + K_COLS + V_COLS,
          Q_COLS + K_COLS + V_COLS + G_COLS,
          Q_COLS + K_COLS + V_COLS + G_COLS + A_COLS,
          Q_COLS + K_COLS + V_COLS + G_COLS + A_COLS + X_COLS]

kernel_name = "hymba_gla_rglru_convffn_trunk"


def rmsnorm(x, g):
    xf = x.astype(jnp.float32)
    y = xf * lax.rsqrt(jnp.mean(xf * xf, axis=-1, keepdims=True) + EPS)
    return (y * g.astype(jnp.float32)).astype(x.dtype)


def causal_dwconv(x, w, b):
    K = w.shape[0]
    T = x.shape[1]
    xp = jnp.pad(x, ((0, 0), (K - 1, 0), (0, 0)))
    out = b
    for j in range(K):
        out = out + xp[:, j:j + T, :] * w[j]
    return out


def gla_heads(q, k, v, g_out, gate_lr, w2, b2, norm_g):
    f32 = jnp.float32
    Bsz, T, _ = q.shape
    N = T // GLA_CHUNK
    C = GLA_CHUNK
    log_alpha = jax.nn.log_sigmoid((gate_lr @ w2 + b2).astype(f32)) / GLA_GATE_NORMALIZER

    def chunks(t, d):
        return t.astype(f32).reshape(Bsz, N, C, GLA_HEADS, d).transpose(0, 3, 1, 2, 4)

    qc = chunks(q, GLA_DK) * (GLA_DK ** -0.5)
    kc = chunks(k, GLA_DK)
    vc = chunks(v, GLA_DV)
    gc = chunks(log_alpha, GLA_DK)
    b = jnp.cumsum(gc, axis=3)
    b_last = b[:, :, :, -1:, :]
    q_i = qc * jnp.exp(b)
    k_i = kc * jnp.exp(-b)
    k_dec = kc * jnp.exp(b_last - b)
    U = jnp.einsum('bhncd,bhnce->bhnde', k_dec, vc)
    decay = jnp.exp(b_last[:, :, :, 0, :])

    def step(S, inp):
        d_n, u_n = inp
        return d_n[..., None] * S + u_n, S

    S0 = jnp.zeros((Bsz, GLA_HEADS, GLA_DK, GLA_DV), f32)
    _, S_prev = lax.scan(step, S0, (jnp.moveaxis(decay, 2, 0), jnp.moveaxis(U, 2, 0)))
    S_prev = jnp.moveaxis(S_prev, 0, 2)
    mask = jnp.tril(jnp.ones((C, C), dtype=bool))
    A = jnp.where(mask, jnp.einsum('bhncd,bhnsd->bhncs', q_i, k_i), 0.0)
    o = (jnp.einsum('bhncs,bhnse->bhnce', A, vc)
         + jnp.einsum('bhncd,bhnde->bhnce', q_i, S_prev))
    o = o.transpose(0, 2, 3, 1, 4).reshape(Bsz, T, GLA_HEADS, GLA_DV)
    o = rmsnorm(o, norm_g)
    o = o * jax.nn.silu(g_out.astype(f32).reshape(Bsz, T, GLA_HEADS, GLA_DV))
    return o.reshape(Bsz, T, GLA_WIDTH).astype(q.dtype)


def rglru_heads(xr, xg, conv_w, conv_b, wa, ba, wx, bx, lam):
    f32 = jnp.float32
    Bsz, T, W = xr.shape
    xc = causal_dwconv(xr, conv_w, conv_b).astype(f32)
    xb = xc.reshape(Bsz, T, LRU_BLOCKS, LRU_BLOCK)
    r_gate = jax.nn.sigmoid(jnp.einsum('btnd,nde->btne', xb, wa.astype(f32)).reshape(Bsz, T, W) + ba)
    i_gate = jax.nn.sigmoid(jnp.einsum('btnd,nde->btne', xb, wx.astype(f32)).reshape(Bsz, T, W) + bx)
    log_a = -LRU_C * r_gate * jax.nn.softplus(-lam.astype(f32))
    a = jnp.exp(log_a)
    mult = jnp.sqrt(-jnp.expm1(2.0 * log_a))
    mult = jnp.where(jnp.arange(T)[None, :, None] == 0, 1.0, mult)
    u = mult * (i_gate * xc)

    def combine(left, right):
        a1, b1 = left
        a2, b2 = right
        return a1 * a2, a2 * b1 + b2

    _, h = lax.associative_scan(combine, (a, u), axis=1)
    y = h * jax.nn.gelu(xg.astype(f32))
    return y.astype(xr.dtype)


def conv_ffn(u, w_in, conv_w, conv_b, w_down):
    z = causal_dwconv(u @ w_in, conv_w, conv_b)
    a, gt = jnp.split(z, 2, axis=-1)
    return (jax.nn.gelu(a) * gt) @ w_down


def _fwd_setup_inputs(seed: int = 0) -> dict:
    key = jax.random.key(seed)
    ks = jax.random.split(key, 24)
    f32 = jnp.float32
    nrm = lambda k, shape, s: jax.random.normal(k, shape, f32) * s
    u_lam = jax.random.uniform(ks[14], (DEPTH, LRU_WIDTH), f32, 0.9, 0.999)
    a0 = u_lam ** (1.0 / LRU_C)
    return {
        "x": jax.random.normal(ks[0], (BATCH, SEQ, D_MODEL), f32),
        "ln_mix": 1.0 + nrm(ks[1], (DEPTH, D_MODEL), 0.02),
        "w_in": nrm(ks[2], (DEPTH, D_MODEL, D_IN), D_MODEL ** -0.5),
        "gla_gate_w2": nrm(ks[3], (DEPTH, GLA_GATE_RANK, Q_COLS), GLA_GATE_RANK ** -0.5),
        "gla_gate_b": nrm(ks[4], (DEPTH, Q_COLS), 0.1),
        "gla_norm": 1.0 + nrm(ks[5], (DEPTH, GLA_DV), 0.02),
        "lru_conv_w": nrm(ks[6], (DEPTH, LRU_CONV, LRU_WIDTH), LRU_CONV ** -0.5),
        "lru_conv_b": nrm(ks[7], (DEPTH, LRU_WIDTH), 0.02),
        "lru_wa": nrm(ks[8], (DEPTH, LRU_BLOCKS, LRU_BLOCK, LRU_BLOCK), LRU_BLOCK ** -0.5),
        "lru_ba": nrm(ks[9], (DEPTH, LRU_WIDTH), 0.02),
        "lru_wx": nrm(ks[10], (DEPTH, LRU_BLOCKS, LRU_BLOCK, LRU_BLOCK), LRU_BLOCK ** -0.5),
        "lru_bx": nrm(ks[11], (DEPTH, LRU_WIDTH), 0.02),
        "lru_lambda": jnp.log(a0) - jnp.log1p(-a0),
        "w_out": nrm(ks[12], (DEPTH, D_MIX, D_MODEL), D_MIX ** -0.5),
        "ln_ffn": 1.0 + nrm(ks[13], (DEPTH, D_MODEL), 0.02),
        "ffn_w_in": nrm(ks[15], (DEPTH, D_MODEL, 2 * FFN_HIDDEN), D_MODEL ** -0.5),
        "ffn_conv_w": nrm(ks[16], (DEPTH, FFN_CONV, 2 * FFN_HIDDEN), FFN_CONV ** -0.5),
        "ffn_conv_b": nrm(ks[17], (DEPTH, 2 * FFN_HIDDEN), 0.02),
        "ffn_w_down": nrm(ks[18], (DEPTH, FFN_HIDDEN, D_MODEL), FFN_HIDDEN ** -0.5),
        "ln_final": 1.0 + nrm(ks[19], (D_MODEL,), 0.02),
    }


def _fwd_reference(x, ln_mix, w_in, gla_gate_w2, gla_gate_b, gla_norm, lru_conv_w, lru_conv_b,
              lru_wa, lru_ba, lru_wx, lru_bx, lru_lambda, w_out, ln_ffn, ffn_w_in,
              ffn_conv_w, ffn_conv_b, ffn_w_down, ln_final):
    h = x
    for l in range(DEPTH):
        u = rmsnorm(h, ln_mix[l])
        proj = u @ w_in[l]
        q, k, v, g_out, gate_lr, lru_x, lru_g = jnp.split(proj, SPLITS, axis=-1)
        y_gla = gla_heads(q, k, v, g_out, gate_lr, gla_gate_w2[l], gla_gate_b[l], gla_norm[l])
        y_lru = rglru_heads(lru_x, lru_g, lru_conv_w[l], lru_conv_b[l], lru_wa[l], lru_ba[l],
                            lru_wx[l], lru_bx[l], lru_lambda[l])
        h = h + jnp.concatenate([y_gla, y_lru], axis=-1) @ w_out[l]
        u = rmsnorm(h, ln_ffn[l])
        h = h + conv_ffn(u, ffn_w_in[l], ffn_conv_w[l], ffn_conv_b[l], ffn_w_down[l])
    return rmsnorm(h, ln_final)


import jax as _jax
import jax.numpy as _jnp

TWIN_FORMAT = 'train_step'
FWD_PARAMS = ['x', 'ln_mix', 'w_in', 'gla_gate_w2', 'gla_gate_b', 'gla_norm', 'lru_conv_w', 'lru_conv_b', 'lru_wa', 'lru_ba', 'lru_wx', 'lru_bx', 'lru_lambda', 'w_out', 'ln_ffn', 'ffn_w_in', 'ffn_conv_w', 'ffn_conv_b', 'ffn_w_down', 'ln_final']
TWIN_WEIGHTS = ['ln_mix', 'w_in', 'gla_gate_w2', 'gla_gate_b', 'gla_norm', 'lru_conv_w', 'lru_conv_b', 'lru_wa', 'lru_ba', 'lru_wx', 'lru_bx', 'lru_lambda', 'w_out', 'ln_ffn', 'ffn_w_in', 'ffn_conv_w', 'ffn_conv_b', 'ffn_w_down', 'ln_final']
TWIN_DIFF_INPUT = 'x'
TWIN_INPUTS = ['x', 'ln_mix', 'w_in', 'gla_gate_w2', 'gla_gate_b', 'gla_norm', 'lru_conv_w', 'lru_conv_b', 'lru_wa', 'lru_ba', 'lru_wx', 'lru_bx', 'lru_lambda', 'w_out', 'ln_ffn', 'ffn_w_in', 'ffn_conv_w', 'ffn_conv_b', 'ffn_w_down', 'ln_final', 'loss_target', 'm_ln_mix', 'm_w_in', 'm_gla_gate_w2', 'm_gla_gate_b', 'm_gla_norm', 'm_lru_conv_w', 'm_lru_conv_b', 'm_lru_wa', 'm_lru_ba', 'm_lru_wx', 'm_lru_bx', 'm_lru_lambda', 'm_w_out', 'm_ln_ffn', 'm_ffn_w_in', 'm_ffn_conv_w', 'm_ffn_conv_b', 'm_ffn_w_down', 'm_ln_final', 'v_ln_mix', 'v_w_in', 'v_gla_gate_w2', 'v_gla_gate_b', 'v_gla_norm', 'v_lru_conv_w', 'v_lru_conv_b', 'v_lru_wa', 'v_lru_ba', 'v_lru_wx', 'v_lru_bx', 'v_lru_lambda', 'v_w_out', 'v_ln_ffn', 'v_ffn_w_in', 'v_ffn_conv_w', 'v_ffn_conv_b', 'v_ffn_w_down', 'v_ln_final']
TWIN_OUTPUTS = ['loss', 'grad_x', 'grad_ln_mix', 'grad_w_in', 'grad_gla_gate_w2', 'grad_gla_gate_b', 'grad_gla_norm', 'grad_lru_conv_w', 'grad_lru_conv_b', 'grad_lru_wa', 'grad_lru_ba', 'grad_lru_wx', 'grad_lru_bx', 'grad_lru_lambda', 'grad_w_out', 'grad_ln_ffn', 'grad_ffn_w_in', 'grad_ffn_conv_w', 'grad_ffn_conv_b', 'grad_ffn_w_down', 'grad_ln_final', 'delta_ln_mix', 'delta_w_in', 'delta_gla_gate_w2', 'delta_gla_gate_b', 'delta_gla_norm', 'delta_lru_conv_w', 'delta_lru_conv_b', 'delta_lru_wa', 'delta_lru_ba', 'delta_lru_wx', 'delta_lru_bx', 'delta_lru_lambda', 'delta_w_out', 'delta_ln_ffn', 'delta_ffn_w_in', 'delta_ffn_conv_w', 'delta_ffn_conv_b', 'delta_ffn_w_down', 'delta_ln_final', 'new_m_ln_mix', 'new_m_w_in', 'new_m_gla_gate_w2', 'new_m_gla_gate_b', 'new_m_gla_norm', 'new_m_lru_conv_w', 'new_m_lru_conv_b', 'new_m_lru_wa', 'new_m_lru_ba', 'new_m_lru_wx', 'new_m_lru_bx', 'new_m_lru_lambda', 'new_m_w_out', 'new_m_ln_ffn', 'new_m_ffn_w_in', 'new_m_ffn_conv_w', 'new_m_ffn_conv_b', 'new_m_ffn_w_down', 'new_m_ln_final', 'new_v_ln_mix', 'new_v_w_in', 'new_v_gla_gate_w2', 'new_v_gla_gate_b', 'new_v_gla_norm', 'new_v_lru_conv_w', 'new_v_lru_conv_b', 'new_v_lru_wa', 'new_v_lru_ba', 'new_v_lru_wx', 'new_v_lru_bx', 'new_v_lru_lambda', 'new_v_w_out', 'new_v_ln_ffn', 'new_v_ffn_w_in', 'new_v_ffn_conv_w', 'new_v_ffn_conv_b', 'new_v_ffn_w_down', 'new_v_ln_final']
TWIN_LEAF_KINDS = {'loss': 'loss', 'grad_x': 'grad_x', 'grad_ln_mix': 'grad_w', 'grad_w_in': 'grad_w', 'grad_gla_gate_w2': 'grad_w', 'grad_gla_gate_b': 'grad_w', 'grad_gla_norm': 'grad_w', 'grad_lru_conv_w': 'grad_w', 'grad_lru_conv_b': 'grad_w', 'grad_lru_wa': 'grad_w', 'grad_lru_ba': 'grad_w', 'grad_lru_wx': 'grad_w', 'grad_lru_bx': 'grad_w', 'grad_lru_lambda': 'grad_w', 'grad_w_out': 'grad_w', 'grad_ln_ffn': 'grad_w', 'grad_ffn_w_in': 'grad_w', 'grad_ffn_conv_w': 'grad_w', 'grad_ffn_conv_b': 'grad_w', 'grad_ffn_w_down': 'grad_w', 'grad_ln_final': 'grad_w', 'delta_ln_mix': 'delta_w', 'delta_w_in': 'delta_w', 'delta_gla_gate_w2': 'delta_w', 'delta_gla_gate_b': 'delta_w', 'delta_gla_norm': 'delta_w', 'delta_lru_conv_w': 'delta_w', 'delta_lru_conv_b': 'delta_w', 'delta_lru_wa': 'delta_w', 'delta_lru_ba': 'delta_w', 'delta_lru_wx': 'delta_w', 'delta_lru_bx': 'delta_w', 'delta_lru_lambda': 'delta_w', 'delta_w_out': 'delta_w', 'delta_ln_ffn': 'delta_w', 'delta_ffn_w_in': 'delta_w', 'delta_ffn_conv_w': 'delta_w', 'delta_ffn_conv_b': 'delta_w', 'delta_ffn_w_down': 'delta_w', 'delta_ln_final': 'delta_w', 'new_m_ln_mix': 'new_m', 'new_m_w_in': 'new_m', 'new_m_gla_gate_w2': 'new_m', 'new_m_gla_gate_b': 'new_m', 'new_m_gla_norm': 'new_m', 'new_m_lru_conv_w': 'new_m', 'new_m_lru_conv_b': 'new_m', 'new_m_lru_wa': 'new_m', 'new_m_lru_ba': 'new_m', 'new_m_lru_wx': 'new_m', 'new_m_lru_bx': 'new_m', 'new_m_lru_lambda': 'new_m', 'new_m_w_out': 'new_m', 'new_m_ln_ffn': 'new_m', 'new_m_ffn_w_in': 'new_m', 'new_m_ffn_conv_w': 'new_m', 'new_m_ffn_conv_b': 'new_m', 'new_m_ffn_w_down': 'new_m', 'new_m_ln_final': 'new_m', 'new_v_ln_mix': 'new_v', 'new_v_w_in': 'new_v', 'new_v_gla_gate_w2': 'new_v', 'new_v_gla_gate_b': 'new_v', 'new_v_gla_norm': 'new_v', 'new_v_lru_conv_w': 'new_v', 'new_v_lru_conv_b': 'new_v', 'new_v_lru_wa': 'new_v', 'new_v_lru_ba': 'new_v', 'new_v_lru_wx': 'new_v', 'new_v_lru_bx': 'new_v', 'new_v_lru_lambda': 'new_v', 'new_v_w_out': 'new_v', 'new_v_ln_ffn': 'new_v', 'new_v_ffn_w_in': 'new_v', 'new_v_ffn_conv_w': 'new_v', 'new_v_ffn_conv_b': 'new_v', 'new_v_ffn_w_down': 'new_v', 'new_v_ln_final': 'new_v'}


def _forward(args):
    return _fwd_reference(*[args[k] for k in FWD_PARAMS])


def _output_shape():
    def fwd():
        inp = _fwd_setup_inputs(0)
        return _fwd_reference(*[inp[k] for k in FWD_PARAMS])
    out = _jax.eval_shape(fwd)
    return out.shape, out.dtype

N_MICROBATCH = 1
ADAM_LR = 0.001
ADAM_B1 = 0.9
ADAM_B2 = 0.999
ADAM_EPS = 1e-08
ADAM_WD = 0.01
ADAM_STEP = 10
PER_EXAMPLE_BATCH_AXIS = {'x': 0, 'loss_target': 0}
SHARED_INPUTS = []
_WEIGHT_DTYPES = {'ln_mix': _jnp.float32, 'w_in': _jnp.float32, 'gla_gate_w2': _jnp.float32, 'gla_gate_b': _jnp.float32, 'gla_norm': _jnp.float32, 'lru_conv_w': _jnp.float32, 'lru_conv_b': _jnp.float32, 'lru_wa': _jnp.float32, 'lru_ba': _jnp.float32, 'lru_wx': _jnp.float32, 'lru_bx': _jnp.float32, 'lru_lambda': _jnp.float32, 'w_out': _jnp.float32, 'ln_ffn': _jnp.float32, 'ffn_w_in': _jnp.float32, 'ffn_conv_w': _jnp.float32, 'ffn_conv_b': _jnp.float32, 'ffn_w_down': _jnp.float32, 'ln_final': _jnp.float32}
MOMENT_SCALE = {'ln_mix': 4.196194e-01, 'w_in': 2.482906e-01, 'gla_gate_w2': 4.116178e-02, 'gla_gate_b': 1.770959e-01, 'gla_norm': 4.612948e-01, 'lru_conv_w': 2.925620e-01, 'lru_conv_b': 3.841768e+00, 'lru_wa': 1.284206e-01, 'lru_ba': 9.618132e-02, 'lru_wx': 2.368764e-01, 'lru_bx': 1.266261e-01, 'lru_lambda': 1.622828e-01, 'w_out': 2.503151e-01, 'ln_ffn': 3.016370e-01, 'ffn_w_in': 1.234787e-01, 'ffn_conv_w': 1.230441e-01, 'ffn_conv_b': 1.661652e-01, 'ffn_w_down': 2.101845e-01, 'ln_final': 1.279686e+02}


def _to_microbatches(a, axis):
    t = _jnp.moveaxis(a, axis, 0)
    t = t.reshape((N_MICROBATCH, t.shape[0] // N_MICROBATCH) + t.shape[1:])
    return _jnp.moveaxis(t, 1, axis + 1)


def setup_inputs(seed: int = 0) -> dict:
    inp = _fwd_setup_inputs(seed)
    key = _jax.random.fold_in(_jax.random.key(seed), 7919)
    shape, _ = _output_shape()
    out = dict(inp)
    out["loss_target"] = _jax.random.normal(_jax.random.fold_in(key, 0), shape, _jnp.float32)
    for i, name in enumerate(TWIN_WEIGHTS):
        w = inp[name].astype(_jnp.float32)
        if MOMENT_SCALE is None:
            s = _jnp.sqrt(_jnp.mean(_jnp.square(w)) + 1e-30)
        else:
            s = MOMENT_SCALE[name]
        km, kv = _jax.random.split(_jax.random.fold_in(key, i + 1))
        out[name] = w
        out["m_" + name] = s * _jax.random.normal(km, w.shape, _jnp.float32)
        out["v_" + name] = (s * s) * _jax.random.uniform(kv, w.shape, _jnp.float32, 0.5, 1.5)
    if N_MICROBATCH > 1:
        for name, axis in PER_EXAMPLE_BATCH_AXIS.items():
            out[name] = _to_microbatches(out[name], axis)
    return {'x': out['x'], 'ln_mix': out['ln_mix'], 'w_in': out['w_in'], 'gla_gate_w2': out['gla_gate_w2'], 'gla_gate_b': out['gla_gate_b'], 'gla_norm': out['gla_norm'], 'lru_conv_w': out['lru_conv_w'], 'lru_conv_b': out['lru_conv_b'], 'lru_wa': out['lru_wa'], 'lru_ba': out['lru_ba'], 'lru_wx': out['lru_wx'], 'lru_bx': out['lru_bx'], 'lru_lambda': out['lru_lambda'], 'w_out': out['w_out'], 'ln_ffn': out['ln_ffn'], 'ffn_w_in': out['ffn_w_in'], 'ffn_conv_w': out['ffn_conv_w'], 'ffn_conv_b': out['ffn_conv_b'], 'ffn_w_down': out['ffn_w_down'], 'ln_final': out['ln_final'], 'loss_target': out['loss_target'], 'm_ln_mix': out['m_ln_mix'], 'm_w_in': out['m_w_in'], 'm_gla_gate_w2': out['m_gla_gate_w2'], 'm_gla_gate_b': out['m_gla_gate_b'], 'm_gla_norm': out['m_gla_norm'], 'm_lru_conv_w': out['m_lru_conv_w'], 'm_lru_conv_b': out['m_lru_conv_b'], 'm_lru_wa': out['m_lru_wa'], 'm_lru_ba': out['m_lru_ba'], 'm_lru_wx': out['m_lru_wx'], 'm_lru_bx': out['m_lru_bx'], 'm_lru_lambda': out['m_lru_lambda'], 'm_w_out': out['m_w_out'], 'm_ln_ffn': out['m_ln_ffn'], 'm_ffn_w_in': out['m_ffn_w_in'], 'm_ffn_conv_w': out['m_ffn_conv_w'], 'm_ffn_conv_b': out['m_ffn_conv_b'], 'm_ffn_w_down': out['m_ffn_w_down'], 'm_ln_final': out['m_ln_final'], 'v_ln_mix': out['v_ln_mix'], 'v_w_in': out['v_w_in'], 'v_gla_gate_w2': out['v_gla_gate_w2'], 'v_gla_gate_b': out['v_gla_gate_b'], 'v_gla_norm': out['v_gla_norm'], 'v_lru_conv_w': out['v_lru_conv_w'], 'v_lru_conv_b': out['v_lru_conv_b'], 'v_lru_wa': out['v_lru_wa'], 'v_lru_ba': out['v_lru_ba'], 'v_lru_wx': out['v_lru_wx'], 'v_lru_bx': out['v_lru_bx'], 'v_lru_lambda': out['v_lru_lambda'], 'v_w_out': out['v_w_out'], 'v_ln_ffn': out['v_ln_ffn'], 'v_ffn_w_in': out['v_ffn_w_in'], 'v_ffn_conv_w': out['v_ffn_conv_w'], 'v_ffn_conv_b': out['v_ffn_conv_b'], 'v_ffn_w_down': out['v_ffn_w_down'], 'v_ln_final': out['v_ln_final']}


def _loss(weights, diff, rest, loss_target):
    with _jax.named_scope("forward"):
        args = {**rest, TWIN_DIFF_INPUT: diff, **{k: w.astype(_WEIGHT_DTYPES[k]) for k, w in weights.items()}}
        y = _forward(args)
    with _jax.named_scope("loss_head"):
        err = _jnp.square(y.astype(_jnp.float32) - loss_target)
        return 0.5 * _jnp.sum(_jnp.mean(err, axis=-1)) if err.ndim else 0.5 * err


def _adamw(w, g, m, v):
    m = ADAM_B1 * m + (1.0 - ADAM_B1) * g
    v = ADAM_B2 * v + (1.0 - ADAM_B2) * _jnp.square(g)
    m_hat = m / (1.0 - ADAM_B1 ** ADAM_STEP)
    v_hat = v / (1.0 - ADAM_B2 ** ADAM_STEP)
    delta = -ADAM_LR * (m_hat / (_jnp.sqrt(v_hat) + ADAM_EPS) + ADAM_WD * w)
    return delta, m, v


def reference(x, ln_mix, w_in, gla_gate_w2, gla_gate_b, gla_norm, lru_conv_w, lru_conv_b, lru_wa, lru_ba, lru_wx, lru_bx, lru_lambda, w_out, ln_ffn, ffn_w_in, ffn_conv_w, ffn_conv_b, ffn_w_down, ln_final, loss_target, m_ln_mix, m_w_in, m_gla_gate_w2, m_gla_gate_b, m_gla_norm, m_lru_conv_w, m_lru_conv_b, m_lru_wa, m_lru_ba, m_lru_wx, m_lru_bx, m_lru_lambda, m_w_out, m_ln_ffn, m_ffn_w_in, m_ffn_conv_w, m_ffn_conv_b, m_ffn_w_down, m_ln_final, v_ln_mix, v_w_in, v_gla_gate_w2, v_gla_gate_b, v_gla_norm, v_lru_conv_w, v_lru_conv_b, v_lru_wa, v_lru_ba, v_lru_wx, v_lru_bx, v_lru_lambda, v_w_out, v_ln_ffn, v_ffn_w_in, v_ffn_conv_w, v_ffn_conv_b, v_ffn_w_down, v_ln_final):
    given = dict(x=x, ln_mix=ln_mix, w_in=w_in, gla_gate_w2=gla_gate_w2, gla_gate_b=gla_gate_b, gla_norm=gla_norm, lru_conv_w=lru_conv_w, lru_conv_b=lru_conv_b, lru_wa=lru_wa, lru_ba=lru_ba, lru_wx=lru_wx, lru_bx=lru_bx, lru_lambda=lru_lambda, w_out=w_out, ln_ffn=ln_ffn, ffn_w_in=ffn_w_in, ffn_conv_w=ffn_conv_w, ffn_conv_b=ffn_conv_b, ffn_w_down=ffn_w_down, ln_final=ln_final, loss_target=loss_target, m_ln_mix=m_ln_mix, m_w_in=m_w_in, m_gla_gate_w2=m_gla_gate_w2, m_gla_gate_b=m_gla_gate_b, m_gla_norm=m_gla_norm, m_lru_conv_w=m_lru_conv_w, m_lru_conv_b=m_lru_conv_b, m_lru_wa=m_lru_wa, m_lru_ba=m_lru_ba, m_lru_wx=m_lru_wx, m_lru_bx=m_lru_bx, m_lru_lambda=m_lru_lambda, m_w_out=m_w_out, m_ln_ffn=m_ln_ffn, m_ffn_w_in=m_ffn_w_in, m_ffn_conv_w=m_ffn_conv_w, m_ffn_conv_b=m_ffn_conv_b, m_ffn_w_down=m_ffn_w_down, m_ln_final=m_ln_final, v_ln_mix=v_ln_mix, v_w_in=v_w_in, v_gla_gate_w2=v_gla_gate_w2, v_gla_gate_b=v_gla_gate_b, v_gla_norm=v_gla_norm, v_lru_conv_w=v_lru_conv_w, v_lru_conv_b=v_lru_conv_b, v_lru_wa=v_lru_wa, v_lru_ba=v_lru_ba, v_lru_wx=v_lru_wx, v_lru_bx=v_lru_bx, v_lru_lambda=v_lru_lambda, v_w_out=v_w_out, v_ln_ffn=v_ln_ffn, v_ffn_w_in=v_ffn_w_in, v_ffn_conv_w=v_ffn_conv_w, v_ffn_conv_b=v_ffn_conv_b, v_ffn_w_down=v_ffn_w_down, v_ln_final=v_ln_final)
    weights = {n: given[n] for n in TWIN_WEIGHTS}
    shared = {n: given[n] for n in SHARED_INPUTS}
    per_example = {n: given[n] for n in ['x']}
    grad_fn = _jax.value_and_grad(_loss, argnums=(0, 1))

    def one_microbatch(ex, loss_target):
        ex = dict(ex)
        diff = ex.pop(TWIN_DIFF_INPUT)
        return grad_fn(weights, diff, {**shared, **ex}, loss_target)

    if N_MICROBATCH == 1:
        loss, (grad_w, grad_x) = one_microbatch(per_example, given["loss_target"])
    else:
        def body(carry, xs):
            loss_sum, grad_sum = carry
            l_k, (gw_k, gx_k) = one_microbatch(xs[0], xs[1])
            with _jax.named_scope("update"):
                return (loss_sum + l_k, _jax.tree.map(_jnp.add, grad_sum, gw_k)), gx_k

        init = (_jnp.zeros((), _jnp.float32), _jax.tree.map(_jnp.zeros_like, weights))
        (loss, grad_w), grad_x = _jax.lax.scan(body, init, (per_example, given["loss_target"]))
    with _jax.named_scope("update"):
        delta_w, new_m, new_v = {}, {}, {}
        for n in TWIN_WEIGHTS:
            delta_w[n], new_m[n], new_v[n] = _adamw(weights[n], grad_w[n], given["m_" + n], given["v_" + n])
    return (loss, grad_x, *[grad_w[n] for n in TWIN_WEIGHTS], *[delta_w[n] for n in TWIN_WEIGHTS],
            *[new_m[n] for n in TWIN_WEIGHTS], *[new_v[n] for n in TWIN_WEIGHTS])
```

```python
import math

import jax
import jax.numpy as jnp
from jax import lax
from jax.experimental import pallas as pl
from jax.experimental.pallas import tpu as pltpu

F32 = jnp.float32
BF16 = jnp.bfloat16
MXU_DTYPE = BF16

D_MODEL = 1024
DEPTH = 4
HEADS, DK, DV, CHUNK, GATE_RANK = 4, 64, 128, 64, 16
QK_W = HEADS * DK
GLA_W = HEADS * DV
LRU_W = 512
LRU_BLOCKS, LRU_BLOCK, LRU_CONV, LRU_C = 8, 64, 4, 8.0
FFN_H = 3 * D_MODEL
FFN_CONV = 3
EPS = 1e-6
GATE_PAD = 128
GLA_COLS = 2 * QK_W + 2 * GLA_W + GATE_PAD
LRU_COLS = 2 * LRU_W
ADAM_LR, ADAM_B1, ADAM_B2, ADAM_EPS, ADAM_WD, ADAM_STEP = 0.001, 0.9, 0.999, 1e-08, 0.01, 10

LANES = 128
SUBLANES = 8
VMEM_LIMIT = 56 * 1024 * 1024
ROWS = 512
TILE_BYTES = 1 << 20
FFN_CB = 512
N_CHIPS = 4
N_DEV = 8
MESH = pl.DeviceIdType.MESH


def _cp(*sem):
    return pltpu.CompilerParams(dimension_semantics=sem, vmem_limit_bytes=VMEM_LIMIT)


def _dot(a, b):
    return jnp.dot(a.astype(MXU_DTYPE), b.astype(MXU_DTYPE), preferred_element_type=F32)


def _dot_nt(a, b):
    return lax.dot_general(a.astype(MXU_DTYPE), b.astype(MXU_DTYPE), (((1,), (1,)), ((), ())),
                           preferred_element_type=F32)


def _dot_tn(a, b):
    return lax.dot_general(a.astype(MXU_DTYPE), b.astype(MXU_DTYPE), (((0,), (0,)), ((), ())),
                           preferred_element_type=F32)


def _bdot(eq, a, b):
    return jnp.einsum(eq, a, b, preferred_element_type=F32)


def _split3(x):
    x1 = x.astype(BF16)
    r1 = x - x1.astype(F32)
    x2 = r1.astype(BF16)
    x3 = (r1 - x2.astype(F32)).astype(BF16)
    return x1, x2, x3


def _gelu(x):
    c = math.sqrt(2.0 / math.pi)
    return x * (0.5 * (1.0 + jnp.tanh(c * (x + 0.044715 * (x * x * x)))))


def _gelu_and_grad(x):
    c = math.sqrt(2.0 / math.pi)
    t = jnp.tanh(c * (x + 0.044715 * (x * x * x)))
    cdf = 0.5 * (1.0 + t)
    dcdf = 0.5 * (1.0 - t * t) * (c * (1.0 + 3.0 * 0.044715 * (x * x)))
    return x * cdf, cdf + x * dcdf


def _expm1(x):
    small = x * (1.0 + x * (0.5 + x * (1.0 / 6.0 + x * (1.0 / 24.0 + x * (1.0 / 120.0)))))
    return jnp.where(jnp.abs(x) < 0.1, small, jnp.exp(x) - 1.0)


def _shift_down(x, k, fill):
    row = lax.broadcasted_iota(jnp.int32, x.shape, 0)
    return jnp.where(row >= k, pltpu.roll(x, k, axis=0), fill)


def _shift_up(x, k, fill):
    n = x.shape[0]
    row = lax.broadcasted_iota(jnp.int32, x.shape, 0)
    return jnp.where(row < n - k, pltpu.roll(x, n - k, axis=0), fill)


def _rms_fwd(h, g, name):
    T, D = h.shape
    R = min(T, ROWS)

    def body(h_ref, g_ref, o_ref):
        x = h_ref[...]
        r = lax.rsqrt(jnp.mean(x * x, axis=-1, keepdims=True) + EPS)
        o_ref[...] = ((x * r) * g_ref[...]).astype(o_ref.dtype)

    return pl.pallas_call(
        body, name=name, grid=(T // R,),
        in_specs=[pl.BlockSpec((R, D), lambda i: (i, 0)), pl.BlockSpec((1, D), lambda i: (0, 0))],
        out_specs=pl.BlockSpec((R, D), lambda i: (i, 0)),
        out_shape=jax.ShapeDtypeStruct((T, D), BF16), compiler_params=_cp("parallel"),
    )(h, g.reshape(1, D))


def _rms_bwd(h, g, du, dres, name):
    T, D = h.shape
    R = min(T, ROWS)

    def body(h_ref, g_ref, du_ref, dres_ref, dh_ref, dg_ref):
        @pl.when(pl.program_id(0) == 0)
        def _():
            dg_ref[...] = jnp.zeros_like(dg_ref)

        x = h_ref[...]
        r = lax.rsqrt(jnp.mean(x * x, axis=-1, keepdims=True) + EPS)
        xhat = x * r
        dy = du_ref[...].astype(F32)
        dg_ref[...] += jnp.sum(dy * xhat, axis=0, keepdims=True)
        dxhat = dy * g_ref[...]
        dx = r * (dxhat - xhat * jnp.mean(dxhat * xhat, axis=-1, keepdims=True))
        dh_ref[...] = dres_ref[...] + dx

    blk = pl.BlockSpec((R, D), lambda i: (i, 0))
    vec = pl.BlockSpec((1, D), lambda i: (0, 0))
    return pl.pallas_call(
        body, name=name, grid=(T // R,), in_specs=[blk, vec, blk, blk], out_specs=[blk, vec],
        out_shape=[jax.ShapeDtypeStruct((T, D), F32), jax.ShapeDtypeStruct((1, D), F32)],
        compiler_params=_cp("arbitrary"),
    )(h, g.reshape(1, D), du, dres)


def _loss_head(h, g, tgt, name):
    T, D = h.shape
    R = min(T, ROWS)

    def body(h_ref, g_ref, t_ref, loss_ref, dh_ref, dg_ref):
        @pl.when(pl.program_id(0) == 0)
        def _():
            dg_ref[...] = jnp.zeros_like(dg_ref)
            loss_ref[...] = jnp.zeros_like(loss_ref)

        x = h_ref[...]
        r = lax.rsqrt(jnp.mean(x * x, axis=-1, keepdims=True) + EPS)
        xhat = x * r
        gg = g_ref[...]
        err = xhat * gg - t_ref[...]
        loss_ref[...] += 0.5 * jnp.sum(jnp.mean(err * err, axis=-1, keepdims=True), axis=0, keepdims=True)
        dy = err * (1.0 / D)
        dg_ref[...] += jnp.sum(dy * xhat, axis=0, keepdims=True)
        dxhat = dy * gg
        dh_ref[...] = r * (dxhat - xhat * jnp.mean(dxhat * xhat, axis=-1, keepdims=True))

    blk = pl.BlockSpec((R, D), lambda i: (i, 0))
    vec = pl.BlockSpec((1, D), lambda i: (0, 0))
    one = pl.BlockSpec((1, LANES), lambda i: (0, 0))
    return pl.pallas_call(
        body, name=name, grid=(T // R,), in_specs=[blk, vec, blk], out_specs=[one, blk, vec],
        out_shape=[jax.ShapeDtypeStruct((1, LANES), F32), jax.ShapeDtypeStruct((T, D), F32),
                   jax.ShapeDtypeStruct((1, D), F32)],
        compiler_params=_cp("arbitrary"),
    )(h, g.reshape(1, D), tgt)


def _mm(a, b, res, out_dtype, name, tm=512, tn=None):
    M, K = a.shape
    N = b.shape[1]
    tm = min(tm, M)
    tn = N if tn is None else tn

    def body(*refs):
        if res is None:
            a_ref, b_ref, o_ref = refs
            acc = _dot(a_ref[...], b_ref[...])
        else:
            a_ref, b_ref, r_ref, o_ref = refs
            acc = r_ref[...].astype(F32) + _dot(a_ref[...], b_ref[...])
        o_ref[...] = acc.astype(o_ref.dtype)

    in_specs = [pl.BlockSpec((tm, K), lambda j, i: (i, 0)), pl.BlockSpec((K, tn), lambda j, i: (0, j))]
    args = [a, b]
    if res is not None:
        in_specs.append(pl.BlockSpec((tm, tn), lambda j, i: (i, j)))
        args.append(res)
    return pl.pallas_call(
        body, name=name, grid=(N // tn, M // tm), in_specs=in_specs,
        out_specs=pl.BlockSpec((tm, tn), lambda j, i: (i, j)),
        out_shape=jax.ShapeDtypeStruct((M, N), out_dtype), compiler_params=_cp("parallel", "parallel"),
    )(*args)


def _mm_tn(a, b, name, tm=512, tk=None, tn=None):
    M, K = a.shape
    N = b.shape[1]
    tm = min(tm, M)
    tk = K if tk is None else tk
    tn = N if tn is None else tn

    def body(a_ref, b_ref, o_ref):
        @pl.when(pl.program_id(2) == 0)
        def _():
            o_ref[...] = jnp.zeros_like(o_ref)

        o_ref[...] += _dot_tn(a_ref[...], b_ref[...])

    return pl.pallas_call(
        body, name=name, grid=(K // tk, N // tn, M // tm),
        in_specs=[pl.BlockSpec((tm, tk), lambda k, n, m: (m, k)), pl.BlockSpec((tm, tn), lambda k, n, m: (m, n))],
        out_specs=pl.BlockSpec((tk, tn), lambda k, n, m: (k, n)),
        out_shape=jax.ShapeDtypeStruct((K, N), F32), compiler_params=_cp("parallel", "parallel", "arbitrary"),
    )(a, b)


def _same_chunk(row, col):
    shift = CHUNK.bit_length() - 1
    return jnp.right_shift(row, shift) == jnp.right_shift(col, shift)


def _gla_common(q, k, glr, w2, b2, R):
    gl = _dot(glr, w2) + b2
    la = jax.nn.log_sigmoid(gl) * (1.0 / 16.0)
    row = lax.broadcasted_iota(jnp.int32, (R, R), 0)
    col = lax.broadcasted_iota(jnp.int32, (R, R), 1)
    same = _same_chunk(row, col)
    m_tri = (same & (col <= row)).astype(BF16)
    m_all = same.astype(BF16)
    la3 = _split3(la)
    b = sum(jnp.dot(m_tri, p, preferred_element_type=F32) for p in la3)
    bl = sum(jnp.dot(m_all, p, preferred_element_type=F32) for p in la3)
    eb = jnp.exp(b)
    enb = jnp.exp(-b)
    ek = jnp.exp(bl - b)
    qi = (q * (DK ** -0.5)) * eb
    ki = k * enb
    kd = k * ek
    return gl, la3, eb, enb, ek, qi, ki, kd


def _bsplit(x, n):
    return x.reshape(n, CHUNK, x.shape[-1])


def _tril():
    return (lax.broadcasted_iota(jnp.int32, (CHUNK, CHUNK), 1)
            <= lax.broadcasted_iota(jnp.int32, (CHUNK, CHUNK), 0))[None]


def _gla_fwd(proj, w2p, b2, norm_g, name):
    T = proj.shape[0]
    R = min(T, ROWS)
    n = R // CHUNK

    def body(q_ref, k_ref, v_ref, g_ref, a_ref, w2_ref, b2_ref, ng_ref, y_ref, o_ref, st_ref, s_ref):
        @pl.when(pl.program_id(0) == 0)
        def _():
            s_ref[...] = jnp.zeros_like(s_ref)

        _, la3, _, _, _, qi, ki, kd = _gla_common(q_ref[...], k_ref[...], a_ref[...], w2_ref[...], b2_ref[...], R)
        tril = _tril()
        ones = jnp.ones((n, CHUNK, DV), BF16)
        for h in range(HEADS):
            sl = slice(h * DK, (h + 1) * DK)
            sv = slice(h * DV, (h + 1) * DV)
            qh = _bsplit(qi[:, sl], n).astype(MXU_DTYPE)
            kh = _bsplit(ki[:, sl], n).astype(MXU_DTYPE)
            kdh = _bsplit(kd[:, sl], n).astype(MXU_DTYPE)
            vh = _bsplit(v_ref[:, sv], n).astype(MXU_DTYPE)
            att = jnp.where(tril, _bdot('ncd,nsd->ncs', qh, kh), 0.0)
            upd = _bdot('ncd,nce->nde', kdh, vh)
            dect = jnp.exp(sum(_bdot('ncd,nce->nde', _bsplit(p[:, sl], n), ones) for p in la3))
            s = s_ref[sl, :]
            for c in range(n):
                st_ref[c, sl, :] = s
                s = dect[c] * s + upd[c]
            s_ref[sl, :] = s
            sp = st_ref[:, sl, :].astype(MXU_DTYPE)
            o = (_bdot('ncs,nse->nce', att.astype(MXU_DTYPE), vh) + _bdot('ncd,nde->nce', qh, sp)).reshape(R, DV)
            o_ref[:, sv] = o
            r = lax.rsqrt(jnp.mean(o * o, axis=-1, keepdims=True) + EPS)
            gate = g_ref[:, sv]
            y_ref[:, sv] = (((o * r) * ng_ref[...]) * (gate * jax.nn.sigmoid(gate))).astype(y_ref.dtype)

    cb = lambda w, j: pl.BlockSpec((R, w), lambda i: (i, j))
    full = lambda s: pl.BlockSpec(s, lambda i: (0,) * len(s))
    return pl.pallas_call(
        body, name=name, grid=(T // R,),
        in_specs=[cb(QK_W, 0), cb(QK_W, 1), cb(GLA_W, 1), cb(GLA_W, 2), cb(GATE_PAD, 12),
                  full((GATE_PAD, QK_W)), full((1, QK_W)), full((1, DV))],
        out_specs=[pl.BlockSpec((R, GLA_W), lambda i: (i, 0)), pl.BlockSpec((R, GLA_W), lambda i: (i, 0)),
                   pl.BlockSpec((n, QK_W, DV), lambda i: (i, 0, 0))],
        out_shape=[jax.ShapeDtypeStruct((T, GLA_W), BF16), jax.ShapeDtypeStruct((T, GLA_W), F32),
                   jax.ShapeDtypeStruct((T // CHUNK, QK_W, DV), F32)],
        scratch_shapes=[pltpu.VMEM((QK_W, DV), F32)],
        compiler_params=_cp("arbitrary"),
    )(proj, proj, proj, proj, proj, w2p, b2.reshape(1, QK_W), norm_g.reshape(1, DV))


def _gla_bwd(dy, proj, o_st, s_st, w2p, b2, norm_g, name):
    T = proj.shape[0]
    R = min(T, ROWS)
    n = R // CHUNK
    nb = T // R

    def body(dy_ref, q_ref, k_ref, v_ref, g_ref, a_ref, o_ref, st_ref, w2_ref, b2_ref, ng_ref,
             dp_ref, dw2_ref, db2_ref, dng_ref, gs_ref, gn_ref, db_ref, dbl_ref):
        @pl.when(pl.program_id(0) == 0)
        def _():
            gs_ref[...] = jnp.zeros_like(gs_ref)
            dw2_ref[...] = jnp.zeros_like(dw2_ref)
            db2_ref[...] = jnp.zeros_like(db2_ref)
            dng_ref[...] = jnp.zeros_like(dng_ref)

        glr = a_ref[...]
        gl, la3, eb, enb, ek, qi, ki, kd = _gla_common(q_ref[...], k_ref[...], glr, w2_ref[...], b2_ref[...], R)
        tril = _tril()
        ones = jnp.ones((n, CHUNK, DV), BF16)
        ng = ng_ref[...]
        dng = jnp.zeros((1, DV), F32)
        for h in range(HEADS):
            sl = slice(h * DK, (h + 1) * DK)
            sv = slice(h * DV, (h + 1) * DV)
            o = o_ref[:, sv]
            r = lax.rsqrt(jnp.mean(o * o, axis=-1, keepdims=True) + EPS)
            xhat = o * r
            gate = g_ref[:, sv]
            sg = jax.nn.sigmoid(gate)
            dyh = dy_ref[:, sv].astype(F32)
            dp_ref[:, 2 * QK_W + GLA_W + h * DV:2 * QK_W + GLA_W + (h + 1) * DV] = (
                dyh * (xhat * ng) * (sg * (1.0 + gate * (1.0 - sg)))).astype(dp_ref.dtype)
            don = dyh * (gate * sg)
            dng = dng + jnp.sum(don * xhat, axis=0, keepdims=True)
            dxhat = don * ng
            do = r * (dxhat - xhat * jnp.mean(dxhat * xhat, axis=-1, keepdims=True))
            qf = _bsplit(qi[:, sl], n)
            kf = _bsplit(ki[:, sl], n)
            kdf = _bsplit(kd[:, sl], n)
            qh, kh, kdh = qf.astype(MXU_DTYPE), kf.astype(MXU_DTYPE), kdf.astype(MXU_DTYPE)
            vh = _bsplit(v_ref[:, sv], n).astype(MXU_DTYPE)
            doh = _bsplit(do, n).astype(MXU_DTYPE)
            spf = st_ref[:, sl, :]
            sp = spf.astype(MXU_DTYPE)
            att = jnp.where(tril, _bdot('ncd,nsd->ncs', qh, kh), 0.0).astype(MXU_DTYPE)
            datt = jnp.where(tril, _bdot('nce,nse->ncs', doh, vh), 0.0).astype(MXU_DTYPE)
            dv = _bdot('ncs,nce->nse', att, doh)
            dqi = _bdot('ncs,nsd->ncd', datt, kh) + _bdot('nce,nde->ncd', doh, sp)
            dki = _bdot('ncs,ncd->nsd', datt, qh)
            wgt = _bdot('ncd,nce->nde', qh, doh)
            dect = jnp.exp(sum(_bdot('ncd,nce->nde', _bsplit(p[:, sl], n), ones) for p in la3))
            g = gs_ref[sl, :]
            for c in reversed(range(n)):
                gn_ref[c] = g
                g = wgt[c] + dect[c] * g
            gs_ref[sl, :] = g
            gnf = gn_ref[...]
            gn = gnf.astype(MXU_DTYPE)
            dkd = _bdot('nce,nde->ncd', vh, gn)
            dv = dv + _bdot('ncd,nde->nce', kdh, gn)
            dp_ref[:, 2 * QK_W + h * DV:2 * QK_W + (h + 1) * DV] = dv.reshape(R, DV).astype(dp_ref.dtype)
            dbl = sum(_bdot('nce,nde->ncd', ones, p) for p in _split3(gnf * spf * dect))
            pk = dkd * kdf
            dbl = dbl + jnp.sum(pk, axis=1, keepdims=True)
            dbl_ref[:, sl] = dbl.reshape(R, DK)
            db_ref[:, sl] = (dqi * qf - dki * kf - pk).reshape(R, DK)
            dp_ref[:, sl] = ((dqi.reshape(R, DK) * (DK ** -0.5)) * eb[:, sl]).astype(dp_ref.dtype)
            dp_ref[:, QK_W + h * DK:QK_W + (h + 1) * DK] = (
                dki.reshape(R, DK) * enb[:, sl] + dkd.reshape(R, DK) * ek[:, sl]).astype(dp_ref.dtype)
        dng_ref[...] += dng
        row = lax.broadcasted_iota(jnp.int32, (R, R), 0)
        col = lax.broadcasted_iota(jnp.int32, (R, R), 1)
        m_rev = (_same_chunk(row, col) & (col >= row)).astype(BF16)
        dla = sum(jnp.dot(m_rev, p, preferred_element_type=F32) for p in _split3(db_ref[...])) + dbl_ref[...]
        dgl = (dla * (1.0 / 16.0)) * jax.nn.sigmoid(-gl)
        dp_ref[:, 2 * QK_W + 2 * GLA_W:GLA_COLS] = _dot_nt(dgl, w2_ref[...]).astype(dp_ref.dtype)
        dw2_ref[...] += _dot_tn(glr, dgl)
        db2_ref[...] += jnp.sum(dgl, axis=0, keepdims=True)

    cb = lambda w, j: pl.BlockSpec((R, w), lambda i: (nb - 1 - i, j))
    full = lambda s: pl.BlockSpec(s, lambda i: (0,) * len(s))
    return pl.pallas_call(
        body, name=name, grid=(nb,),
        in_specs=[cb(GLA_W, 0), cb(QK_W, 0), cb(QK_W, 1), cb(GLA_W, 1), cb(GLA_W, 2), cb(GATE_PAD, 12),
                  cb(GLA_W, 0), pl.BlockSpec((n, QK_W, DV), lambda i: (nb - 1 - i, 0, 0)),
                  full((GATE_PAD, QK_W)), full((1, QK_W)), full((1, DV))],
        out_specs=[pl.BlockSpec((R, GLA_COLS), lambda i: (nb - 1 - i, 0)),
                   full((GATE_PAD, QK_W)), full((1, QK_W)), full((1, DV))],
        out_shape=[jax.ShapeDtypeStruct((T, GLA_COLS), BF16), jax.ShapeDtypeStruct((GATE_PAD, QK_W), F32),
                   jax.ShapeDtypeStruct((1, QK_W), F32), jax.ShapeDtypeStruct((1, DV), F32)],
        scratch_shapes=[pltpu.VMEM((QK_W, DV), F32), pltpu.VMEM((n, DK, DV), F32),
                        pltpu.VMEM((R, QK_W), F32), pltpu.VMEM((R, QK_W), F32)],
        compiler_params=_cp("arbitrary"),
    )(dy, proj, proj, proj, proj, proj, o_st, s_st, w2p, b2.reshape(1, QK_W), norm_g.reshape(1, DV))


def _lru_conv(ext_ref, cw_ref, cb_ref, R):
    xc = cb_ref[...] + ext_ref[pl.ds(SUBLANES - 3, R), :] * cw_ref[0:1, :]
    xc = xc + ext_ref[pl.ds(SUBLANES - 2, R), :] * cw_ref[1:2, :]
    xc = xc + ext_ref[pl.ds(SUBLANES - 1, R), :] * cw_ref[2:3, :]
    return xc + ext_ref[pl.ds(SUBLANES, R), :] * cw_ref[3:4, :]


def _lru_gates(xc, wa, ba, wx, bx, lam, first):
    r = jax.nn.sigmoid(_dot(xc, wa) + ba)
    ig = jax.nn.sigmoid(_dot(xc, wx) + bx)
    sp = jax.nn.softplus(-lam)
    la = (-LRU_C * r) * sp
    a = jnp.exp(la)
    mult = jnp.where(first, 1.0, jnp.sqrt(-_expm1(2.0 * la)))
    return r, ig, sp, a, mult


def _lru_fwd(proj, cw, cb, wa, ba, wx, bx, lam, name):
    T = proj.shape[0]
    R = min(T, ROWS)
    W = LRU_W

    def body(xr_ref, xh_ref, xg_ref, cw_ref, cb_ref, wa_ref, ba_ref, wx_ref, bx_ref, lam_ref,
             y_ref, hs_ref, ext_ref, hc_ref):
        i = pl.program_id(0)

        @pl.when(i == 0)
        def _():
            hc_ref[...] = jnp.zeros_like(hc_ref)

        ext_ref[0:SUBLANES, :] = jnp.where(i > 0, xh_ref[...], 0.0)
        ext_ref[pl.ds(SUBLANES, R), :] = xr_ref[...]
        xc = _lru_conv(ext_ref, cw_ref, cb_ref, R)
        row = lax.broadcasted_iota(jnp.int32, (R, W), 0)
        first = (row == 0) & (i == 0)
        _, ig, _, a, mult = _lru_gates(xc, wa_ref[...], ba_ref[...], wx_ref[...], bx_ref[...], lam_ref[...], first)
        u = mult * (ig * xc)
        k = 1
        while k < R:
            u = u + a * _shift_down(u, k, 0.0)
            a = a * _shift_down(a, k, 1.0)
            k *= 2
        hs_ref[...] = u + a * hc_ref[0:1, :]
        hc_ref[0:1, :] = hs_ref[R - 1:R, :]
        y_ref[...] = (hs_ref[...] * _gelu(xg_ref[...])).astype(y_ref.dtype)

    rb = R // SUBLANES
    full = lambda s: pl.BlockSpec(s, lambda i: (0,) * len(s))
    return pl.pallas_call(
        body, name=name, grid=(T // R,),
        in_specs=[pl.BlockSpec((R, W), lambda i: (i, 0)),
                  pl.BlockSpec((SUBLANES, W), lambda i: (jnp.maximum(i * rb - 1, 0), 0)),
                  pl.BlockSpec((R, W), lambda i: (i, 1)),
                  full((SUBLANES, W)), full((1, W)), full((W, W)), full((1, W)), full((W, W)), full((1, W)),
                  full((1, W))],
        out_specs=[pl.BlockSpec((R, W), lambda i: (i, 0)), pl.BlockSpec((R, W), lambda i: (i, 0))],
        out_shape=[jax.ShapeDtypeStruct((T, W), BF16), jax.ShapeDtypeStruct((T, W), F32)],
        scratch_shapes=[pltpu.VMEM((R + SUBLANES, W), F32), pltpu.VMEM((SUBLANES, W), F32)],
        compiler_params=_cp("arbitrary"),
    )(proj, proj, proj, cw, cb, wa, ba, wx, bx, lam)


def _lru_bwd(dy, proj, hs, cw, cb, wa, ba, wx, bx, lam, name):
    T = proj.shape[0]
    R = min(T, ROWS)
    W = LRU_W
    nb = T // R

    def body(dy_ref, xr_ref, xh_ref, xg_ref, hs_ref, hh_ref, cw_ref, cb_ref, wa_ref, ba_ref, wx_ref, bx_ref, lam_ref,
             dp_ref, dcw_ref, dvec_ref, dwa_ref, dwx_ref, ext_ref, ext2_ref, lc_ref):
        ib = pl.program_id(0)
        i = nb - 1 - ib

        @pl.when(ib == 0)
        def _():
            lc_ref[...] = jnp.zeros_like(lc_ref)
            ext2_ref[pl.ds(R, SUBLANES), :] = jnp.zeros((SUBLANES, W), F32)
            dcw_ref[...] = jnp.zeros_like(dcw_ref)
            dvec_ref[...] = jnp.zeros_like(dvec_ref)
            dwa_ref[...] = jnp.zeros_like(dwa_ref)
            dwx_ref[...] = jnp.zeros_like(dwx_ref)

        ext_ref[0:SUBLANES, :] = jnp.where(i > 0, xh_ref[...], 0.0)
        ext_ref[pl.ds(SUBLANES, R), :] = xr_ref[...]
        xc = _lru_conv(ext_ref, cw_ref, cb_ref, R)
        row = lax.broadcasted_iota(jnp.int32, (R, W), 0)
        first = (row == 0) & (i == 0)
        lam = lam_ref[...]
        r, ig, sp, a, mult = _lru_gates(xc, wa_ref[...], ba_ref[...], wx_ref[...], bx_ref[...], lam, first)
        h = hs_ref[...]
        gel, dgel = _gelu_and_grad(xg_ref[...])
        dy = dy_ref[...].astype(F32)
        dp_ref[:, W:2 * W] = (dy * h * dgel).astype(dp_ref.dtype)
        v = dy * gel + jnp.where(row == R - 1, lc_ref[0:1, :], 0.0)
        p = _shift_up(a, 1, 1.0)
        k = 1
        while k < R:
            v = v + p * _shift_up(v, k, 0.0)
            p = p * _shift_up(p, k, 1.0)
            k *= 2
        lc_ref[...] = (a * v)[0:SUBLANES, :]
        hprev = _shift_down(h, 1, 0.0) + jnp.where((row == 0) & (i > 0), hh_ref[SUBLANES - 1:SUBLANES, :], 0.0)
        da = v * hprev
        dmult = jnp.where(first, 0.0, v * (ig * xc))
        dig = v * (mult * xc)
        dxc = v * (mult * ig)
        dla = da * a - dmult * ((a * a) / mult)
        dra = (dla * (-LRU_C * sp)) * (r * (1.0 - r))
        drx = dig * (ig * (1.0 - ig))
        dxc = dxc + _dot_nt(dra, wa_ref[...]) + _dot_nt(drx, wx_ref[...])
        dwa_ref[...] += _dot_tn(xc, dra)
        dwx_ref[...] += _dot_tn(xc, drx)
        dvec_ref[0:1, :] += jnp.sum(dxc, axis=0, keepdims=True)
        dvec_ref[1:2, :] += jnp.sum(dra, axis=0, keepdims=True)
        dvec_ref[2:3, :] += jnp.sum(drx, axis=0, keepdims=True)
        dvec_ref[3:4, :] += jnp.sum(dla * (-LRU_C * r), axis=0, keepdims=True) * (-jax.nn.sigmoid(-lam))
        ext2_ref[pl.ds(0, R), :] = dxc
        dxr = ext2_ref[pl.ds(0, R), :] * cw_ref[3:4, :]
        dxr = dxr + ext2_ref[pl.ds(1, R), :] * cw_ref[2:3, :]
        dxr = dxr + ext2_ref[pl.ds(2, R), :] * cw_ref[1:2, :]
        dxr = dxr + ext2_ref[pl.ds(3, R), :] * cw_ref[0:1, :]
        dp_ref[:, 0:W] = dxr.astype(dp_ref.dtype)
        for j in range(LRU_CONV):
            dcw_ref[j:j + 1, :] += jnp.sum(dxc * ext_ref[pl.ds(SUBLANES - 3 + j, R), :], axis=0, keepdims=True)
        ext2_ref[pl.ds(R, SUBLANES), :] = dxc[0:SUBLANES, :]

    rb = R // SUBLANES
    full = lambda s: pl.BlockSpec(s, lambda i: (0,) * len(s))
    blk = lambda j: pl.BlockSpec((R, W), lambda i: (nb - 1 - i, j))
    halo = pl.BlockSpec((SUBLANES, W), lambda i: (jnp.maximum((nb - 1 - i) * rb - 1, 0), 0))
    return pl.pallas_call(
        body, name=name, grid=(nb,),
        in_specs=[blk(1), blk(0), halo, blk(1), blk(0), halo,
                  full((SUBLANES, W)), full((1, W)), full((W, W)), full((1, W)), full((W, W)), full((1, W)),
                  full((1, W))],
        out_specs=[pl.BlockSpec((R, 2 * W), lambda i: (nb - 1 - i, 0)), full((SUBLANES, W)), full((SUBLANES, W)),
                   full((W, W)), full((W, W))],
        out_shape=[jax.ShapeDtypeStruct((T, 2 * W), BF16), jax.ShapeDtypeStruct((SUBLANES, W), F32),
                   jax.ShapeDtypeStruct((SUBLANES, W), F32), jax.ShapeDtypeStruct((W, W), F32),
                   jax.ShapeDtypeStruct((W, W), F32)],
        scratch_shapes=[pltpu.VMEM((R + SUBLANES, W), F32), pltpu.VMEM((R + SUBLANES, W), F32),
                        pltpu.VMEM((SUBLANES, W), F32)],
        compiler_params=_cp("arbitrary"),
    )(dy, proj, proj, proj, hs, hs, cw, cb, wa, ba, wx, bx, lam)


def _ffn_conv(ext_ref, cw_ref, cb_ref, n):
    z = cb_ref[...] + ext_ref[pl.ds(SUBLANES - 2, n), :] * cw_ref[0:1, :]
    z = z + ext_ref[pl.ds(SUBLANES - 1, n), :] * cw_ref[1:2, :]
    return z + ext_ref[pl.ds(SUBLANES, n), :] * cw_ref[2:3, :]


def _ffn_act_fwd(za, zg, cwa, cwg, cba, cbg, name):
    T, Fh = za.shape
    R = min(T, ROWS)
    CB = FFN_CB
    rb = R // SUBLANES

    def body(a_ref, ah_ref, g_ref, gh_ref, cwa_ref, cwg_ref, cba_ref, cbg_ref, o_ref, ea_ref, eg_ref):
        i = pl.program_id(1)
        ea_ref[0:SUBLANES, :] = jnp.where(i > 0, ah_ref[...], 0.0)
        ea_ref[pl.ds(SUBLANES, R), :] = a_ref[...]
        eg_ref[0:SUBLANES, :] = jnp.where(i > 0, gh_ref[...], 0.0)
        eg_ref[pl.ds(SUBLANES, R), :] = g_ref[...]
        o_ref[...] = (_gelu(_ffn_conv(ea_ref, cwa_ref, cba_ref, R)) * _ffn_conv(eg_ref, cwg_ref, cbg_ref, R)
                      ).astype(o_ref.dtype)

    blk = pl.BlockSpec((R, CB), lambda j, i: (i, j))
    halo = pl.BlockSpec((SUBLANES, CB), lambda j, i: (jnp.maximum(i * rb - 1, 0), j))
    w8 = pl.BlockSpec((SUBLANES, CB), lambda j, i: (0, j))
    w1 = pl.BlockSpec((1, CB), lambda j, i: (0, j))
    return pl.pallas_call(
        body, name=name, grid=(Fh // CB, T // R),
        in_specs=[blk, halo, blk, halo, w8, w8, w1, w1], out_specs=blk,
        out_shape=jax.ShapeDtypeStruct((T, Fh), BF16),
        scratch_shapes=[pltpu.VMEM((R + SUBLANES, CB), F32), pltpu.VMEM((R + SUBLANES, CB), F32)],
        compiler_params=_cp("parallel", "parallel"),
    )(za, za, zg, zg, cwa, cwg, cba, cbg)


def _ffn_act_bwd(dact, za, zg, cwa, cwg, cba, cbg, name):
    T, Fh = za.shape
    R = min(T, ROWS)
    CB = FFN_CB
    rb = R // SUBLANES
    nb = T // R
    RE = R + SUBLANES

    def body(d_ref, dn_ref, a_ref, ap_ref, an_ref, g_ref, gp_ref, gn_ref, cwa_ref, cwg_ref, cba_ref, cbg_ref,
             dza_ref, dzg_ref, dca_ref, dcg_ref, ea_ref, eg_ref, ed_ref, sa_ref, sg_ref):
        i = pl.program_id(1)

        @pl.when(i == 0)
        def _():
            dca_ref[...] = jnp.zeros_like(dca_ref)
            dcg_ref[...] = jnp.zeros_like(dcg_ref)

        for e_ref, m_ref, p_ref, n_ref in ((ea_ref, a_ref, ap_ref, an_ref), (eg_ref, g_ref, gp_ref, gn_ref)):
            e_ref[0:SUBLANES, :] = jnp.where(i > 0, p_ref[...], 0.0)
            e_ref[pl.ds(SUBLANES, R), :] = m_ref[...]
            e_ref[pl.ds(SUBLANES + R, SUBLANES), :] = n_ref[...]
        ed_ref[pl.ds(0, R), :] = d_ref[...].astype(F32)
        ed_ref[pl.ds(R, SUBLANES), :] = jnp.where(i < nb - 1, dn_ref[...].astype(F32), 0.0)
        za_c = _ffn_conv(ea_ref, cwa_ref, cba_ref, RE)
        zg_c = _ffn_conv(eg_ref, cwg_ref, cbg_ref, RE)
        gel, dgel = _gelu_and_grad(za_c)
        dact_e = ed_ref[...]
        sa_ref[...] = dact_e * zg_c * dgel
        sg_ref[...] = dact_e * gel
        for s_ref, e_ref, cw_ref, dz_ref, dc_ref in ((sa_ref, ea_ref, cwa_ref, dza_ref, dca_ref),
                                                     (sg_ref, eg_ref, cwg_ref, dzg_ref, dcg_ref)):
            dz = s_ref[pl.ds(0, R), :]
            dzp = dz * cw_ref[2:3, :] + s_ref[pl.ds(1, R), :] * cw_ref[1:2, :] + s_ref[pl.ds(2, R), :] * cw_ref[0:1, :]
            dz_ref[...] = dzp.astype(dz_ref.dtype)
            for j in range(FFN_CONV):
                dc_ref[j:j + 1, :] += jnp.sum(dz * e_ref[pl.ds(SUBLANES - 2 + j, R), :], axis=0, keepdims=True)
            dc_ref[3:4, :] += jnp.sum(dz, axis=0, keepdims=True)

    blk = pl.BlockSpec((R, CB), lambda j, i: (i, j))
    prev = pl.BlockSpec((SUBLANES, CB), lambda j, i: (jnp.maximum(i * rb - 1, 0), j))
    nxt = pl.BlockSpec((SUBLANES, CB), lambda j, i: (jnp.minimum((i + 1) * rb, T // SUBLANES - 1), j))
    w8 = pl.BlockSpec((SUBLANES, CB), lambda j, i: (0, j))
    w1 = pl.BlockSpec((1, CB), lambda j, i: (0, j))
    return pl.pallas_call(
        body, name=name, grid=(Fh // CB, nb),
        in_specs=[blk, nxt, blk, prev, nxt, blk, prev, nxt, w8, w8, w1, w1],
        out_specs=[blk, blk, w8, w8],
        out_shape=[jax.ShapeDtypeStruct((T, Fh), BF16), jax.ShapeDtypeStruct((T, Fh), BF16),
                   jax.ShapeDtypeStruct((SUBLANES, Fh), F32), jax.ShapeDtypeStruct((SUBLANES, Fh), F32)],
        scratch_shapes=[pltpu.VMEM((RE + SUBLANES, CB), F32), pltpu.VMEM((RE + SUBLANES, CB), F32),
                        pltpu.VMEM((RE, CB), F32), pltpu.VMEM((RE, CB), F32), pltpu.VMEM((RE, CB), F32)],
        compiler_params=_cp("parallel", "arbitrary"),
    )(dact, dact, za, za, za, zg, zg, zg, cwa, cwg, cba, cbg)


def _adamw(w, grads, m, v, name):
    rows, cols = w.shape
    tr = _row_tile(rows, max(SUBLANES, min(512, TILE_BYTES // (4 * cols)) // SUBLANES * SUBLANES))
    ng = len(grads)

    def body(*refs):
        w_ref, g_refs, m_ref, v_ref = refs[0], refs[1:1 + ng], refs[1 + ng], refs[2 + ng]
        go_ref, d_ref, mo_ref, vo_ref = refs[3 + ng:]
        g = g_refs[0][...]
        for r in g_refs[1:]:
            g = g + r[...]
        mm = ADAM_B1 * m_ref[...] + (1.0 - ADAM_B1) * g
        vv = ADAM_B2 * v_ref[...] + (1.0 - ADAM_B2) * (g * g)
        m_hat = mm / (1.0 - ADAM_B1 ** ADAM_STEP)
        v_hat = vv / (1.0 - ADAM_B2 ** ADAM_STEP)
        go_ref[...] = g
        d_ref[...] = -ADAM_LR * (m_hat / (jnp.sqrt(v_hat) + ADAM_EPS) + ADAM_WD * w_ref[...])
        mo_ref[...] = mm
        vo_ref[...] = vv

    blk = pl.BlockSpec((tr, cols), lambda i: (i, 0))
    return pl.pallas_call(
        body, name=name, grid=(rows // tr,), in_specs=[blk] * (3 + ng), out_specs=[blk] * 4,
        out_shape=[jax.ShapeDtypeStruct((rows, cols), F32)] * 4, compiler_params=_cp("parallel"),
    )(w, *grads, m, v)


def _sum_leading(parts, name):
    n, rows, cols = parts.shape
    tr = _row_tile(rows, max(SUBLANES, min(512, TILE_BYTES // (4 * cols)) // SUBLANES * SUBLANES))

    def body(p_ref, o_ref):
        acc = p_ref[0]
        for d in range(1, n):
            acc = acc + p_ref[d]
        o_ref[...] = acc

    return pl.pallas_call(
        body, name=name, grid=(rows // tr,),
        in_specs=[pl.BlockSpec((n, tr, cols), lambda i: (0, i, 0))], out_specs=pl.BlockSpec((tr, cols), lambda i: (i, 0)),
        out_shape=jax.ShapeDtypeStruct((rows, cols), F32), compiler_params=_cp("parallel"),
    )(parts)


def _row_tile(rows, cap=512):
    if rows <= cap:
        return rows
    return max(t for t in range(SUBLANES, cap + 1, SUBLANES) if rows % t == 0)


def _place():
    return lax.axis_index("x"), lax.axis_index("y"), lax.axis_index("c")


def _gather_shards(arrs, name):
    n = len(arrs)

    def body(*refs):
        ins, outs = refs[:n], refs[n:2 * n]
        send_sems, recv_sems, local_sems = refs[2 * n:]
        x, y, c = _place()
        chips = [(1 - x, y), (x, 1 - y), (1 - x, 1 - y)]
        local = [pltpu.make_async_copy(ins[a], outs[a].at[2 * x + y], local_sems.at[a]) for a in range(n)]
        for cp in local:
            cp.start()

        def copy(a, j, shard):
            px, py = chips[j]
            return pltpu.make_async_remote_copy(
                src_ref=ins[a], dst_ref=outs[a].at[shard], send_sem=send_sems.at[3 * a + j],
                recv_sem=recv_sems.at[3 * a + j], device_id=(px, py, c), device_id_type=MESH)

        sends = [copy(a, j, 2 * x + y) for a in range(n) for j in range(3)]
        for cp in sends:
            cp.start()
        for a in range(n):
            for j, (px, py) in enumerate(chips):
                copy(a, j, 2 * px + py).wait_recv()
        for cp in sends:
            cp.wait_send()
        for cp in local:
            cp.wait()

    hbm = pl.BlockSpec(memory_space=pl.ANY)
    return pl.pallas_call(
        body, name=name, in_specs=[hbm] * n, out_specs=[hbm] * n,
        out_shape=[jax.ShapeDtypeStruct((N_CHIPS,) + a.shape, a.dtype) for a in arrs],
        scratch_shapes=[pltpu.SemaphoreType.DMA((3 * n,)), pltpu.SemaphoreType.DMA((3 * n,)),
                        pltpu.SemaphoreType.DMA((n,))],
        compiler_params=pltpu.CompilerParams(has_side_effects=True),
    )(*arrs)


def _exchange_grads(slabs, small, name):
    n = len(slabs)

    def body(*refs):
        ins, small_ref = refs[:n], refs[n]
        outs, small_out = refs[n + 1:2 * n + 1], refs[2 * n + 1]
        send_sems, recv_sems, ssend, srecv, local_sems = refs[2 * n + 2:]
        x, y, c = _place()
        chips = [(1 - x, y), (x, 1 - y), (1 - x, 1 - y)]
        me = 4 * x + 2 * y + c
        flips = [(fx, fy, fc) for fx in (0, 1) for fy in (0, 1) for fc in (0, 1)][1:]
        local = [pltpu.make_async_copy(small_ref, small_out.at[me], local_sems.at[n])]
        local += [pltpu.make_async_copy(ins[a].at[2 * x + y], outs[a].at[3], local_sems.at[a]) for a in range(n)]
        for cp in local:
            cp.start()

        def copy(a, j):
            px, py = chips[j]
            return pltpu.make_async_remote_copy(
                src_ref=ins[a].at[2 * px + py], dst_ref=outs[a].at[j], send_sem=send_sems.at[3 * a + j],
                recv_sem=recv_sems.at[3 * a + j], device_id=(px, py, c), device_id_type=MESH)

        def scopy(k, row):
            fx, fy, fc = flips[k]
            return pltpu.make_async_remote_copy(
                src_ref=small_ref, dst_ref=small_out.at[row], send_sem=ssend.at[k], recv_sem=srecv.at[k],
                device_id=(x ^ fx, y ^ fy, c ^ fc), device_id_type=MESH)

        sends = [copy(a, j) for a in range(n) for j in range(3)] + [scopy(k, me) for k in range(7)]
        for cp in sends:
            cp.start()
        for k, (fx, fy, fc) in enumerate(flips):
            scopy(k, 4 * (x ^ fx) + 2 * (y ^ fy) + (c ^ fc)).wait_recv()
        for a in range(n):
            for j in range(3):
                copy(a, j).wait_recv()
        for cp in sends:
            cp.wait_send()
        for cp in local:
            cp.wait()

    hbm = pl.BlockSpec(memory_space=pl.ANY)
    return pl.pallas_call(
        body, name=name, in_specs=[hbm] * (n + 1), out_specs=[hbm] * (n + 1),
        out_shape=[jax.ShapeDtypeStruct(s.shape, s.dtype) for s in slabs]
        + [jax.ShapeDtypeStruct((N_DEV,) + small.shape, small.dtype)],
        scratch_shapes=[pltpu.SemaphoreType.DMA((3 * n,)), pltpu.SemaphoreType.DMA((3 * n,)),
                        pltpu.SemaphoreType.DMA((7,)), pltpu.SemaphoreType.DMA((7,)),
                        pltpu.SemaphoreType.DMA((n + 1,))],
        compiler_params=pltpu.CompilerParams(has_side_effects=True),
    )(*slabs, small)


def _swap_with_sibling(arrs, name):
    n = len(arrs)

    def body(*refs):
        ins, outs = refs[:n], refs[n:2 * n]
        send_sems, recv_sems = refs[2 * n:]
        x, y, c = _place()
        copies = [pltpu.make_async_remote_copy(
            src_ref=ins[a], dst_ref=outs[a], send_sem=send_sems.at[a], recv_sem=recv_sems.at[a],
            device_id=(x, y, 1 - c), device_id_type=MESH) for a in range(n)]
        for cp in copies:
            cp.start()
        for cp in copies:
            cp.wait()

    hbm = pl.BlockSpec(memory_space=pl.ANY)
    return pl.pallas_call(
        body, name=name, in_specs=[hbm] * n, out_specs=[hbm] * n,
        out_shape=[jax.ShapeDtypeStruct(a.shape, a.dtype) for a in arrs],
        scratch_shapes=[pltpu.SemaphoreType.DMA((n,)), pltpu.SemaphoreType.DMA((n,))],
        compiler_params=pltpu.CompilerParams(has_side_effects=True),
    )(*arrs)


def _block_diag(w):
    eye = jnp.eye(LRU_BLOCKS, dtype=w.dtype)
    return (eye[:, None, :, None] * w[:, :, None, :]).reshape(LRU_W, LRU_W)


def _diag_blocks(m):
    m4 = m.reshape(LRU_BLOCKS, LRU_BLOCK, LRU_BLOCKS, LRU_BLOCK)
    return jnp.stack([m4[b, :, b, :] for b in range(LRU_BLOCKS)])


def _pad_rows(a, rows):
    return jnp.pad(a, ((0, rows - a.shape[0]), (0, 0)))


def _layer_weights(p, l):
    w_in = p["w_in"][l]
    n_gla = 2 * QK_W + 2 * GLA_W
    gate = jnp.pad(w_in[:, n_gla:n_gla + GATE_RANK], ((0, 0), (0, GATE_PAD - GATE_RANK)))
    wg = jnp.concatenate([w_in[:, :n_gla], gate], axis=1)
    wl = w_in[:, n_gla + GATE_RANK:]
    w_out = p["w_out"][l]
    fa, fg = p["ffn_w_in"][l][:, :FFN_H], p["ffn_w_in"][l][:, FFN_H:]
    wd = p["ffn_w_down"][l]
    return dict(
        wg=wg, wl=wl, wgT=wg.T, wlT=wl.T, wo_g=w_out[:GLA_W], wo_l=w_out[GLA_W:], woT=w_out.T,
        fa=fa, fg=fg, faT=fa.T, fgT=fg.T, wd=wd, wdT=wd.T,
        w2p=_pad_rows(p["gla_gate_w2"][l], GATE_PAD).astype(BF16),
        wa=_block_diag(p["lru_wa"][l]).astype(BF16), wx=_block_diag(p["lru_wx"][l]).astype(BF16),
        lcw=_pad_rows(p["lru_conv_w"][l], SUBLANES),
        fcwa=_pad_rows(p["ffn_conv_w"][l][:, :FFN_H], SUBLANES), fcwg=_pad_rows(p["ffn_conv_w"][l][:, FFN_H:], SUBLANES),
    )


def _local_step(x, tgt, p):
    row = lambda v: v.reshape(1, -1)
    h = x
    stash = []
    for l in range(DEPTH):
        w = _layer_weights(p, l)
        s = dict(w=w, h0=h)
        u = _rms_fwd(h, p["ln_mix"][l], f"mix_norm_fwd{l}")
        pg = _mm(u, w["wg"], None, F32, f"proj_gla_fwd{l}")
        plr = _mm(u, w["wl"], None, F32, f"proj_lru_fwd{l}")
        yg, o_st, s_st = _gla_fwd(pg, w["w2p"], p["gla_gate_b"][l], p["gla_norm"][l], f"gla_fwd{l}")
        yl, hs = _lru_fwd(plr, w["lcw"], row(p["lru_conv_b"][l]), w["wa"], row(p["lru_ba"][l]), w["wx"],
                          row(p["lru_bx"][l]), row(p["lru_lambda"][l]), f"lru_fwd{l}")
        h = _mm(yg, w["wo_g"], h, F32, f"out_gla_fwd{l}")
        h = _mm(yl, w["wo_l"], h, F32, f"out_lru_fwd{l}")
        s.update(u=u, pg=pg, plr=plr, yg=yg, yl=yl, o_st=o_st, s_st=s_st, hs=hs, h1=h)
        u2 = _rms_fwd(h, p["ln_ffn"][l], f"ffn_norm_fwd{l}")
        za = _mm(u2, w["fa"], None, F32, f"ffn_in_a_fwd{l}", tn=FFN_H // 2)
        zg = _mm(u2, w["fg"], None, F32, f"ffn_in_g_fwd{l}", tn=FFN_H // 2)
        cba, cbg = row(p["ffn_conv_b"][l][:FFN_H]), row(p["ffn_conv_b"][l][FFN_H:])
        act = _ffn_act_fwd(za, zg, w["fcwa"], w["fcwg"], cba, cbg, f"ffn_act_fwd{l}")
        h = _mm(act, w["wd"], h, F32, f"ffn_down_fwd{l}")
        s.update(u2=u2, za=za, zg=zg, act=act, cba=cba, cbg=cbg)
        stash.append(s)

    loss, dh, d_ln_final = _loss_head(h, p["ln_final"], tgt, "loss_head")

    g = {k: [None] * DEPTH for k in ("ln_mix", "w_in", "gla_gate_w2", "gla_gate_b", "gla_norm", "lru_conv_w",
                                     "lru_conv_b", "lru_wa", "lru_ba", "lru_wx", "lru_bx", "lru_lambda", "w_out",
                                     "ln_ffn", "ffn_w_in", "ffn_conv_w", "ffn_conv_b", "ffn_w_down")}
    n_gla = 2 * QK_W + 2 * GLA_W
    for l in reversed(range(DEPTH)):
        s = stash[l]
        w = s["w"]
        dhb = dh.astype(BF16)
        g["ffn_w_down"][l] = _mm_tn(s["act"], dhb, f"ffn_down_dw{l}", tk=FFN_H // 3)
        dact = _mm(dhb, w["wdT"], None, F32, f"ffn_down_dx{l}", tn=FFN_H // 2)
        dza, dzg, dca, dcg = _ffn_act_bwd(dact, s["za"], s["zg"], w["fcwa"], w["fcwg"], s["cba"], s["cbg"],
                                          f"ffn_act_bwd{l}")
        g["ffn_conv_w"][l] = jnp.concatenate([dca[:FFN_CONV], dcg[:FFN_CONV]], axis=1)
        g["ffn_conv_b"][l] = jnp.concatenate([dca[3], dcg[3]])
        g["ffn_w_in"][l] = jnp.concatenate([_mm_tn(s["u2"], dza, f"ffn_in_a_dw{l}", tn=FFN_H // 3),
                                            _mm_tn(s["u2"], dzg, f"ffn_in_g_dw{l}", tn=FFN_H // 3)], axis=1)
        du2 = _mm(dza, w["faT"], None, F32, f"ffn_in_a_dx{l}", tm=256)
        du2 = _mm(dzg, w["fgT"], du2, F32, f"ffn_in_g_dx{l}", tm=256)
        dh, dln = _rms_bwd(s["h1"], p["ln_ffn"][l], du2, dh, f"ffn_norm_bwd{l}")
        g["ln_ffn"][l] = dln[0]
        dhb = dh.astype(BF16)
        g["w_out"][l] = jnp.concatenate([_mm_tn(s["yg"], dhb, f"out_gla_dw{l}"),
                                         _mm_tn(s["yl"], dhb, f"out_lru_dw{l}")], axis=0)
        dyc = _mm(dhb, w["woT"], None, F32, f"out_dx{l}")
        dpg, dw2, db2, dng = _gla_bwd(dyc, s["pg"], s["o_st"], s["s_st"], w["w2p"], p["gla_gate_b"][l],
                                      p["gla_norm"][l], f"gla_bwd{l}")
        dpl, dcw, dvec, dwa, dwx = _lru_bwd(dyc, s["plr"], s["hs"], w["lcw"], row(p["lru_conv_b"][l]), w["wa"],
                                            row(p["lru_ba"][l]), w["wx"], row(p["lru_bx"][l]),
                                            row(p["lru_lambda"][l]), f"lru_bwd{l}")
        g["gla_gate_w2"][l] = dw2[:GATE_RANK]
        g["gla_gate_b"][l] = db2[0]
        g["gla_norm"][l] = dng[0]
        g["lru_conv_w"][l] = dcw[:LRU_CONV]
        g["lru_conv_b"][l], g["lru_ba"][l], g["lru_bx"][l], g["lru_lambda"][l] = dvec[0], dvec[1], dvec[2], dvec[3]
        g["lru_wa"][l], g["lru_wx"][l] = _diag_blocks(dwa), _diag_blocks(dwx)
        dwg = _mm_tn(s["u"], dpg, f"proj_gla_dw{l}")
        dwl = _mm_tn(s["u"], dpl, f"proj_lru_dw{l}")
        g["w_in"][l] = jnp.concatenate([dwg[:, :n_gla + GATE_RANK], dwl], axis=1)
        du = _mm(dpg, w["wgT"], None, F32, f"proj_gla_dx{l}")
        du = _mm(dpl, w["wlT"], du, F32, f"proj_lru_dx{l}")
        dh, dln = _rms_bwd(s["h0"], p["ln_mix"][l], du, dh, f"mix_norm_bwd{l}")
        g["ln_mix"][l] = dln[0]
    grads = {k: jnp.stack(v) for k, v in g.items()}
    grads["ln_final"] = d_ln_final[0]
    return loss, dh, grads


BIG = ("w_in", "w_out", "ffn_w_in", "ffn_w_down")
COL_SHARDED = ("w_in", "ffn_w_in", "gla_gate_w2", "lru_conv_w", "ffn_conv_w")
SMALL = ("ln_mix", "gla_gate_w2", "gla_gate_b", "gla_norm", "lru_conv_w", "lru_conv_b", "lru_wa", "lru_ba", "lru_wx",
         "lru_bx", "lru_lambda", "ln_ffn", "ffn_conv_w", "ffn_conv_b", "ln_final")
WEIGHTS = ("ln_mix", "w_in", "gla_gate_w2", "gla_gate_b", "gla_norm", "lru_conv_w", "lru_conv_b", "lru_wa", "lru_ba",
           "lru_wx", "lru_bx", "lru_lambda", "w_out", "ln_ffn", "ffn_w_in", "ffn_conv_w", "ffn_conv_b", "ffn_w_down",
           "ln_final")
PACK = SUBLANES * LANES


def _whole_from_shards(name, g):
    if name in COL_SHARDED:
        return jnp.moveaxis(g, 0, -2).reshape(g.shape[1:-1] + (N_CHIPS * g.shape[-1],))
    return jnp.moveaxis(g, 0, 1).reshape((g.shape[1], N_CHIPS * g.shape[2]) + g.shape[3:])


def _slabs_from_whole(name, w):
    L, r, c = w.shape
    if name in COL_SHARDED:
        s = jnp.moveaxis(w.reshape(L, r, N_CHIPS, c // N_CHIPS), 2, 0)
    else:
        s = jnp.moveaxis(w.reshape(L, N_CHIPS, r // N_CHIPS, c), 1, 0)
    return s.reshape(N_CHIPS, -1, s.shape[-1])


def _pack(arrs):
    flat = []
    for a in arrs:
        f = a.reshape(-1)
        flat.append(jnp.pad(f, (0, (-f.shape[0]) % PACK)))
    return jnp.concatenate(flat).reshape(-1, LANES)


def _unpack(packed, shapes):
    out, at = [], 0
    flat = packed.reshape(-1)
    for s in shapes:
        size = math.prod(s)
        out.append(flat[at:at + size].reshape(s))
        at += size + (-size) % PACK
    return out


def kernel(x, ln_mix, w_in, gla_gate_w2, gla_gate_b, gla_norm, lru_conv_w, lru_conv_b, lru_wa, lru_ba, lru_wx, lru_bx, lru_lambda, w_out, ln_ffn, ffn_w_in, ffn_conv_w, ffn_conv_b, ffn_w_down, ln_final, loss_target, m_ln_mix, m_w_in, m_gla_gate_w2, m_gla_gate_b, m_gla_norm, m_lru_conv_w, m_lru_conv_b, m_lru_wa, m_lru_ba, m_lru_wx, m_lru_bx, m_lru_lambda, m_w_out, m_ln_ffn, m_ffn_w_in, m_ffn_conv_w, m_ffn_conv_b, m_ffn_w_down, m_ln_final, v_ln_mix, v_w_in, v_gla_gate_w2, v_gla_gate_b, v_gla_norm, v_lru_conv_w, v_lru_conv_b, v_lru_wa, v_lru_ba, v_lru_wx, v_lru_bx, v_lru_lambda, v_w_out, v_ln_ffn, v_ffn_w_in, v_ffn_conv_w, v_ffn_conv_b, v_ffn_w_down, v_ln_final):
    w = dict(ln_mix=ln_mix, w_in=w_in, gla_gate_w2=gla_gate_w2, gla_gate_b=gla_gate_b, gla_norm=gla_norm,
             lru_conv_w=lru_conv_w, lru_conv_b=lru_conv_b, lru_wa=lru_wa, lru_ba=lru_ba, lru_wx=lru_wx, lru_bx=lru_bx,
             lru_lambda=lru_lambda, w_out=w_out, ln_ffn=ln_ffn, ffn_w_in=ffn_w_in, ffn_conv_w=ffn_conv_w,
             ffn_conv_b=ffn_conv_b, ffn_w_down=ffn_w_down, ln_final=ln_final)
    m = dict(ln_mix=m_ln_mix, w_in=m_w_in, gla_gate_w2=m_gla_gate_w2, gla_gate_b=m_gla_gate_b, gla_norm=m_gla_norm,
             lru_conv_w=m_lru_conv_w, lru_conv_b=m_lru_conv_b, lru_wa=m_lru_wa, lru_ba=m_lru_ba, lru_wx=m_lru_wx,
             lru_bx=m_lru_bx, lru_lambda=m_lru_lambda, w_out=m_w_out, ln_ffn=m_ln_ffn, ffn_w_in=m_ffn_w_in,
             ffn_conv_w=m_ffn_conv_w, ffn_conv_b=m_ffn_conv_b, ffn_w_down=m_ffn_w_down, ln_final=m_ln_final)
    v = dict(ln_mix=v_ln_mix, w_in=v_w_in, gla_gate_w2=v_gla_gate_w2, gla_gate_b=v_gla_gate_b, gla_norm=v_gla_norm,
             lru_conv_w=v_lru_conv_w, lru_conv_b=v_lru_conv_b, lru_wa=v_lru_wa, lru_ba=v_lru_ba, lru_wx=v_lru_wx,
             lru_bx=v_lru_bx, lru_lambda=v_lru_lambda, w_out=v_w_out, ln_ffn=v_ln_ffn, ffn_w_in=v_ffn_w_in,
             ffn_conv_w=v_ffn_conv_w, ffn_conv_b=v_ffn_conv_b, ffn_w_down=v_ffn_w_down, ln_final=v_ln_final)

    sharded = BIG + ("gla_gate_w2", "lru_conv_w", "ffn_conv_w")
    gathered = _gather_shards([w[k].astype(MXU_DTYPE) if k in BIG else w[k] for k in sharded], "gather_weights")
    p = dict(w)
    for k, gk in zip(sharded, gathered):
        p[k] = _whole_from_shards(k, gk)

    loss, grad_x, grads = _local_step(x[0], loss_target[0], p)
    loss = lax.psum(loss[0, 0], ("x", "y", "c"))

    slabs = [_slabs_from_whole(k, grads[k]) for k in BIG]
    small = _pack([grads[k] for k in SMALL])
    *recv, small_all = _exchange_grads(slabs, small, "exchange_grads")
    partial = [_sum_leading(r, f"chip_sum_{k}") for k, r in zip(BIG, recv)]
    other = _swap_with_sibling(partial, "swap_core_sums")
    small_sum = _unpack(_sum_leading(small_all, "sum_small_grads"), [grads[k].shape for k in SMALL])

    me = 2 * lax.axis_index("x") + lax.axis_index("y")
    out_g, out_d, out_m, out_v = {}, {}, {}, {}
    for k, mine, theirs in zip(BIG, partial, other):
        shape = w[k].shape
        cols = shape[-1]
        res = _adamw(w[k].reshape(-1, cols), [mine, theirs], m[k].reshape(-1, cols), v[k].reshape(-1, cols), f"adamw_{k}")
        out_g[k], out_d[k], out_m[k], out_v[k] = [r.reshape(shape) for r in res]
    small_g = []
    for k, gk in zip(SMALL, small_sum):
        if k in COL_SHARDED:
            width = w[k].shape[-1]
            gk = lax.dynamic_slice_in_dim(gk, me * width, width, axis=gk.ndim - 1)
        small_g.append(gk)
    shapes = [w[k].shape for k in SMALL]
    res = _adamw(_pack([w[k] for k in SMALL]), [_pack(small_g)], _pack([m[k] for k in SMALL]),
                 _pack([v[k] for k in SMALL]), "adamw_small")
    for out, packed in zip((out_g, out_d, out_m, out_v), res):
        for k, a in zip(SMALL, _unpack(packed, shapes)):
            out[k] = a
    return (loss, grad_x[None], *[out_g[k] for k in WEIGHTS], *[out_d[k] for k in WEIGHTS],
            *[out_m[k] for k in WEIGHTS], *[out_v[k] for k in WEIGHTS])
```

```python
import math

import jax
import jax.numpy as jnp
from jax import lax
from jax.experimental import pallas as pl
from jax.experimental.pallas import tpu as pltpu

F32 = jnp.float32
BF16 = jnp.bfloat16
MXU_DTYPE = BF16

D_MODEL = 1024
DEPTH = 4
HEADS, DK, DV, CHUNK, GATE_RANK = 4, 64, 128, 64, 16
QK_W = HEADS * DK
GLA_W = HEADS * DV
LRU_W = 512
LRU_BLOCKS, LRU_BLOCK, LRU_CONV, LRU_C = 8, 64, 4, 8.0
FFN_H = 3 * D_MODEL
FFN_CONV = 3
EPS = 1e-6
GATE_PAD = 128
GLA_COLS = 2 * QK_W + 2 * GLA_W + GATE_PAD
LRU_COLS = 2 * LRU_W
ADAM_LR, ADAM_B1, ADAM_B2, ADAM_EPS, ADAM_WD, ADAM_STEP = 0.001, 0.9, 0.999, 1e-08, 0.01, 10

LANES = 128
SUBLANES = 8
VMEM_LIMIT = 56 * 1024 * 1024
ROWS = 512
TILE_BYTES = 1 << 20
FFN_CB = 512
FFN_CW = 1024
FFN_SB = 256
FFN_RC = 32
N_CHIPS = 4
N_DEV = 8
MESH = pl.DeviceIdType.MESH


def _cp(*sem):
    return pltpu.CompilerParams(dimension_semantics=sem, vmem_limit_bytes=VMEM_LIMIT)


def _dot(a, b):
    return jnp.dot(a.astype(MXU_DTYPE), b.astype(MXU_DTYPE), preferred_element_type=F32)


def _dot_nt(a, b):
    return lax.dot_general(a.astype(MXU_DTYPE), b.astype(MXU_DTYPE), (((1,), (1,)), ((), ())),
                           preferred_element_type=F32)


def _dot_tn(a, b):
    return lax.dot_general(a.astype(MXU_DTYPE), b.astype(MXU_DTYPE), (((0,), (0,)), ((), ())),
                           preferred_element_type=F32)


def _bdot(eq, a, b):
    return jnp.einsum(eq, a, b, preferred_element_type=F32)


def _split3(x):
    x1 = x.astype(BF16)
    r1 = x - x1.astype(F32)
    x2 = r1.astype(BF16)
    x3 = (r1 - x2.astype(F32)).astype(BF16)
    return x1, x2, x3


def _gelu(x):
    c = math.sqrt(2.0 / math.pi)
    return x * (0.5 * (1.0 + jnp.tanh(c * (x + 0.044715 * (x * x * x)))))


def _gelu_and_grad(x):
    c = math.sqrt(2.0 / math.pi)
    t = jnp.tanh(c * (x + 0.044715 * (x * x * x)))
    cdf = 0.5 * (1.0 + t)
    dcdf = 0.5 * (1.0 - t * t) * (c * (1.0 + 3.0 * 0.044715 * (x * x)))
    return x * cdf, cdf + x * dcdf


def _expm1(x):
    small = x * (1.0 + x * (0.5 + x * (1.0 / 6.0 + x * (1.0 / 24.0 + x * (1.0 / 120.0)))))
    return jnp.where(jnp.abs(x) < 0.1, small, jnp.exp(x) - 1.0)


def _shift_down(x, k, fill):
    row = lax.broadcasted_iota(jnp.int32, x.shape, 0)
    return jnp.where(row >= k, pltpu.roll(x, k, axis=0), fill)


def _shift_up(x, k, fill):
    n = x.shape[0]
    row = lax.broadcasted_iota(jnp.int32, x.shape, 0)
    return jnp.where(row < n - k, pltpu.roll(x, n - k, axis=0), fill)


def _rms_fwd(h, g, name):
    T, D = h.shape
    R = min(T, ROWS)

    def body(h_ref, g_ref, o_ref):
        x = h_ref[...]
        r = lax.rsqrt(jnp.mean(x * x, axis=-1, keepdims=True) + EPS)
        o_ref[...] = ((x * r) * g_ref[...]).astype(o_ref.dtype)

    return pl.pallas_call(
        body, name=name, grid=(T // R,),
        in_specs=[pl.BlockSpec((R, D), lambda i: (i, 0)), pl.BlockSpec((1, D), lambda i: (0, 0))],
        out_specs=pl.BlockSpec((R, D), lambda i: (i, 0)),
        out_shape=jax.ShapeDtypeStruct((T, D), BF16), compiler_params=_cp("parallel"),
    )(h, g.reshape(1, D))


def _rms_bwd(h, g, du, dres, name):
    T, D = h.shape
    R = min(T, ROWS)

    def body(h_ref, g_ref, du_ref, dres_ref, dh_ref, dhb_ref, dg_ref):
        @pl.when(pl.program_id(0) == 0)
        def _():
            dg_ref[...] = jnp.zeros_like(dg_ref)

        x = h_ref[...]
        r = lax.rsqrt(jnp.mean(x * x, axis=-1, keepdims=True) + EPS)
        xhat = x * r
        dy = du_ref[...].astype(F32)
        dg_ref[...] += jnp.sum(dy * xhat, axis=0, keepdims=True)
        dxhat = dy * g_ref[...]
        dx = r * (dxhat - xhat * jnp.mean(dxhat * xhat, axis=-1, keepdims=True))
        dh = dres_ref[...] + dx
        dh_ref[...] = dh
        dhb_ref[...] = dh.astype(dhb_ref.dtype)

    blk = pl.BlockSpec((R, D), lambda i: (i, 0))
    vec = pl.BlockSpec((1, D), lambda i: (0, 0))
    return pl.pallas_call(
        body, name=name, grid=(T // R,), in_specs=[blk, vec, blk, blk], out_specs=[blk, blk, vec],
        out_shape=[jax.ShapeDtypeStruct((T, D), F32), jax.ShapeDtypeStruct((T, D), BF16),
                   jax.ShapeDtypeStruct((1, D), F32)],
        compiler_params=_cp("arbitrary"),
    )(h, g.reshape(1, D), du, dres)


def _loss_head(h, g, tgt, name):
    T, D = h.shape
    R = min(T, ROWS)

    def body(h_ref, g_ref, t_ref, loss_ref, dh_ref, dhb_ref, dg_ref):
        @pl.when(pl.program_id(0) == 0)
        def _():
            dg_ref[...] = jnp.zeros_like(dg_ref)
            loss_ref[...] = jnp.zeros_like(loss_ref)

        x = h_ref[...]
        r = lax.rsqrt(jnp.mean(x * x, axis=-1, keepdims=True) + EPS)
        xhat = x * r
        gg = g_ref[...]
        err = xhat * gg - t_ref[...]
        loss_ref[...] += 0.5 * jnp.sum(jnp.mean(err * err, axis=-1, keepdims=True), axis=0, keepdims=True)
        dy = err * (1.0 / D)
        dg_ref[...] += jnp.sum(dy * xhat, axis=0, keepdims=True)
        dxhat = dy * gg
        dh = r * (dxhat - xhat * jnp.mean(dxhat * xhat, axis=-1, keepdims=True))
        dh_ref[...] = dh
        dhb_ref[...] = dh.astype(dhb_ref.dtype)

    blk = pl.BlockSpec((R, D), lambda i: (i, 0))
    vec = pl.BlockSpec((1, D), lambda i: (0, 0))
    one = pl.BlockSpec((1, LANES), lambda i: (0, 0))
    return pl.pallas_call(
        body, name=name, grid=(T // R,), in_specs=[blk, vec, blk], out_specs=[one, blk, blk, vec],
        out_shape=[jax.ShapeDtypeStruct((1, LANES), F32), jax.ShapeDtypeStruct((T, D), F32),
                   jax.ShapeDtypeStruct((T, D), BF16), jax.ShapeDtypeStruct((1, D), F32)],
        compiler_params=_cp("arbitrary"),
    )(h, g.reshape(1, D), tgt)


def _mm(a, b, res, out_dtype, name, tm=512, tn=None):
    M, K = a.shape
    N = b.shape[1]
    tm = min(tm, M)
    tn = N if tn is None else tn

    def body(*refs):
        if res is None:
            a_ref, b_ref, o_ref = refs
            acc = _dot(a_ref[...], b_ref[...])
        else:
            a_ref, b_ref, r_ref, o_ref = refs
            acc = r_ref[...].astype(F32) + _dot(a_ref[...], b_ref[...])
        o_ref[...] = acc.astype(o_ref.dtype)

    in_specs = [pl.BlockSpec((tm, K), lambda j, i: (i, 0)), pl.BlockSpec((K, tn), lambda j, i: (0, j))]
    args = [a, b]
    if res is not None:
        in_specs.append(pl.BlockSpec((tm, tn), lambda j, i: (i, j)))
        args.append(res)
    return pl.pallas_call(
        body, name=name, grid=(N // tn, M // tm), in_specs=in_specs,
        out_specs=pl.BlockSpec((tm, tn), lambda j, i: (i, j)),
        out_shape=jax.ShapeDtypeStruct((M, N), out_dtype), compiler_params=_cp("parallel", "parallel"),
    )(*args)


def _mm_tn(a, b, name, tm=512, tk=None, tn=None):
    M, K = a.shape
    N = b.shape[1]
    tm = min(tm, M)
    tk = K if tk is None else tk
    tn = N if tn is None else tn

    def body(a_ref, b_ref, o_ref):
        @pl.when(pl.program_id(2) == 0)
        def _():
            o_ref[...] = jnp.zeros_like(o_ref)

        o_ref[...] += _dot_tn(a_ref[...], b_ref[...])

    return pl.pallas_call(
        body, name=name, grid=(K // tk, N // tn, M // tm),
        in_specs=[pl.BlockSpec((tm, tk), lambda k, n, m: (m, k)), pl.BlockSpec((tm, tn), lambda k, n, m: (m, n))],
        out_specs=pl.BlockSpec((tk, tn), lambda k, n, m: (k, n)),
        out_shape=jax.ShapeDtypeStruct((K, N), F32), compiler_params=_cp("parallel", "parallel", "arbitrary"),
    )(a, b)


def _same_chunk(row, col):
    shift = CHUNK.bit_length() - 1
    return jnp.right_shift(row, shift) == jnp.right_shift(col, shift)


def _gla_common(q, k, glr, w2, b2, R):
    gl = _dot(glr, w2) + b2
    la = jax.nn.log_sigmoid(gl) * (1.0 / 16.0)
    row = lax.broadcasted_iota(jnp.int32, (R, R), 0)
    col = lax.broadcasted_iota(jnp.int32, (R, R), 1)
    same = _same_chunk(row, col)
    m_tri = (same & (col <= row)).astype(BF16)
    m_all = same.astype(BF16)
    la3 = _split3(la)
    b = sum(jnp.dot(m_tri, p, preferred_element_type=F32) for p in la3)
    bl = sum(jnp.dot(m_all, p, preferred_element_type=F32) for p in la3)
    eb = jnp.exp(b)
    enb = jnp.exp(-b)
    ek = jnp.exp(bl - b)
    qi = (q * (DK ** -0.5)) * eb
    ki = k * enb
    kd = k * ek
    return gl, la3, eb, enb, ek, qi, ki, kd


def _bsplit(x, n):
    return x.reshape(n, CHUNK, x.shape[-1])


def _tril():
    return (lax.broadcasted_iota(jnp.int32, (CHUNK, CHUNK), 1)
            <= lax.broadcasted_iota(jnp.int32, (CHUNK, CHUNK), 0))[None]


def _gla_fwd(proj, w2p, b2, norm_g, name):
    T = proj.shape[0]
    R = min(T, ROWS)
    n = R // CHUNK

    def body(q_ref, k_ref, v_ref, g_ref, a_ref, w2_ref, b2_ref, ng_ref, y_ref, o_ref, st_ref, s_ref):
        @pl.when(pl.program_id(0) == 0)
        def _():
            s_ref[...] = jnp.zeros_like(s_ref)

        _, la3, _, _, _, qi, ki, kd = _gla_common(q_ref[...], k_ref[...], a_ref[...], w2_ref[...], b2_ref[...], R)
        tril = _tril()
        ones = jnp.ones((n, CHUNK, DV), BF16)
        for h in range(HEADS):
            sl = slice(h * DK, (h + 1) * DK)
            sv = slice(h * DV, (h + 1) * DV)
            qh = _bsplit(qi[:, sl], n).astype(MXU_DTYPE)
            kh = _bsplit(ki[:, sl], n).astype(MXU_DTYPE)
            kdh = _bsplit(kd[:, sl], n).astype(MXU_DTYPE)
            vh = _bsplit(v_ref[:, sv], n).astype(MXU_DTYPE)
            att = jnp.where(tril, _bdot('ncd,nsd->ncs', qh, kh), 0.0)
            upd = _bdot('ncd,nce->nde', kdh, vh)
            dect = jnp.exp(sum(_bdot('ncd,nce->nde', _bsplit(p[:, sl], n), ones) for p in la3))
            s = s_ref[sl, :]
            for c in range(n):
                st_ref[c, sl, :] = s
                s = dect[c] * s + upd[c]
            s_ref[sl, :] = s
            sp = st_ref[:, sl, :].astype(MXU_DTYPE)
            o = (_bdot('ncs,nse->nce', att.astype(MXU_DTYPE), vh) + _bdot('ncd,nde->nce', qh, sp)).reshape(R, DV)
            o_ref[:, sv] = o
            r = lax.rsqrt(jnp.mean(o * o, axis=-1, keepdims=True) + EPS)
            gate = g_ref[:, sv]
            y_ref[:, sv] = (((o * r) * ng_ref[...]) * (gate * jax.nn.sigmoid(gate))).astype(y_ref.dtype)

    cb = lambda w, j: pl.BlockSpec((R, w), lambda i: (i, j))
    full = lambda s: pl.BlockSpec(s, lambda i: (0,) * len(s))
    return pl.pallas_call(
        body, name=name, grid=(T // R,),
        in_specs=[cb(QK_W, 0), cb(QK_W, 1), cb(GLA_W, 1), cb(GLA_W, 2), cb(GATE_PAD, 12),
                  full((GATE_PAD, QK_W)), full((1, QK_W)), full((1, DV))],
        out_specs=[pl.BlockSpec((R, GLA_W), lambda i: (i, 0)), pl.BlockSpec((R, GLA_W), lambda i: (i, 0)),
                   pl.BlockSpec((n, QK_W, DV), lambda i: (i, 0, 0))],
        out_shape=[jax.ShapeDtypeStruct((T, GLA_W), BF16), jax.ShapeDtypeStruct((T, GLA_W), F32),
                   jax.ShapeDtypeStruct((T // CHUNK, QK_W, DV), F32)],
        scratch_shapes=[pltpu.VMEM((QK_W, DV), F32)],
        compiler_params=_cp("arbitrary"),
    )(proj, proj, proj, proj, proj, w2p, b2.reshape(1, QK_W), norm_g.reshape(1, DV))


def _gla_bwd(dy, proj, o_st, s_st, w2p, b2, norm_g, name):
    T = proj.shape[0]
    R = min(T, ROWS)
    n = R // CHUNK
    nb = T // R

    def body(dy_ref, q_ref, k_ref, v_ref, g_ref, a_ref, o_ref, st_ref, w2_ref, b2_ref, ng_ref,
             dp_ref, dw2_ref, db2_ref, dng_ref, gs_ref, gn_ref, db_ref, dbl_ref):
        @pl.when(pl.program_id(0) == 0)
        def _():
            gs_ref[...] = jnp.zeros_like(gs_ref)
            dw2_ref[...] = jnp.zeros_like(dw2_ref)
            db2_ref[...] = jnp.zeros_like(db2_ref)
            dng_ref[...] = jnp.zeros_like(dng_ref)

        glr = a_ref[...]
        gl, la3, eb, enb, ek, qi, ki, kd = _gla_common(q_ref[...], k_ref[...], glr, w2_ref[...], b2_ref[...], R)
        tril = _tril()
        ones = jnp.ones((n, CHUNK, DV), BF16)
        ng = ng_ref[...]
        dng = jnp.zeros((1, DV), F32)
        for h in range(HEADS):
            sl = slice(h * DK, (h + 1) * DK)
            sv = slice(h * DV, (h + 1) * DV)
            o = o_ref[:, sv]
            r = lax.rsqrt(jnp.mean(o * o, axis=-1, keepdims=True) + EPS)
            xhat = o * r
            gate = g_ref[:, sv]
            sg = jax.nn.sigmoid(gate)
            dyh = dy_ref[:, sv].astype(F32)
            dp_ref[:, 2 * QK_W + GLA_W + h * DV:2 * QK_W + GLA_W + (h + 1) * DV] = (
                dyh * (xhat * ng) * (sg * (1.0 + gate * (1.0 - sg)))).astype(dp_ref.dtype)
            don = dyh * (gate * sg)
            dng = dng + jnp.sum(don * xhat, axis=0, keepdims=True)
            dxhat = don * ng
            do = r * (dxhat - xhat * jnp.mean(dxhat * xhat, axis=-1, keepdims=True))
            qf = _bsplit(qi[:, sl], n)
            kf = _bsplit(ki[:, sl], n)
            kdf = _bsplit(kd[:, sl], n)
            qh, kh, kdh = qf.astype(MXU_DTYPE), kf.astype(MXU_DTYPE), kdf.astype(MXU_DTYPE)
            vh = _bsplit(v_ref[:, sv], n).astype(MXU_DTYPE)
            doh = _bsplit(do, n).astype(MXU_DTYPE)
            spf = st_ref[:, sl, :]
            sp = spf.astype(MXU_DTYPE)
            att = jnp.where(tril, _bdot('ncd,nsd->ncs', qh, kh), 0.0).astype(MXU_DTYPE)
            datt = jnp.where(tril, _bdot('nce,nse->ncs', doh, vh), 0.0).astype(MXU_DTYPE)
            dv = _bdot('ncs,nce->nse', att, doh)
            dqi = _bdot('ncs,nsd->ncd', datt, kh) + _bdot('nce,nde->ncd', doh, sp)
            dki = _bdot('ncs,ncd->nsd', datt, qh)
            wgt = _bdot('ncd,nce->nde', qh, doh)
            dect = jnp.exp(sum(_bdot('ncd,nce->nde', _bsplit(p[:, sl], n), ones) for p in la3))
            g = gs_ref[sl, :]
            for c in reversed(range(n)):
                gn_ref[c] = g
                g = wgt[c] + dect[c] * g
            gs_ref[sl, :] = g
            gnf = gn_ref[...]
            gn = gnf.astype(MXU_DTYPE)
            dkd = _bdot('nce,nde->ncd', vh, gn)
            dv = dv + _bdot('ncd,nde->nce', kdh, gn)
            dp_ref[:, 2 * QK_W + h * DV:2 * QK_W + (h + 1) * DV] = dv.reshape(R, DV).astype(dp_ref.dtype)
            dbl = sum(_bdot('nce,nde->ncd', ones, p) for p in _split3(gnf * spf * dect))
            pk = dkd * kdf
            dbl = dbl + jnp.sum(pk, axis=1, keepdims=True)
            dbl_ref[:, sl] = dbl.reshape(R, DK)
            db_ref[:, sl] = (dqi * qf - dki * kf - pk).reshape(R, DK)
            dp_ref[:, sl] = ((dqi.reshape(R, DK) * (DK ** -0.5)) * eb[:, sl]).astype(dp_ref.dtype)
            dp_ref[:, QK_W + h * DK:QK_W + (h + 1) * DK] = (
                dki.reshape(R, DK) * enb[:, sl] + dkd.reshape(R, DK) * ek[:, sl]).astype(dp_ref.dtype)
        dng_ref[...] += dng
        row = lax.broadcasted_iota(jnp.int32, (R, R), 0)
        col = lax.broadcasted_iota(jnp.int32, (R, R), 1)
        m_rev = (_same_chunk(row, col) & (col >= row)).astype(BF16)
        dla = sum(jnp.dot(m_rev, p, preferred_element_type=F32) for p in _split3(db_ref[...])) + dbl_ref[...]
        dgl = (dla * (1.0 / 16.0)) * jax.nn.sigmoid(-gl)
        dp_ref[:, 2 * QK_W + 2 * GLA_W:GLA_COLS] = _dot_nt(dgl, w2_ref[...]).astype(dp_ref.dtype)
        dw2_ref[...] += _dot_tn(glr, dgl)
        db2_ref[...] += jnp.sum(dgl, axis=0, keepdims=True)

    cb = lambda w, j: pl.BlockSpec((R, w), lambda i: (nb - 1 - i, j))
    full = lambda s: pl.BlockSpec(s, lambda i: (0,) * len(s))
    return pl.pallas_call(
        body, name=name, grid=(nb,),
        in_specs=[cb(GLA_W, 0), cb(QK_W, 0), cb(QK_W, 1), cb(GLA_W, 1), cb(GLA_W, 2), cb(GATE_PAD, 12),
                  cb(GLA_W, 0), pl.BlockSpec((n, QK_W, DV), lambda i: (nb - 1 - i, 0, 0)),
                  full((GATE_PAD, QK_W)), full((1, QK_W)), full((1, DV))],
        out_specs=[pl.BlockSpec((R, GLA_COLS), lambda i: (nb - 1 - i, 0)),
                   full((GATE_PAD, QK_W)), full((1, QK_W)), full((1, DV))],
        out_shape=[jax.ShapeDtypeStruct((T, GLA_COLS), BF16), jax.ShapeDtypeStruct((GATE_PAD, QK_W), F32),
                   jax.ShapeDtypeStruct((1, QK_W), F32), jax.ShapeDtypeStruct((1, DV), F32)],
        scratch_shapes=[pltpu.VMEM((QK_W, DV), F32), pltpu.VMEM((n, DK, DV), F32),
                        pltpu.VMEM((R, QK_W), F32), pltpu.VMEM((R, QK_W), F32)],
        compiler_params=_cp("arbitrary"),
    )(dy, proj, proj, proj, proj, proj, o_st, s_st, w2p, b2.reshape(1, QK_W), norm_g.reshape(1, DV))


def _lru_conv(ext_ref, cw_ref, cb_ref, R):
    xc = cb_ref[...] + ext_ref[pl.ds(SUBLANES - 3, R), :] * cw_ref[0:1, :]
    xc = xc + ext_ref[pl.ds(SUBLANES - 2, R), :] * cw_ref[1:2, :]
    xc = xc + ext_ref[pl.ds(SUBLANES - 1, R), :] * cw_ref[2:3, :]
    return xc + ext_ref[pl.ds(SUBLANES, R), :] * cw_ref[3:4, :]


def _lru_gates(xc, wa, ba, wx, bx, lam, first):
    r = jax.nn.sigmoid(_dot(xc, wa) + ba)
    ig = jax.nn.sigmoid(_dot(xc, wx) + bx)
    sp = jax.nn.softplus(-lam)
    la = (-LRU_C * r) * sp
    a = jnp.exp(la)
    mult = jnp.where(first, 1.0, jnp.sqrt(-_expm1(2.0 * la)))
    return r, ig, sp, a, mult


def _lru_fwd(proj, cw, cb, wa, ba, wx, bx, lam, name):
    T = proj.shape[0]
    R = min(T, ROWS)
    W = LRU_W

    def body(xr_ref, xh_ref, xg_ref, cw_ref, cb_ref, wa_ref, ba_ref, wx_ref, bx_ref, lam_ref,
             y_ref, hs_ref, ext_ref, hc_ref):
        i = pl.program_id(0)

        @pl.when(i == 0)
        def _():
            hc_ref[...] = jnp.zeros_like(hc_ref)

        ext_ref[0:SUBLANES, :] = jnp.where(i > 0, xh_ref[...], 0.0)
        ext_ref[pl.ds(SUBLANES, R), :] = xr_ref[...]
        xc = _lru_conv(ext_ref, cw_ref, cb_ref, R)
        row = lax.broadcasted_iota(jnp.int32, (R, W), 0)
        first = (row == 0) & (i == 0)
        _, ig, _, a, mult = _lru_gates(xc, wa_ref[...], ba_ref[...], wx_ref[...], bx_ref[...], lam_ref[...], first)
        u = mult * (ig * xc)
        k = 1
        while k < R:
            u = u + a * _shift_down(u, k, 0.0)
            a = a * _shift_down(a, k, 1.0)
            k *= 2
        hs_ref[...] = u + a * hc_ref[0:1, :]
        hc_ref[0:1, :] = hs_ref[R - 1:R, :]
        y_ref[...] = (hs_ref[...] * _gelu(xg_ref[...])).astype(y_ref.dtype)

    rb = R // SUBLANES
    full = lambda s: pl.BlockSpec(s, lambda i: (0,) * len(s))
    return pl.pallas_call(
        body, name=name, grid=(T // R,),
        in_specs=[pl.BlockSpec((R, W), lambda i: (i, 0)),
                  pl.BlockSpec((SUBLANES, W), lambda i: (jnp.maximum(i * rb - 1, 0), 0)),
                  pl.BlockSpec((R, W), lambda i: (i, 1)),
                  full((SUBLANES, W)), full((1, W)), full((W, W)), full((1, W)), full((W, W)), full((1, W)),
                  full((1, W))],
        out_specs=[pl.BlockSpec((R, W), lambda i: (i, 0)), pl.BlockSpec((R, W), lambda i: (i, 0))],
        out_shape=[jax.ShapeDtypeStruct((T, W), BF16), jax.ShapeDtypeStruct((T, W), F32)],
        scratch_shapes=[pltpu.VMEM((R + SUBLANES, W), F32), pltpu.VMEM((SUBLANES, W), F32)],
        compiler_params=_cp("arbitrary"),
    )(proj, proj, proj, cw, cb, wa, ba, wx, bx, lam)


def _lru_bwd(dy, proj, hs, cw, cb, wa, ba, wx, bx, lam, name):
    T = proj.shape[0]
    R = min(T, ROWS)
    W = LRU_W
    nb = T // R

    def body(dy_ref, xr_ref, xh_ref, xg_ref, hs_ref, hh_ref, cw_ref, cb_ref, wa_ref, ba_ref, wx_ref, bx_ref, lam_ref,
             dp_ref, dcw_ref, dvec_ref, dwa_ref, dwx_ref, ext_ref, ext2_ref, lc_ref):
        ib = pl.program_id(0)
        i = nb - 1 - ib

        @pl.when(ib == 0)
        def _():
            lc_ref[...] = jnp.zeros_like(lc_ref)
            ext2_ref[pl.ds(R, SUBLANES), :] = jnp.zeros((SUBLANES, W), F32)
            dcw_ref[...] = jnp.zeros_like(dcw_ref)
            dvec_ref[...] = jnp.zeros_like(dvec_ref)
            dwa_ref[...] = jnp.zeros_like(dwa_ref)
            dwx_ref[...] = jnp.zeros_like(dwx_ref)

        ext_ref[0:SUBLANES, :] = jnp.where(i > 0, xh_ref[...], 0.0)
        ext_ref[pl.ds(SUBLANES, R), :] = xr_ref[...]
        xc = _lru_conv(ext_ref, cw_ref, cb_ref, R)
        row = lax.broadcasted_iota(jnp.int32, (R, W), 0)
        first = (row == 0) & (i == 0)
        lam = lam_ref[...]
        r, ig, sp, a, mult = _lru_gates(xc, wa_ref[...], ba_ref[...], wx_ref[...], bx_ref[...], lam, first)
        h = hs_ref[...]
        gel, dgel = _gelu_and_grad(xg_ref[...])
        dy = dy_ref[...].astype(F32)
        dp_ref[:, W:2 * W] = (dy * h * dgel).astype(dp_ref.dtype)
        v = dy * gel + jnp.where(row == R - 1, lc_ref[0:1, :], 0.0)
        p = _shift_up(a, 1, 1.0)
        k = 1
        while k < R:
            v = v + p * _shift_up(v, k, 0.0)
            p = p * _shift_up(p, k, 1.0)
            k *= 2
        lc_ref[...] = (a * v)[0:SUBLANES, :]
        hprev = _shift_down(h, 1, 0.0) + jnp.where((row == 0) & (i > 0), hh_ref[SUBLANES - 1:SUBLANES, :], 0.0)
        da = v * hprev
        dmult = jnp.where(first, 0.0, v * (ig * xc))
        dig = v * (mult * xc)
        dxc = v * (mult * ig)
        dla = da * a - dmult * ((a * a) / mult)
        dra = (dla * (-LRU_C * sp)) * (r * (1.0 - r))
        drx = dig * (ig * (1.0 - ig))
        dxc = dxc + _dot_nt(dra, wa_ref[...]) + _dot_nt(drx, wx_ref[...])
        dwa_ref[...] += _dot_tn(xc, dra)
        dwx_ref[...] += _dot_tn(xc, drx)
        dvec_ref[0:1, :] += jnp.sum(dxc, axis=0, keepdims=True)
        dvec_ref[1:2, :] += jnp.sum(dra, axis=0, keepdims=True)
        dvec_ref[2:3, :] += jnp.sum(drx, axis=0, keepdims=True)
        dvec_ref[3:4, :] += jnp.sum(dla * (-LRU_C * r), axis=0, keepdims=True) * (-jax.nn.sigmoid(-lam))
        ext2_ref[pl.ds(0, R), :] = dxc
        dxr = ext2_ref[pl.ds(0, R), :] * cw_ref[3:4, :]
        dxr = dxr + ext2_ref[pl.ds(1, R), :] * cw_ref[2:3, :]
        dxr = dxr + ext2_ref[pl.ds(2, R), :] * cw_ref[1:2, :]
        dxr = dxr + ext2_ref[pl.ds(3, R), :] * cw_ref[0:1, :]
        dp_ref[:, 0:W] = dxr.astype(dp_ref.dtype)
        for j in range(LRU_CONV):
            dcw_ref[j:j + 1, :] += jnp.sum(dxc * ext_ref[pl.ds(SUBLANES - 3 + j, R), :], axis=0, keepdims=True)
        ext2_ref[pl.ds(R, SUBLANES), :] = dxc[0:SUBLANES, :]

    rb = R // SUBLANES
    full = lambda s: pl.BlockSpec(s, lambda i: (0,) * len(s))
    blk = lambda j: pl.BlockSpec((R, W), lambda i: (nb - 1 - i, j))
    halo = pl.BlockSpec((SUBLANES, W), lambda i: (jnp.maximum((nb - 1 - i) * rb - 1, 0), 0))
    return pl.pallas_call(
        body, name=name, grid=(nb,),
        in_specs=[blk(1), blk(0), halo, blk(1), blk(0), halo,
                  full((SUBLANES, W)), full((1, W)), full((W, W)), full((1, W)), full((W, W)), full((1, W)),
                  full((1, W))],
        out_specs=[pl.BlockSpec((R, 2 * W), lambda i: (nb - 1 - i, 0)), full((SUBLANES, W)), full((SUBLANES, W)),
                   full((W, W)), full((W, W))],
        out_shape=[jax.ShapeDtypeStruct((T, 2 * W), BF16), jax.ShapeDtypeStruct((SUBLANES, W), F32),
                   jax.ShapeDtypeStruct((SUBLANES, W), F32), jax.ShapeDtypeStruct((W, W), F32),
                   jax.ShapeDtypeStruct((W, W), F32)],
        scratch_shapes=[pltpu.VMEM((R + SUBLANES, W), F32), pltpu.VMEM((R + SUBLANES, W), F32),
                        pltpu.VMEM((SUBLANES, W), F32)],
        compiler_params=_cp("arbitrary"),
    )(dy, proj, proj, proj, hs, hs, cw, cb, wa, ba, wx, bx, lam)


def _ffn_conv(ext_ref, cw_ref, cb_ref, n):
    z = cb_ref[...] + ext_ref[pl.ds(SUBLANES - 2, n), :] * cw_ref[0:1, :]
    z = z + ext_ref[pl.ds(SUBLANES - 1, n), :] * cw_ref[1:2, :]
    return z + ext_ref[pl.ds(SUBLANES, n), :] * cw_ref[2:3, :]


def _ffn_act_fwd(za, zg, cwa, cwg, cba, cbg, name):
    T, Fh = za.shape
    R = min(T, ROWS)
    CB = FFN_CB
    rb = R // SUBLANES

    def body(a_ref, ah_ref, g_ref, gh_ref, cwa_ref, cwg_ref, cba_ref, cbg_ref, o_ref, ea_ref, eg_ref):
        i = pl.program_id(1)
        ea_ref[0:SUBLANES, :] = jnp.where(i > 0, ah_ref[...], 0.0)
        ea_ref[pl.ds(SUBLANES, R), :] = a_ref[...]
        eg_ref[0:SUBLANES, :] = jnp.where(i > 0, gh_ref[...], 0.0)
        eg_ref[pl.ds(SUBLANES, R), :] = g_ref[...]
        o_ref[...] = (_gelu(_ffn_conv(ea_ref, cwa_ref, cba_ref, R)) * _ffn_conv(eg_ref, cwg_ref, cbg_ref, R)
                      ).astype(o_ref.dtype)

    blk = pl.BlockSpec((R, CB), lambda j, i: (i, j))
    halo = pl.BlockSpec((SUBLANES, CB), lambda j, i: (jnp.maximum(i * rb - 1, 0), j))
    w8 = pl.BlockSpec((SUBLANES, CB), lambda j, i: (0, j))
    w1 = pl.BlockSpec((1, CB), lambda j, i: (0, j))
    return pl.pallas_call(
        body, name=name, grid=(Fh // CB, T // R),
        in_specs=[blk, halo, blk, halo, w8, w8, w1, w1], out_specs=blk,
        out_shape=jax.ShapeDtypeStruct((T, Fh), BF16),
        scratch_shapes=[pltpu.VMEM((R + SUBLANES, CB), F32), pltpu.VMEM((R + SUBLANES, CB), F32)],
        compiler_params=_cp("parallel", "parallel"),
    )(za, za, zg, zg, cwa, cwg, cba, cbg)


def _ffn_act_bwd(dact, za, zg, cwa, cwg, cba, cbg, name):
    T, Fh = za.shape
    R = min(T, ROWS)
    CB = FFN_CB
    rb = R // SUBLANES
    nb = T // R
    RE = R + SUBLANES

    def body(d_ref, dn_ref, a_ref, ap_ref, an_ref, g_ref, gp_ref, gn_ref, cwa_ref, cwg_ref, cba_ref, cbg_ref,
             dza_ref, dzg_ref, dca_ref, dcg_ref, ea_ref, eg_ref, ed_ref, sa_ref, sg_ref):
        i = pl.program_id(1)

        @pl.when(i == 0)
        def _():
            dca_ref[...] = jnp.zeros_like(dca_ref)
            dcg_ref[...] = jnp.zeros_like(dcg_ref)

        for e_ref, m_ref, p_ref, n_ref in ((ea_ref, a_ref, ap_ref, an_ref), (eg_ref, g_ref, gp_ref, gn_ref)):
            e_ref[0:SUBLANES, :] = jnp.where(i > 0, p_ref[...], 0.0)
            e_ref[pl.ds(SUBLANES, R), :] = m_ref[...]
            e_ref[pl.ds(SUBLANES + R, SUBLANES), :] = n_ref[...]
        ed_ref[pl.ds(0, R), :] = d_ref[...].astype(F32)
        ed_ref[pl.ds(R, SUBLANES), :] = jnp.where(i < nb - 1, dn_ref[...].astype(F32), 0.0)
        za_c = _ffn_conv(ea_ref, cwa_ref, cba_ref, RE)
        zg_c = _ffn_conv(eg_ref, cwg_ref, cbg_ref, RE)
        gel, dgel = _gelu_and_grad(za_c)
        dact_e = ed_ref[...]
        sa_ref[...] = dact_e * zg_c * dgel
        sg_ref[...] = dact_e * gel
        for s_ref, e_ref, cw_ref, dz_ref, dc_ref in ((sa_ref, ea_ref, cwa_ref, dza_ref, dca_ref),
                                                     (sg_ref, eg_ref, cwg_ref, dzg_ref, dcg_ref)):
            dz = s_ref[pl.ds(0, R), :]
            dzp = dz * cw_ref[2:3, :] + s_ref[pl.ds(1, R), :] * cw_ref[1:2, :] + s_ref[pl.ds(2, R), :] * cw_ref[0:1, :]
            dz_ref[...] = dzp.astype(dz_ref.dtype)
            for j in range(FFN_CONV):
                dc_ref[j:j + 1, :] += jnp.sum(dz * e_ref[pl.ds(SUBLANES - 2 + j, R), :], axis=0, keepdims=True)
            dc_ref[3:4, :] += jnp.sum(dz, axis=0, keepdims=True)

    blk = pl.BlockSpec((R, CB), lambda j, i: (i, j))
    prev = pl.BlockSpec((SUBLANES, CB), lambda j, i: (jnp.maximum(i * rb - 1, 0), j))
    nxt = pl.BlockSpec((SUBLANES, CB), lambda j, i: (jnp.minimum((i + 1) * rb, T // SUBLANES - 1), j))
    w8 = pl.BlockSpec((SUBLANES, CB), lambda j, i: (0, j))
    w1 = pl.BlockSpec((1, CB), lambda j, i: (0, j))
    return pl.pallas_call(
        body, name=name, grid=(Fh // CB, nb),
        in_specs=[blk, nxt, blk, prev, nxt, blk, prev, nxt, w8, w8, w1, w1],
        out_specs=[blk, blk, w8, w8],
        out_shape=[jax.ShapeDtypeStruct((T, Fh), BF16), jax.ShapeDtypeStruct((T, Fh), BF16),
                   jax.ShapeDtypeStruct((SUBLANES, Fh), F32), jax.ShapeDtypeStruct((SUBLANES, Fh), F32)],
        scratch_shapes=[pltpu.VMEM((RE + SUBLANES, CB), F32), pltpu.VMEM((RE + SUBLANES, CB), F32),
                        pltpu.VMEM((RE, CB), F32), pltpu.VMEM((RE, CB), F32), pltpu.VMEM((RE, CB), F32)],
        compiler_params=_cp("parallel", "arbitrary"),
    )(dact, dact, za, za, za, zg, zg, zg, cwa, cwg, cba, cbg)


def _conv3_window(src, start, cs, w, b):
    win = src[pl.ds(start, FFN_RC + SUBLANES), cs]
    x2 = pltpu.roll(win, 2, axis=0)[SUBLANES:]
    x1 = pltpu.roll(win, 1, axis=0)[SUBLANES:]
    x0 = win[SUBLANES:]
    return ((b + x2 * w[0]) + x1 * w[1]) + x0 * w[2], (x2, x1, x0)


def _ffn_up_fwd(u2, fa, fg, cwa, cwg, cba, cbg, name):
    T, D = u2.shape
    Fh = fa.shape[1]
    tm = min(T, ROWS)
    CW, SB, RC = FFN_CW, FFN_SB, FFN_RC
    ns = CW // SB

    def body(u_ref, fa_ref, fg_ref, cwa_ref, cwg_ref, cba_ref, cbg_ref, za_ref, zg_ref, act_ref,
             ka_ref, kg_ref, ea_ref, eg_ref):
        @pl.when(pl.program_id(1) == 0)
        def _():
            ka_ref[...] = jnp.zeros_like(ka_ref)
            kg_ref[...] = jnp.zeros_like(kg_ref)

        def matmul(s):
            cs = pl.ds(s * SB, SB)
            za_ref[:, cs] = _dot(u_ref[...], fa_ref[:, cs])
            zg_ref[:, cs] = _dot(u_ref[...], fg_ref[:, cs])

        def gate(s):
            cs = pl.ds(s * SB, SB)
            wa = [cwa_ref[j:j + 1, cs] for j in range(FFN_CONV)]
            wg = [cwg_ref[j:j + 1, cs] for j in range(FFN_CONV)]
            ba, bg = cba_ref[:, cs], cbg_ref[:, cs]
            ea_ref[0:SUBLANES, cs] = ka_ref[:, cs]
            ea_ref[pl.ds(SUBLANES, RC), cs] = za_ref[0:RC, cs]
            eg_ref[0:SUBLANES, cs] = kg_ref[:, cs]
            eg_ref[pl.ds(SUBLANES, RC), cs] = zg_ref[0:RC, cs]
            for c in range(tm // RC):
                sa, sg, start = (ea_ref, eg_ref, 0) if c == 0 else (za_ref, zg_ref, c * RC - SUBLANES)
                a_c, _ = _conv3_window(sa, start, cs, wa, ba)
                g_c, _ = _conv3_window(sg, start, cs, wg, bg)
                act_ref[pl.ds(c * RC, RC), cs] = (_gelu(a_c) * g_c).astype(act_ref.dtype)
            ka_ref[:, cs] = za_ref[tm - SUBLANES:tm, cs]
            kg_ref[:, cs] = zg_ref[tm - SUBLANES:tm, cs]

        matmul(0)
        for s in range(1, ns):
            matmul(s)
            gate(s - 1)
        gate(ns - 1)

    blk = pl.BlockSpec((tm, CW), lambda j, i: (i, j))
    wblk = pl.BlockSpec((D, CW), lambda j, i: (0, j))
    w8 = pl.BlockSpec((SUBLANES, CW), lambda j, i: (0, j))
    w1 = pl.BlockSpec((1, CW), lambda j, i: (0, j))
    return pl.pallas_call(
        body, name=name, grid=(Fh // CW, T // tm),
        in_specs=[pl.BlockSpec((tm, D), lambda j, i: (i, 0)), wblk, wblk, w8, w8, w1, w1],
        out_specs=[blk, blk, blk],
        out_shape=[jax.ShapeDtypeStruct((T, Fh), F32), jax.ShapeDtypeStruct((T, Fh), F32),
                   jax.ShapeDtypeStruct((T, Fh), BF16)],
        scratch_shapes=[pltpu.VMEM((SUBLANES, CW), F32), pltpu.VMEM((SUBLANES, CW), F32),
                        pltpu.VMEM((RC + SUBLANES, CW), F32), pltpu.VMEM((RC + SUBLANES, CW), F32)],
        compiler_params=_cp("parallel", "arbitrary"),
    )(u2, fa, fg, cwa, cwg, cba, cbg)


def _ffn_bwd_core(dhb, za, zg, wdT, faT, fgT, cwa, cwg, cba, cbg, name):
    T, D = dhb.shape
    Fh = za.shape[1]
    tm = min(T, ROWS)
    CW, SB, RC = FFN_CW, FFN_SB, FFN_RC
    ns = CW // SB
    nj = Fh // CW
    nb = T // tm
    rb = tm // SUBLANES
    nc = tm // RC

    def body(dh_ref, a_ref, ap_ref, g_ref, gp_ref, wd_ref, fa_ref, fg_ref, cwa_ref, cwg_ref, cba_ref, cbg_ref,
             dza_ref, dzg_ref, du_ref, dca_ref, dcg_ref,
             d_ref, sa_ref, sg_ref, ka_ref, kg_ref, ea_ref, eg_ref):
        ib, j = pl.program_id(0), pl.program_id(1)
        i = nb - 1 - ib

        @pl.when((ib == 0) & (j == 0))
        def _():
            dca_ref[...] = jnp.zeros_like(dca_ref)
            dcg_ref[...] = jnp.zeros_like(dcg_ref)

        @pl.when(ib == 0)
        def _():
            ka_ref[j] = jnp.zeros((SUBLANES, CW), F32)
            kg_ref[j] = jnp.zeros((SUBLANES, CW), F32)

        @pl.when(j == 0)
        def _():
            du_ref[...] = jnp.zeros_like(du_ref)

        def matmul_in(s):
            cs = pl.ds(s * SB, SB)
            d_ref[:, cs] = _dot(dh_ref[...], wd_ref[:, cs])

        def matmul_out(s):
            cs = pl.ds(s * SB, SB)
            du_ref[...] += _dot(dza_ref[:, cs], fa_ref[cs, :]) + _dot(dzg_ref[:, cs], fg_ref[cs, :])

        def fold(v):
            return jnp.sum(v.reshape(RC // SUBLANES, SUBLANES, SB), axis=0)

        def gate(s):
            cs = pl.ds(s * SB, SB)
            wa = [cwa_ref[t:t + 1, cs] for t in range(FFN_CONV)]
            wg = [cwg_ref[t:t + 1, cs] for t in range(FFN_CONV)]
            ba, bg = cba_ref[:, cs], cbg_ref[:, cs]
            ea_ref[0:SUBLANES, cs] = jnp.where(i > 0, ap_ref[:, cs], 0.0)
            ea_ref[pl.ds(SUBLANES, RC), cs] = a_ref[0:RC, cs]
            eg_ref[0:SUBLANES, cs] = jnp.where(i > 0, gp_ref[:, cs], 0.0)
            eg_ref[pl.ds(SUBLANES, RC), cs] = g_ref[0:RC, cs]
            sa_ref[pl.ds(tm, SUBLANES), cs] = ka_ref[j, :, cs]
            sg_ref[pl.ds(tm, SUBLANES), cs] = kg_ref[j, :, cs]
            acc_a = [jnp.zeros((SUBLANES, SB), F32) for _ in range(FFN_CONV + 1)]
            acc_g = [jnp.zeros((SUBLANES, SB), F32) for _ in range(FFN_CONV + 1)]
            for c in range(nc):
                src_a, src_g, start = (ea_ref, eg_ref, 0) if c == 0 else (a_ref, g_ref, c * RC - SUBLANES)
                a_c, xa = _conv3_window(src_a, start, cs, wa, ba)
                g_c, xg = _conv3_window(src_g, start, cs, wg, bg)
                gel, dgel = _gelu_and_grad(a_c)
                dact = d_ref[pl.ds(c * RC, RC), cs]
                dza = dact * g_c * dgel
                dzg = dact * gel
                sa_ref[pl.ds(c * RC, RC), cs] = dza
                sg_ref[pl.ds(c * RC, RC), cs] = dzg
                for t in range(FFN_CONV):
                    acc_a[t] = acc_a[t] + fold(dza * xa[t])
                    acc_g[t] = acc_g[t] + fold(dzg * xg[t])
                acc_a[FFN_CONV] = acc_a[FFN_CONV] + fold(dza)
                acc_g[FFN_CONV] = acc_g[FFN_CONV] + fold(dzg)
            for t in range(FFN_CONV + 1):
                dca_ref[j, t:t + 1, cs] += jnp.sum(acc_a[t], axis=0, keepdims=True)
                dcg_ref[j, t:t + 1, cs] += jnp.sum(acc_g[t], axis=0, keepdims=True)
            n = RC + SUBLANES
            for c in range(nc):
                for s_ref, o_ref, w in ((sa_ref, dza_ref, wa), (sg_ref, dzg_ref, wg)):
                    win = s_ref[pl.ds(c * RC, n), cs]
                    d1 = pltpu.roll(win, n - 1, axis=0)[:RC]
                    d2 = pltpu.roll(win, n - 2, axis=0)[:RC]
                    o_ref[pl.ds(c * RC, RC), cs] = ((win[:RC] * w[2] + d1 * w[1]) + d2 * w[0]).astype(o_ref.dtype)
            ka_ref[j, :, cs] = sa_ref[0:SUBLANES, cs]
            kg_ref[j, :, cs] = sg_ref[0:SUBLANES, cs]

        matmul_in(0)
        for s in range(1, ns):
            matmul_in(s)
            gate(s - 1)
            if s >= 2:
                matmul_out(s - 2)
        gate(ns - 1)
        if ns >= 2:
            matmul_out(ns - 2)
        matmul_out(ns - 1)

    blk = pl.BlockSpec((tm, CW), lambda ib, j: (nb - 1 - ib, j))
    prev = pl.BlockSpec((SUBLANES, CW), lambda ib, j: (jnp.maximum((nb - 1 - ib) * rb - 1, 0), j))
    w8 = pl.BlockSpec((SUBLANES, CW), lambda ib, j: (0, j))
    w1 = pl.BlockSpec((1, CW), lambda ib, j: (0, j))
    wrow = pl.BlockSpec((CW, D), lambda ib, j: (j, 0))
    acc = pl.BlockSpec((nj, SUBLANES, CW), lambda ib, j: (0, 0, 0))
    return pl.pallas_call(
        body, name=name, grid=(nb, nj),
        in_specs=[pl.BlockSpec((tm, D), lambda ib, j: (nb - 1 - ib, 0)), blk, prev, blk, prev,
                  pl.BlockSpec((D, CW), lambda ib, j: (0, j)), wrow, wrow, w8, w8, w1, w1],
        out_specs=[blk, blk, pl.BlockSpec((tm, D), lambda ib, j: (nb - 1 - ib, 0)), acc, acc],
        out_shape=[jax.ShapeDtypeStruct((T, Fh), BF16), jax.ShapeDtypeStruct((T, Fh), BF16),
                   jax.ShapeDtypeStruct((T, D), F32), jax.ShapeDtypeStruct((nj, SUBLANES, CW), F32),
                   jax.ShapeDtypeStruct((nj, SUBLANES, CW), F32)],
        scratch_shapes=[pltpu.VMEM((tm, CW), F32), pltpu.VMEM((tm + SUBLANES, CW), F32),
                        pltpu.VMEM((tm + SUBLANES, CW), F32), pltpu.VMEM((nj, SUBLANES, CW), F32),
                        pltpu.VMEM((nj, SUBLANES, CW), F32), pltpu.VMEM((RC + SUBLANES, CW), F32),
                        pltpu.VMEM((RC + SUBLANES, CW), F32)],
        compiler_params=_cp("arbitrary", "arbitrary"),
    )(dhb, za, za, zg, zg, wdT, faT, fgT, cwa, cwg, cba, cbg)


def _adamw(w, grads, m, v, name):
    rows, cols = w.shape
    tr = _row_tile(rows, max(SUBLANES, min(512, TILE_BYTES // (4 * cols)) // SUBLANES * SUBLANES))
    ng = len(grads)

    def body(*refs):
        w_ref, g_refs, m_ref, v_ref = refs[0], refs[1:1 + ng], refs[1 + ng], refs[2 + ng]
        go_ref, d_ref, mo_ref, vo_ref = refs[3 + ng:]
        g = g_refs[0][...]
        for r in g_refs[1:]:
            g = g + r[...]
        mm = ADAM_B1 * m_ref[...] + (1.0 - ADAM_B1) * g
        vv = ADAM_B2 * v_ref[...] + (1.0 - ADAM_B2) * (g * g)
        m_hat = mm / (1.0 - ADAM_B1 ** ADAM_STEP)
        v_hat = vv / (1.0 - ADAM_B2 ** ADAM_STEP)
        go_ref[...] = g
        d_ref[...] = -ADAM_LR * (m_hat / (jnp.sqrt(v_hat) + ADAM_EPS) + ADAM_WD * w_ref[...])
        mo_ref[...] = mm
        vo_ref[...] = vv

    blk = pl.BlockSpec((tr, cols), lambda i: (i, 0))
    return pl.pallas_call(
        body, name=name, grid=(rows // tr,), in_specs=[blk] * (3 + ng), out_specs=[blk] * 4,
        out_shape=[jax.ShapeDtypeStruct((rows, cols), F32)] * 4, compiler_params=_cp("parallel"),
    )(w, *grads, m, v)


def _add_slabs(a, b, out_dtype, name):
    n, rows, cols = a.shape
    tr = _row_tile(rows, max(SUBLANES, min(512, TILE_BYTES // (4 * cols)) // SUBLANES * SUBLANES))

    def body(a_ref, b_ref, o_ref):
        o_ref[...] = (a_ref[...] + b_ref[...]).astype(o_ref.dtype)

    blk = pl.BlockSpec((1, tr, cols), lambda k, i: (k, i, 0))
    return pl.pallas_call(
        body, name=name, grid=(n, rows // tr), in_specs=[blk, blk], out_specs=blk,
        out_shape=jax.ShapeDtypeStruct((n, rows, cols), out_dtype), compiler_params=_cp("parallel", "parallel"),
    )(a, b)


def _sum_leading(parts, name):
    n, rows, cols = parts.shape
    tr = _row_tile(rows, max(SUBLANES, min(512, TILE_BYTES // (4 * cols)) // SUBLANES * SUBLANES))

    def body(p_ref, o_ref):
        acc = p_ref[0].astype(F32)
        for d in range(1, n):
            acc = acc + p_ref[d].astype(F32)
        o_ref[...] = acc

    return pl.pallas_call(
        body, name=name, grid=(rows // tr,),
        in_specs=[pl.BlockSpec((n, tr, cols), lambda i: (0, i, 0))], out_specs=pl.BlockSpec((tr, cols), lambda i: (i, 0)),
        out_shape=jax.ShapeDtypeStruct((rows, cols), F32), compiler_params=_cp("parallel"),
    )(parts)


def _row_tile(rows, cap=512):
    if rows <= cap:
        return rows
    return max(t for t in range(SUBLANES, cap + 1, SUBLANES) if rows % t == 0)


def _place():
    return lax.axis_index("x"), lax.axis_index("y"), lax.axis_index("c")


def _gather_shards(arrs, name):
    n = len(arrs)
    hl = DEPTH // 2

    def body(*refs):
        ins, outs = refs[:n], refs[n:2 * n]
        send_sems, recv_sems, pass_send, pass_recv, local_sems = refs[2 * n:]
        x, y, c = _place()
        chips = [(1 - x, y), (x, 1 - y), (1 - x, 1 - y)]
        mine, theirs = pl.ds(c * hl, hl), pl.ds((1 - c) * hl, hl)
        local = [pltpu.make_async_copy(ins[a], outs[a].at[2 * x + y], local_sems.at[a]) for a in range(n)]
        for cp in local:
            cp.start()

        def send(a, j, shard):
            px, py = chips[j]
            return pltpu.make_async_remote_copy(
                src_ref=ins[a].at[mine], dst_ref=outs[a].at[shard, mine], send_sem=send_sems.at[3 * a + j],
                recv_sem=recv_sems.at[3 * a + j], device_id=(px, py, c), device_id_type=MESH)

        def passed(a, j, half):
            px, py = chips[j]
            blk = outs[a].at[2 * px + py, half]
            return pltpu.make_async_remote_copy(
                src_ref=blk, dst_ref=blk, send_sem=pass_send.at[3 * a + j], recv_sem=pass_recv.at[3 * a + j],
                device_id=(x, y, 1 - c), device_id_type=MESH)

        sends = [send(a, j, 2 * x + y) for a in range(n) for j in range(3)]
        for cp in sends:
            cp.start()
        passes = []
        for a in range(n):
            for j, (px, py) in enumerate(chips):
                send(a, j, 2 * px + py).wait_recv()
                passes.append(passed(a, j, mine))
                passes[-1].start()
        for a in range(n):
            for j in range(3):
                passed(a, j, theirs).wait_recv()
        for cp in sends + passes:
            cp.wait_send()
        for cp in local:
            cp.wait()

    hbm = pl.BlockSpec(memory_space=pl.ANY)
    return pl.pallas_call(
        body, name=name, in_specs=[hbm] * n, out_specs=[hbm] * n,
        out_shape=[jax.ShapeDtypeStruct((N_CHIPS,) + a.shape, a.dtype) for a in arrs],
        scratch_shapes=[pltpu.SemaphoreType.DMA((3 * n,)), pltpu.SemaphoreType.DMA((3 * n,)),
                        pltpu.SemaphoreType.DMA((3 * n,)), pltpu.SemaphoreType.DMA((3 * n,)),
                        pltpu.SemaphoreType.DMA((n,))],
        compiler_params=pltpu.CompilerParams(has_side_effects=True),
    )(*arrs)


def _split_with_sibling(slabs, name):
    n = len(slabs)

    def body(*refs):
        ins, kept, got = refs[:n], refs[n:2 * n], refs[2 * n:3 * n]
        send_sems, recv_sems, local_sems = refs[3 * n:]
        x, y, c = _place()
        local, remote = [], []
        for a in range(n):
            half = ins[a].shape[1] // 2
            local.append(pltpu.make_async_copy(ins[a].at[:, pl.ds(c * half, half)], kept[a], local_sems.at[a]))
            remote.append(pltpu.make_async_remote_copy(
                src_ref=ins[a].at[:, pl.ds((1 - c) * half, half)], dst_ref=got[a], send_sem=send_sems.at[a],
                recv_sem=recv_sems.at[a], device_id=(x, y, 1 - c), device_id_type=MESH))
        for cp in remote + local:
            cp.start()
        for cp in remote + local:
            cp.wait()

    hbm = pl.BlockSpec(memory_space=pl.ANY)
    halves = [jax.ShapeDtypeStruct((s.shape[0], s.shape[1] // 2, s.shape[2]), s.dtype) for s in slabs]
    res = pl.pallas_call(
        body, name=name, in_specs=[hbm] * n, out_specs=[hbm] * (2 * n), out_shape=halves + halves,
        scratch_shapes=[pltpu.SemaphoreType.DMA((n,)), pltpu.SemaphoreType.DMA((n,)), pltpu.SemaphoreType.DMA((n,))],
        compiler_params=pltpu.CompilerParams(has_side_effects=True),
    )(*slabs)
    return res[:n], res[n:]


def _exchange_grads(slabs, small, name):
    n = len(slabs)

    def body(*refs):
        ins, small_ref = refs[:n], refs[n]
        outs, small_out = refs[n + 1:2 * n + 1], refs[2 * n + 1]
        send_sems, recv_sems, ssend, srecv, local_sems = refs[2 * n + 2:]
        x, y, c = _place()
        chips = [(1 - x, y), (x, 1 - y), (1 - x, 1 - y)]
        me = 4 * x + 2 * y + c
        flips = [(fx, fy, fc) for fx in (0, 1) for fy in (0, 1) for fc in (0, 1)][1:]
        local = [pltpu.make_async_copy(small_ref, small_out.at[me], local_sems.at[n])]
        local += [pltpu.make_async_copy(ins[a].at[2 * x + y], outs[a].at[3], local_sems.at[a]) for a in range(n)]
        for cp in local:
            cp.start()

        def copy(a, j):
            px, py = chips[j]
            return pltpu.make_async_remote_copy(
                src_ref=ins[a].at[2 * px + py], dst_ref=outs[a].at[j], send_sem=send_sems.at[3 * a + j],
                recv_sem=recv_sems.at[3 * a + j], device_id=(px, py, c), device_id_type=MESH)

        def scopy(k, row):
            fx, fy, fc = flips[k]
            return pltpu.make_async_remote_copy(
                src_ref=small_ref, dst_ref=small_out.at[row], send_sem=ssend.at[k], recv_sem=srecv.at[k],
                device_id=(x ^ fx, y ^ fy, c ^ fc), device_id_type=MESH)

        sends = [copy(a, j) for a in range(n) for j in range(3)] + [scopy(k, me) for k in range(7)]
        for cp in sends:
            cp.start()
        for k, (fx, fy, fc) in enumerate(flips):
            scopy(k, 4 * (x ^ fx) + 2 * (y ^ fy) + (c ^ fc)).wait_recv()
        for a in range(n):
            for j in range(3):
                copy(a, j).wait_recv()
        for cp in sends:
            cp.wait_send()
        for cp in local:
            cp.wait()

    hbm = pl.BlockSpec(memory_space=pl.ANY)
    return pl.pallas_call(
        body, name=name, in_specs=[hbm] * (n + 1), out_specs=[hbm] * (n + 1),
        out_shape=[jax.ShapeDtypeStruct(s.shape, s.dtype) for s in slabs]
        + [jax.ShapeDtypeStruct((N_DEV,) + small.shape, small.dtype)],
        scratch_shapes=[pltpu.SemaphoreType.DMA((3 * n,)), pltpu.SemaphoreType.DMA((3 * n,)),
                        pltpu.SemaphoreType.DMA((7,)), pltpu.SemaphoreType.DMA((7,)),
                        pltpu.SemaphoreType.DMA((n + 1,))],
        compiler_params=pltpu.CompilerParams(has_side_effects=True),
    )(*slabs, small)


def _join_with_sibling(arrs, name):
    n = len(arrs)

    def body(*refs):
        ins, outs = refs[:n], refs[n:2 * n]
        send_sems, recv_sems, local_sems = refs[2 * n:]
        x, y, c = _place()
        local, remote = [], []
        for a in range(n):
            half = ins[a].shape[0]
            here = outs[a].at[pl.ds(c * half, half)]
            local.append(pltpu.make_async_copy(ins[a], here, local_sems.at[a]))
            remote.append(pltpu.make_async_remote_copy(
                src_ref=ins[a], dst_ref=here, send_sem=send_sems.at[a], recv_sem=recv_sems.at[a],
                device_id=(x, y, 1 - c), device_id_type=MESH))
        for cp in remote + local:
            cp.start()
        for a in range(n):
            half = ins[a].shape[0]
            pltpu.make_async_remote_copy(
                src_ref=ins[a], dst_ref=outs[a].at[pl.ds((1 - c) * half, half)], send_sem=send_sems.at[a],
                recv_sem=recv_sems.at[a], device_id=(x, y, 1 - c), device_id_type=MESH).wait_recv()
        for cp in remote:
            cp.wait_send()
        for cp in local:
            cp.wait()

    hbm = pl.BlockSpec(memory_space=pl.ANY)
    return pl.pallas_call(
        body, name=name, in_specs=[hbm] * n, out_specs=[hbm] * n,
        out_shape=[jax.ShapeDtypeStruct((2 * a.shape[0], a.shape[1]), a.dtype) for a in arrs],
        scratch_shapes=[pltpu.SemaphoreType.DMA((n,)), pltpu.SemaphoreType.DMA((n,)), pltpu.SemaphoreType.DMA((n,))],
        compiler_params=pltpu.CompilerParams(has_side_effects=True),
    )(*arrs)


def _block_diag(w):
    eye = jnp.eye(LRU_BLOCKS, dtype=w.dtype)
    return (eye[:, None, :, None] * w[:, :, None, :]).reshape(LRU_W, LRU_W)


def _diag_blocks(m):
    m4 = m.reshape(LRU_BLOCKS, LRU_BLOCK, LRU_BLOCKS, LRU_BLOCK)
    return jnp.stack([m4[b, :, b, :] for b in range(LRU_BLOCKS)])


def _pad_rows(a, rows):
    return jnp.pad(a, ((0, rows - a.shape[0]), (0, 0)))


def _layer_weights(p, l):
    w_in = p["w_in"][l]
    n_gla = 2 * QK_W + 2 * GLA_W
    gate = jnp.pad(w_in[:, n_gla:n_gla + GATE_RANK], ((0, 0), (0, GATE_PAD - GATE_RANK)))
    wg = jnp.concatenate([w_in[:, :n_gla], gate], axis=1)
    wl = w_in[:, n_gla + GATE_RANK:]
    w_out = p["w_out"][l]
    fa, fg = p["ffn_w_in"][l][:, :FFN_H], p["ffn_w_in"][l][:, FFN_H:]
    wd = p["ffn_w_down"][l]
    return dict(
        wg=wg, wl=wl, wgT=wg.T, wlT=wl.T, wo_g=w_out[:GLA_W], wo_l=w_out[GLA_W:], woT=w_out.T,
        fa=fa, fg=fg, faT=fa.T, fgT=fg.T, wd=wd, wdT=wd.T,
        w2p=_pad_rows(p["gla_gate_w2"][l], GATE_PAD).astype(BF16),
        wa=_block_diag(p["lru_wa"][l]).astype(BF16), wx=_block_diag(p["lru_wx"][l]).astype(BF16),
        lcw=_pad_rows(p["lru_conv_w"][l], SUBLANES),
        fcwa=_pad_rows(p["ffn_conv_w"][l][:, :FFN_H], SUBLANES), fcwg=_pad_rows(p["ffn_conv_w"][l][:, FFN_H:], SUBLANES),
    )


def _local_step(x, tgt, p):
    row = lambda v: v.reshape(1, -1)
    h = x
    stash = []
    for l in range(DEPTH):
        w = _layer_weights(p, l)
        s = dict(w=w, h0=h)
        u = _rms_fwd(h, p["ln_mix"][l], f"mix_norm_fwd{l}")
        pg = _mm(u, w["wg"], None, F32, f"proj_gla_fwd{l}")
        plr = _mm(u, w["wl"], None, F32, f"proj_lru_fwd{l}")
        yg, o_st, s_st = _gla_fwd(pg, w["w2p"], p["gla_gate_b"][l], p["gla_norm"][l], f"gla_fwd{l}")
        yl, hs = _lru_fwd(plr, w["lcw"], row(p["lru_conv_b"][l]), w["wa"], row(p["lru_ba"][l]), w["wx"],
                          row(p["lru_bx"][l]), row(p["lru_lambda"][l]), f"lru_fwd{l}")
        h = _mm(yg, w["wo_g"], h, F32, f"out_gla_fwd{l}")
        h = _mm(yl, w["wo_l"], h, F32, f"out_lru_fwd{l}")
        s.update(u=u, pg=pg, plr=plr, yg=yg, yl=yl, o_st=o_st, s_st=s_st, hs=hs, h1=h)
        u2 = _rms_fwd(h, p["ln_ffn"][l], f"ffn_norm_fwd{l}")
        cba, cbg = row(p["ffn_conv_b"][l][:FFN_H]), row(p["ffn_conv_b"][l][FFN_H:])
        za, zg, act = _ffn_up_fwd(u2, w["fa"], w["fg"], w["fcwa"], w["fcwg"], cba, cbg, f"ffn_up_fwd{l}")
        h = _mm(act, w["wd"], h, F32, f"ffn_down_fwd{l}")
        s.update(u2=u2, za=za, zg=zg, act=act, cba=cba, cbg=cbg)
        stash.append(s)

    loss, dh, dhb, d_ln_final = _loss_head(h, p["ln_final"], tgt, "loss_head")

    g = {k: [None] * DEPTH for k in ("ln_mix", "w_in", "gla_gate_w2", "gla_gate_b", "gla_norm", "lru_conv_w",
                                     "lru_conv_b", "lru_wa", "lru_ba", "lru_wx", "lru_bx", "lru_lambda", "w_out",
                                     "ln_ffn", "ffn_w_in", "ffn_conv_w", "ffn_conv_b", "ffn_w_down")}
    n_gla = 2 * QK_W + 2 * GLA_W
    for l in reversed(range(DEPTH)):
        s = stash[l]
        w = s["w"]
        g["ffn_w_down"][l] = _mm_tn(s["act"], dhb, f"ffn_down_dw{l}", tk=FFN_H // 3)
        dza, dzg, du2, dca, dcg = _ffn_bwd_core(dhb, s["za"], s["zg"], w["wdT"], w["faT"], w["fgT"], w["fcwa"],
                                                w["fcwg"], s["cba"], s["cbg"], f"ffn_bwd_core{l}")
        dca, dcg = (jnp.moveaxis(d, 0, 1).reshape(SUBLANES, FFN_H) for d in (dca, dcg))
        g["ffn_conv_w"][l] = jnp.concatenate([dca[:FFN_CONV], dcg[:FFN_CONV]], axis=1)
        g["ffn_conv_b"][l] = jnp.concatenate([dca[FFN_CONV], dcg[FFN_CONV]])
        g["ffn_w_in"][l] = jnp.concatenate([_mm_tn(s["u2"], dza, f"ffn_in_a_dw{l}", tn=FFN_H // 3),
                                            _mm_tn(s["u2"], dzg, f"ffn_in_g_dw{l}", tn=FFN_H // 3)], axis=1)
        dh, dhb, dln = _rms_bwd(s["h1"], p["ln_ffn"][l], du2, dh, f"ffn_norm_bwd{l}")
        g["ln_ffn"][l] = dln[0]
        g["w_out"][l] = jnp.concatenate([_mm_tn(s["yg"], dhb, f"out_gla_dw{l}"),
                                         _mm_tn(s["yl"], dhb, f"out_lru_dw{l}")], axis=0)
        dyc = _mm(dhb, w["woT"], None, F32, f"out_dx{l}")
        dpg, dw2, db2, dng = _gla_bwd(dyc, s["pg"], s["o_st"], s["s_st"], w["w2p"], p["gla_gate_b"][l],
                                      p["gla_norm"][l], f"gla_bwd{l}")
        dpl, dcw, dvec, dwa, dwx = _lru_bwd(dyc, s["plr"], s["hs"], w["lcw"], row(p["lru_conv_b"][l]), w["wa"],
                                            row(p["lru_ba"][l]), w["wx"], row(p["lru_bx"][l]),
                                            row(p["lru_lambda"][l]), f"lru_bwd{l}")
        g["gla_gate_w2"][l] = dw2[:GATE_RANK]
        g["gla_gate_b"][l] = db2[0]
        g["gla_norm"][l] = dng[0]
        g["lru_conv_w"][l] = dcw[:LRU_CONV]
        g["lru_conv_b"][l], g["lru_ba"][l], g["lru_bx"][l], g["lru_lambda"][l] = dvec[0], dvec[1], dvec[2], dvec[3]
        g["lru_wa"][l], g["lru_wx"][l] = _diag_blocks(dwa), _diag_blocks(dwx)
        dwg = _mm_tn(s["u"], dpg, f"proj_gla_dw{l}")
        dwl = _mm_tn(s["u"], dpl, f"proj_lru_dw{l}")
        g["w_in"][l] = jnp.concatenate([dwg[:, :n_gla + GATE_RANK], dwl], axis=1)
        du = _mm(dpg, w["wgT"], None, F32, f"proj_gla_dx{l}")
        du = _mm(dpl, w["wlT"], du, F32, f"proj_lru_dx{l}")
        dh, dhb, dln = _rms_bwd(s["h0"], p["ln_mix"][l], du, dh, f"mix_norm_bwd{l}")
        g["ln_mix"][l] = dln[0]
    grads = {k: jnp.stack(v) for k, v in g.items()}
    grads["ln_final"] = d_ln_final[0]
    return loss, dh, grads


BIG = ("w_in", "w_out", "ffn_w_in", "ffn_w_down")
COL_SHARDED = ("w_in", "ffn_w_in", "gla_gate_w2", "lru_conv_w", "ffn_conv_w")
SMALL = ("ln_mix", "gla_gate_w2", "gla_gate_b", "gla_norm", "lru_conv_w", "lru_conv_b", "lru_wa", "lru_ba", "lru_wx",
         "lru_bx", "lru_lambda", "ln_ffn", "ffn_conv_w", "ffn_conv_b", "ln_final")
WEIGHTS = ("ln_mix", "w_in", "gla_gate_w2", "gla_gate_b", "gla_norm", "lru_conv_w", "lru_conv_b", "lru_wa", "lru_ba",
           "lru_wx", "lru_bx", "lru_lambda", "w_out", "ln_ffn", "ffn_w_in", "ffn_conv_w", "ffn_conv_b", "ffn_w_down",
           "ln_final")
PACK = SUBLANES * LANES


def _whole_from_shards(name, g):
    if name in COL_SHARDED:
        return jnp.moveaxis(g, 0, -2).reshape(g.shape[1:-1] + (N_CHIPS * g.shape[-1],))
    return jnp.moveaxis(g, 0, 1).reshape((g.shape[1], N_CHIPS * g.shape[2]) + g.shape[3:])


def _slabs_from_whole(name, w):
    L, r, c = w.shape
    if name in COL_SHARDED:
        s = jnp.moveaxis(w.reshape(L, r, N_CHIPS, c // N_CHIPS), 2, 0)
    else:
        s = jnp.moveaxis(w.reshape(L, N_CHIPS, r // N_CHIPS, c), 1, 0)
    return s.reshape(N_CHIPS, -1, s.shape[-1])


def _pack(arrs):
    flat = []
    for a in arrs:
        f = a.reshape(-1)
        flat.append(jnp.pad(f, (0, (-f.shape[0]) % PACK)))
    return jnp.concatenate(flat).reshape(-1, LANES)


def _unpack(packed, shapes):
    out, at = [], 0
    flat = packed.reshape(-1)
    for s in shapes:
        size = math.prod(s)
        out.append(flat[at:at + size].reshape(s))
        at += size + (-size) % PACK
    return out


def kernel(x, ln_mix, w_in, gla_gate_w2, gla_gate_b, gla_norm, lru_conv_w, lru_conv_b, lru_wa, lru_ba, lru_wx, lru_bx, lru_lambda, w_out, ln_ffn, ffn_w_in, ffn_conv_w, ffn_conv_b, ffn_w_down, ln_final, loss_target, m_ln_mix, m_w_in, m_gla_gate_w2, m_gla_gate_b, m_gla_norm, m_lru_conv_w, m_lru_conv_b, m_lru_wa, m_lru_ba, m_lru_wx, m_lru_bx, m_lru_lambda, m_w_out, m_ln_ffn, m_ffn_w_in, m_ffn_conv_w, m_ffn_conv_b, m_ffn_w_down, m_ln_final, v_ln_mix, v_w_in, v_gla_gate_w2, v_gla_gate_b, v_gla_norm, v_lru_conv_w, v_lru_conv_b, v_lru_wa, v_lru_ba, v_lru_wx, v_lru_bx, v_lru_lambda, v_w_out, v_ln_ffn, v_ffn_w_in, v_ffn_conv_w, v_ffn_conv_b, v_ffn_w_down, v_ln_final):
    w = dict(ln_mix=ln_mix, w_in=w_in, gla_gate_w2=gla_gate_w2, gla_gate_b=gla_gate_b, gla_norm=gla_norm,
             lru_conv_w=lru_conv_w, lru_conv_b=lru_conv_b, lru_wa=lru_wa, lru_ba=lru_ba, lru_wx=lru_wx, lru_bx=lru_bx,
             lru_lambda=lru_lambda, w_out=w_out, ln_ffn=ln_ffn, ffn_w_in=ffn_w_in, ffn_conv_w=ffn_conv_w,
             ffn_conv_b=ffn_conv_b, ffn_w_down=ffn_w_down, ln_final=ln_final)
    m = dict(ln_mix=m_ln_mix, w_in=m_w_in, gla_gate_w2=m_gla_gate_w2, gla_gate_b=m_gla_gate_b, gla_norm=m_gla_norm,
             lru_conv_w=m_lru_conv_w, lru_conv_b=m_lru_conv_b, lru_wa=m_lru_wa, lru_ba=m_lru_ba, lru_wx=m_lru_wx,
             lru_bx=m_lru_bx, lru_lambda=m_lru_lambda, w_out=m_w_out, ln_ffn=m_ln_ffn, ffn_w_in=m_ffn_w_in,
             ffn_conv_w=m_ffn_conv_w, ffn_conv_b=m_ffn_conv_b, ffn_w_down=m_ffn_w_down, ln_final=m_ln_final)
    v = dict(ln_mix=v_ln_mix, w_in=v_w_in, gla_gate_w2=v_gla_gate_w2, gla_gate_b=v_gla_gate_b, gla_norm=v_gla_norm,
             lru_conv_w=v_lru_conv_w, lru_conv_b=v_lru_conv_b, lru_wa=v_lru_wa, lru_ba=v_lru_ba, lru_wx=v_lru_wx,
             lru_bx=v_lru_bx, lru_lambda=v_lru_lambda, w_out=v_w_out, ln_ffn=v_ln_ffn, ffn_w_in=v_ffn_w_in,
             ffn_conv_w=v_ffn_conv_w, ffn_conv_b=v_ffn_conv_b, ffn_w_down=v_ffn_w_down, ln_final=v_ln_final)

    sharded = BIG + ("gla_gate_w2", "lru_conv_w", "ffn_conv_w")
    gathered = _gather_shards([w[k].astype(MXU_DTYPE) if k in BIG else w[k] for k in sharded], "gather_weights")
    p = dict(w)
    for k, gk in zip(sharded, gathered):
        p[k] = _whole_from_shards(k, gk)

    loss, grad_x, grads = _local_step(x[0], loss_target[0], p)
    loss = lax.psum(loss[0, 0], ("x", "y", "c"))

    slabs = [_slabs_from_whole(k, grads[k]) for k in BIG]
    kept, got = _split_with_sibling(slabs, "split_core_halves")
    chip_half = [_add_slabs(a, b, BF16, f"core_sum_{k}") for k, a, b in zip(BIG, kept, got)]
    small = _pack([grads[k] for k in SMALL])
    *recv, small_all = _exchange_grads(chip_half, small, "exchange_grads")
    big_g = _join_with_sibling([_sum_leading(r, f"chip_sum_{k}") for k, r in zip(BIG, recv)], "join_core_halves")
    small_sum = _unpack(_sum_leading(small_all, "sum_small_grads"), [grads[k].shape for k in SMALL])

    me = 2 * lax.axis_index("x") + lax.axis_index("y")
    out_g, out_d, out_m, out_v = {}, {}, {}, {}
    for k, gk in zip(BIG, big_g):
        shape = w[k].shape
        cols = shape[-1]
        res = _adamw(w[k].reshape(-1, cols), [gk], m[k].reshape(-1, cols), v[k].reshape(-1, cols), f"adamw_{k}")
        out_g[k], out_d[k], out_m[k], out_v[k] = [r.reshape(shape) for r in res]
    small_g = []
    for k, gk in zip(SMALL, small_sum):
        if k in COL_SHARDED:
            width = w[k].shape[-1]
            gk = lax.dynamic_slice_in_dim(gk, me * width, width, axis=gk.ndim - 1)
        small_g.append(gk)
    shapes = [w[k].shape for k in SMALL]
    res = _adamw(_pack([w[k] for k in SMALL]), [_pack(small_g)], _pack([m[k] for k in SMALL]),
                 _pack([v[k] for k in SMALL]), "adamw_small")
    for out, packed in zip((out_g, out_d, out_m, out_v), res):
        for k, a in zip(SMALL, _unpack(packed, shapes)):
            out[k] = a
    return (loss, grad_x[None], *[out_g[k] for k in WEIGHTS], *[out_d[k] for k in WEIGHTS],
            *[out_m[k] for k in WEIGHTS], *[out_v[k] for k in WEIGHTS])
```

```python
import math

import jax
import jax.numpy as jnp
from jax import lax
from jax.experimental import pallas as pl
from jax.experimental.pallas import tpu as pltpu

F32 = jnp.float32
BF16 = jnp.bfloat16
MXU_DTYPE = BF16

D_MODEL = 1024
DEPTH = 4
HEADS, DK, DV, CHUNK, GATE_RANK = 4, 64, 128, 64, 16
QK_W = HEADS * DK
GLA_W = HEADS * DV
LRU_W = 512
LRU_BLOCKS, LRU_BLOCK, LRU_CONV, LRU_C = 8, 64, 4, 8.0
FFN_H = 3 * D_MODEL
FFN_CONV = 3
EPS = 1e-6
GATE_PAD = 128
GLA_COLS = 2 * QK_W + 2 * GLA_W + GATE_PAD
LRU_COLS = 2 * LRU_W
ADAM_LR, ADAM_B1, ADAM_B2, ADAM_EPS, ADAM_WD, ADAM_STEP = 0.001, 0.9, 0.999, 1e-08, 0.01, 10

LANES = 128
SUBLANES = 8
VMEM_LIMIT = 56 * 1024 * 1024
ROWS = 512
TILE_BYTES = 1 << 20
FFN_CB = 512
FFN_CW = 1024
FFN_SB = 256
FFN_RC = 32
N_CHIPS = 4
N_DEV = 8
MESH = pl.DeviceIdType.MESH


def _cp(*sem):
    return pltpu.CompilerParams(dimension_semantics=sem, vmem_limit_bytes=VMEM_LIMIT)


def _dot(a, b):
    return jnp.dot(a.astype(MXU_DTYPE), b.astype(MXU_DTYPE), preferred_element_type=F32)


def _dot_nt(a, b):
    return lax.dot_general(a.astype(MXU_DTYPE), b.astype(MXU_DTYPE), (((1,), (1,)), ((), ())),
                           preferred_element_type=F32)


def _dot_tn(a, b):
    return lax.dot_general(a.astype(MXU_DTYPE), b.astype(MXU_DTYPE), (((0,), (0,)), ((), ())),
                           preferred_element_type=F32)


def _bdot(eq, a, b):
    return jnp.einsum(eq, a, b, preferred_element_type=F32)


def _split3(x):
    x1 = x.astype(BF16)
    r1 = x - x1.astype(F32)
    x2 = r1.astype(BF16)
    x3 = (r1 - x2.astype(F32)).astype(BF16)
    return x1, x2, x3


def _gelu(x):
    c = math.sqrt(2.0 / math.pi)
    return x * (0.5 * (1.0 + jnp.tanh(c * (x + 0.044715 * (x * x * x)))))


def _gelu_and_grad(x):
    c = math.sqrt(2.0 / math.pi)
    t = jnp.tanh(c * (x + 0.044715 * (x * x * x)))
    cdf = 0.5 * (1.0 + t)
    dcdf = 0.5 * (1.0 - t * t) * (c * (1.0 + 3.0 * 0.044715 * (x * x)))
    return x * cdf, cdf + x * dcdf


def _expm1(x):
    small = x * (1.0 + x * (0.5 + x * (1.0 / 6.0 + x * (1.0 / 24.0 + x * (1.0 / 120.0)))))
    return jnp.where(jnp.abs(x) < 0.1, small, jnp.exp(x) - 1.0)


def _shift_down(x, k, fill):
    row = lax.broadcasted_iota(jnp.int32, x.shape, 0)
    return jnp.where(row >= k, pltpu.roll(x, k, axis=0), fill)


def _shift_up(x, k, fill):
    n = x.shape[0]
    row = lax.broadcasted_iota(jnp.int32, x.shape, 0)
    return jnp.where(row < n - k, pltpu.roll(x, n - k, axis=0), fill)


def _rms_fwd(h, g, name):
    T, D = h.shape
    R = min(T, ROWS)

    def body(h_ref, g_ref, o_ref):
        x = h_ref[...]
        r = lax.rsqrt(jnp.mean(x * x, axis=-1, keepdims=True) + EPS)
        o_ref[...] = ((x * r) * g_ref[...]).astype(o_ref.dtype)

    return pl.pallas_call(
        body, name=name, grid=(T // R,),
        in_specs=[pl.BlockSpec((R, D), lambda i: (i, 0)), pl.BlockSpec((1, D), lambda i: (0, 0))],
        out_specs=pl.BlockSpec((R, D), lambda i: (i, 0)),
        out_shape=jax.ShapeDtypeStruct((T, D), BF16), compiler_params=_cp("parallel"),
    )(h, g.reshape(1, D))


def _rms_bwd(h, g, du, dres, name):
    T, D = h.shape
    R = min(T, ROWS)

    def body(h_ref, g_ref, du_ref, dres_ref, dh_ref, dhb_ref, dg_ref):
        @pl.when(pl.program_id(0) == 0)
        def _():
            dg_ref[...] = jnp.zeros_like(dg_ref)

        x = h_ref[...]
        r = lax.rsqrt(jnp.mean(x * x, axis=-1, keepdims=True) + EPS)
        xhat = x * r
        dy = du_ref[...].astype(F32)
        dg_ref[...] += jnp.sum(dy * xhat, axis=0, keepdims=True)
        dxhat = dy * g_ref[...]
        dx = r * (dxhat - xhat * jnp.mean(dxhat * xhat, axis=-1, keepdims=True))
        dh = dres_ref[...] + dx
        dh_ref[...] = dh
        dhb_ref[...] = dh.astype(dhb_ref.dtype)

    blk = pl.BlockSpec((R, D), lambda i: (i, 0))
    vec = pl.BlockSpec((1, D), lambda i: (0, 0))
    return pl.pallas_call(
        body, name=name, grid=(T // R,), in_specs=[blk, vec, blk, blk], out_specs=[blk, blk, vec],
        out_shape=[jax.ShapeDtypeStruct((T, D), F32), jax.ShapeDtypeStruct((T, D), BF16),
                   jax.ShapeDtypeStruct((1, D), F32)],
        compiler_params=_cp("arbitrary"),
    )(h, g.reshape(1, D), du, dres)


def _loss_head(h, g, tgt, name):
    T, D = h.shape
    R = min(T, ROWS)

    def body(h_ref, g_ref, t_ref, loss_ref, dh_ref, dhb_ref, dg_ref):
        @pl.when(pl.program_id(0) == 0)
        def _():
            dg_ref[...] = jnp.zeros_like(dg_ref)
            loss_ref[...] = jnp.zeros_like(loss_ref)

        x = h_ref[...]
        r = lax.rsqrt(jnp.mean(x * x, axis=-1, keepdims=True) + EPS)
        xhat = x * r
        gg = g_ref[...]
        err = xhat * gg - t_ref[...]
        loss_ref[...] += 0.5 * jnp.sum(jnp.mean(err * err, axis=-1, keepdims=True), axis=0, keepdims=True)
        dy = err * (1.0 / D)
        dg_ref[...] += jnp.sum(dy * xhat, axis=0, keepdims=True)
        dxhat = dy * gg
        dh = r * (dxhat - xhat * jnp.mean(dxhat * xhat, axis=-1, keepdims=True))
        dh_ref[...] = dh
        dhb_ref[...] = dh.astype(dhb_ref.dtype)

    blk = pl.BlockSpec((R, D), lambda i: (i, 0))
    vec = pl.BlockSpec((1, D), lambda i: (0, 0))
    one = pl.BlockSpec((1, LANES), lambda i: (0, 0))
    return pl.pallas_call(
        body, name=name, grid=(T // R,), in_specs=[blk, vec, blk], out_specs=[one, blk, blk, vec],
        out_shape=[jax.ShapeDtypeStruct((1, LANES), F32), jax.ShapeDtypeStruct((T, D), F32),
                   jax.ShapeDtypeStruct((T, D), BF16), jax.ShapeDtypeStruct((1, D), F32)],
        compiler_params=_cp("arbitrary"),
    )(h, g.reshape(1, D), tgt)


def _mm(a, b, res, out_dtype, name, tm=512, tn=None):
    M, K = a.shape
    N = b.shape[1]
    tm = min(tm, M)
    tn = N if tn is None else tn

    def body(*refs):
        if res is None:
            a_ref, b_ref, o_ref = refs
            acc = _dot(a_ref[...], b_ref[...])
        else:
            a_ref, b_ref, r_ref, o_ref = refs
            acc = r_ref[...].astype(F32) + _dot(a_ref[...], b_ref[...])
        o_ref[...] = acc.astype(o_ref.dtype)

    in_specs = [pl.BlockSpec((tm, K), lambda j, i: (i, 0)), pl.BlockSpec((K, tn), lambda j, i: (0, j))]
    args = [a, b]
    if res is not None:
        in_specs.append(pl.BlockSpec((tm, tn), lambda j, i: (i, j)))
        args.append(res)
    return pl.pallas_call(
        body, name=name, grid=(N // tn, M // tm), in_specs=in_specs,
        out_specs=pl.BlockSpec((tm, tn), lambda j, i: (i, j)),
        out_shape=jax.ShapeDtypeStruct((M, N), out_dtype), compiler_params=_cp("parallel", "parallel"),
    )(*args)


def _mm_tn_into(slab, a, b, layer, chip0, name, tk, tn, tm=1024):
    M, K = a.shape
    N = b.shape[1]
    tm = min(tm, M)
    assert slab.shape[2] == tn and (K // tk == 1 or N // tn == 1)

    def body(a_ref, b_ref, slab_ref, o_ref):
        del slab_ref

        @pl.when(pl.program_id(2) == 0)
        def _():
            o_ref[...] = jnp.zeros_like(o_ref)

        o_ref[0] += _dot_tn(a_ref[...], b_ref[...])

    return pl.pallas_call(
        body, name=name, grid=(K // tk, N // tn, M // tm),
        in_specs=[pl.BlockSpec((tm, tk), lambda k, n, m: (m, k)), pl.BlockSpec((tm, tn), lambda k, n, m: (m, n)),
                  pl.BlockSpec(memory_space=pl.ANY)],
        out_specs=pl.BlockSpec((1, tk, tn), lambda k, n, m: (chip0 + k + n, layer, 0)),
        out_shape=jax.ShapeDtypeStruct(slab.shape, F32), input_output_aliases={2: 0},
        compiler_params=_cp("parallel", "parallel", "arbitrary"),
    )(a, b, slab)


def _mm_tn(a, b, name, tm=2048, tk=None, tn=None):
    M, K = a.shape
    N = b.shape[1]
    tm = min(tm, M)
    tk = K if tk is None else tk
    tn = N if tn is None else tn

    def body(a_ref, b_ref, o_ref):
        @pl.when(pl.program_id(2) == 0)
        def _():
            o_ref[...] = jnp.zeros_like(o_ref)

        o_ref[...] += _dot_tn(a_ref[...], b_ref[...])

    return pl.pallas_call(
        body, name=name, grid=(K // tk, N // tn, M // tm),
        in_specs=[pl.BlockSpec((tm, tk), lambda k, n, m: (m, k)), pl.BlockSpec((tm, tn), lambda k, n, m: (m, n))],
        out_specs=pl.BlockSpec((tk, tn), lambda k, n, m: (k, n)),
        out_shape=jax.ShapeDtypeStruct((K, N), F32), compiler_params=_cp("parallel", "parallel", "arbitrary"),
    )(a, b)


def _same_chunk(row, col):
    shift = CHUNK.bit_length() - 1
    return jnp.right_shift(row, shift) == jnp.right_shift(col, shift)


def _gla_common(q, k, glr, w2, b2, R):
    gl = _dot(glr, w2) + b2
    la = jax.nn.log_sigmoid(gl) * (1.0 / 16.0)
    row = lax.broadcasted_iota(jnp.int32, (R, R), 0)
    col = lax.broadcasted_iota(jnp.int32, (R, R), 1)
    same = _same_chunk(row, col)
    m_tri = (same & (col <= row)).astype(BF16)
    m_all = same.astype(BF16)
    la3 = _split3(la)
    b = sum(jnp.dot(m_tri, p, preferred_element_type=F32) for p in la3)
    bl = sum(jnp.dot(m_all, p, preferred_element_type=F32) for p in la3)
    eb = jnp.exp(b)
    enb = jnp.exp(-b)
    ek = jnp.exp(bl - b)
    qi = (q * (DK ** -0.5)) * eb
    ki = k * enb
    kd = k * ek
    return gl, la3, eb, enb, ek, qi, ki, kd


def _bsplit(x, n):
    return x.reshape(n, CHUNK, x.shape[-1])


def _tril():
    return (lax.broadcasted_iota(jnp.int32, (CHUNK, CHUNK), 1)
            <= lax.broadcasted_iota(jnp.int32, (CHUNK, CHUNK), 0))[None]


def _gla_fwd(proj, w2p, b2, norm_g, name):
    T = proj.shape[0]
    R = min(T, ROWS)
    n = R // CHUNK

    def body(q_ref, k_ref, v_ref, g_ref, a_ref, w2_ref, b2_ref, ng_ref, y_ref, o_ref, st_ref, s_ref):
        @pl.when(pl.program_id(0) == 0)
        def _():
            s_ref[...] = jnp.zeros_like(s_ref)

        _, la3, _, _, _, qi, ki, kd = _gla_common(q_ref[...], k_ref[...], a_ref[...], w2_ref[...], b2_ref[...], R)
        tril = _tril()
        ones = jnp.ones((n, CHUNK, DV), BF16)
        for h in range(HEADS):
            sl = slice(h * DK, (h + 1) * DK)
            sv = slice(h * DV, (h + 1) * DV)
            qh = _bsplit(qi[:, sl], n).astype(MXU_DTYPE)
            kh = _bsplit(ki[:, sl], n).astype(MXU_DTYPE)
            kdh = _bsplit(kd[:, sl], n).astype(MXU_DTYPE)
            vh = _bsplit(v_ref[:, sv], n).astype(MXU_DTYPE)
            att = jnp.where(tril, _bdot('ncd,nsd->ncs', qh, kh), 0.0)
            upd = _bdot('ncd,nce->nde', kdh, vh)
            dect = jnp.exp(sum(_bdot('ncd,nce->nde', _bsplit(p[:, sl], n), ones) for p in la3))
            s = s_ref[sl, :]
            for c in range(n):
                st_ref[c, sl, :] = s
                s = dect[c] * s + upd[c]
            s_ref[sl, :] = s
            sp = st_ref[:, sl, :].astype(MXU_DTYPE)
            o = (_bdot('ncs,nse->nce', att.astype(MXU_DTYPE), vh) + _bdot('ncd,nde->nce', qh, sp)).reshape(R, DV)
            o_ref[:, sv] = o
            r = lax.rsqrt(jnp.mean(o * o, axis=-1, keepdims=True) + EPS)
            gate = g_ref[:, sv]
            y_ref[:, sv] = (((o * r) * ng_ref[...]) * (gate * jax.nn.sigmoid(gate))).astype(y_ref.dtype)

    cb = lambda w, j: pl.BlockSpec((R, w), lambda i: (i, j))
    full = lambda s: pl.BlockSpec(s, lambda i: (0,) * len(s))
    return pl.pallas_call(
        body, name=name, grid=(T // R,),
        in_specs=[cb(QK_W, 0), cb(QK_W, 1), cb(GLA_W, 1), cb(GLA_W, 2), cb(GATE_PAD, 12),
                  full((GATE_PAD, QK_W)), full((1, QK_W)), full((1, DV))],
        out_specs=[pl.BlockSpec((R, GLA_W), lambda i: (i, 0)), pl.BlockSpec((R, GLA_W), lambda i: (i, 0)),
                   pl.BlockSpec((n, QK_W, DV), lambda i: (i, 0, 0))],
        out_shape=[jax.ShapeDtypeStruct((T, GLA_W), BF16), jax.ShapeDtypeStruct((T, GLA_W), F32),
                   jax.ShapeDtypeStruct((T // CHUNK, QK_W, DV), F32)],
        scratch_shapes=[pltpu.VMEM((QK_W, DV), F32)],
        compiler_params=_cp("arbitrary"),
    )(proj, proj, proj, proj, proj, w2p, b2.reshape(1, QK_W), norm_g.reshape(1, DV))


def _gla_bwd(dy, proj, o_st, s_st, w2p, b2, norm_g, name):
    T = proj.shape[0]
    R = min(T, ROWS)
    n = R // CHUNK
    nb = T // R

    def body(dy_ref, q_ref, k_ref, v_ref, g_ref, a_ref, o_ref, st_ref, w2_ref, b2_ref, ng_ref,
             dp_ref, dw2_ref, db2_ref, dng_ref, gs_ref, gn_ref, db_ref, dbl_ref):
        @pl.when(pl.program_id(0) == 0)
        def _():
            gs_ref[...] = jnp.zeros_like(gs_ref)
            dw2_ref[...] = jnp.zeros_like(dw2_ref)
            db2_ref[...] = jnp.zeros_like(db2_ref)
            dng_ref[...] = jnp.zeros_like(dng_ref)

        glr = a_ref[...]
        gl, la3, eb, enb, ek, qi, ki, kd = _gla_common(q_ref[...], k_ref[...], glr, w2_ref[...], b2_ref[...], R)
        tril = _tril()
        ones = jnp.ones((n, CHUNK, DV), BF16)
        ng = ng_ref[...]
        dng = jnp.zeros((1, DV), F32)
        for h in range(HEADS):
            sl = slice(h * DK, (h + 1) * DK)
            sv = slice(h * DV, (h + 1) * DV)
            o = o_ref[:, sv]
            r = lax.rsqrt(jnp.mean(o * o, axis=-1, keepdims=True) + EPS)
            xhat = o * r
            gate = g_ref[:, sv]
            sg = jax.nn.sigmoid(gate)
            dyh = dy_ref[:, sv].astype(F32)
            dp_ref[:, 2 * QK_W + GLA_W + h * DV:2 * QK_W + GLA_W + (h + 1) * DV] = (
                dyh * (xhat * ng) * (sg * (1.0 + gate * (1.0 - sg)))).astype(dp_ref.dtype)
            don = dyh * (gate * sg)
            dng = dng + jnp.sum(don * xhat, axis=0, keepdims=True)
            dxhat = don * ng
            do = r * (dxhat - xhat * jnp.mean(dxhat * xhat, axis=-1, keepdims=True))
            qf = _bsplit(qi[:, sl], n)
            kf = _bsplit(ki[:, sl], n)
            kdf = _bsplit(kd[:, sl], n)
            qh, kh, kdh = qf.astype(MXU_DTYPE), kf.astype(MXU_DTYPE), kdf.astype(MXU_DTYPE)
            vh = _bsplit(v_ref[:, sv], n).astype(MXU_DTYPE)
            doh = _bsplit(do, n).astype(MXU_DTYPE)
            spf = st_ref[:, sl, :]
            sp = spf.astype(MXU_DTYPE)
            att = jnp.where(tril, _bdot('ncd,nsd->ncs', qh, kh), 0.0).astype(MXU_DTYPE)
            datt = jnp.where(tril, _bdot('nce,nse->ncs', doh, vh), 0.0).astype(MXU_DTYPE)
            dv = _bdot('ncs,nce->nse', att, doh)
            dqi = _bdot('ncs,nsd->ncd', datt, kh) + _bdot('nce,nde->ncd', doh, sp)
            dki = _bdot('ncs,ncd->nsd', datt, qh)
            wgt = _bdot('ncd,nce->nde', qh, doh)
            dect = jnp.exp(sum(_bdot('ncd,nce->nde', _bsplit(p[:, sl], n), ones) for p in la3))
            g = gs_ref[sl, :]
            for c in reversed(range(n)):
                gn_ref[c] = g
                g = wgt[c] + dect[c] * g
            gs_ref[sl, :] = g
            gnf = gn_ref[...]
            gn = gnf.astype(MXU_DTYPE)
            dkd = _bdot('nce,nde->ncd', vh, gn)
            dv = dv + _bdot('ncd,nde->nce', kdh, gn)
            dp_ref[:, 2 * QK_W + h * DV:2 * QK_W + (h + 1) * DV] = dv.reshape(R, DV).astype(dp_ref.dtype)
            dbl = sum(_bdot('nce,nde->ncd', ones, p) for p in _split3(gnf * spf * dect))
            pk = dkd * kdf
            dbl = dbl + jnp.sum(pk, axis=1, keepdims=True)
            dbl_ref[:, sl] = dbl.reshape(R, DK)
            db_ref[:, sl] = (dqi * qf - dki * kf - pk).reshape(R, DK)
            dp_ref[:, sl] = ((dqi.reshape(R, DK) * (DK ** -0.5)) * eb[:, sl]).astype(dp_ref.dtype)
            dp_ref[:, QK_W + h * DK:QK_W + (h + 1) * DK] = (
                dki.reshape(R, DK) * enb[:, sl] + dkd.reshape(R, DK) * ek[:, sl]).astype(dp_ref.dtype)
        dng_ref[...] += dng
        row = lax.broadcasted_iota(jnp.int32, (R, R), 0)
        col = lax.broadcasted_iota(jnp.int32, (R, R), 1)
        m_rev = (_same_chunk(row, col) & (col >= row)).astype(BF16)
        dla = sum(jnp.dot(m_rev, p, preferred_element_type=F32) for p in _split3(db_ref[...])) + dbl_ref[...]
        dgl = (dla * (1.0 / 16.0)) * jax.nn.sigmoid(-gl)
        dp_ref[:, 2 * QK_W + 2 * GLA_W:GLA_COLS] = _dot_nt(dgl, w2_ref[...]).astype(dp_ref.dtype)
        dw2_ref[...] += _dot_tn(glr, dgl)
        db2_ref[...] += jnp.sum(dgl, axis=0, keepdims=True)

    cb = lambda w, j: pl.BlockSpec((R, w), lambda i: (nb - 1 - i, j))
    full = lambda s: pl.BlockSpec(s, lambda i: (0,) * len(s))
    return pl.pallas_call(
        body, name=name, grid=(nb,),
        in_specs=[cb(GLA_W, 0), cb(QK_W, 0), cb(QK_W, 1), cb(GLA_W, 1), cb(GLA_W, 2), cb(GATE_PAD, 12),
                  cb(GLA_W, 0), pl.BlockSpec((n, QK_W, DV), lambda i: (nb - 1 - i, 0, 0)),
                  full((GATE_PAD, QK_W)), full((1, QK_W)), full((1, DV))],
        out_specs=[pl.BlockSpec((R, GLA_COLS), lambda i: (nb - 1 - i, 0)),
                   full((GATE_PAD, QK_W)), full((1, QK_W)), full((1, DV))],
        out_shape=[jax.ShapeDtypeStruct((T, GLA_COLS), BF16), jax.ShapeDtypeStruct((GATE_PAD, QK_W), F32),
                   jax.ShapeDtypeStruct((1, QK_W), F32), jax.ShapeDtypeStruct((1, DV), F32)],
        scratch_shapes=[pltpu.VMEM((QK_W, DV), F32), pltpu.VMEM((n, DK, DV), F32),
                        pltpu.VMEM((R, QK_W), F32), pltpu.VMEM((R, QK_W), F32)],
        compiler_params=_cp("arbitrary"),
    )(dy, proj, proj, proj, proj, proj, o_st, s_st, w2p, b2.reshape(1, QK_W), norm_g.reshape(1, DV))


def _scan_scratch(R, W):
    return [pltpu.VMEM((W // LANES, R, LANES), F32), pltpu.VMEM((W // LANES, R, LANES), F32),
            pltpu.VMEM((W // LANES, R // SUBLANES, LANES), F32)]


def _scan_rows(a, u, c0, a_ref, u_ref, c_ref, out_ref, reverse):
    R, W = a.shape
    nt = R // SUBLANES
    shift = _shift_up if reverse else _shift_down
    a, u = a.reshape(nt, SUBLANES, W), u.reshape(nt, SUBLANES, W)
    sub = lax.broadcasted_iota(jnp.int32, (nt, SUBLANES, W), 1)
    for k in (1, 2, 4):
        inside = (sub < SUBLANES - k) if reverse else (sub >= k)
        turn = SUBLANES - k if reverse else k
        u = u + a * jnp.where(inside, pltpu.roll(u, turn, axis=1), 0.0)
        a = a * jnp.where(inside, pltpu.roll(a, turn, axis=1), 1.0)
    a, u = a.reshape(R, W), u.reshape(R, W)
    end = 0 if reverse else SUBLANES - 1
    edge = nt - 1 if reverse else 0
    for j in range(W // LANES):
        cols = slice(j * LANES, (j + 1) * LANES)
        a_ref[j] = a[:, cols]
        u_ref[j] = u[:, cols]
        at = a_ref.at[j][pl.ds(end, nt, stride=SUBLANES), :]
        ut = u_ref.at[j][pl.ds(end, nt, stride=SUBLANES), :]
        k = 1
        while k < nt:
            ut = ut + at * shift(ut, k, 0.0)
            at = at * shift(at, k, 1.0)
            k *= 2
        c_ref[j] = shift(ut + at * c0[:, cols], 1, 0.0)
        c_ref[j, edge:edge + 1, :] = c0[:, cols]
        for r in range(nt):
            rows = pl.ds(r * SUBLANES, SUBLANES)
            out_ref[rows, cols] = u_ref[j, rows, :] + a_ref[j, rows, :] * c_ref[j, r:r + 1, :]


def _lru_conv(ext_ref, cw_ref, cb_ref, R):
    xc = cb_ref[...] + ext_ref[pl.ds(SUBLANES - 3, R), :] * cw_ref[0:1, :]
    xc = xc + ext_ref[pl.ds(SUBLANES - 2, R), :] * cw_ref[1:2, :]
    xc = xc + ext_ref[pl.ds(SUBLANES - 1, R), :] * cw_ref[2:3, :]
    return xc + ext_ref[pl.ds(SUBLANES, R), :] * cw_ref[3:4, :]


def _lru_gates(xc, wa, ba, wx, bx, lam, first):
    r = jax.nn.sigmoid(_dot(xc, wa) + ba)
    ig = jax.nn.sigmoid(_dot(xc, wx) + bx)
    sp = jax.nn.softplus(-lam)
    la = (-LRU_C * r) * sp
    a = jnp.exp(la)
    mult = jnp.where(first, 1.0, jnp.sqrt(-_expm1(2.0 * la)))
    return r, ig, sp, a, mult


def _lru_fwd(proj, cw, cb, wa, ba, wx, bx, lam, name):
    T = proj.shape[0]
    R = min(T, ROWS)
    W = LRU_W

    def body(xr_ref, xh_ref, xg_ref, cw_ref, cb_ref, wa_ref, ba_ref, wx_ref, bx_ref, lam_ref,
             y_ref, hs_ref, ext_ref, hc_ref, sa_ref, su_ref, sc_ref):
        i = pl.program_id(0)

        @pl.when(i == 0)
        def _():
            hc_ref[...] = jnp.zeros_like(hc_ref)

        ext_ref[0:SUBLANES, :] = jnp.where(i > 0, xh_ref[...], 0.0)
        ext_ref[pl.ds(SUBLANES, R), :] = xr_ref[...]
        xc = _lru_conv(ext_ref, cw_ref, cb_ref, R)
        row = lax.broadcasted_iota(jnp.int32, (R, W), 0)
        first = (row == 0) & (i == 0)
        _, ig, _, a, mult = _lru_gates(xc, wa_ref[...], ba_ref[...], wx_ref[...], bx_ref[...], lam_ref[...], first)
        _scan_rows(a, mult * (ig * xc), hc_ref[0:1, :], sa_ref, su_ref, sc_ref, hs_ref, reverse=False)
        hc_ref[0:1, :] = hs_ref[R - 1:R, :]
        y_ref[...] = (hs_ref[...] * _gelu(xg_ref[...])).astype(y_ref.dtype)

    rb = R // SUBLANES
    full = lambda s: pl.BlockSpec(s, lambda i: (0,) * len(s))
    return pl.pallas_call(
        body, name=name, grid=(T // R,),
        in_specs=[pl.BlockSpec((R, W), lambda i: (i, 0)),
                  pl.BlockSpec((SUBLANES, W), lambda i: (jnp.maximum(i * rb - 1, 0), 0)),
                  pl.BlockSpec((R, W), lambda i: (i, 1)),
                  full((SUBLANES, W)), full((1, W)), full((W, W)), full((1, W)), full((W, W)), full((1, W)),
                  full((1, W))],
        out_specs=[pl.BlockSpec((R, W), lambda i: (i, 0)), pl.BlockSpec((R, W), lambda i: (i, 0))],
        out_shape=[jax.ShapeDtypeStruct((T, W), BF16), jax.ShapeDtypeStruct((T, W), F32)],
        scratch_shapes=[pltpu.VMEM((R + SUBLANES, W), F32), pltpu.VMEM((SUBLANES, W), F32),
                        *_scan_scratch(R, W)],
        compiler_params=_cp("arbitrary"),
    )(proj, proj, proj, cw, cb, wa, ba, wx, bx, lam)


def _lru_bwd(dy, proj, hs, cw, cb, wa, ba, wx, bx, lam, name):
    T = proj.shape[0]
    R = min(T, ROWS)
    W = LRU_W
    nb = T // R

    def body(dy_ref, xr_ref, xh_ref, xg_ref, hs_ref, hh_ref, cw_ref, cb_ref, wa_ref, ba_ref, wx_ref, bx_ref, lam_ref,
             dp_ref, dcw_ref, dvec_ref, dwa_ref, dwx_ref, ext_ref, ext2_ref, lc_ref, sa_ref, su_ref, sc_ref, adj_ref):
        ib = pl.program_id(0)
        i = nb - 1 - ib

        @pl.when(ib == 0)
        def _():
            lc_ref[...] = jnp.zeros_like(lc_ref)
            ext2_ref[pl.ds(R, SUBLANES), :] = jnp.zeros((SUBLANES, W), F32)
            dcw_ref[...] = jnp.zeros_like(dcw_ref)
            dvec_ref[...] = jnp.zeros_like(dvec_ref)
            dwa_ref[...] = jnp.zeros_like(dwa_ref)
            dwx_ref[...] = jnp.zeros_like(dwx_ref)

        ext_ref[0:SUBLANES, :] = jnp.where(i > 0, xh_ref[...], 0.0)
        ext_ref[pl.ds(SUBLANES, R), :] = xr_ref[...]
        xc = _lru_conv(ext_ref, cw_ref, cb_ref, R)
        row = lax.broadcasted_iota(jnp.int32, (R, W), 0)
        first = (row == 0) & (i == 0)
        lam = lam_ref[...]
        r, ig, sp, a, mult = _lru_gates(xc, wa_ref[...], ba_ref[...], wx_ref[...], bx_ref[...], lam, first)
        h = hs_ref[...]
        gel, dgel = _gelu_and_grad(xg_ref[...])
        dy = dy_ref[...].astype(F32)
        dp_ref[:, W:2 * W] = (dy * h * dgel).astype(dp_ref.dtype)
        _scan_rows(_shift_up(a, 1, 1.0), dy * gel, lc_ref[0:1, :], sa_ref, su_ref, sc_ref, adj_ref, reverse=True)
        v = adj_ref[...]
        lc_ref[...] = (a * v)[0:SUBLANES, :]
        hprev = _shift_down(h, 1, 0.0) + jnp.where((row == 0) & (i > 0), hh_ref[SUBLANES - 1:SUBLANES, :], 0.0)
        da = v * hprev
        dmult = jnp.where(first, 0.0, v * (ig * xc))
        dig = v * (mult * xc)
        dxc = v * (mult * ig)
        dla = da * a - dmult * ((a * a) / mult)
        dra = (dla * (-LRU_C * sp)) * (r * (1.0 - r))
        drx = dig * (ig * (1.0 - ig))
        dxc = dxc + _dot_nt(dra, wa_ref[...]) + _dot_nt(drx, wx_ref[...])
        dwa_ref[...] += _dot_tn(xc, dra)
        dwx_ref[...] += _dot_tn(xc, drx)
        dvec_ref[0:1, :] += jnp.sum(dxc, axis=0, keepdims=True)
        dvec_ref[1:2, :] += jnp.sum(dra, axis=0, keepdims=True)
        dvec_ref[2:3, :] += jnp.sum(drx, axis=0, keepdims=True)
        dvec_ref[3:4, :] += jnp.sum(dla * (-LRU_C * r), axis=0, keepdims=True) * (-jax.nn.sigmoid(-lam))
        ext2_ref[pl.ds(0, R), :] = dxc
        dxr = ext2_ref[pl.ds(0, R), :] * cw_ref[3:4, :]
        dxr = dxr + ext2_ref[pl.ds(1, R), :] * cw_ref[2:3, :]
        dxr = dxr + ext2_ref[pl.ds(2, R), :] * cw_ref[1:2, :]
        dxr = dxr + ext2_ref[pl.ds(3, R), :] * cw_ref[0:1, :]
        dp_ref[:, 0:W] = dxr.astype(dp_ref.dtype)
        for j in range(LRU_CONV):
            dcw_ref[j:j + 1, :] += jnp.sum(dxc * ext_ref[pl.ds(SUBLANES - 3 + j, R), :], axis=0, keepdims=True)
        ext2_ref[pl.ds(R, SUBLANES), :] = dxc[0:SUBLANES, :]

    rb = R // SUBLANES
    full = lambda s: pl.BlockSpec(s, lambda i: (0,) * len(s))
    blk = lambda j: pl.BlockSpec((R, W), lambda i: (nb - 1 - i, j))
    halo = pl.BlockSpec((SUBLANES, W), lambda i: (jnp.maximum((nb - 1 - i) * rb - 1, 0), 0))
    return pl.pallas_call(
        body, name=name, grid=(nb,),
        in_specs=[blk(1), blk(0), halo, blk(1), blk(0), halo,
                  full((SUBLANES, W)), full((1, W)), full((W, W)), full((1, W)), full((W, W)), full((1, W)),
                  full((1, W))],
        out_specs=[pl.BlockSpec((R, 2 * W), lambda i: (nb - 1 - i, 0)), full((SUBLANES, W)), full((SUBLANES, W)),
                   full((W, W)), full((W, W))],
        out_shape=[jax.ShapeDtypeStruct((T, 2 * W), BF16), jax.ShapeDtypeStruct((SUBLANES, W), F32),
                   jax.ShapeDtypeStruct((SUBLANES, W), F32), jax.ShapeDtypeStruct((W, W), F32),
                   jax.ShapeDtypeStruct((W, W), F32)],
        scratch_shapes=[pltpu.VMEM((R + SUBLANES, W), F32), pltpu.VMEM((R + SUBLANES, W), F32),
                        pltpu.VMEM((SUBLANES, W), F32), *_scan_scratch(R, W), pltpu.VMEM((R, W), F32)],
        compiler_params=_cp("arbitrary"),
    )(dy, proj, proj, proj, hs, hs, cw, cb, wa, ba, wx, bx, lam)


def _ffn_conv(ext_ref, cw_ref, cb_ref, n):
    z = cb_ref[...] + ext_ref[pl.ds(SUBLANES - 2, n), :] * cw_ref[0:1, :]
    z = z + ext_ref[pl.ds(SUBLANES - 1, n), :] * cw_ref[1:2, :]
    return z + ext_ref[pl.ds(SUBLANES, n), :] * cw_ref[2:3, :]


def _ffn_act_fwd(za, zg, cwa, cwg, cba, cbg, name):
    T, Fh = za.shape
    R = min(T, ROWS)
    CB = FFN_CB
    rb = R // SUBLANES

    def body(a_ref, ah_ref, g_ref, gh_ref, cwa_ref, cwg_ref, cba_ref, cbg_ref, o_ref, ea_ref, eg_ref):
        i = pl.program_id(1)
        ea_ref[0:SUBLANES, :] = jnp.where(i > 0, ah_ref[...], 0.0)
        ea_ref[pl.ds(SUBLANES, R), :] = a_ref[...]
        eg_ref[0:SUBLANES, :] = jnp.where(i > 0, gh_ref[...], 0.0)
        eg_ref[pl.ds(SUBLANES, R), :] = g_ref[...]
        o_ref[...] = (_gelu(_ffn_conv(ea_ref, cwa_ref, cba_ref, R)) * _ffn_conv(eg_ref, cwg_ref, cbg_ref, R)
                      ).astype(o_ref.dtype)

    blk = pl.BlockSpec((R, CB), lambda j, i: (i, j))
    halo = pl.BlockSpec((SUBLANES, CB), lambda j, i: (jnp.maximum(i * rb - 1, 0), j))
    w8 = pl.BlockSpec((SUBLANES, CB), lambda j, i: (0, j))
    w1 = pl.BlockSpec((1, CB), lambda j, i: (0, j))
    return pl.pallas_call(
        body, name=name, grid=(Fh // CB, T // R),
        in_specs=[blk, halo, blk, halo, w8, w8, w1, w1], out_specs=blk,
        out_shape=jax.ShapeDtypeStruct((T, Fh), BF16),
        scratch_shapes=[pltpu.VMEM((R + SUBLANES, CB), F32), pltpu.VMEM((R + SUBLANES, CB), F32)],
        compiler_params=_cp("parallel", "parallel"),
    )(za, za, zg, zg, cwa, cwg, cba, cbg)


def _ffn_act_bwd(dact, za, zg, cwa, cwg, cba, cbg, name):
    T, Fh = za.shape
    R = min(T, ROWS)
    CB = FFN_CB
    rb = R // SUBLANES
    nb = T // R
    RE = R + SUBLANES

    def body(d_ref, dn_ref, a_ref, ap_ref, an_ref, g_ref, gp_ref, gn_ref, cwa_ref, cwg_ref, cba_ref, cbg_ref,
             dza_ref, dzg_ref, dca_ref, dcg_ref, ea_ref, eg_ref, ed_ref, sa_ref, sg_ref):
        i = pl.program_id(1)

        @pl.when(i == 0)
        def _():
            dca_ref[...] = jnp.zeros_like(dca_ref)
            dcg_ref[...] = jnp.zeros_like(dcg_ref)

        for e_ref, m_ref, p_ref, n_ref in ((ea_ref, a_ref, ap_ref, an_ref), (eg_ref, g_ref, gp_ref, gn_ref)):
            e_ref[0:SUBLANES, :] = jnp.where(i > 0, p_ref[...], 0.0)
            e_ref[pl.ds(SUBLANES, R), :] = m_ref[...]
            e_ref[pl.ds(SUBLANES + R, SUBLANES), :] = n_ref[...]
        ed_ref[pl.ds(0, R), :] = d_ref[...].astype(F32)
        ed_ref[pl.ds(R, SUBLANES), :] = jnp.where(i < nb - 1, dn_ref[...].astype(F32), 0.0)
        za_c = _ffn_conv(ea_ref, cwa_ref, cba_ref, RE)
        zg_c = _ffn_conv(eg_ref, cwg_ref, cbg_ref, RE)
        gel, dgel = _gelu_and_grad(za_c)
        dact_e = ed_ref[...]
        sa_ref[...] = dact_e * zg_c * dgel
        sg_ref[...] = dact_e * gel
        for s_ref, e_ref, cw_ref, dz_ref, dc_ref in ((sa_ref, ea_ref, cwa_ref, dza_ref, dca_ref),
                                                     (sg_ref, eg_ref, cwg_ref, dzg_ref, dcg_ref)):
            dz = s_ref[pl.ds(0, R), :]
            dzp = dz * cw_ref[2:3, :] + s_ref[pl.ds(1, R), :] * cw_ref[1:2, :] + s_ref[pl.ds(2, R), :] * cw_ref[0:1, :]
            dz_ref[...] = dzp.astype(dz_ref.dtype)
            for j in range(FFN_CONV):
                dc_ref[j:j + 1, :] += jnp.sum(dz * e_ref[pl.ds(SUBLANES - 2 + j, R), :], axis=0, keepdims=True)
            dc_ref[3:4, :] += jnp.sum(dz, axis=0, keepdims=True)

    blk = pl.BlockSpec((R, CB), lambda j, i: (i, j))
    prev = pl.BlockSpec((SUBLANES, CB), lambda j, i: (jnp.maximum(i * rb - 1, 0), j))
    nxt = pl.BlockSpec((SUBLANES, CB), lambda j, i: (jnp.minimum((i + 1) * rb, T // SUBLANES - 1), j))
    w8 = pl.BlockSpec((SUBLANES, CB), lambda j, i: (0, j))
    w1 = pl.BlockSpec((1, CB), lambda j, i: (0, j))
    return pl.pallas_call(
        body, name=name, grid=(Fh // CB, nb),
        in_specs=[blk, nxt, blk, prev, nxt, blk, prev, nxt, w8, w8, w1, w1],
        out_specs=[blk, blk, w8, w8],
        out_shape=[jax.ShapeDtypeStruct((T, Fh), BF16), jax.ShapeDtypeStruct((T, Fh), BF16),
                   jax.ShapeDtypeStruct((SUBLANES, Fh), F32), jax.ShapeDtypeStruct((SUBLANES, Fh), F32)],
        scratch_shapes=[pltpu.VMEM((RE + SUBLANES, CB), F32), pltpu.VMEM((RE + SUBLANES, CB), F32),
                        pltpu.VMEM((RE, CB), F32), pltpu.VMEM((RE, CB), F32), pltpu.VMEM((RE, CB), F32)],
        compiler_params=_cp("parallel", "arbitrary"),
    )(dact, dact, za, za, za, zg, zg, zg, cwa, cwg, cba, cbg)


def _conv3_window(src, start, cs, w, b):
    win = src[pl.ds(start, FFN_RC + SUBLANES), cs]
    x2 = pltpu.roll(win, 2, axis=0)[SUBLANES:]
    x1 = pltpu.roll(win, 1, axis=0)[SUBLANES:]
    x0 = win[SUBLANES:]
    return ((b + x2 * w[0]) + x1 * w[1]) + x0 * w[2], (x2, x1, x0)


def _ffn_up_fwd(u2, fa, fg, cwa, cwg, cba, cbg, name):
    T, D = u2.shape
    Fh = fa.shape[1]
    tm = min(T, ROWS)
    CW, SB, RC = FFN_CW, FFN_SB, FFN_RC
    ns = CW // SB

    def body(u_ref, fa_ref, fg_ref, cwa_ref, cwg_ref, cba_ref, cbg_ref, za_ref, zg_ref, act_ref,
             ka_ref, kg_ref, ea_ref, eg_ref):
        @pl.when(pl.program_id(1) == 0)
        def _():
            ka_ref[...] = jnp.zeros_like(ka_ref)
            kg_ref[...] = jnp.zeros_like(kg_ref)

        def matmul(s):
            cs = pl.ds(s * SB, SB)
            za_ref[:, cs] = _dot(u_ref[...], fa_ref[:, cs])
            zg_ref[:, cs] = _dot(u_ref[...], fg_ref[:, cs])

        def gate(s):
            cs = pl.ds(s * SB, SB)
            wa = [cwa_ref[j:j + 1, cs] for j in range(FFN_CONV)]
            wg = [cwg_ref[j:j + 1, cs] for j in range(FFN_CONV)]
            ba, bg = cba_ref[:, cs], cbg_ref[:, cs]
            ea_ref[0:SUBLANES, cs] = ka_ref[:, cs]
            ea_ref[pl.ds(SUBLANES, RC), cs] = za_ref[0:RC, cs]
            eg_ref[0:SUBLANES, cs] = kg_ref[:, cs]
            eg_ref[pl.ds(SUBLANES, RC), cs] = zg_ref[0:RC, cs]
            for c in range(tm // RC):
                sa, sg, start = (ea_ref, eg_ref, 0) if c == 0 else (za_ref, zg_ref, c * RC - SUBLANES)
                a_c, _ = _conv3_window(sa, start, cs, wa, ba)
                g_c, _ = _conv3_window(sg, start, cs, wg, bg)
                act_ref[pl.ds(c * RC, RC), cs] = (_gelu(a_c) * g_c).astype(act_ref.dtype)
            ka_ref[:, cs] = za_ref[tm - SUBLANES:tm, cs]
            kg_ref[:, cs] = zg_ref[tm - SUBLANES:tm, cs]

        matmul(0)
        for s in range(1, ns):
            matmul(s)
            gate(s - 1)
        gate(ns - 1)

    blk = pl.BlockSpec((tm, CW), lambda j, i: (i, j))
    wblk = pl.BlockSpec((D, CW), lambda j, i: (0, j))
    w8 = pl.BlockSpec((SUBLANES, CW), lambda j, i: (0, j))
    w1 = pl.BlockSpec((1, CW), lambda j, i: (0, j))
    return pl.pallas_call(
        body, name=name, grid=(Fh // CW, T // tm),
        in_specs=[pl.BlockSpec((tm, D), lambda j, i: (i, 0)), wblk, wblk, w8, w8, w1, w1],
        out_specs=[blk, blk, blk],
        out_shape=[jax.ShapeDtypeStruct((T, Fh), F32), jax.ShapeDtypeStruct((T, Fh), F32),
                   jax.ShapeDtypeStruct((T, Fh), BF16)],
        scratch_shapes=[pltpu.VMEM((SUBLANES, CW), F32), pltpu.VMEM((SUBLANES, CW), F32),
                        pltpu.VMEM((RC + SUBLANES, CW), F32), pltpu.VMEM((RC + SUBLANES, CW), F32)],
        compiler_params=_cp("parallel", "arbitrary"),
    )(u2, fa, fg, cwa, cwg, cba, cbg)


def _ffn_bwd_core(dhb, za, zg, wdT, faT, fgT, cwa, cwg, cba, cbg, name):
    T, D = dhb.shape
    Fh = za.shape[1]
    tm = min(T, ROWS)
    CW, SB, RC = FFN_CW, FFN_SB, FFN_RC
    ns = CW // SB
    nj = Fh // CW
    nb = T // tm
    rb = tm // SUBLANES
    nc = tm // RC

    def body(dh_ref, a_ref, ap_ref, g_ref, gp_ref, wd_ref, fa_ref, fg_ref, cwa_ref, cwg_ref, cba_ref, cbg_ref,
             dza_ref, dzg_ref, du_ref, dca_ref, dcg_ref,
             d_ref, sa_ref, sg_ref, ka_ref, kg_ref, ea_ref, eg_ref):
        ib, j = pl.program_id(0), pl.program_id(1)
        i = nb - 1 - ib

        @pl.when((ib == 0) & (j == 0))
        def _():
            dca_ref[...] = jnp.zeros_like(dca_ref)
            dcg_ref[...] = jnp.zeros_like(dcg_ref)

        @pl.when(ib == 0)
        def _():
            ka_ref[j] = jnp.zeros((SUBLANES, CW), F32)
            kg_ref[j] = jnp.zeros((SUBLANES, CW), F32)

        @pl.when(j == 0)
        def _():
            du_ref[...] = jnp.zeros_like(du_ref)

        def matmul_in(s):
            cs = pl.ds(s * SB, SB)
            d_ref[:, cs] = _dot(dh_ref[...], wd_ref[:, cs])

        def matmul_out(s):
            cs = pl.ds(s * SB, SB)
            du_ref[...] += _dot(dza_ref[:, cs], fa_ref[cs, :]) + _dot(dzg_ref[:, cs], fg_ref[cs, :])

        def fold(v):
            return jnp.sum(v.reshape(RC // SUBLANES, SUBLANES, SB), axis=0)

        def gate(s):
            cs = pl.ds(s * SB, SB)
            wa = [cwa_ref[t:t + 1, cs] for t in range(FFN_CONV)]
            wg = [cwg_ref[t:t + 1, cs] for t in range(FFN_CONV)]
            ba, bg = cba_ref[:, cs], cbg_ref[:, cs]
            ea_ref[0:SUBLANES, cs] = jnp.where(i > 0, ap_ref[:, cs], 0.0)
            ea_ref[pl.ds(SUBLANES, RC), cs] = a_ref[0:RC, cs]
            eg_ref[0:SUBLANES, cs] = jnp.where(i > 0, gp_ref[:, cs], 0.0)
            eg_ref[pl.ds(SUBLANES, RC), cs] = g_ref[0:RC, cs]
            sa_ref[pl.ds(tm, SUBLANES), cs] = ka_ref[j, :, cs]
            sg_ref[pl.ds(tm, SUBLANES), cs] = kg_ref[j, :, cs]
            acc_a = [jnp.zeros((SUBLANES, SB), F32) for _ in range(FFN_CONV + 1)]
            acc_g = [jnp.zeros((SUBLANES, SB), F32) for _ in range(FFN_CONV + 1)]
            for c in range(nc):
                src_a, src_g, start = (ea_ref, eg_ref, 0) if c == 0 else (a_ref, g_ref, c * RC - SUBLANES)
                a_c, xa = _conv3_window(src_a, start, cs, wa, ba)
                g_c, xg = _conv3_window(src_g, start, cs, wg, bg)
                gel, dgel = _gelu_and_grad(a_c)
                dact = d_ref[pl.ds(c * RC, RC), cs]
                dza = dact * g_c * dgel
                dzg = dact * gel
                sa_ref[pl.ds(c * RC, RC), cs] = dza
                sg_ref[pl.ds(c * RC, RC), cs] = dzg
                for t in range(FFN_CONV):
                    acc_a[t] = acc_a[t] + fold(dza * xa[t])
                    acc_g[t] = acc_g[t] + fold(dzg * xg[t])
                acc_a[FFN_CONV] = acc_a[FFN_CONV] + fold(dza)
                acc_g[FFN_CONV] = acc_g[FFN_CONV] + fold(dzg)
            for t in range(FFN_CONV + 1):
                dca_ref[j, t:t + 1, cs] += jnp.sum(acc_a[t], axis=0, keepdims=True)
                dcg_ref[j, t:t + 1, cs] += jnp.sum(acc_g[t], axis=0, keepdims=True)
            n = RC + SUBLANES
            for c in range(nc):
                for s_ref, o_ref, w in ((sa_ref, dza_ref, wa), (sg_ref, dzg_ref, wg)):
                    win = s_ref[pl.ds(c * RC, n), cs]
                    d1 = pltpu.roll(win, n - 1, axis=0)[:RC]
                    d2 = pltpu.roll(win, n - 2, axis=0)[:RC]
                    o_ref[pl.ds(c * RC, RC), cs] = ((win[:RC] * w[2] + d1 * w[1]) + d2 * w[0]).astype(o_ref.dtype)
            ka_ref[j, :, cs] = sa_ref[0:SUBLANES, cs]
            kg_ref[j, :, cs] = sg_ref[0:SUBLANES, cs]

        matmul_in(0)
        for s in range(1, ns):
            matmul_in(s)
            gate(s - 1)
            if s >= 2:
                matmul_out(s - 2)
        gate(ns - 1)
        if ns >= 2:
            matmul_out(ns - 2)
        matmul_out(ns - 1)

    blk = pl.BlockSpec((tm, CW), lambda ib, j: (nb - 1 - ib, j))
    prev = pl.BlockSpec((SUBLANES, CW), lambda ib, j: (jnp.maximum((nb - 1 - ib) * rb - 1, 0), j))
    w8 = pl.BlockSpec((SUBLANES, CW), lambda ib, j: (0, j))
    w1 = pl.BlockSpec((1, CW), lambda ib, j: (0, j))
    wrow = pl.BlockSpec((CW, D), lambda ib, j: (j, 0))
    acc = pl.BlockSpec((nj, SUBLANES, CW), lambda ib, j: (0, 0, 0))
    return pl.pallas_call(
        body, name=name, grid=(nb, nj),
        in_specs=[pl.BlockSpec((tm, D), lambda ib, j: (nb - 1 - ib, 0)), blk, prev, blk, prev,
                  pl.BlockSpec((D, CW), lambda ib, j: (0, j)), wrow, wrow, w8, w8, w1, w1],
        out_specs=[blk, blk, pl.BlockSpec((tm, D), lambda ib, j: (nb - 1 - ib, 0)), acc, acc],
        out_shape=[jax.ShapeDtypeStruct((T, Fh), BF16), jax.ShapeDtypeStruct((T, Fh), BF16),
                   jax.ShapeDtypeStruct((T, D), F32), jax.ShapeDtypeStruct((nj, SUBLANES, CW), F32),
                   jax.ShapeDtypeStruct((nj, SUBLANES, CW), F32)],
        scratch_shapes=[pltpu.VMEM((tm, CW), F32), pltpu.VMEM((tm + SUBLANES, CW), F32),
                        pltpu.VMEM((tm + SUBLANES, CW), F32), pltpu.VMEM((nj, SUBLANES, CW), F32),
                        pltpu.VMEM((nj, SUBLANES, CW), F32), pltpu.VMEM((RC + SUBLANES, CW), F32),
                        pltpu.VMEM((RC + SUBLANES, CW), F32)],
        compiler_params=_cp("arbitrary", "arbitrary"),
    )(dhb, za, za, zg, zg, wdT, faT, fgT, cwa, cwg, cba, cbg)


def _adamw(w, grads, m, v, name):
    rows, cols = w.shape
    tr = _row_tile(rows, max(SUBLANES, min(512, TILE_BYTES // (4 * cols)) // SUBLANES * SUBLANES))
    ng = len(grads)

    def body(*refs):
        w_ref, g_refs, m_ref, v_ref = refs[0], refs[1:1 + ng], refs[1 + ng], refs[2 + ng]
        go_ref, d_ref, mo_ref, vo_ref = refs[3 + ng:]
        g = g_refs[0][...]
        for r in g_refs[1:]:
            g = g + r[...]
        mm = ADAM_B1 * m_ref[...] + (1.0 - ADAM_B1) * g
        vv = ADAM_B2 * v_ref[...] + (1.0 - ADAM_B2) * (g * g)
        m_hat = mm / (1.0 - ADAM_B1 ** ADAM_STEP)
        v_hat = vv / (1.0 - ADAM_B2 ** ADAM_STEP)
        go_ref[...] = g
        d_ref[...] = -ADAM_LR * (m_hat / (jnp.sqrt(v_hat) + ADAM_EPS) + ADAM_WD * w_ref[...])
        mo_ref[...] = mm
        vo_ref[...] = vv

    blk = pl.BlockSpec((tr, cols), lambda i: (i, 0))
    return pl.pallas_call(
        body, name=name, grid=(rows // tr,), in_specs=[blk] * (3 + ng), out_specs=[blk] * 4,
        out_shape=[jax.ShapeDtypeStruct((rows, cols), F32)] * 4, compiler_params=_cp("parallel"),
    )(w, *grads, m, v)


def _add_slabs(a, b, out_dtype, name):
    n, rows, cols = a.shape
    tr = _row_tile(rows, max(SUBLANES, min(512, TILE_BYTES // (4 * cols)) // SUBLANES * SUBLANES))

    def body(a_ref, b_ref, o_ref):
        o_ref[...] = (a_ref[...] + b_ref[...]).astype(o_ref.dtype)

    blk = pl.BlockSpec((1, tr, cols), lambda k, i: (k, i, 0))
    return pl.pallas_call(
        body, name=name, grid=(n, rows // tr), in_specs=[blk, blk], out_specs=blk,
        out_shape=jax.ShapeDtypeStruct((n, rows, cols), out_dtype), compiler_params=_cp("parallel", "parallel"),
    )(a, b)


def _sum_leading(parts, name):
    n, rows, cols = parts.shape
    tr = _row_tile(rows, max(SUBLANES, min(512, TILE_BYTES // (4 * cols)) // SUBLANES * SUBLANES))

    def body(p_ref, o_ref):
        acc = p_ref[0].astype(F32)
        for d in range(1, n):
            acc = acc + p_ref[d].astype(F32)
        o_ref[...] = acc

    return pl.pallas_call(
        body, name=name, grid=(rows // tr,),
        in_specs=[pl.BlockSpec((n, tr, cols), lambda i: (0, i, 0))], out_specs=pl.BlockSpec((tr, cols), lambda i: (i, 0)),
        out_shape=jax.ShapeDtypeStruct((rows, cols), F32), compiler_params=_cp("parallel"),
    )(parts)


def _row_tile(rows, cap=512):
    if rows <= cap:
        return rows
    return max(t for t in range(SUBLANES, cap + 1, SUBLANES) if rows % t == 0)


def _place():
    return lax.axis_index("x"), lax.axis_index("y"), lax.axis_index("c")


def _gather_shards(arrs, name):
    n = len(arrs)
    hl = DEPTH // 2

    def body(*refs):
        ins, outs = refs[:n], refs[n:2 * n]
        send_sems, recv_sems, pass_send, pass_recv, local_sems = refs[2 * n:]
        x, y, c = _place()
        chips = [(1 - x, y), (x, 1 - y), (1 - x, 1 - y)]
        mine, theirs = pl.ds(c * hl, hl), pl.ds((1 - c) * hl, hl)
        local = [pltpu.make_async_copy(ins[a], outs[a].at[2 * x + y], local_sems.at[a]) for a in range(n)]
        for cp in local:
            cp.start()

        def send(a, j, shard):
            px, py = chips[j]
            return pltpu.make_async_remote_copy(
                src_ref=ins[a].at[mine], dst_ref=outs[a].at[shard, mine], send_sem=send_sems.at[3 * a + j],
                recv_sem=recv_sems.at[3 * a + j], device_id=(px, py, c), device_id_type=MESH)

        def passed(a, j, half):
            px, py = chips[j]
            blk = outs[a].at[2 * px + py, half]
            return pltpu.make_async_remote_copy(
                src_ref=blk, dst_ref=blk, send_sem=pass_send.at[3 * a + j], recv_sem=pass_recv.at[3 * a + j],
                device_id=(x, y, 1 - c), device_id_type=MESH)

        sends = [send(a, j, 2 * x + y) for a in range(n) for j in range(3)]
        for cp in sends:
            cp.start()
        passes = []
        for a in range(n):
            for j, (px, py) in enumerate(chips):
                send(a, j, 2 * px + py).wait_recv()
                passes.append(passed(a, j, mine))
                passes[-1].start()
        for a in range(n):
            for j in range(3):
                passed(a, j, theirs).wait_recv()
        for cp in sends + passes:
            cp.wait_send()
        for cp in local:
            cp.wait()

    hbm = pl.BlockSpec(memory_space=pl.ANY)
    return pl.pallas_call(
        body, name=name, in_specs=[hbm] * n, out_specs=[hbm] * n,
        out_shape=[jax.ShapeDtypeStruct((N_CHIPS,) + a.shape, a.dtype) for a in arrs],
        scratch_shapes=[pltpu.SemaphoreType.DMA((3 * n,)), pltpu.SemaphoreType.DMA((3 * n,)),
                        pltpu.SemaphoreType.DMA((3 * n,)), pltpu.SemaphoreType.DMA((3 * n,)),
                        pltpu.SemaphoreType.DMA((n,))],
        compiler_params=pltpu.CompilerParams(has_side_effects=True),
    )(*arrs)


def _split_with_sibling(slabs, name):
    n = len(slabs)

    def body(*refs):
        ins, kept, got = refs[:n], refs[n:2 * n], refs[2 * n:3 * n]
        send_sems, recv_sems, local_sems = refs[3 * n:]
        x, y, c = _place()
        local, remote = [], []
        for a in range(n):
            local.append(pltpu.make_async_copy(ins[a].at[:, c], kept[a], local_sems.at[a]))
            remote.append(pltpu.make_async_remote_copy(
                src_ref=ins[a].at[:, 1 - c], dst_ref=got[a], send_sem=send_sems.at[a],
                recv_sem=recv_sems.at[a], device_id=(x, y, 1 - c), device_id_type=MESH))
        for cp in remote + local:
            cp.start()
        for cp in remote + local:
            cp.wait()

    hbm = pl.BlockSpec(memory_space=pl.ANY)
    halves = [jax.ShapeDtypeStruct((s.shape[0], s.shape[1] // 2, s.shape[2]), s.dtype) for s in slabs]
    split = [s.reshape(s.shape[0], 2, s.shape[1] // 2, s.shape[2]) for s in slabs]
    res = pl.pallas_call(
        body, name=name, in_specs=[hbm] * n, out_specs=[hbm] * (2 * n), out_shape=halves + halves,
        scratch_shapes=[pltpu.SemaphoreType.DMA((n,)), pltpu.SemaphoreType.DMA((n,)), pltpu.SemaphoreType.DMA((n,))],
        compiler_params=pltpu.CompilerParams(has_side_effects=True),
    )(*split)
    return res[:n], res[n:]


def _exchange_grads(slabs, small, name):
    n = len(slabs)

    def body(*refs):
        ins, small_ref = refs[:n], refs[n]
        outs, small_out = refs[n + 1:2 * n + 1], refs[2 * n + 1]
        send_sems, recv_sems, ssend, srecv, local_sems = refs[2 * n + 2:]
        x, y, c = _place()
        chips = [(1 - x, y), (x, 1 - y), (1 - x, 1 - y)]
        me = 4 * x + 2 * y + c
        flips = [(fx, fy, fc) for fx in (0, 1) for fy in (0, 1) for fc in (0, 1)][1:]
        local = [pltpu.make_async_copy(small_ref, small_out.at[me], local_sems.at[n])]
        local += [pltpu.make_async_copy(ins[a].at[2 * x + y], outs[a].at[3], local_sems.at[a]) for a in range(n)]
        for cp in local:
            cp.start()

        def copy(a, j):
            px, py = chips[j]
            return pltpu.make_async_remote_copy(
                src_ref=ins[a].at[2 * px + py], dst_ref=outs[a].at[j], send_sem=send_sems.at[3 * a + j],
                recv_sem=recv_sems.at[3 * a + j], device_id=(px, py, c), device_id_type=MESH)

        def scopy(k, row):
            fx, fy, fc = flips[k]
            return pltpu.make_async_remote_copy(
                src_ref=small_ref, dst_ref=small_out.at[row], send_sem=ssend.at[k], recv_sem=srecv.at[k],
                device_id=(x ^ fx, y ^ fy, c ^ fc), device_id_type=MESH)

        sends = [copy(a, j) for a in range(n) for j in range(3)] + [scopy(k, me) for k in range(7)]
        for cp in sends:
            cp.start()
        for k, (fx, fy, fc) in enumerate(flips):
            scopy(k, 4 * (x ^ fx) + 2 * (y ^ fy) + (c ^ fc)).wait_recv()
        for a in range(n):
            for j in range(3):
                copy(a, j).wait_recv()
        for cp in sends:
            cp.wait_send()
        for cp in local:
            cp.wait()

    hbm = pl.BlockSpec(memory_space=pl.ANY)
    return pl.pallas_call(
        body, name=name, in_specs=[hbm] * (n + 1), out_specs=[hbm] * (n + 1),
        out_shape=[jax.ShapeDtypeStruct(s.shape, s.dtype) for s in slabs]
        + [jax.ShapeDtypeStruct((N_DEV,) + small.shape, small.dtype)],
        scratch_shapes=[pltpu.SemaphoreType.DMA((3 * n,)), pltpu.SemaphoreType.DMA((3 * n,)),
                        pltpu.SemaphoreType.DMA((7,)), pltpu.SemaphoreType.DMA((7,)),
                        pltpu.SemaphoreType.DMA((n + 1,))],
        compiler_params=pltpu.CompilerParams(has_side_effects=True),
    )(*slabs, small)


def _join_with_sibling(arrs, name):
    n = len(arrs)

    def body(*refs):
        ins, outs = refs[:n], refs[n:2 * n]
        send_sems, recv_sems, local_sems = refs[2 * n:]
        x, y, c = _place()
        local, remote = [], []
        for a in range(n):
            local.append(pltpu.make_async_copy(ins[a], outs[a].at[c], local_sems.at[a]))
            remote.append(pltpu.make_async_remote_copy(
                src_ref=ins[a], dst_ref=outs[a].at[c], send_sem=send_sems.at[a], recv_sem=recv_sems.at[a],
                device_id=(x, y, 1 - c), device_id_type=MESH))
        for cp in remote + local:
            cp.start()
        for a in range(n):
            pltpu.make_async_remote_copy(
                src_ref=ins[a], dst_ref=outs[a].at[1 - c], send_sem=send_sems.at[a],
                recv_sem=recv_sems.at[a], device_id=(x, y, 1 - c), device_id_type=MESH).wait_recv()
        for cp in remote:
            cp.wait_send()
        for cp in local:
            cp.wait()

    hbm = pl.BlockSpec(memory_space=pl.ANY)
    res = pl.pallas_call(
        body, name=name, in_specs=[hbm] * n, out_specs=[hbm] * n,
        out_shape=[jax.ShapeDtypeStruct((2,) + a.shape, a.dtype) for a in arrs],
        scratch_shapes=[pltpu.SemaphoreType.DMA((n,)), pltpu.SemaphoreType.DMA((n,)), pltpu.SemaphoreType.DMA((n,))],
        compiler_params=pltpu.CompilerParams(has_side_effects=True),
    )(*arrs)
    return [r.reshape(2 * a.shape[0], a.shape[1]) for r, a in zip(res, arrs)]


def _block_diag(w):
    eye = jnp.eye(LRU_BLOCKS, dtype=w.dtype)
    return (eye[:, None, :, None] * w[:, :, None, :]).reshape(LRU_W, LRU_W)


def _diag_blocks(m):
    m4 = m.reshape(LRU_BLOCKS, LRU_BLOCK, LRU_BLOCKS, LRU_BLOCK)
    return jnp.stack([m4[b, :, b, :] for b in range(LRU_BLOCKS)])


def _pad_rows(a, rows):
    return jnp.pad(a, ((0, rows - a.shape[0]), (0, 0)))


def _layer_weights(p, l):
    w_in = p["w_in"][l]
    n_gla = 2 * QK_W + 2 * GLA_W
    gate = jnp.pad(w_in[:, n_gla:n_gla + GATE_RANK], ((0, 0), (0, GATE_PAD - GATE_RANK)))
    wg = jnp.concatenate([w_in[:, :n_gla], gate], axis=1)
    wl = w_in[:, n_gla + GATE_RANK:]
    w_out = p["w_out"][l]
    fa, fg = p["ffn_w_in"][l][:, :FFN_H], p["ffn_w_in"][l][:, FFN_H:]
    wd = p["ffn_w_down"][l]
    return dict(
        wg=wg, wl=wl, wgT=wg.T, wlT=wl.T, wo_g=w_out[:GLA_W], wo_l=w_out[GLA_W:], woT=w_out.T,
        fa=fa, fg=fg, faT=fa.T, fgT=fg.T, wd=wd, wdT=wd.T,
        w2p=_pad_rows(p["gla_gate_w2"][l], GATE_PAD).astype(BF16),
        wa=_block_diag(p["lru_wa"][l]).astype(BF16), wx=_block_diag(p["lru_wx"][l]).astype(BF16),
        lcw=_pad_rows(p["lru_conv_w"][l], SUBLANES),
        fcwa=_pad_rows(p["ffn_conv_w"][l][:, :FFN_H], SUBLANES), fcwg=_pad_rows(p["ffn_conv_w"][l][:, FFN_H:], SUBLANES),
    )


def _local_step(x, tgt, p):
    row = lambda v: v.reshape(1, -1)
    h = x
    stash = []
    for l in range(DEPTH):
        w = _layer_weights(p, l)
        s = dict(w=w, h0=h)
        u = _rms_fwd(h, p["ln_mix"][l], f"mix_norm_fwd{l}")
        pg = _mm(u, w["wg"], None, F32, f"proj_gla_fwd{l}")
        plr = _mm(u, w["wl"], None, F32, f"proj_lru_fwd{l}")
        yg, o_st, s_st = _gla_fwd(pg, w["w2p"], p["gla_gate_b"][l], p["gla_norm"][l], f"gla_fwd{l}")
        yl, hs = _lru_fwd(plr, w["lcw"], row(p["lru_conv_b"][l]), w["wa"], row(p["lru_ba"][l]), w["wx"],
                          row(p["lru_bx"][l]), row(p["lru_lambda"][l]), f"lru_fwd{l}")
        h = _mm(yg, w["wo_g"], h, F32, f"out_gla_fwd{l}")
        h = _mm(yl, w["wo_l"], h, F32, f"out_lru_fwd{l}")
        s.update(u=u, pg=pg, plr=plr, yg=yg, yl=yl, o_st=o_st, s_st=s_st, hs=hs, h1=h)
        u2 = _rms_fwd(h, p["ln_ffn"][l], f"ffn_norm_fwd{l}")
        cba, cbg = row(p["ffn_conv_b"][l][:FFN_H]), row(p["ffn_conv_b"][l][FFN_H:])
        za, zg, act = _ffn_up_fwd(u2, w["fa"], w["fg"], w["fcwa"], w["fcwg"], cba, cbg, f"ffn_up_fwd{l}")
        h = _mm(act, w["wd"], h, F32, f"ffn_down_fwd{l}")
        s.update(u2=u2, za=za, zg=zg, act=act, cba=cba, cbg=cbg)
        stash.append(s)

    loss, dh, dhb, d_ln_final = _loss_head(h, p["ln_final"], tgt, "loss_head")

    g = {k: [None] * DEPTH for k in ("ln_mix", "w_in", "gla_gate_w2", "gla_gate_b", "gla_norm", "lru_conv_w",
                                     "lru_conv_b", "lru_wa", "lru_ba", "lru_wx", "lru_bx", "lru_lambda",
                                     "ln_ffn", "ffn_conv_w", "ffn_conv_b")}
    slab = dict(w_out=jnp.zeros((N_CHIPS, DEPTH * D_MODEL // N_CHIPS, D_MODEL), F32),
                ffn_w_in=jnp.zeros((N_CHIPS, DEPTH * D_MODEL, 2 * FFN_H // N_CHIPS), F32),
                ffn_w_down=jnp.zeros((N_CHIPS, DEPTH * FFN_H // N_CHIPS, D_MODEL), F32))
    n_gla = 2 * QK_W + 2 * GLA_W
    for l in reversed(range(DEPTH)):
        s = stash[l]
        w = s["w"]
        slab["ffn_w_down"] = _mm_tn_into(slab["ffn_w_down"], s["act"], dhb, l, 0, f"ffn_down_dw{l}",
                                         tk=FFN_H // N_CHIPS, tn=D_MODEL, tm=2048)
        dza, dzg, du2, dca, dcg = _ffn_bwd_core(dhb, s["za"], s["zg"], w["wdT"], w["faT"], w["fgT"], w["fcwa"],
                                                w["fcwg"], s["cba"], s["cbg"], f"ffn_bwd_core{l}")
        dca, dcg = (jnp.moveaxis(d, 0, 1).reshape(SUBLANES, FFN_H) for d in (dca, dcg))
        g["ffn_conv_w"][l] = jnp.concatenate([dca[:FFN_CONV], dcg[:FFN_CONV]], axis=1)
        g["ffn_conv_b"][l] = jnp.concatenate([dca[FFN_CONV], dcg[FFN_CONV]])
        for half, dz in enumerate((dza, dzg)):
            slab["ffn_w_in"] = _mm_tn_into(slab["ffn_w_in"], s["u2"], dz, l, 2 * half, f"ffn_in_dw{l}_{half}",
                                           tk=D_MODEL, tn=2 * FFN_H // N_CHIPS)
        dh, dhb, dln = _rms_bwd(s["h1"], p["ln_ffn"][l], du2, dh, f"ffn_norm_bwd{l}")
        g["ln_ffn"][l] = dln[0]
        for half, y in enumerate((s["yg"], s["yl"])):
            slab["w_out"] = _mm_tn_into(slab["w_out"], y, dhb, l, 2 * half, f"out_dw{l}_{half}",
                                        tk=D_MODEL // N_CHIPS, tn=D_MODEL, tm=2048)
        dyc = _mm(dhb, w["woT"], None, F32, f"out_dx{l}")
        dpg, dw2, db2, dng = _gla_bwd(dyc, s["pg"], s["o_st"], s["s_st"], w["w2p"], p["gla_gate_b"][l],
                                      p["gla_norm"][l], f"gla_bwd{l}")
        dpl, dcw, dvec, dwa, dwx = _lru_bwd(dyc, s["plr"], s["hs"], w["lcw"], row(p["lru_conv_b"][l]), w["wa"],
                                            row(p["lru_ba"][l]), w["wx"], row(p["lru_bx"][l]),
                                            row(p["lru_lambda"][l]), f"lru_bwd{l}")
        g["gla_gate_w2"][l] = dw2[:GATE_RANK]
        g["gla_gate_b"][l] = db2[0]
        g["gla_norm"][l] = dng[0]
        g["lru_conv_w"][l] = dcw[:LRU_CONV]
        g["lru_conv_b"][l], g["lru_ba"][l], g["lru_bx"][l], g["lru_lambda"][l] = dvec[0], dvec[1], dvec[2], dvec[3]
        g["lru_wa"][l], g["lru_wx"][l] = _diag_blocks(dwa), _diag_blocks(dwx)
        dwg = _mm_tn(s["u"], dpg, f"proj_gla_dw{l}")
        dwl = _mm_tn(s["u"], dpl, f"proj_lru_dw{l}")
        g["w_in"][l] = jnp.concatenate([dwg[:, :n_gla + GATE_RANK], dwl], axis=1)
        du = _mm(dpg, w["wgT"], None, F32, f"proj_gla_dx{l}")
        du = _mm(dpl, w["wlT"], du, F32, f"proj_lru_dx{l}")
        dh, dhb, dln = _rms_bwd(s["h0"], p["ln_mix"][l], du, dh, f"mix_norm_bwd{l}")
        g["ln_mix"][l] = dln[0]
    grads = {k: jnp.stack(v) for k, v in g.items()}
    grads["w_in"] = _slabs_from_whole("w_in", grads["w_in"])
    grads.update(slab)
    grads["ln_final"] = d_ln_final[0]
    return loss, dh, grads


BIG = ("w_in", "w_out", "ffn_w_in", "ffn_w_down")
COL_SHARDED = ("w_in", "ffn_w_in", "gla_gate_w2", "lru_conv_w", "ffn_conv_w")
SMALL = ("ln_mix", "gla_gate_w2", "gla_gate_b", "gla_norm", "lru_conv_w", "lru_conv_b", "lru_wa", "lru_ba", "lru_wx",
         "lru_bx", "lru_lambda", "ln_ffn", "ffn_conv_w", "ffn_conv_b", "ln_final")
WEIGHTS = ("ln_mix", "w_in", "gla_gate_w2", "gla_gate_b", "gla_norm", "lru_conv_w", "lru_conv_b", "lru_wa", "lru_ba",
           "lru_wx", "lru_bx", "lru_lambda", "w_out", "ln_ffn", "ffn_w_in", "ffn_conv_w", "ffn_conv_b", "ffn_w_down",
           "ln_final")
PACK = SUBLANES * LANES


def _whole_from_shards(name, g):
    if name in COL_SHARDED:
        return jnp.moveaxis(g, 0, -2).reshape(g.shape[1:-1] + (N_CHIPS * g.shape[-1],))
    return jnp.moveaxis(g, 0, 1).reshape((g.shape[1], N_CHIPS * g.shape[2]) + g.shape[3:])


def _slabs_from_whole(name, w):
    L, r, c = w.shape
    if name in COL_SHARDED:
        s = jnp.moveaxis(w.reshape(L, r, N_CHIPS, c // N_CHIPS), 2, 0)
    else:
        s = jnp.moveaxis(w.reshape(L, N_CHIPS, r // N_CHIPS, c), 1, 0)
    return s.reshape(N_CHIPS, -1, s.shape[-1])


def _pack(arrs):
    flat = []
    for a in arrs:
        f = a.reshape(-1)
        flat.append(jnp.pad(f, (0, (-f.shape[0]) % PACK)))
    return jnp.concatenate(flat).reshape(-1, LANES)


def _unpack(packed, shapes):
    out, at = [], 0
    flat = packed.reshape(-1)
    for s in shapes:
        size = math.prod(s)
        out.append(flat[at:at + size].reshape(s))
        at += size + (-size) % PACK
    return out


def kernel(x, ln_mix, w_in, gla_gate_w2, gla_gate_b, gla_norm, lru_conv_w, lru_conv_b, lru_wa, lru_ba, lru_wx, lru_bx, lru_lambda, w_out, ln_ffn, ffn_w_in, ffn_conv_w, ffn_conv_b, ffn_w_down, ln_final, loss_target, m_ln_mix, m_w_in, m_gla_gate_w2, m_gla_gate_b, m_gla_norm, m_lru_conv_w, m_lru_conv_b, m_lru_wa, m_lru_ba, m_lru_wx, m_lru_bx, m_lru_lambda, m_w_out, m_ln_ffn, m_ffn_w_in, m_ffn_conv_w, m_ffn_conv_b, m_ffn_w_down, m_ln_final, v_ln_mix, v_w_in, v_gla_gate_w2, v_gla_gate_b, v_gla_norm, v_lru_conv_w, v_lru_conv_b, v_lru_wa, v_lru_ba, v_lru_wx, v_lru_bx, v_lru_lambda, v_w_out, v_ln_ffn, v_ffn_w_in, v_ffn_conv_w, v_ffn_conv_b, v_ffn_w_down, v_ln_final):
    w = dict(ln_mix=ln_mix, w_in=w_in, gla_gate_w2=gla_gate_w2, gla_gate_b=gla_gate_b, gla_norm=gla_norm,
             lru_conv_w=lru_conv_w, lru_conv_b=lru_conv_b, lru_wa=lru_wa, lru_ba=lru_ba, lru_wx=lru_wx, lru_bx=lru_bx,
             lru_lambda=lru_lambda, w_out=w_out, ln_ffn=ln_ffn, ffn_w_in=ffn_w_in, ffn_conv_w=ffn_conv_w,
             ffn_conv_b=ffn_conv_b, ffn_w_down=ffn_w_down, ln_final=ln_final)
    m = dict(ln_mix=m_ln_mix, w_in=m_w_in, gla_gate_w2=m_gla_gate_w2, gla_gate_b=m_gla_gate_b, gla_norm=m_gla_norm,
             lru_conv_w=m_lru_conv_w, lru_conv_b=m_lru_conv_b, lru_wa=m_lru_wa, lru_ba=m_lru_ba, lru_wx=m_lru_wx,
             lru_bx=m_lru_bx, lru_lambda=m_lru_lambda, w_out=m_w_out, ln_ffn=m_ln_ffn, ffn_w_in=m_ffn_w_in,
             ffn_conv_w=m_ffn_conv_w, ffn_conv_b=m_ffn_conv_b, ffn_w_down=m_ffn_w_down, ln_final=m_ln_final)
    v = dict(ln_mix=v_ln_mix, w_in=v_w_in, gla_gate_w2=v_gla_gate_w2, gla_gate_b=v_gla_gate_b, gla_norm=v_gla_norm,
             lru_conv_w=v_lru_conv_w, lru_conv_b=v_lru_conv_b, lru_wa=v_lru_wa, lru_ba=v_lru_ba, lru_wx=v_lru_wx,
             lru_bx=v_lru_bx, lru_lambda=v_lru_lambda, w_out=v_w_out, ln_ffn=v_ln_ffn, ffn_w_in=v_ffn_w_in,
             ffn_conv_w=v_ffn_conv_w, ffn_conv_b=v_ffn_conv_b, ffn_w_down=v_ffn_w_down, ln_final=v_ln_final)

    sharded = BIG + ("gla_gate_w2", "lru_conv_w", "ffn_conv_w")
    gathered = _gather_shards([w[k].astype(MXU_DTYPE) if k in BIG else w[k] for k in sharded], "gather_weights")
    p = dict(w)
    for k, gk in zip(sharded, gathered):
        p[k] = _whole_from_shards(k, gk)

    loss, grad_x, grads = _local_step(x[0], loss_target[0], p)
    loss = lax.psum(loss[0, 0], ("x", "y", "c"))

    slabs = [grads[k] for k in BIG]
    kept, got = _split_with_sibling(slabs, "split_core_halves")
    chip_half = [_add_slabs(a, b, BF16, f"core_sum_{k}") for k, a, b in zip(BIG, kept, got)]
    small = _pack([grads[k] for k in SMALL])
    *recv, small_all = _exchange_grads(chip_half, small, "exchange_grads")
    big_g = _join_with_sibling([_sum_leading(r, f"chip_sum_{k}") for k, r in zip(BIG, recv)], "join_core_halves")
    small_sum = _unpack(_sum_leading(small_all, "sum_small_grads"), [grads[k].shape for k in SMALL])

    me = 2 * lax.axis_index("x") + lax.axis_index("y")
    out_g, out_d, out_m, out_v = {}, {}, {}, {}
    for k, gk in zip(BIG, big_g):
        shape = w[k].shape
        cols = shape[-1]
        res = _adamw(w[k].reshape(-1, cols), [gk], m[k].reshape(-1, cols), v[k].reshape(-1, cols), f"adamw_{k}")
        out_g[k], out_d[k], out_m[k], out_v[k] = [r.reshape(shape) for r in res]
    small_g = []
    for k, gk in zip(SMALL, small_sum):
        if k in COL_SHARDED:
            width = w[k].shape[-1]
            gk = lax.dynamic_slice_in_dim(gk, me * width, width, axis=gk.ndim - 1)
        small_g.append(gk)
    shapes = [w[k].shape for k in SMALL]
    res = _adamw(_pack([w[k] for k in SMALL]), [_pack(small_g)], _pack([m[k] for k in SMALL]),
                 _pack([v[k] for k in SMALL]), "adamw_small")
    for out, packed in zip((out_g, out_d, out_m, out_v), res):
        for k, a in zip(SMALL, _unpack(packed, shapes)):
            out[k] = a
    return (loss, grad_x[None], *[out_g[k] for k in WEIGHTS], *[out_d[k] for k in WEIGHTS],
            *[out_m[k] for k in WEIGHTS], *[out_v[k] for k in WEIGHTS])
```

```python
import math

import jax
import jax.numpy as jnp
from jax import lax
from jax.experimental import pallas as pl
from jax.experimental.pallas import tpu as pltpu

F32 = jnp.float32
BF16 = jnp.bfloat16
MXU_DTYPE = BF16

D_MODEL = 1024
DEPTH = 4
HEADS, DK, DV, CHUNK, GATE_RANK = 4, 64, 128, 64, 16
QK_W = HEADS * DK
GLA_W = HEADS * DV
LRU_W = 512
LRU_BLOCKS, LRU_BLOCK, LRU_CONV, LRU_C = 8, 64, 4, 8.0
FFN_H = 3 * D_MODEL
FFN_CONV = 3
EPS = 1e-6
GATE_PAD = 128
GLA_COLS = 2 * QK_W + 2 * GLA_W + GATE_PAD
LRU_COLS = 2 * LRU_W
ADAM_LR, ADAM_B1, ADAM_B2, ADAM_EPS, ADAM_WD, ADAM_STEP = 0.001, 0.9, 0.999, 1e-08, 0.01, 10

LANES = 128
SUBLANES = 8
VMEM_LIMIT = 56 * 1024 * 1024
ROWS = 512
TILE_BYTES = 1 << 20
FFN_CB = 512
FFN_CW = 1024
FFN_SB = 256
FFN_RC = 32
N_CHIPS = 4
N_DEV = 8
MESH = pl.DeviceIdType.MESH


def _cp(*sem):
    return pltpu.CompilerParams(dimension_semantics=sem, vmem_limit_bytes=VMEM_LIMIT)


def _dot(a, b):
    return jnp.dot(a.astype(MXU_DTYPE), b.astype(MXU_DTYPE), preferred_element_type=F32)


def _dot_nt(a, b):
    return lax.dot_general(a.astype(MXU_DTYPE), b.astype(MXU_DTYPE), (((1,), (1,)), ((), ())),
                           preferred_element_type=F32)


def _dot_tn(a, b):
    return lax.dot_general(a.astype(MXU_DTYPE), b.astype(MXU_DTYPE), (((0,), (0,)), ((), ())),
                           preferred_element_type=F32)


def _bdot(eq, a, b):
    return jnp.einsum(eq, a, b, preferred_element_type=F32)


def _split3(x):
    x1 = x.astype(BF16)
    r1 = x - x1.astype(F32)
    x2 = r1.astype(BF16)
    x3 = (r1 - x2.astype(F32)).astype(BF16)
    return x1, x2, x3


def _gelu(x):
    c = math.sqrt(2.0 / math.pi)
    return x * (0.5 * (1.0 + jnp.tanh(c * (x + 0.044715 * (x * x * x)))))


def _gelu_and_grad(x):
    c = math.sqrt(2.0 / math.pi)
    t = jnp.tanh(c * (x + 0.044715 * (x * x * x)))
    cdf = 0.5 * (1.0 + t)
    dcdf = 0.5 * (1.0 - t * t) * (c * (1.0 + 3.0 * 0.044715 * (x * x)))
    return x * cdf, cdf + x * dcdf


def _expm1(x):
    small = x * (1.0 + x * (0.5 + x * (1.0 / 6.0 + x * (1.0 / 24.0 + x * (1.0 / 120.0)))))
    return jnp.where(jnp.abs(x) < 0.1, small, jnp.exp(x) - 1.0)


def _shift_down(x, k, fill):
    row = lax.broadcasted_iota(jnp.int32, x.shape, 0)
    return jnp.where(row >= k, pltpu.roll(x, k, axis=0), fill)


def _shift_up(x, k, fill):
    n = x.shape[0]
    row = lax.broadcasted_iota(jnp.int32, x.shape, 0)
    return jnp.where(row < n - k, pltpu.roll(x, n - k, axis=0), fill)


def _rms_fwd(h, g, name):
    T, D = h.shape
    R = min(T, ROWS)

    def body(h_ref, g_ref, o_ref):
        x = h_ref[...]
        r = lax.rsqrt(jnp.mean(x * x, axis=-1, keepdims=True) + EPS)
        o_ref[...] = ((x * r) * g_ref[...]).astype(o_ref.dtype)

    return pl.pallas_call(
        body, name=name, grid=(T // R,),
        in_specs=[pl.BlockSpec((R, D), lambda i: (i, 0)), pl.BlockSpec((1, D), lambda i: (0, 0))],
        out_specs=pl.BlockSpec((R, D), lambda i: (i, 0)),
        out_shape=jax.ShapeDtypeStruct((T, D), BF16), compiler_params=_cp("parallel"),
    )(h, g.reshape(1, D))


def _rms_bwd(h, g, du, dres, name):
    T, D = h.shape
    R = min(T, ROWS)

    def body(h_ref, g_ref, du_ref, dres_ref, dh_ref, dhb_ref, dg_ref):
        @pl.when(pl.program_id(0) == 0)
        def _():
            dg_ref[...] = jnp.zeros_like(dg_ref)

        x = h_ref[...]
        r = lax.rsqrt(jnp.mean(x * x, axis=-1, keepdims=True) + EPS)
        xhat = x * r
        dy = du_ref[...].astype(F32)
        dg_ref[...] += jnp.sum(dy * xhat, axis=0, keepdims=True)
        dxhat = dy * g_ref[...]
        dx = r * (dxhat - xhat * jnp.mean(dxhat * xhat, axis=-1, keepdims=True))
        dh = dres_ref[...] + dx
        dh_ref[...] = dh
        dhb_ref[...] = dh.astype(dhb_ref.dtype)

    blk = pl.BlockSpec((R, D), lambda i: (i, 0))
    vec = pl.BlockSpec((1, D), lambda i: (0, 0))
    return pl.pallas_call(
        body, name=name, grid=(T // R,), in_specs=[blk, vec, blk, blk], out_specs=[blk, blk, vec],
        out_shape=[jax.ShapeDtypeStruct((T, D), F32), jax.ShapeDtypeStruct((T, D), BF16),
                   jax.ShapeDtypeStruct((1, D), F32)],
        compiler_params=_cp("arbitrary"),
    )(h, g.reshape(1, D), du, dres)


def _loss_head(h, g, tgt, name):
    T, D = h.shape
    R = min(T, ROWS)

    def body(h_ref, g_ref, t_ref, loss_ref, dh_ref, dhb_ref, dg_ref):
        @pl.when(pl.program_id(0) == 0)
        def _():
            dg_ref[...] = jnp.zeros_like(dg_ref)
            loss_ref[...] = jnp.zeros_like(loss_ref)

        x = h_ref[...]
        r = lax.rsqrt(jnp.mean(x * x, axis=-1, keepdims=True) + EPS)
        xhat = x * r
        gg = g_ref[...]
        err = xhat * gg - t_ref[...]
        loss_ref[...] += 0.5 * jnp.sum(jnp.mean(err * err, axis=-1, keepdims=True), axis=0, keepdims=True)
        dy = err * (1.0 / D)
        dg_ref[...] += jnp.sum(dy * xhat, axis=0, keepdims=True)
        dxhat = dy * gg
        dh = r * (dxhat - xhat * jnp.mean(dxhat * xhat, axis=-1, keepdims=True))
        dh_ref[...] = dh
        dhb_ref[...] = dh.astype(dhb_ref.dtype)

    blk = pl.BlockSpec((R, D), lambda i: (i, 0))
    vec = pl.BlockSpec((1, D), lambda i: (0, 0))
    one = pl.BlockSpec((1, LANES), lambda i: (0, 0))
    return pl.pallas_call(
        body, name=name, grid=(T // R,), in_specs=[blk, vec, blk], out_specs=[one, blk, blk, vec],
        out_shape=[jax.ShapeDtypeStruct((1, LANES), F32), jax.ShapeDtypeStruct((T, D), F32),
                   jax.ShapeDtypeStruct((T, D), BF16), jax.ShapeDtypeStruct((1, D), F32)],
        compiler_params=_cp("arbitrary"),
    )(h, g.reshape(1, D), tgt)


def _mm(a, b, res, out_dtype, name, tm=512, tn=None):
    M, K = a.shape
    N = b.shape[1]
    tm = min(tm, M)
    tn = N if tn is None else tn

    def body(*refs):
        if res is None:
            a_ref, b_ref, o_ref = refs
            acc = _dot(a_ref[...], b_ref[...])
        else:
            a_ref, b_ref, r_ref, o_ref = refs
            acc = r_ref[...].astype(F32) + _dot(a_ref[...], b_ref[...])
        o_ref[...] = acc.astype(o_ref.dtype)

    in_specs = [pl.BlockSpec((tm, K), lambda j, i: (i, 0)), pl.BlockSpec((K, tn), lambda j, i: (0, j))]
    args = [a, b]
    if res is not None:
        in_specs.append(pl.BlockSpec((tm, tn), lambda j, i: (i, j)))
        args.append(res)
    return pl.pallas_call(
        body, name=name, grid=(N // tn, M // tm), in_specs=in_specs,
        out_specs=pl.BlockSpec((tm, tn), lambda j, i: (i, j)),
        out_shape=jax.ShapeDtypeStruct((M, N), out_dtype), compiler_params=_cp("parallel", "parallel"),
    )(*args)


def _mm_tn_into(slab, a, b, layer, chip0, name, tk, tn, tm=1024):
    M, K = a.shape
    N = b.shape[1]
    tm = min(tm, M)
    assert slab.shape[2] == tn and (K // tk == 1 or N // tn == 1)

    def body(a_ref, b_ref, slab_ref, o_ref):
        del slab_ref

        @pl.when(pl.program_id(2) == 0)
        def _():
            o_ref[...] = jnp.zeros_like(o_ref)

        o_ref[0] += _dot_tn(a_ref[...], b_ref[...])

    return pl.pallas_call(
        body, name=name, grid=(K // tk, N // tn, M // tm),
        in_specs=[pl.BlockSpec((tm, tk), lambda k, n, m: (m, k)), pl.BlockSpec((tm, tn), lambda k, n, m: (m, n)),
                  pl.BlockSpec(memory_space=pl.ANY)],
        out_specs=pl.BlockSpec((1, tk, tn), lambda k, n, m: (chip0 + k + n, layer, 0)),
        out_shape=jax.ShapeDtypeStruct(slab.shape, F32), input_output_aliases={2: 0},
        compiler_params=_cp("parallel", "parallel", "arbitrary"),
    )(a, b, slab)


def _mm_tn(a, b, name, tm=2048, tk=None, tn=None):
    M, K = a.shape
    N = b.shape[1]
    tm = min(tm, M)
    tk = K if tk is None else tk
    tn = N if tn is None else tn

    def body(a_ref, b_ref, o_ref):
        @pl.when(pl.program_id(2) == 0)
        def _():
            o_ref[...] = jnp.zeros_like(o_ref)

        o_ref[...] += _dot_tn(a_ref[...], b_ref[...])

    return pl.pallas_call(
        body, name=name, grid=(K // tk, N // tn, M // tm),
        in_specs=[pl.BlockSpec((tm, tk), lambda k, n, m: (m, k)), pl.BlockSpec((tm, tn), lambda k, n, m: (m, n))],
        out_specs=pl.BlockSpec((tk, tn), lambda k, n, m: (k, n)),
        out_shape=jax.ShapeDtypeStruct((K, N), F32), compiler_params=_cp("parallel", "parallel", "arbitrary"),
    )(a, b)


def _same_chunk(row, col):
    shift = CHUNK.bit_length() - 1
    return jnp.right_shift(row, shift) == jnp.right_shift(col, shift)


def _gla_common(q, k, glr, w2, b2, R):
    gl = _dot(glr, w2) + b2
    la = jax.nn.log_sigmoid(gl) * (1.0 / 16.0)
    row = lax.broadcasted_iota(jnp.int32, (R, R), 0)
    col = lax.broadcasted_iota(jnp.int32, (R, R), 1)
    same = _same_chunk(row, col)
    m_tri = (same & (col <= row)).astype(BF16)
    m_all = same.astype(BF16)
    la3 = _split3(la)
    b = sum(jnp.dot(m_tri, p, preferred_element_type=F32) for p in la3)
    bl = sum(jnp.dot(m_all, p, preferred_element_type=F32) for p in la3)
    eb = jnp.exp(b)
    enb = jnp.exp(-b)
    ek = jnp.exp(bl - b)
    qi = (q * (DK ** -0.5)) * eb
    ki = k * enb
    kd = k * ek
    return gl, la3, eb, enb, ek, qi, ki, kd


def _bsplit(x, n):
    return x.reshape(n, CHUNK, x.shape[-1])


def _tril():
    return (lax.broadcasted_iota(jnp.int32, (CHUNK, CHUNK), 1)
            <= lax.broadcasted_iota(jnp.int32, (CHUNK, CHUNK), 0))[None]


def _gla_fwd(proj, w2p, b2, norm_g, name):
    T = proj.shape[0]
    R = min(T, ROWS)
    n = R // CHUNK

    def body(q_ref, k_ref, v_ref, g_ref, a_ref, w2_ref, b2_ref, ng_ref, y_ref, o_ref, st_ref, s_ref):
        @pl.when(pl.program_id(0) == 0)
        def _():
            s_ref[...] = jnp.zeros_like(s_ref)

        _, la3, _, _, _, qi, ki, kd = _gla_common(q_ref[...], k_ref[...], a_ref[...], w2_ref[...], b2_ref[...], R)
        tril = _tril()
        ones = jnp.ones((n, CHUNK, DV), BF16)
        for h in range(HEADS):
            sl = slice(h * DK, (h + 1) * DK)
            sv = slice(h * DV, (h + 1) * DV)
            qh = _bsplit(qi[:, sl], n).astype(MXU_DTYPE)
            kh = _bsplit(ki[:, sl], n).astype(MXU_DTYPE)
            kdh = _bsplit(kd[:, sl], n).astype(MXU_DTYPE)
            vh = _bsplit(v_ref[:, sv], n).astype(MXU_DTYPE)
            att = jnp.where(tril, _bdot('ncd,nsd->ncs', qh, kh), 0.0)
            upd = _bdot('ncd,nce->nde', kdh, vh)
            dect = jnp.exp(sum(_bdot('ncd,nce->nde', _bsplit(p[:, sl], n), ones) for p in la3))
            s = s_ref[sl, :]
            for c in range(n):
                st_ref[c, sl, :] = s
                s = dect[c] * s + upd[c]
            s_ref[sl, :] = s
            sp = st_ref[:, sl, :].astype(MXU_DTYPE)
            o = (_bdot('ncs,nse->nce', att.astype(MXU_DTYPE), vh) + _bdot('ncd,nde->nce', qh, sp)).reshape(R, DV)
            o_ref[:, sv] = o
            r = lax.rsqrt(jnp.mean(o * o, axis=-1, keepdims=True) + EPS)
            gate = g_ref[:, sv]
            y_ref[:, sv] = (((o * r) * ng_ref[...]) * (gate * jax.nn.sigmoid(gate))).astype(y_ref.dtype)

    cb = lambda w, j: pl.BlockSpec((R, w), lambda i: (i, j))
    full = lambda s: pl.BlockSpec(s, lambda i: (0,) * len(s))
    return pl.pallas_call(
        body, name=name, grid=(T // R,),
        in_specs=[cb(QK_W, 0), cb(QK_W, 1), cb(GLA_W, 1), cb(GLA_W, 2), cb(GATE_PAD, 12),
                  full((GATE_PAD, QK_W)), full((1, QK_W)), full((1, DV))],
        out_specs=[pl.BlockSpec((R, GLA_W), lambda i: (i, 0)), pl.BlockSpec((R, GLA_W), lambda i: (i, 0)),
                   pl.BlockSpec((n, QK_W, DV), lambda i: (i, 0, 0))],
        out_shape=[jax.ShapeDtypeStruct((T, GLA_W), BF16), jax.ShapeDtypeStruct((T, GLA_W), F32),
                   jax.ShapeDtypeStruct((T // CHUNK, QK_W, DV), F32)],
        scratch_shapes=[pltpu.VMEM((QK_W, DV), F32)],
        compiler_params=_cp("arbitrary"),
    )(proj, proj, proj, proj, proj, w2p, b2.reshape(1, QK_W), norm_g.reshape(1, DV))


def _gla_bwd(dy, proj, o_st, s_st, w2p, b2, norm_g, name):
    T = proj.shape[0]
    R = min(T, ROWS)
    n = R // CHUNK
    nb = T // R

    def body(dy_ref, q_ref, k_ref, v_ref, g_ref, a_ref, o_ref, st_ref, w2_ref, b2_ref, ng_ref,
             dp_ref, dw2_ref, db2_ref, dng_ref, gs_ref, gn_ref, db_ref, dbl_ref):
        @pl.when(pl.program_id(0) == 0)
        def _():
            gs_ref[...] = jnp.zeros_like(gs_ref)
            dw2_ref[...] = jnp.zeros_like(dw2_ref)
            db2_ref[...] = jnp.zeros_like(db2_ref)
            dng_ref[...] = jnp.zeros_like(dng_ref)

        glr = a_ref[...]
        gl, la3, eb, enb, ek, qi, ki, kd = _gla_common(q_ref[...], k_ref[...], glr, w2_ref[...], b2_ref[...], R)
        tril = _tril()
        ones = jnp.ones((n, CHUNK, DV), BF16)
        ng = ng_ref[...]
        dng = jnp.zeros((1, DV), F32)
        for h in range(HEADS):
            sl = slice(h * DK, (h + 1) * DK)
            sv = slice(h * DV, (h + 1) * DV)
            o = o_ref[:, sv]
            r = lax.rsqrt(jnp.mean(o * o, axis=-1, keepdims=True) + EPS)
            xhat = o * r
            gate = g_ref[:, sv]
            sg = jax.nn.sigmoid(gate)
            dyh = dy_ref[:, sv].astype(F32)
            dp_ref[:, 2 * QK_W + GLA_W + h * DV:2 * QK_W + GLA_W + (h + 1) * DV] = (
                dyh * (xhat * ng) * (sg * (1.0 + gate * (1.0 - sg)))).astype(dp_ref.dtype)
            don = dyh * (gate * sg)
            dng = dng + jnp.sum(don * xhat, axis=0, keepdims=True)
            dxhat = don * ng
            do = r * (dxhat - xhat * jnp.mean(dxhat * xhat, axis=-1, keepdims=True))
            qf = _bsplit(qi[:, sl], n)
            kf = _bsplit(ki[:, sl], n)
            kdf = _bsplit(kd[:, sl], n)
            qh, kh, kdh = qf.astype(MXU_DTYPE), kf.astype(MXU_DTYPE), kdf.astype(MXU_DTYPE)
            vh = _bsplit(v_ref[:, sv], n).astype(MXU_DTYPE)
            doh = _bsplit(do, n).astype(MXU_DTYPE)
            spf = st_ref[:, sl, :]
            sp = spf.astype(MXU_DTYPE)
            att = jnp.where(tril, _bdot('ncd,nsd->ncs', qh, kh), 0.0).astype(MXU_DTYPE)
            datt = jnp.where(tril, _bdot('nce,nse->ncs', doh, vh), 0.0).astype(MXU_DTYPE)
            dv = _bdot('ncs,nce->nse', att, doh)
            dqi = _bdot('ncs,nsd->ncd', datt, kh) + _bdot('nce,nde->ncd', doh, sp)
            dki = _bdot('ncs,ncd->nsd', datt, qh)
            wgt = _bdot('ncd,nce->nde', qh, doh)
            dect = jnp.exp(sum(_bdot('ncd,nce->nde', _bsplit(p[:, sl], n), ones) for p in la3))
            g = gs_ref[sl, :]
            for c in reversed(range(n)):
                gn_ref[c] = g
                g = wgt[c] + dect[c] * g
            gs_ref[sl, :] = g
            gnf = gn_ref[...]
            gn = gnf.astype(MXU_DTYPE)
            dkd = _bdot('nce,nde->ncd', vh, gn)
            dv = dv + _bdot('ncd,nde->nce', kdh, gn)
            dp_ref[:, 2 * QK_W + h * DV:2 * QK_W + (h + 1) * DV] = dv.reshape(R, DV).astype(dp_ref.dtype)
            dbl = sum(_bdot('nce,nde->ncd', ones, p) for p in _split3(gnf * spf * dect))
            pk = dkd * kdf
            dbl = dbl + jnp.sum(pk, axis=1, keepdims=True)
            dbl_ref[:, sl] = dbl.reshape(R, DK)
            db_ref[:, sl] = (dqi * qf - dki * kf - pk).reshape(R, DK)
            dp_ref[:, sl] = ((dqi.reshape(R, DK) * (DK ** -0.5)) * eb[:, sl]).astype(dp_ref.dtype)
            dp_ref[:, QK_W + h * DK:QK_W + (h + 1) * DK] = (
                dki.reshape(R, DK) * enb[:, sl] + dkd.reshape(R, DK) * ek[:, sl]).astype(dp_ref.dtype)
        dng_ref[...] += dng
        row = lax.broadcasted_iota(jnp.int32, (R, R), 0)
        col = lax.broadcasted_iota(jnp.int32, (R, R), 1)
        m_rev = (_same_chunk(row, col) & (col >= row)).astype(BF16)
        dla = sum(jnp.dot(m_rev, p, preferred_element_type=F32) for p in _split3(db_ref[...])) + dbl_ref[...]
        dgl = (dla * (1.0 / 16.0)) * jax.nn.sigmoid(-gl)
        dp_ref[:, 2 * QK_W + 2 * GLA_W:GLA_COLS] = _dot_nt(dgl, w2_ref[...]).astype(dp_ref.dtype)
        dw2_ref[...] += _dot_tn(glr, dgl)
        db2_ref[...] += jnp.sum(dgl, axis=0, keepdims=True)

    cb = lambda w, j: pl.BlockSpec((R, w), lambda i: (nb - 1 - i, j))
    full = lambda s: pl.BlockSpec(s, lambda i: (0,) * len(s))
    return pl.pallas_call(
        body, name=name, grid=(nb,),
        in_specs=[cb(GLA_W, 0), cb(QK_W, 0), cb(QK_W, 1), cb(GLA_W, 1), cb(GLA_W, 2), cb(GATE_PAD, 12),
                  cb(GLA_W, 0), pl.BlockSpec((n, QK_W, DV), lambda i: (nb - 1 - i, 0, 0)),
                  full((GATE_PAD, QK_W)), full((1, QK_W)), full((1, DV))],
        out_specs=[pl.BlockSpec((R, GLA_COLS), lambda i: (nb - 1 - i, 0)),
                   full((GATE_PAD, QK_W)), full((1, QK_W)), full((1, DV))],
        out_shape=[jax.ShapeDtypeStruct((T, GLA_COLS), BF16), jax.ShapeDtypeStruct((GATE_PAD, QK_W), F32),
                   jax.ShapeDtypeStruct((1, QK_W), F32), jax.ShapeDtypeStruct((1, DV), F32)],
        scratch_shapes=[pltpu.VMEM((QK_W, DV), F32), pltpu.VMEM((n, DK, DV), F32),
                        pltpu.VMEM((R, QK_W), F32), pltpu.VMEM((R, QK_W), F32)],
        compiler_params=_cp("arbitrary"),
    )(dy, proj, proj, proj, proj, proj, o_st, s_st, w2p, b2.reshape(1, QK_W), norm_g.reshape(1, DV))


def _scan_scratch(R, W):
    return [pltpu.VMEM((W // LANES, R, LANES), F32), pltpu.VMEM((W // LANES, R, LANES), F32),
            pltpu.VMEM((W // LANES, R // SUBLANES, LANES), F32)]


def _scan_rows(a, u, c0, a_ref, u_ref, c_ref, out_ref, reverse):
    R, W = a.shape
    nt = R // SUBLANES
    shift = _shift_up if reverse else _shift_down
    a, u = a.reshape(nt, SUBLANES, W), u.reshape(nt, SUBLANES, W)
    sub = lax.broadcasted_iota(jnp.int32, (nt, SUBLANES, W), 1)
    for k in (1, 2, 4):
        inside = (sub < SUBLANES - k) if reverse else (sub >= k)
        turn = SUBLANES - k if reverse else k
        u = u + a * jnp.where(inside, pltpu.roll(u, turn, axis=1), 0.0)
        a = a * jnp.where(inside, pltpu.roll(a, turn, axis=1), 1.0)
    a, u = a.reshape(R, W), u.reshape(R, W)
    end = 0 if reverse else SUBLANES - 1
    edge = nt - 1 if reverse else 0
    for j in range(W // LANES):
        cols = slice(j * LANES, (j + 1) * LANES)
        a_ref[j] = a[:, cols]
        u_ref[j] = u[:, cols]
        at = a_ref.at[j][pl.ds(end, nt, stride=SUBLANES), :]
        ut = u_ref.at[j][pl.ds(end, nt, stride=SUBLANES), :]
        k = 1
        while k < nt:
            ut = ut + at * shift(ut, k, 0.0)
            at = at * shift(at, k, 1.0)
            k *= 2
        c_ref[j] = shift(ut + at * c0[:, cols], 1, 0.0)
        c_ref[j, edge:edge + 1, :] = c0[:, cols]
        for r in range(nt):
            rows = pl.ds(r * SUBLANES, SUBLANES)
            out_ref[rows, cols] = u_ref[j, rows, :] + a_ref[j, rows, :] * c_ref[j, r:r + 1, :]


def _lru_conv(ext_ref, cw_ref, cb_ref, R):
    xc = cb_ref[...] + ext_ref[pl.ds(SUBLANES - 3, R), :] * cw_ref[0:1, :]
    xc = xc + ext_ref[pl.ds(SUBLANES - 2, R), :] * cw_ref[1:2, :]
    xc = xc + ext_ref[pl.ds(SUBLANES - 1, R), :] * cw_ref[2:3, :]
    return xc + ext_ref[pl.ds(SUBLANES, R), :] * cw_ref[3:4, :]


def _lru_gates(xc, wa, ba, wx, bx, lam, first):
    r = jax.nn.sigmoid(_dot(xc, wa) + ba)
    ig = jax.nn.sigmoid(_dot(xc, wx) + bx)
    sp = jax.nn.softplus(-lam)
    la = (-LRU_C * r) * sp
    a = jnp.exp(la)
    mult = jnp.where(first, 1.0, jnp.sqrt(-_expm1(2.0 * la)))
    return r, ig, sp, a, mult


def _lru_fwd(proj, cw, cb, wa, ba, wx, bx, lam, name):
    T = proj.shape[0]
    R = min(T, ROWS)
    W = LRU_W

    def body(xr_ref, xh_ref, xg_ref, cw_ref, cb_ref, wa_ref, ba_ref, wx_ref, bx_ref, lam_ref,
             y_ref, hs_ref, ext_ref, hc_ref, sa_ref, su_ref, sc_ref):
        i = pl.program_id(0)

        @pl.when(i == 0)
        def _():
            hc_ref[...] = jnp.zeros_like(hc_ref)

        ext_ref[0:SUBLANES, :] = jnp.where(i > 0, xh_ref[...], 0.0)
        ext_ref[pl.ds(SUBLANES, R), :] = xr_ref[...]
        xc = _lru_conv(ext_ref, cw_ref, cb_ref, R)
        row = lax.broadcasted_iota(jnp.int32, (R, W), 0)
        first = (row == 0) & (i == 0)
        _, ig, _, a, mult = _lru_gates(xc, wa_ref[...], ba_ref[...], wx_ref[...], bx_ref[...], lam_ref[...], first)
        _scan_rows(a, mult * (ig * xc), hc_ref[0:1, :], sa_ref, su_ref, sc_ref, hs_ref, reverse=False)
        hc_ref[0:1, :] = hs_ref[R - 1:R, :]
        y_ref[...] = (hs_ref[...] * _gelu(xg_ref[...])).astype(y_ref.dtype)

    rb = R // SUBLANES
    full = lambda s: pl.BlockSpec(s, lambda i: (0,) * len(s))
    return pl.pallas_call(
        body, name=name, grid=(T // R,),
        in_specs=[pl.BlockSpec((R, W), lambda i: (i, 0)),
                  pl.BlockSpec((SUBLANES, W), lambda i: (jnp.maximum(i * rb - 1, 0), 0)),
                  pl.BlockSpec((R, W), lambda i: (i, 1)),
                  full((SUBLANES, W)), full((1, W)), full((W, W)), full((1, W)), full((W, W)), full((1, W)),
                  full((1, W))],
        out_specs=[pl.BlockSpec((R, W), lambda i: (i, 0)), pl.BlockSpec((R, W), lambda i: (i, 0))],
        out_shape=[jax.ShapeDtypeStruct((T, W), BF16), jax.ShapeDtypeStruct((T, W), F32)],
        scratch_shapes=[pltpu.VMEM((R + SUBLANES, W), F32), pltpu.VMEM((SUBLANES, W), F32),
                        *_scan_scratch(R, W)],
        compiler_params=_cp("arbitrary"),
    )(proj, proj, proj, cw, cb, wa, ba, wx, bx, lam)


def _lru_bwd(dy, proj, hs, cw, cb, wa, ba, wx, bx, lam, name):
    T = proj.shape[0]
    R = min(T, ROWS)
    W = LRU_W
    nb = T // R

    def body(dy_ref, xr_ref, xh_ref, xg_ref, hs_ref, hh_ref, cw_ref, cb_ref, wa_ref, ba_ref, wx_ref, bx_ref, lam_ref,
             dp_ref, dcw_ref, dvec_ref, dwa_ref, dwx_ref, ext_ref, ext2_ref, lc_ref, sa_ref, su_ref, sc_ref, adj_ref):
        ib = pl.program_id(0)
        i = nb - 1 - ib

        @pl.when(ib == 0)
        def _():
            lc_ref[...] = jnp.zeros_like(lc_ref)
            ext2_ref[pl.ds(R, SUBLANES), :] = jnp.zeros((SUBLANES, W), F32)
            dcw_ref[...] = jnp.zeros_like(dcw_ref)
            dvec_ref[...] = jnp.zeros_like(dvec_ref)
            dwa_ref[...] = jnp.zeros_like(dwa_ref)
            dwx_ref[...] = jnp.zeros_like(dwx_ref)

        ext_ref[0:SUBLANES, :] = jnp.where(i > 0, xh_ref[...], 0.0)
        ext_ref[pl.ds(SUBLANES, R), :] = xr_ref[...]
        xc = _lru_conv(ext_ref, cw_ref, cb_ref, R)
        row = lax.broadcasted_iota(jnp.int32, (R, W), 0)
        first = (row == 0) & (i == 0)
        lam = lam_ref[...]
        r, ig, sp, a, mult = _lru_gates(xc, wa_ref[...], ba_ref[...], wx_ref[...], bx_ref[...], lam, first)
        h = hs_ref[...]
        gel, dgel = _gelu_and_grad(xg_ref[...])
        dy = dy_ref[...].astype(F32)
        dp_ref[:, W:2 * W] = (dy * h * dgel).astype(dp_ref.dtype)
        _scan_rows(_shift_up(a, 1, 1.0), dy * gel, lc_ref[0:1, :], sa_ref, su_ref, sc_ref, adj_ref, reverse=True)
        v = adj_ref[...]
        lc_ref[...] = (a * v)[0:SUBLANES, :]
        hprev = _shift_down(h, 1, 0.0) + jnp.where((row == 0) & (i > 0), hh_ref[SUBLANES - 1:SUBLANES, :], 0.0)
        da = v * hprev
        dmult = jnp.where(first, 0.0, v * (ig * xc))
        dig = v * (mult * xc)
        dxc = v * (mult * ig)
        dla = da * a - dmult * ((a * a) / mult)
        dra = (dla * (-LRU_C * sp)) * (r * (1.0 - r))
        drx = dig * (ig * (1.0 - ig))
        dxc = dxc + _dot_nt(dra, wa_ref[...]) + _dot_nt(drx, wx_ref[...])
        dwa_ref[...] += _dot_tn(xc, dra)
        dwx_ref[...] += _dot_tn(xc, drx)
        dvec_ref[0:1, :] += jnp.sum(dxc, axis=0, keepdims=True)
        dvec_ref[1:2, :] += jnp.sum(dra, axis=0, keepdims=True)
        dvec_ref[2:3, :] += jnp.sum(drx, axis=0, keepdims=True)
        dvec_ref[3:4, :] += jnp.sum(dla * (-LRU_C * r), axis=0, keepdims=True) * (-jax.nn.sigmoid(-lam))
        ext2_ref[pl.ds(0, R), :] = dxc
        dxr = ext2_ref[pl.ds(0, R), :] * cw_ref[3:4, :]
        dxr = dxr + ext2_ref[pl.ds(1, R), :] * cw_ref[2:3, :]
        dxr = dxr + ext2_ref[pl.ds(2, R), :] * cw_ref[1:2, :]
        dxr = dxr + ext2_ref[pl.ds(3, R), :] * cw_ref[0:1, :]
        dp_ref[:, 0:W] = dxr.astype(dp_ref.dtype)
        for j in range(LRU_CONV):
            dcw_ref[j:j + 1, :] += jnp.sum(dxc * ext_ref[pl.ds(SUBLANES - 3 + j, R), :], axis=0, keepdims=True)
        ext2_ref[pl.ds(R, SUBLANES), :] = dxc[0:SUBLANES, :]

    rb = R // SUBLANES
    full = lambda s: pl.BlockSpec(s, lambda i: (0,) * len(s))
    blk = lambda j: pl.BlockSpec((R, W), lambda i: (nb - 1 - i, j))
    halo = pl.BlockSpec((SUBLANES, W), lambda i: (jnp.maximum((nb - 1 - i) * rb - 1, 0), 0))
    return pl.pallas_call(
        body, name=name, grid=(nb,),
        in_specs=[blk(1), blk(0), halo, blk(1), blk(0), halo,
                  full((SUBLANES, W)), full((1, W)), full((W, W)), full((1, W)), full((W, W)), full((1, W)),
                  full((1, W))],
        out_specs=[pl.BlockSpec((R, 2 * W), lambda i: (nb - 1 - i, 0)), full((SUBLANES, W)), full((SUBLANES, W)),
                   full((W, W)), full((W, W))],
        out_shape=[jax.ShapeDtypeStruct((T, 2 * W), BF16), jax.ShapeDtypeStruct((SUBLANES, W), F32),
                   jax.ShapeDtypeStruct((SUBLANES, W), F32), jax.ShapeDtypeStruct((W, W), F32),
                   jax.ShapeDtypeStruct((W, W), F32)],
        scratch_shapes=[pltpu.VMEM((R + SUBLANES, W), F32), pltpu.VMEM((R + SUBLANES, W), F32),
                        pltpu.VMEM((SUBLANES, W), F32), *_scan_scratch(R, W), pltpu.VMEM((R, W), F32)],
        compiler_params=_cp("arbitrary"),
    )(dy, proj, proj, proj, hs, hs, cw, cb, wa, ba, wx, bx, lam)


def _ffn_conv(ext_ref, cw_ref, cb_ref, n):
    z = cb_ref[...] + ext_ref[pl.ds(SUBLANES - 2, n), :] * cw_ref[0:1, :]
    z = z + ext_ref[pl.ds(SUBLANES - 1, n), :] * cw_ref[1:2, :]
    return z + ext_ref[pl.ds(SUBLANES, n), :] * cw_ref[2:3, :]


def _ffn_act_fwd(za, zg, cwa, cwg, cba, cbg, name):
    T, Fh = za.shape
    R = min(T, ROWS)
    CB = FFN_CB
    rb = R // SUBLANES

    def body(a_ref, ah_ref, g_ref, gh_ref, cwa_ref, cwg_ref, cba_ref, cbg_ref, o_ref, ea_ref, eg_ref):
        i = pl.program_id(1)
        ea_ref[0:SUBLANES, :] = jnp.where(i > 0, ah_ref[...], 0.0)
        ea_ref[pl.ds(SUBLANES, R), :] = a_ref[...]
        eg_ref[0:SUBLANES, :] = jnp.where(i > 0, gh_ref[...], 0.0)
        eg_ref[pl.ds(SUBLANES, R), :] = g_ref[...]
        o_ref[...] = (_gelu(_ffn_conv(ea_ref, cwa_ref, cba_ref, R)) * _ffn_conv(eg_ref, cwg_ref, cbg_ref, R)
                      ).astype(o_ref.dtype)

    blk = pl.BlockSpec((R, CB), lambda j, i: (i, j))
    halo = pl.BlockSpec((SUBLANES, CB), lambda j, i: (jnp.maximum(i * rb - 1, 0), j))
    w8 = pl.BlockSpec((SUBLANES, CB), lambda j, i: (0, j))
    w1 = pl.BlockSpec((1, CB), lambda j, i: (0, j))
    return pl.pallas_call(
        body, name=name, grid=(Fh // CB, T // R),
        in_specs=[blk, halo, blk, halo, w8, w8, w1, w1], out_specs=blk,
        out_shape=jax.ShapeDtypeStruct((T, Fh), BF16),
        scratch_shapes=[pltpu.VMEM((R + SUBLANES, CB), F32), pltpu.VMEM((R + SUBLANES, CB), F32)],
        compiler_params=_cp("parallel", "parallel"),
    )(za, za, zg, zg, cwa, cwg, cba, cbg)


def _ffn_act_bwd(dact, za, zg, cwa, cwg, cba, cbg, name):
    T, Fh = za.shape
    R = min(T, ROWS)
    CB = FFN_CB
    rb = R // SUBLANES
    nb = T // R
    RE = R + SUBLANES

    def body(d_ref, dn_ref, a_ref, ap_ref, an_ref, g_ref, gp_ref, gn_ref, cwa_ref, cwg_ref, cba_ref, cbg_ref,
             dza_ref, dzg_ref, dca_ref, dcg_ref, ea_ref, eg_ref, ed_ref, sa_ref, sg_ref):
        i = pl.program_id(1)

        @pl.when(i == 0)
        def _():
            dca_ref[...] = jnp.zeros_like(dca_ref)
            dcg_ref[...] = jnp.zeros_like(dcg_ref)

        for e_ref, m_ref, p_ref, n_ref in ((ea_ref, a_ref, ap_ref, an_ref), (eg_ref, g_ref, gp_ref, gn_ref)):
            e_ref[0:SUBLANES, :] = jnp.where(i > 0, p_ref[...], 0.0)
            e_ref[pl.ds(SUBLANES, R), :] = m_ref[...]
            e_ref[pl.ds(SUBLANES + R, SUBLANES), :] = n_ref[...]
        ed_ref[pl.ds(0, R), :] = d_ref[...].astype(F32)
        ed_ref[pl.ds(R, SUBLANES), :] = jnp.where(i < nb - 1, dn_ref[...].astype(F32), 0.0)
        za_c = _ffn_conv(ea_ref, cwa_ref, cba_ref, RE)
        zg_c = _ffn_conv(eg_ref, cwg_ref, cbg_ref, RE)
        gel, dgel = _gelu_and_grad(za_c)
        dact_e = ed_ref[...]
        sa_ref[...] = dact_e * zg_c * dgel
        sg_ref[...] = dact_e * gel
        for s_ref, e_ref, cw_ref, dz_ref, dc_ref in ((sa_ref, ea_ref, cwa_ref, dza_ref, dca_ref),
                                                     (sg_ref, eg_ref, cwg_ref, dzg_ref, dcg_ref)):
            dz = s_ref[pl.ds(0, R), :]
            dzp = dz * cw_ref[2:3, :] + s_ref[pl.ds(1, R), :] * cw_ref[1:2, :] + s_ref[pl.ds(2, R), :] * cw_ref[0:1, :]
            dz_ref[...] = dzp.astype(dz_ref.dtype)
            for j in range(FFN_CONV):
                dc_ref[j:j + 1, :] += jnp.sum(dz * e_ref[pl.ds(SUBLANES - 2 + j, R), :], axis=0, keepdims=True)
            dc_ref[3:4, :] += jnp.sum(dz, axis=0, keepdims=True)

    blk = pl.BlockSpec((R, CB), lambda j, i: (i, j))
    prev = pl.BlockSpec((SUBLANES, CB), lambda j, i: (jnp.maximum(i * rb - 1, 0), j))
    nxt = pl.BlockSpec((SUBLANES, CB), lambda j, i: (jnp.minimum((i + 1) * rb, T // SUBLANES - 1), j))
    w8 = pl.BlockSpec((SUBLANES, CB), lambda j, i: (0, j))
    w1 = pl.BlockSpec((1, CB), lambda j, i: (0, j))
    return pl.pallas_call(
        body, name=name, grid=(Fh // CB, nb),
        in_specs=[blk, nxt, blk, prev, nxt, blk, prev, nxt, w8, w8, w1, w1],
        out_specs=[blk, blk, w8, w8],
        out_shape=[jax.ShapeDtypeStruct((T, Fh), BF16), jax.ShapeDtypeStruct((T, Fh), BF16),
                   jax.ShapeDtypeStruct((SUBLANES, Fh), F32), jax.ShapeDtypeStruct((SUBLANES, Fh), F32)],
        scratch_shapes=[pltpu.VMEM((RE + SUBLANES, CB), F32), pltpu.VMEM((RE + SUBLANES, CB), F32),
                        pltpu.VMEM((RE, CB), F32), pltpu.VMEM((RE, CB), F32), pltpu.VMEM((RE, CB), F32)],
        compiler_params=_cp("parallel", "arbitrary"),
    )(dact, dact, za, za, za, zg, zg, zg, cwa, cwg, cba, cbg)


def _conv3_window(src, start, cs, w, b):
    win = src[pl.ds(start, FFN_RC + SUBLANES), cs]
    x2 = pltpu.roll(win, 2, axis=0)[SUBLANES:]
    x1 = pltpu.roll(win, 1, axis=0)[SUBLANES:]
    x0 = win[SUBLANES:]
    return ((b + x2 * w[0]) + x1 * w[1]) + x0 * w[2], (x2, x1, x0)


def _ffn_up_fwd(u2, fa, fg, cwa, cwg, cba, cbg, name):
    T, D = u2.shape
    Fh = fa.shape[1]
    tm = min(T, ROWS)
    CW, SB, RC = FFN_CW, FFN_SB, FFN_RC
    ns = CW // SB

    def body(u_ref, fa_ref, fg_ref, cwa_ref, cwg_ref, cba_ref, cbg_ref, za_ref, zg_ref, act_ref,
             ka_ref, kg_ref, ea_ref, eg_ref):
        @pl.when(pl.program_id(1) == 0)
        def _():
            ka_ref[...] = jnp.zeros_like(ka_ref)
            kg_ref[...] = jnp.zeros_like(kg_ref)

        def matmul(s):
            cs = pl.ds(s * SB, SB)
            za_ref[:, cs] = _dot(u_ref[...], fa_ref[:, cs])
            zg_ref[:, cs] = _dot(u_ref[...], fg_ref[:, cs])

        def gate(s):
            cs = pl.ds(s * SB, SB)
            wa = [cwa_ref[j:j + 1, cs] for j in range(FFN_CONV)]
            wg = [cwg_ref[j:j + 1, cs] for j in range(FFN_CONV)]
            ba, bg = cba_ref[:, cs], cbg_ref[:, cs]
            ea_ref[0:SUBLANES, cs] = ka_ref[:, cs]
            ea_ref[pl.ds(SUBLANES, RC), cs] = za_ref[0:RC, cs]
            eg_ref[0:SUBLANES, cs] = kg_ref[:, cs]
            eg_ref[pl.ds(SUBLANES, RC), cs] = zg_ref[0:RC, cs]
            for c in range(tm // RC):
                sa, sg, start = (ea_ref, eg_ref, 0) if c == 0 else (za_ref, zg_ref, c * RC - SUBLANES)
                a_c, _ = _conv3_window(sa, start, cs, wa, ba)
                g_c, _ = _conv3_window(sg, start, cs, wg, bg)
                act_ref[pl.ds(c * RC, RC), cs] = (_gelu(a_c) * g_c).astype(act_ref.dtype)
            ka_ref[:, cs] = za_ref[tm - SUBLANES:tm, cs]
            kg_ref[:, cs] = zg_ref[tm - SUBLANES:tm, cs]

        matmul(0)
        for s in range(1, ns):
            matmul(s)
            gate(s - 1)
        gate(ns - 1)

    blk = pl.BlockSpec((tm, CW), lambda j, i: (i, j))
    wblk = pl.BlockSpec((D, CW), lambda j, i: (0, j))
    w8 = pl.BlockSpec((SUBLANES, CW), lambda j, i: (0, j))
    w1 = pl.BlockSpec((1, CW), lambda j, i: (0, j))
    return pl.pallas_call(
        body, name=name, grid=(Fh // CW, T // tm),
        in_specs=[pl.BlockSpec((tm, D), lambda j, i: (i, 0)), wblk, wblk, w8, w8, w1, w1],
        out_specs=[blk, blk, blk],
        out_shape=[jax.ShapeDtypeStruct((T, Fh), F32), jax.ShapeDtypeStruct((T, Fh), F32),
                   jax.ShapeDtypeStruct((T, Fh), BF16)],
        scratch_shapes=[pltpu.VMEM((SUBLANES, CW), F32), pltpu.VMEM((SUBLANES, CW), F32),
                        pltpu.VMEM((RC + SUBLANES, CW), F32), pltpu.VMEM((RC + SUBLANES, CW), F32)],
        compiler_params=_cp("parallel", "arbitrary"),
    )(u2, fa, fg, cwa, cwg, cba, cbg)


def _ffn_bwd_core(dhb, za, zg, wdT, faT, fgT, cwa, cwg, cba, cbg, name):
    T, D = dhb.shape
    Fh = za.shape[1]
    tm = min(T, ROWS)
    CW, SB, RC = FFN_CW, FFN_SB, FFN_RC
    ns = CW // SB
    nj = Fh // CW
    nb = T // tm
    rb = tm // SUBLANES
    nc = tm // RC

    def body(dh_ref, a_ref, ap_ref, g_ref, gp_ref, wd_ref, fa_ref, fg_ref, cwa_ref, cwg_ref, cba_ref, cbg_ref,
             dza_ref, dzg_ref, du_ref, dca_ref, dcg_ref,
             d_ref, sa_ref, sg_ref, ka_ref, kg_ref, ea_ref, eg_ref):
        ib, j = pl.program_id(0), pl.program_id(1)
        i = nb - 1 - ib

        @pl.when((ib == 0) & (j == 0))
        def _():
            dca_ref[...] = jnp.zeros_like(dca_ref)
            dcg_ref[...] = jnp.zeros_like(dcg_ref)

        @pl.when(ib == 0)
        def _():
            ka_ref[j] = jnp.zeros((SUBLANES, CW), F32)
            kg_ref[j] = jnp.zeros((SUBLANES, CW), F32)

        @pl.when(j == 0)
        def _():
            du_ref[...] = jnp.zeros_like(du_ref)

        def matmul_in(s):
            cs = pl.ds(s * SB, SB)
            d_ref[:, cs] = _dot(dh_ref[...], wd_ref[:, cs])

        def matmul_out(s):
            cs = pl.ds(s * SB, SB)
            du_ref[...] += _dot(dza_ref[:, cs], fa_ref[cs, :]) + _dot(dzg_ref[:, cs], fg_ref[cs, :])

        def fold(v):
            return jnp.sum(v.reshape(RC // SUBLANES, SUBLANES, SB), axis=0)

        def gate(s):
            cs = pl.ds(s * SB, SB)
            wa = [cwa_ref[t:t + 1, cs] for t in range(FFN_CONV)]
            wg = [cwg_ref[t:t + 1, cs] for t in range(FFN_CONV)]
            ba, bg = cba_ref[:, cs], cbg_ref[:, cs]
            ea_ref[0:SUBLANES, cs] = jnp.where(i > 0, ap_ref[:, cs], 0.0)
            ea_ref[pl.ds(SUBLANES, RC), cs] = a_ref[0:RC, cs]
            eg_ref[0:SUBLANES, cs] = jnp.where(i > 0, gp_ref[:, cs], 0.0)
            eg_ref[pl.ds(SUBLANES, RC), cs] = g_ref[0:RC, cs]
            sa_ref[pl.ds(tm, SUBLANES), cs] = ka_ref[j, :, cs]
            sg_ref[pl.ds(tm, SUBLANES), cs] = kg_ref[j, :, cs]
            acc_a = [jnp.zeros((SUBLANES, SB), F32) for _ in range(FFN_CONV + 1)]
            acc_g = [jnp.zeros((SUBLANES, SB), F32) for _ in range(FFN_CONV + 1)]
            for c in range(nc):
                src_a, src_g, start = (ea_ref, eg_ref, 0) if c == 0 else (a_ref, g_ref, c * RC - SUBLANES)
                a_c, xa = _conv3_window(src_a, start, cs, wa, ba)
                g_c, xg = _conv3_window(src_g, start, cs, wg, bg)
                gel, dgel = _gelu_and_grad(a_c)
                dact = d_ref[pl.ds(c * RC, RC), cs]
                dza = dact * g_c * dgel
                dzg = dact * gel
                sa_ref[pl.ds(c * RC, RC), cs] = dza
                sg_ref[pl.ds(c * RC, RC), cs] = dzg
                for t in range(FFN_CONV):
                    acc_a[t] = acc_a[t] + fold(dza * xa[t])
                    acc_g[t] = acc_g[t] + fold(dzg * xg[t])
                acc_a[FFN_CONV] = acc_a[FFN_CONV] + fold(dza)
                acc_g[FFN_CONV] = acc_g[FFN_CONV] + fold(dzg)
            for t in range(FFN_CONV + 1):
                dca_ref[j, t:t + 1, cs] += jnp.sum(acc_a[t], axis=0, keepdims=True)
                dcg_ref[j, t:t + 1, cs] += jnp.sum(acc_g[t], axis=0, keepdims=True)
            n = RC + SUBLANES
            for c in range(nc):
                for s_ref, o_ref, w in ((sa_ref, dza_ref, wa), (sg_ref, dzg_ref, wg)):
                    win = s_ref[pl.ds(c * RC, n), cs]
                    d1 = pltpu.roll(win, n - 1, axis=0)[:RC]
                    d2 = pltpu.roll(win, n - 2, axis=0)[:RC]
                    o_ref[pl.ds(c * RC, RC), cs] = ((win[:RC] * w[2] + d1 * w[1]) + d2 * w[0]).astype(o_ref.dtype)
            ka_ref[j, :, cs] = sa_ref[0:SUBLANES, cs]
            kg_ref[j, :, cs] = sg_ref[0:SUBLANES, cs]

        matmul_in(0)
        for s in range(1, ns):
            matmul_in(s)
            gate(s - 1)
            if s >= 2:
                matmul_out(s - 2)
        gate(ns - 1)
        if ns >= 2:
            matmul_out(ns - 2)
        matmul_out(ns - 1)

    blk = pl.BlockSpec((tm, CW), lambda ib, j: (nb - 1 - ib, j))
    prev = pl.BlockSpec((SUBLANES, CW), lambda ib, j: (jnp.maximum((nb - 1 - ib) * rb - 1, 0), j))
    w8 = pl.BlockSpec((SUBLANES, CW), lambda ib, j: (0, j))
    w1 = pl.BlockSpec((1, CW), lambda ib, j: (0, j))
    wrow = pl.BlockSpec((CW, D), lambda ib, j: (j, 0))
    acc = pl.BlockSpec((nj, SUBLANES, CW), lambda ib, j: (0, 0, 0))
    return pl.pallas_call(
        body, name=name, grid=(nb, nj),
        in_specs=[pl.BlockSpec((tm, D), lambda ib, j: (nb - 1 - ib, 0)), blk, prev, blk, prev,
                  pl.BlockSpec((D, CW), lambda ib, j: (0, j)), wrow, wrow, w8, w8, w1, w1],
        out_specs=[blk, blk, pl.BlockSpec((tm, D), lambda ib, j: (nb - 1 - ib, 0)), acc, acc],
        out_shape=[jax.ShapeDtypeStruct((T, Fh), BF16), jax.ShapeDtypeStruct((T, Fh), BF16),
                   jax.ShapeDtypeStruct((T, D), F32), jax.ShapeDtypeStruct((nj, SUBLANES, CW), F32),
                   jax.ShapeDtypeStruct((nj, SUBLANES, CW), F32)],
        scratch_shapes=[pltpu.VMEM((tm, CW), F32), pltpu.VMEM((tm + SUBLANES, CW), F32),
                        pltpu.VMEM((tm + SUBLANES, CW), F32), pltpu.VMEM((nj, SUBLANES, CW), F32),
                        pltpu.VMEM((nj, SUBLANES, CW), F32), pltpu.VMEM((RC + SUBLANES, CW), F32),
                        pltpu.VMEM((RC + SUBLANES, CW), F32)],
        compiler_params=_cp("arbitrary", "arbitrary"),
    )(dhb, za, za, zg, zg, wdT, faT, fgT, cwa, cwg, cba, cbg)


def _adamw(w, grads, m, v, name):
    rows, cols = w.shape
    tr = _row_tile(rows, max(SUBLANES, min(512, TILE_BYTES // (4 * cols)) // SUBLANES * SUBLANES))
    ng = len(grads)

    def body(*refs):
        w_ref, g_refs, m_ref, v_ref = refs[0], refs[1:1 + ng], refs[1 + ng], refs[2 + ng]
        go_ref, d_ref, mo_ref, vo_ref = refs[3 + ng:]
        g = g_refs[0][...]
        for r in g_refs[1:]:
            g = g + r[...]
        mm = ADAM_B1 * m_ref[...] + (1.0 - ADAM_B1) * g
        vv = ADAM_B2 * v_ref[...] + (1.0 - ADAM_B2) * (g * g)
        m_hat = mm / (1.0 - ADAM_B1 ** ADAM_STEP)
        v_hat = vv / (1.0 - ADAM_B2 ** ADAM_STEP)
        go_ref[...] = g
        d_ref[...] = -ADAM_LR * (m_hat / (jnp.sqrt(v_hat) + ADAM_EPS) + ADAM_WD * w_ref[...])
        mo_ref[...] = mm
        vo_ref[...] = vv

    blk = pl.BlockSpec((tr, cols), lambda i: (i, 0))
    return pl.pallas_call(
        body, name=name, grid=(rows // tr,), in_specs=[blk] * (3 + ng), out_specs=[blk] * 4,
        out_shape=[jax.ShapeDtypeStruct((rows, cols), F32)] * 4, compiler_params=_cp("parallel"),
    )(w, *grads, m, v)


def _add_slabs(a, b, out_dtype, name):
    n, rows, cols = a.shape
    tr = _row_tile(rows, max(SUBLANES, min(512, TILE_BYTES // (4 * cols)) // SUBLANES * SUBLANES))

    def body(a_ref, b_ref, o_ref):
        o_ref[...] = (a_ref[...] + b_ref[...]).astype(o_ref.dtype)

    blk = pl.BlockSpec((1, tr, cols), lambda k, i: (k, i, 0))
    return pl.pallas_call(
        body, name=name, grid=(n, rows // tr), in_specs=[blk, blk], out_specs=blk,
        out_shape=jax.ShapeDtypeStruct((n, rows, cols), out_dtype), compiler_params=_cp("parallel", "parallel"),
    )(a, b)


def _sum_leading(parts, name, last=None):
    n, rows, cols = parts.shape
    tr = _row_tile(rows, max(SUBLANES, min(512, TILE_BYTES // (4 * cols)) // SUBLANES * SUBLANES))

    def body(*refs):
        p_ref, o_ref = refs[0], refs[-1]
        acc = p_ref[0].astype(F32)
        for d in range(1, n):
            acc = acc + p_ref[d].astype(F32)
        if last is not None:
            acc = acc + refs[1][...].astype(F32)
        o_ref[...] = acc

    blk = pl.BlockSpec((tr, cols), lambda i: (i, 0))
    return pl.pallas_call(
        body, name=name, grid=(rows // tr,),
        in_specs=[pl.BlockSpec((n, tr, cols), lambda i: (0, i, 0))] + ([] if last is None else [blk]), out_specs=blk,
        out_shape=jax.ShapeDtypeStruct((rows, cols), F32), compiler_params=_cp("parallel"),
    )(*((parts,) if last is None else (parts, last)))


def _row_tile(rows, cap=512):
    if rows <= cap:
        return rows
    return max(t for t in range(SUBLANES, cap + 1, SUBLANES) if rows % t == 0)


def _place():
    return lax.axis_index("x"), lax.axis_index("y"), lax.axis_index("c")


def _gather_shards(arrs, name):
    n = len(arrs)
    hl = DEPTH // 2

    def body(*refs):
        ins, outs = refs[:n], refs[n:2 * n]
        send_sems, recv_sems, pass_send, pass_recv = refs[2 * n:]
        x, y, c = _place()
        chips = [(1 - x, y), (x, 1 - y), (1 - x, 1 - y)]
        mine, theirs = pl.ds(c * hl, hl), pl.ds((1 - c) * hl, hl)

        def send(a, j, shard):
            px, py = chips[j]
            return pltpu.make_async_remote_copy(
                src_ref=ins[a].at[mine], dst_ref=outs[a].at[shard, mine], send_sem=send_sems.at[3 * a + j],
                recv_sem=recv_sems.at[3 * a + j], device_id=(px, py, c), device_id_type=MESH)

        def passed(a, j, half):
            px, py = chips[j]
            blk = outs[a].at[2 * px + py, half]
            return pltpu.make_async_remote_copy(
                src_ref=blk, dst_ref=blk, send_sem=pass_send.at[3 * a + j], recv_sem=pass_recv.at[3 * a + j],
                device_id=(x, y, 1 - c), device_id_type=MESH)

        sends = [send(a, j, 2 * x + y) for a in range(n) for j in range(3)]
        for cp in sends:
            cp.start()
        passes = []
        for a in range(n):
            for j, (px, py) in enumerate(chips):
                send(a, j, 2 * px + py).wait_recv()
                passes.append(passed(a, j, mine))
                passes[-1].start()
        for a in range(n):
            for j in range(3):
                passed(a, j, theirs).wait_recv()
        for cp in sends + passes:
            cp.wait_send()

    hbm = pl.BlockSpec(memory_space=pl.ANY)
    return pl.pallas_call(
        body, name=name, in_specs=[hbm] * n, out_specs=[hbm] * n,
        out_shape=[jax.ShapeDtypeStruct((N_CHIPS,) + a.shape, a.dtype) for a in arrs],
        scratch_shapes=[pltpu.SemaphoreType.DMA((3 * n,)), pltpu.SemaphoreType.DMA((3 * n,)),
                        pltpu.SemaphoreType.DMA((3 * n,)), pltpu.SemaphoreType.DMA((3 * n,))],
        compiler_params=pltpu.CompilerParams(has_side_effects=True),
    )(*arrs)


def _other_half_to_sibling(slabs, name):
    n = len(slabs)

    def body(*refs):
        ins, got = refs[:n], refs[n:2 * n]
        send_sems, recv_sems = refs[2 * n:]
        x, y, c = _place()
        copies = [pltpu.make_async_remote_copy(
            src_ref=ins[a].at[:, 1 - c], dst_ref=got[a], send_sem=send_sems.at[a], recv_sem=recv_sems.at[a],
            device_id=(x, y, 1 - c), device_id_type=MESH) for a in range(n)]
        for cp in copies:
            cp.start()
        for cp in copies:
            cp.wait()

    hbm = pl.BlockSpec(memory_space=pl.ANY)
    return pl.pallas_call(
        body, name=name, in_specs=[hbm] * n, out_specs=[hbm] * n,
        out_shape=[jax.ShapeDtypeStruct((s.shape[0],) + s.shape[2:], s.dtype) for s in slabs],
        scratch_shapes=[pltpu.SemaphoreType.DMA((n,)), pltpu.SemaphoreType.DMA((n,))],
        compiler_params=pltpu.CompilerParams(has_side_effects=True),
    )(*slabs)


def _exchange_grads(slabs, small, name):
    n = len(slabs)

    def body(*refs):
        ins, small_ref = refs[:n], refs[n]
        outs, small_out = refs[n + 1:2 * n + 1], refs[2 * n + 1]
        send_sems, recv_sems, ssend, srecv = refs[2 * n + 2:]
        x, y, c = _place()
        chips = [(1 - x, y), (x, 1 - y), (1 - x, 1 - y)]
        me = 4 * x + 2 * y + c
        flips = [(fx, fy, fc) for fx in (0, 1) for fy in (0, 1) for fc in (0, 1)][1:]

        def copy(a, j):
            px, py = chips[j]
            return pltpu.make_async_remote_copy(
                src_ref=ins[a].at[2 * px + py], dst_ref=outs[a].at[j], send_sem=send_sems.at[3 * a + j],
                recv_sem=recv_sems.at[3 * a + j], device_id=(px, py, c), device_id_type=MESH)

        def scopy(k, row):
            fx, fy, fc = flips[k]
            return pltpu.make_async_remote_copy(
                src_ref=small_ref, dst_ref=small_out.at[row], send_sem=ssend.at[k], recv_sem=srecv.at[k],
                device_id=(x ^ fx, y ^ fy, c ^ fc), device_id_type=MESH)

        sends = [copy(a, j) for a in range(n) for j in range(3)] + [scopy(k, me) for k in range(7)]
        for cp in sends:
            cp.start()
        for k, (fx, fy, fc) in enumerate(flips):
            scopy(k, 4 * (x ^ fx) + 2 * (y ^ fy) + (c ^ fc)).wait_recv()
        for a in range(n):
            for j in range(3):
                copy(a, j).wait_recv()
        for cp in sends:
            cp.wait_send()

    hbm = pl.BlockSpec(memory_space=pl.ANY)
    return pl.pallas_call(
        body, name=name, in_specs=[hbm] * (n + 1), out_specs=[hbm] * (n + 1),
        out_shape=[jax.ShapeDtypeStruct((3,) + s.shape[1:], s.dtype) for s in slabs]
        + [jax.ShapeDtypeStruct((N_DEV,) + small.shape, small.dtype)],
        scratch_shapes=[pltpu.SemaphoreType.DMA((3 * n,)), pltpu.SemaphoreType.DMA((3 * n,)),
                        pltpu.SemaphoreType.DMA((7,)), pltpu.SemaphoreType.DMA((7,))],
        compiler_params=pltpu.CompilerParams(has_side_effects=True),
    )(*slabs, small)


def _swap_with_sibling(arrs, name):
    n = len(arrs)

    def body(*refs):
        ins, outs = refs[:n], refs[n:2 * n]
        send_sems, recv_sems = refs[2 * n:]
        x, y, c = _place()
        copies = [pltpu.make_async_remote_copy(
            src_ref=ins[a], dst_ref=outs[a], send_sem=send_sems.at[a], recv_sem=recv_sems.at[a],
            device_id=(x, y, 1 - c), device_id_type=MESH) for a in range(n)]
        for cp in copies:
            cp.start()
        for cp in copies:
            cp.wait()

    hbm = pl.BlockSpec(memory_space=pl.ANY)
    return pl.pallas_call(
        body, name=name, in_specs=[hbm] * n, out_specs=[hbm] * n,
        out_shape=[jax.ShapeDtypeStruct(a.shape, a.dtype) for a in arrs],
        scratch_shapes=[pltpu.SemaphoreType.DMA((n,)), pltpu.SemaphoreType.DMA((n,))],
        compiler_params=pltpu.CompilerParams(has_side_effects=True),
    )(*arrs)


def _block_diag(w):
    eye = jnp.eye(LRU_BLOCKS, dtype=w.dtype)
    return (eye[:, None, :, None] * w[:, :, None, :]).reshape(LRU_W, LRU_W)


def _diag_blocks(m):
    m4 = m.reshape(LRU_BLOCKS, LRU_BLOCK, LRU_BLOCKS, LRU_BLOCK)
    return jnp.stack([m4[b, :, b, :] for b in range(LRU_BLOCKS)])


def _pad_rows(a, rows):
    return jnp.pad(a, ((0, rows - a.shape[0]), (0, 0)))


def _layer_weights(p, l):
    w_in = p["w_in"][l]
    n_gla = 2 * QK_W + 2 * GLA_W
    gate = jnp.pad(w_in[:, n_gla:n_gla + GATE_RANK], ((0, 0), (0, GATE_PAD - GATE_RANK)))
    wg = jnp.concatenate([w_in[:, :n_gla], gate], axis=1)
    wl = w_in[:, n_gla + GATE_RANK:]
    w_out = p["w_out"][l]
    fa, fg = p["ffn_w_in"][l][:, :FFN_H], p["ffn_w_in"][l][:, FFN_H:]
    wd = p["ffn_w_down"][l]
    return dict(
        wg=wg, wl=wl, wgT=wg.T, wlT=wl.T, wo_g=w_out[:GLA_W], wo_l=w_out[GLA_W:], woT=w_out.T,
        fa=fa, fg=fg, faT=fa.T, fgT=fg.T, wd=wd, wdT=wd.T,
        w2p=_pad_rows(p["gla_gate_w2"][l], GATE_PAD).astype(BF16),
        wa=_block_diag(p["lru_wa"][l]).astype(BF16), wx=_block_diag(p["lru_wx"][l]).astype(BF16),
        lcw=_pad_rows(p["lru_conv_w"][l], SUBLANES),
        fcwa=_pad_rows(p["ffn_conv_w"][l][:, :FFN_H], SUBLANES), fcwg=_pad_rows(p["ffn_conv_w"][l][:, FFN_H:], SUBLANES),
    )


def _local_step(x, tgt, p):
    row = lambda v: v.reshape(1, -1)
    h = x
    stash = []
    for l in range(DEPTH):
        w = _layer_weights(p, l)
        s = dict(w=w, h0=h)
        u = _rms_fwd(h, p["ln_mix"][l], f"mix_norm_fwd{l}")
        pg = _mm(u, w["wg"], None, F32, f"proj_gla_fwd{l}")
        plr = _mm(u, w["wl"], None, F32, f"proj_lru_fwd{l}")
        yg, o_st, s_st = _gla_fwd(pg, w["w2p"], p["gla_gate_b"][l], p["gla_norm"][l], f"gla_fwd{l}")
        yl, hs = _lru_fwd(plr, w["lcw"], row(p["lru_conv_b"][l]), w["wa"], row(p["lru_ba"][l]), w["wx"],
                          row(p["lru_bx"][l]), row(p["lru_lambda"][l]), f"lru_fwd{l}")
        h = _mm(yg, w["wo_g"], h, F32, f"out_gla_fwd{l}")
        h = _mm(yl, w["wo_l"], h, F32, f"out_lru_fwd{l}")
        s.update(u=u, pg=pg, plr=plr, yg=yg, yl=yl, o_st=o_st, s_st=s_st, hs=hs, h1=h)
        u2 = _rms_fwd(h, p["ln_ffn"][l], f"ffn_norm_fwd{l}")
        cba, cbg = row(p["ffn_conv_b"][l][:FFN_H]), row(p["ffn_conv_b"][l][FFN_H:])
        za, zg, act = _ffn_up_fwd(u2, w["fa"], w["fg"], w["fcwa"], w["fcwg"], cba, cbg, f"ffn_up_fwd{l}")
        h = _mm(act, w["wd"], h, F32, f"ffn_down_fwd{l}")
        s.update(u2=u2, za=za, zg=zg, act=act, cba=cba, cbg=cbg)
        stash.append(s)

    loss, dh, dhb, d_ln_final = _loss_head(h, p["ln_final"], tgt, "loss_head")

    g = {k: [None] * DEPTH for k in ("ln_mix", "w_in", "gla_gate_w2", "gla_gate_b", "gla_norm", "lru_conv_w",
                                     "lru_conv_b", "lru_wa", "lru_ba", "lru_wx", "lru_bx", "lru_lambda",
                                     "ln_ffn", "ffn_conv_w", "ffn_conv_b")}
    slab = dict(w_out=jnp.zeros((N_CHIPS, DEPTH * D_MODEL // N_CHIPS, D_MODEL), F32),
                ffn_w_in=jnp.zeros((N_CHIPS, DEPTH * D_MODEL, 2 * FFN_H // N_CHIPS), F32),
                ffn_w_down=jnp.zeros((N_CHIPS, DEPTH * FFN_H // N_CHIPS, D_MODEL), F32))
    n_gla = 2 * QK_W + 2 * GLA_W
    for l in reversed(range(DEPTH)):
        s = stash[l]
        w = s["w"]
        slab["ffn_w_down"] = _mm_tn_into(slab["ffn_w_down"], s["act"], dhb, l, 0, f"ffn_down_dw{l}",
                                         tk=FFN_H // N_CHIPS, tn=D_MODEL, tm=2048)
        dza, dzg, du2, dca, dcg = _ffn_bwd_core(dhb, s["za"], s["zg"], w["wdT"], w["faT"], w["fgT"], w["fcwa"],
                                                w["fcwg"], s["cba"], s["cbg"], f"ffn_bwd_core{l}")
        dca, dcg = (jnp.moveaxis(d, 0, 1).reshape(SUBLANES, FFN_H) for d in (dca, dcg))
        g["ffn_conv_w"][l] = jnp.concatenate([dca[:FFN_CONV], dcg[:FFN_CONV]], axis=1)
        g["ffn_conv_b"][l] = jnp.concatenate([dca[FFN_CONV], dcg[FFN_CONV]])
        for half, dz in enumerate((dza, dzg)):
            slab["ffn_w_in"] = _mm_tn_into(slab["ffn_w_in"], s["u2"], dz, l, 2 * half, f"ffn_in_dw{l}_{half}",
                                           tk=D_MODEL, tn=2 * FFN_H // N_CHIPS)
        dh, dhb, dln = _rms_bwd(s["h1"], p["ln_ffn"][l], du2, dh, f"ffn_norm_bwd{l}")
        g["ln_ffn"][l] = dln[0]
        for half, y in enumerate((s["yg"], s["yl"])):
            slab["w_out"] = _mm_tn_into(slab["w_out"], y, dhb, l, 2 * half, f"out_dw{l}_{half}",
                                        tk=D_MODEL // N_CHIPS, tn=D_MODEL, tm=2048)
        dyc = _mm(dhb, w["woT"], None, F32, f"out_dx{l}")
        dpg, dw2, db2, dng = _gla_bwd(dyc, s["pg"], s["o_st"], s["s_st"], w["w2p"], p["gla_gate_b"][l],
                                      p["gla_norm"][l], f"gla_bwd{l}")
        dpl, dcw, dvec, dwa, dwx = _lru_bwd(dyc, s["plr"], s["hs"], w["lcw"], row(p["lru_conv_b"][l]), w["wa"],
                                            row(p["lru_ba"][l]), w["wx"], row(p["lru_bx"][l]),
                                            row(p["lru_lambda"][l]), f"lru_bwd{l}")
        g["gla_gate_w2"][l] = dw2[:GATE_RANK]
        g["gla_gate_b"][l] = db2[0]
        g["gla_norm"][l] = dng[0]
        g["lru_conv_w"][l] = dcw[:LRU_CONV]
        g["lru_conv_b"][l], g["lru_ba"][l], g["lru_bx"][l], g["lru_lambda"][l] = dvec[0], dvec[1], dvec[2], dvec[3]
        g["lru_wa"][l], g["lru_wx"][l] = _diag_blocks(dwa), _diag_blocks(dwx)
        dwg = _mm_tn(s["u"], dpg, f"proj_gla_dw{l}")
        dwl = _mm_tn(s["u"], dpl, f"proj_lru_dw{l}")
        g["w_in"][l] = jnp.concatenate([dwg[:, :n_gla + GATE_RANK], dwl], axis=1)
        du = _mm(dpg, w["wgT"], None, F32, f"proj_gla_dx{l}")
        du = _mm(dpl, w["wlT"], du, F32, f"proj_lru_dx{l}")
        dh, dhb, dln = _rms_bwd(s["h0"], p["ln_mix"][l], du, dh, f"mix_norm_bwd{l}")
        g["ln_mix"][l] = dln[0]
    grads = {k: jnp.stack(v) for k, v in g.items()}
    grads["w_in"] = _slabs_from_whole("w_in", grads["w_in"])
    grads.update(slab)
    grads["ln_final"] = d_ln_final[0]
    return loss, dh, grads


BIG = ("w_in", "w_out", "ffn_w_in", "ffn_w_down")
COL_SHARDED = ("w_in", "ffn_w_in", "gla_gate_w2", "lru_conv_w", "ffn_conv_w")
SMALL = ("ln_mix", "gla_gate_w2", "gla_gate_b", "gla_norm", "lru_conv_w", "lru_conv_b", "lru_wa", "lru_ba", "lru_wx",
         "lru_bx", "lru_lambda", "ln_ffn", "ffn_conv_w", "ffn_conv_b", "ln_final")
WEIGHTS = ("ln_mix", "w_in", "gla_gate_w2", "gla_gate_b", "gla_norm", "lru_conv_w", "lru_conv_b", "lru_wa", "lru_ba",
           "lru_wx", "lru_bx", "lru_lambda", "w_out", "ln_ffn", "ffn_w_in", "ffn_conv_w", "ffn_conv_b", "ffn_w_down",
           "ln_final")
PACK = SUBLANES * LANES


def _whole_from_shards(name, g):
    if name in COL_SHARDED:
        return jnp.moveaxis(g, 0, -2).reshape(g.shape[1:-1] + (N_CHIPS * g.shape[-1],))
    return jnp.moveaxis(g, 0, 1).reshape((g.shape[1], N_CHIPS * g.shape[2]) + g.shape[3:])


def _slabs_from_whole(name, w):
    L, r, c = w.shape
    if name in COL_SHARDED:
        s = jnp.moveaxis(w.reshape(L, r, N_CHIPS, c // N_CHIPS), 2, 0)
    else:
        s = jnp.moveaxis(w.reshape(L, N_CHIPS, r // N_CHIPS, c), 1, 0)
    return s.reshape(N_CHIPS, -1, s.shape[-1])


def _pack(arrs):
    flat = []
    for a in arrs:
        f = a.reshape(-1)
        flat.append(jnp.pad(f, (0, (-f.shape[0]) % PACK)))
    return jnp.concatenate(flat).reshape(-1, LANES)


def _unpack(packed, shapes):
    out, at = [], 0
    flat = packed.reshape(-1)
    for s in shapes:
        size = math.prod(s)
        out.append(flat[at:at + size].reshape(s))
        at += size + (-size) % PACK
    return out


def kernel(x, ln_mix, w_in, gla_gate_w2, gla_gate_b, gla_norm, lru_conv_w, lru_conv_b, lru_wa, lru_ba, lru_wx, lru_bx, lru_lambda, w_out, ln_ffn, ffn_w_in, ffn_conv_w, ffn_conv_b, ffn_w_down, ln_final, loss_target, m_ln_mix, m_w_in, m_gla_gate_w2, m_gla_gate_b, m_gla_norm, m_lru_conv_w, m_lru_conv_b, m_lru_wa, m_lru_ba, m_lru_wx, m_lru_bx, m_lru_lambda, m_w_out, m_ln_ffn, m_ffn_w_in, m_ffn_conv_w, m_ffn_conv_b, m_ffn_w_down, m_ln_final, v_ln_mix, v_w_in, v_gla_gate_w2, v_gla_gate_b, v_gla_norm, v_lru_conv_w, v_lru_conv_b, v_lru_wa, v_lru_ba, v_lru_wx, v_lru_bx, v_lru_lambda, v_w_out, v_ln_ffn, v_ffn_w_in, v_ffn_conv_w, v_ffn_conv_b, v_ffn_w_down, v_ln_final):
    w = dict(ln_mix=ln_mix, w_in=w_in, gla_gate_w2=gla_gate_w2, gla_gate_b=gla_gate_b, gla_norm=gla_norm,
             lru_conv_w=lru_conv_w, lru_conv_b=lru_conv_b, lru_wa=lru_wa, lru_ba=lru_ba, lru_wx=lru_wx, lru_bx=lru_bx,
             lru_lambda=lru_lambda, w_out=w_out, ln_ffn=ln_ffn, ffn_w_in=ffn_w_in, ffn_conv_w=ffn_conv_w,
             ffn_conv_b=ffn_conv_b, ffn_w_down=ffn_w_down, ln_final=ln_final)
    m = dict(ln_mix=m_ln_mix, w_in=m_w_in, gla_gate_w2=m_gla_gate_w2, gla_gate_b=m_gla_gate_b, gla_norm=m_gla_norm,
             lru_conv_w=m_lru_conv_w, lru_conv_b=m_lru_conv_b, lru_wa=m_lru_wa, lru_ba=m_lru_ba, lru_wx=m_lru_wx,
             lru_bx=m_lru_bx, lru_lambda=m_lru_lambda, w_out=m_w_out, ln_ffn=m_ln_ffn, ffn_w_in=m_ffn_w_in,
             ffn_conv_w=m_ffn_conv_w, ffn_conv_b=m_ffn_conv_b, ffn_w_down=m_ffn_w_down, ln_final=m_ln_final)
    v = dict(ln_mix=v_ln_mix, w_in=v_w_in, gla_gate_w2=v_gla_gate_w2, gla_gate_b=v_gla_gate_b, gla_norm=v_gla_norm,
             lru_conv_w=v_lru_conv_w, lru_conv_b=v_lru_conv_b, lru_wa=v_lru_wa, lru_ba=v_lru_ba, lru_wx=v_lru_wx,
             lru_bx=v_lru_bx, lru_lambda=v_lru_lambda, w_out=v_w_out, ln_ffn=v_ln_ffn, ffn_w_in=v_ffn_w_in,
             ffn_conv_w=v_ffn_conv_w, ffn_conv_b=v_ffn_conv_b, ffn_w_down=v_ffn_w_down, ln_final=v_ln_final)

    sharded = BIG + ("gla_gate_w2", "lru_conv_w", "ffn_conv_w")
    chip = 2 * lax.axis_index("x") + lax.axis_index("y")
    core = lax.axis_index("c")
    shards = [w[k].astype(MXU_DTYPE) if k in BIG else w[k] for k in sharded]
    gathered = _gather_shards(shards, "gather_weights")
    p = dict(w)
    for k, gk, own in zip(sharded, gathered, shards):
        p[k] = _whole_from_shards(k, lax.dynamic_update_index_in_dim(gk, own, chip, 0))

    loss, grad_x, grads = _local_step(x[0], loss_target[0], p)
    loss = lax.psum(loss[0, 0], ("x", "y", "c"))

    slabs = [grads[k].reshape(N_CHIPS, 2, grads[k].shape[1] // 2, grads[k].shape[2]) for k in BIG]
    got = _other_half_to_sibling(slabs, "other_half_to_sibling")
    kept = [lax.dynamic_index_in_dim(s, core, 1, keepdims=False) for s in slabs]
    chip_half = [_add_slabs(a, b, BF16, f"core_sum_{k}") for k, a, b in zip(BIG, kept, got)]
    small = _pack([grads[k] for k in SMALL])
    *recv, small_all = _exchange_grads(chip_half, small, "exchange_grads")
    own = [lax.dynamic_index_in_dim(h, chip, 0, keepdims=False) for h in chip_half]
    mine = [_sum_leading(r, f"chip_sum_{k}", last=o) for k, r, o in zip(BIG, recv, own)]
    theirs = _swap_with_sibling(mine, "swap_core_halves")
    big_g = [jnp.concatenate([jnp.where(core == 0, a, b), jnp.where(core == 0, b, a)]) for a, b in zip(mine, theirs)]
    small_all = lax.dynamic_update_index_in_dim(small_all, small, 2 * chip + core, 0)
    small_sum = _unpack(_sum_leading(small_all, "sum_small_grads"), [grads[k].shape for k in SMALL])

    me = 2 * lax.axis_index("x") + lax.axis_index("y")
    out_g, out_d, out_m, out_v = {}, {}, {}, {}
    for k, gk in zip(BIG, big_g):
        shape = w[k].shape
        cols = shape[-1]
        res = _adamw(w[k].reshape(-1, cols), [gk], m[k].reshape(-1, cols), v[k].reshape(-1, cols), f"adamw_{k}")
        out_g[k], out_d[k], out_m[k], out_v[k] = [r.reshape(shape) for r in res]
    small_g = []
    for k, gk in zip(SMALL, small_sum):
        if k in COL_SHARDED:
            width = w[k].shape[-1]
            gk = lax.dynamic_slice_in_dim(gk, me * width, width, axis=gk.ndim - 1)
        small_g.append(gk)
    shapes = [w[k].shape for k in SMALL]
    res = _adamw(_pack([w[k] for k in SMALL]), [_pack(small_g)], _pack([m[k] for k in SMALL]),
                 _pack([v[k] for k in SMALL]), "adamw_small")
    for out, packed in zip((out_g, out_d, out_m, out_v), res):
        for k, a in zip(SMALL, _unpack(packed, shapes)):
            out[k] = a
    return (loss, grad_x[None], *[out_g[k] for k in WEIGHTS], *[out_d[k] for k in WEIGHTS],
            *[out_m[k] for k in WEIGHTS], *[out_v[k] for k in WEIGHTS])
```

```python
import math

import jax
import jax.numpy as jnp
from jax import lax
from jax.experimental import pallas as pl
from jax.experimental.pallas import tpu as pltpu

F32 = jnp.float32
BF16 = jnp.bfloat16
MXU_DTYPE = BF16

D_MODEL = 1024
DEPTH = 4
HEADS, DK, DV, CHUNK, GATE_RANK = 4, 64, 128, 64, 16
QK_W = HEADS * DK
GLA_W = HEADS * DV
LRU_W = 512
LRU_BLOCKS, LRU_BLOCK, LRU_CONV, LRU_C = 8, 64, 4, 8.0
FFN_H = 3 * D_MODEL
FFN_CONV = 3
EPS = 1e-6
GATE_PAD = 128
GLA_COLS = 2 * QK_W + 2 * GLA_W + GATE_PAD
LRU_COLS = 2 * LRU_W
ADAM_LR, ADAM_B1, ADAM_B2, ADAM_EPS, ADAM_WD, ADAM_STEP = 0.001, 0.9, 0.999, 1e-08, 0.01, 10

LANES = 128
SUBLANES = 8
VMEM_LIMIT = 56 * 1024 * 1024
ROWS = 512
TILE_BYTES = 1 << 20
FFN_CW = 1024
FFN_SB = 256
FFN_RC = 32
N_CHIPS = 4
N_DEV = 8
MESH = pl.DeviceIdType.MESH


def _cp(*sem):
    return pltpu.CompilerParams(dimension_semantics=sem, vmem_limit_bytes=VMEM_LIMIT)


def _dot(a, b):
    return jnp.dot(a.astype(MXU_DTYPE), b.astype(MXU_DTYPE), preferred_element_type=F32)


def _dot_nt(a, b):
    return lax.dot_general(a.astype(MXU_DTYPE), b.astype(MXU_DTYPE), (((1,), (1,)), ((), ())),
                           preferred_element_type=F32)


def _dot_tn(a, b):
    return lax.dot_general(a.astype(MXU_DTYPE), b.astype(MXU_DTYPE), (((0,), (0,)), ((), ())),
                           preferred_element_type=F32)


def _bdot(eq, a, b):
    return jnp.einsum(eq, a, b, preferred_element_type=F32)


def _split3(x):
    x1 = x.astype(BF16)
    r1 = x - x1.astype(F32)
    x2 = r1.astype(BF16)
    x3 = (r1 - x2.astype(F32)).astype(BF16)
    return x1, x2, x3


def _gelu(x):
    c = math.sqrt(2.0 / math.pi)
    return x * (0.5 * (1.0 + jnp.tanh(c * (x + 0.044715 * (x * x * x)))))


def _gelu_and_grad(x):
    c = math.sqrt(2.0 / math.pi)
    t = jnp.tanh(c * (x + 0.044715 * (x * x * x)))
    cdf = 0.5 * (1.0 + t)
    dcdf = 0.5 * (1.0 - t * t) * (c * (1.0 + 3.0 * 0.044715 * (x * x)))
    return x * cdf, cdf + x * dcdf


def _expm1(x):
    small = x * (1.0 + x * (0.5 + x * (1.0 / 6.0 + x * (1.0 / 24.0 + x * (1.0 / 120.0)))))
    return jnp.where(jnp.abs(x) < 0.1, small, jnp.exp(x) - 1.0)


def _shift_down(x, k, fill):
    row = lax.broadcasted_iota(jnp.int32, x.shape, 0)
    return jnp.where(row >= k, pltpu.roll(x, k, axis=0), fill)


def _shift_up(x, k, fill):
    n = x.shape[0]
    row = lax.broadcasted_iota(jnp.int32, x.shape, 0)
    return jnp.where(row < n - k, pltpu.roll(x, n - k, axis=0), fill)


def _rms_fwd(h, g, name):
    T, D = h.shape
    R = min(T, ROWS)

    def body(h_ref, g_ref, o_ref):
        x = h_ref[...]
        r = lax.rsqrt(jnp.mean(x * x, axis=-1, keepdims=True) + EPS)
        o_ref[...] = ((x * r) * g_ref[...]).astype(o_ref.dtype)

    return pl.pallas_call(
        body, name=name, grid=(T // R,),
        in_specs=[pl.BlockSpec((R, D), lambda i: (i, 0)), pl.BlockSpec((1, D), lambda i: (0, 0))],
        out_specs=pl.BlockSpec((R, D), lambda i: (i, 0)),
        out_shape=jax.ShapeDtypeStruct((T, D), BF16), compiler_params=_cp("parallel"),
    )(h, g.reshape(1, D))


def _rms_bwd(h, g, du, dres, name):
    T, D = h.shape
    R = min(T, ROWS)

    def body(h_ref, g_ref, du_ref, dres_ref, dh_ref, dhb_ref, dg_ref):
        @pl.when(pl.program_id(0) == 0)
        def _():
            dg_ref[...] = jnp.zeros_like(dg_ref)

        x = h_ref[...]
        r = lax.rsqrt(jnp.mean(x * x, axis=-1, keepdims=True) + EPS)
        xhat = x * r
        dy = du_ref[...].astype(F32)
        dg_ref[...] += jnp.sum(dy * xhat, axis=0, keepdims=True)
        dxhat = dy * g_ref[...]
        dx = r * (dxhat - xhat * jnp.mean(dxhat * xhat, axis=-1, keepdims=True))
        dh = dres_ref[...] + dx
        dh_ref[...] = dh
        dhb_ref[...] = dh.astype(dhb_ref.dtype)

    blk = pl.BlockSpec((R, D), lambda i: (i, 0))
    vec = pl.BlockSpec((1, D), lambda i: (0, 0))
    return pl.pallas_call(
        body, name=name, grid=(T // R,), in_specs=[blk, vec, blk, blk], out_specs=[blk, blk, vec],
        out_shape=[jax.ShapeDtypeStruct((T, D), F32), jax.ShapeDtypeStruct((T, D), BF16),
                   jax.ShapeDtypeStruct((1, D), F32)],
        compiler_params=_cp("arbitrary"),
    )(h, g.reshape(1, D), du, dres)


def _loss_head(h, g, tgt, name):
    T, D = h.shape
    R = min(T, ROWS)

    def body(h_ref, g_ref, t_ref, loss_ref, dh_ref, dhb_ref, dg_ref):
        @pl.when(pl.program_id(0) == 0)
        def _():
            dg_ref[...] = jnp.zeros_like(dg_ref)
            loss_ref[...] = jnp.zeros_like(loss_ref)

        x = h_ref[...]
        r = lax.rsqrt(jnp.mean(x * x, axis=-1, keepdims=True) + EPS)
        xhat = x * r
        gg = g_ref[...]
        err = xhat * gg - t_ref[...]
        loss_ref[...] += 0.5 * jnp.sum(jnp.mean(err * err, axis=-1, keepdims=True), axis=0, keepdims=True)
        dy = err * (1.0 / D)
        dg_ref[...] += jnp.sum(dy * xhat, axis=0, keepdims=True)
        dxhat = dy * gg
        dh = r * (dxhat - xhat * jnp.mean(dxhat * xhat, axis=-1, keepdims=True))
        dh_ref[...] = dh
        dhb_ref[...] = dh.astype(dhb_ref.dtype)

    blk = pl.BlockSpec((R, D), lambda i: (i, 0))
    vec = pl.BlockSpec((1, D), lambda i: (0, 0))
    one = pl.BlockSpec((1, LANES), lambda i: (0, 0))
    return pl.pallas_call(
        body, name=name, grid=(T // R,), in_specs=[blk, vec, blk], out_specs=[one, blk, blk, vec],
        out_shape=[jax.ShapeDtypeStruct((1, LANES), F32), jax.ShapeDtypeStruct((T, D), F32),
                   jax.ShapeDtypeStruct((T, D), BF16), jax.ShapeDtypeStruct((1, D), F32)],
        compiler_params=_cp("arbitrary"),
    )(h, g.reshape(1, D), tgt)


def _mm(a, b, res, out_dtype, name, tm=512, tn=None, more=None):
    M, K = a.shape
    N = b.shape[1]
    tm = min(tm, M)
    tn = N if tn is None else tn
    pairs = [(a, b)] + ([more] if more is not None else [])

    def body(*refs):
        o_ref = refs[-1]
        acc = _dot(refs[0][...], refs[1][...])
        if more is not None:
            acc = acc + _dot(refs[2][...], refs[3][...])
        if res is not None:
            acc = refs[2 * len(pairs)][...].astype(F32) + acc
        o_ref[...] = acc.astype(o_ref.dtype)

    in_specs, args = [], []
    for x, y in pairs:
        in_specs += [pl.BlockSpec((tm, x.shape[1]), lambda j, i: (i, 0)),
                     pl.BlockSpec((x.shape[1], tn), lambda j, i: (0, j))]
        args += [x, y]
    if res is not None:
        in_specs.append(pl.BlockSpec((tm, tn), lambda j, i: (i, j)))
        args.append(res)
    return pl.pallas_call(
        body, name=name, grid=(N // tn, M // tm), in_specs=in_specs,
        out_specs=pl.BlockSpec((tm, tn), lambda j, i: (i, j)),
        out_shape=jax.ShapeDtypeStruct((M, N), out_dtype), compiler_params=_cp("parallel", "parallel"),
    )(*args)


def _mm_tn_into(slab, a, b, layer, chip0, name, tk, tn, tm=1024):
    M, K = a.shape
    N = b.shape[1]
    tm = min(tm, M)
    assert slab.shape[2] == tn and (K // tk == 1 or N // tn == 1)

    def body(a_ref, b_ref, slab_ref, o_ref):
        del slab_ref

        @pl.when(pl.program_id(2) == 0)
        def _():
            o_ref[...] = jnp.zeros_like(o_ref)

        o_ref[0] += _dot_tn(a_ref[...], b_ref[...])

    return pl.pallas_call(
        body, name=name, grid=(K // tk, N // tn, M // tm),
        in_specs=[pl.BlockSpec((tm, tk), lambda k, n, m: (m, k)), pl.BlockSpec((tm, tn), lambda k, n, m: (m, n)),
                  pl.BlockSpec(memory_space=pl.ANY)],
        out_specs=pl.BlockSpec((1, tk, tn), lambda k, n, m: (chip0 + k + n, layer, 0)),
        out_shape=jax.ShapeDtypeStruct(slab.shape, F32), input_output_aliases={2: 0},
        compiler_params=_cp("parallel", "parallel", "arbitrary"),
    )(a, b, slab)


def _mm_tn(a, b, name, tm=2048, tk=None, tn=None):
    M, K = a.shape
    N = b.shape[1]
    tm = min(tm, M)
    tk = K if tk is None else tk
    tn = N if tn is None else tn

    def body(a_ref, b_ref, o_ref):
        @pl.when(pl.program_id(2) == 0)
        def _():
            o_ref[...] = jnp.zeros_like(o_ref)

        o_ref[...] += _dot_tn(a_ref[...], b_ref[...])

    return pl.pallas_call(
        body, name=name, grid=(K // tk, N // tn, M // tm),
        in_specs=[pl.BlockSpec((tm, tk), lambda k, n, m: (m, k)), pl.BlockSpec((tm, tn), lambda k, n, m: (m, n))],
        out_specs=pl.BlockSpec((tk, tn), lambda k, n, m: (k, n)),
        out_shape=jax.ShapeDtypeStruct((K, N), F32), compiler_params=_cp("parallel", "parallel", "arbitrary"),
    )(a, b)


def _same_chunk(row, col):
    shift = CHUNK.bit_length() - 1
    return jnp.right_shift(row, shift) == jnp.right_shift(col, shift)


def _gla_common(q, k, glr, w2, b2, R):
    gl = _dot(glr, w2) + b2
    la = jax.nn.log_sigmoid(gl) * (1.0 / 16.0)
    row = lax.broadcasted_iota(jnp.int32, (R, R), 0)
    col = lax.broadcasted_iota(jnp.int32, (R, R), 1)
    same = _same_chunk(row, col)
    m_tri = (same & (col <= row)).astype(BF16)
    m_all = same.astype(BF16)
    la3 = _split3(la)
    b = sum(jnp.dot(m_tri, p, preferred_element_type=F32) for p in la3)
    bl = sum(jnp.dot(m_all, p, preferred_element_type=F32) for p in la3)
    eb = jnp.exp(b)
    enb = jnp.exp(-b)
    ek = jnp.exp(bl - b)
    qi = (q * (DK ** -0.5)) * eb
    ki = k * enb
    kd = k * ek
    return gl, la3, eb, enb, ek, qi, ki, kd


def _bsplit(x, n):
    return x.reshape(n, CHUNK, x.shape[-1])


def _tril():
    return (lax.broadcasted_iota(jnp.int32, (CHUNK, CHUNK), 1)
            <= lax.broadcasted_iota(jnp.int32, (CHUNK, CHUNK), 0))[None]


def _gla_fwd(proj, w2p, b2, norm_g, name):
    T = proj.shape[0]
    R = min(T, ROWS)
    n = R // CHUNK

    def body(q_ref, k_ref, v_ref, g_ref, a_ref, w2_ref, b2_ref, ng_ref, y_ref, o_ref, st_ref, s_ref):
        @pl.when(pl.program_id(0) == 0)
        def _():
            s_ref[...] = jnp.zeros_like(s_ref)

        _, la3, _, _, _, qi, ki, kd = _gla_common(q_ref[...], k_ref[...], a_ref[...], w2_ref[...], b2_ref[...], R)
        tril = _tril()
        ones = jnp.ones((n, CHUNK, DV), BF16)
        for h in range(HEADS):
            sl = slice(h * DK, (h + 1) * DK)
            sv = slice(h * DV, (h + 1) * DV)
            qh = _bsplit(qi[:, sl], n).astype(MXU_DTYPE)
            kh = _bsplit(ki[:, sl], n).astype(MXU_DTYPE)
            kdh = _bsplit(kd[:, sl], n).astype(MXU_DTYPE)
            vh = _bsplit(v_ref[:, sv], n).astype(MXU_DTYPE)
            att = jnp.where(tril, _bdot('ncd,nsd->ncs', qh, kh), 0.0)
            upd = _bdot('ncd,nce->nde', kdh, vh)
            dect = jnp.exp(sum(_bdot('ncd,nce->nde', _bsplit(p[:, sl], n), ones) for p in la3))
            s = s_ref[sl, :]
            for c in range(n):
                st_ref[c, sl, :] = s
                s = dect[c] * s + upd[c]
            s_ref[sl, :] = s
            sp = st_ref[:, sl, :].astype(MXU_DTYPE)
            o = (_bdot('ncs,nse->nce', att.astype(MXU_DTYPE), vh) + _bdot('ncd,nde->nce', qh, sp)).reshape(R, DV)
            o_ref[:, sv] = o
            r = lax.rsqrt(jnp.mean(o * o, axis=-1, keepdims=True) + EPS)
            gate = g_ref[:, sv]
            y_ref[:, sv] = (((o * r) * ng_ref[...]) * (gate * jax.nn.sigmoid(gate))).astype(y_ref.dtype)

    cb = lambda w, j: pl.BlockSpec((R, w), lambda i: (i, j))
    full = lambda s: pl.BlockSpec(s, lambda i: (0,) * len(s))
    return pl.pallas_call(
        body, name=name, grid=(T // R,),
        in_specs=[cb(QK_W, 0), cb(QK_W, 1), cb(GLA_W, 1), cb(GLA_W, 2), cb(GATE_PAD, 12),
                  full((GATE_PAD, QK_W)), full((1, QK_W)), full((1, DV))],
        out_specs=[pl.BlockSpec((R, GLA_W), lambda i: (i, 0)), pl.BlockSpec((R, GLA_W), lambda i: (i, 0)),
                   pl.BlockSpec((n, QK_W, DV), lambda i: (i, 0, 0))],
        out_shape=[jax.ShapeDtypeStruct((T, GLA_W + LRU_W), BF16), jax.ShapeDtypeStruct((T, GLA_W), F32),
                   jax.ShapeDtypeStruct((T // CHUNK, QK_W, DV), F32)],
        scratch_shapes=[pltpu.VMEM((QK_W, DV), F32)],
        compiler_params=_cp("arbitrary"),
    )(proj, proj, proj, proj, proj, w2p, b2.reshape(1, QK_W), norm_g.reshape(1, DV))


def _gla_bwd(dy, proj, o_st, s_st, w2p, b2, norm_g, name):
    T = proj.shape[0]
    R = min(T, ROWS)
    n = R // CHUNK
    nb = T // R

    def body(dy_ref, q_ref, k_ref, v_ref, g_ref, a_ref, o_ref, st_ref, w2_ref, b2_ref, ng_ref,
             dp_ref, dw2_ref, db2_ref, dng_ref, gs_ref, gn_ref, db_ref, dbl_ref):
        @pl.when(pl.program_id(0) == 0)
        def _():
            gs_ref[...] = jnp.zeros_like(gs_ref)
            dw2_ref[...] = jnp.zeros_like(dw2_ref)
            db2_ref[...] = jnp.zeros_like(db2_ref)
            dng_ref[...] = jnp.zeros_like(dng_ref)

        glr = a_ref[...]
        gl, la3, eb, enb, ek, qi, ki, kd = _gla_common(q_ref[...], k_ref[...], glr, w2_ref[...], b2_ref[...], R)
        tril = _tril()
        ones = jnp.ones((n, CHUNK, DV), BF16)
        ng = ng_ref[...]
        dng = jnp.zeros((1, DV), F32)
        for h in range(HEADS):
            sl = slice(h * DK, (h + 1) * DK)
            sv = slice(h * DV, (h + 1) * DV)
            o = o_ref[:, sv]
            r = lax.rsqrt(jnp.mean(o * o, axis=-1, keepdims=True) + EPS)
            xhat = o * r
            gate = g_ref[:, sv]
            sg = jax.nn.sigmoid(gate)
            dyh = dy_ref[:, sv].astype(F32)
            dp_ref[:, 2 * QK_W + GLA_W + h * DV:2 * QK_W + GLA_W + (h + 1) * DV] = (
                dyh * (xhat * ng) * (sg * (1.0 + gate * (1.0 - sg)))).astype(dp_ref.dtype)
            don = dyh * (gate * sg)
            dng = dng + jnp.sum(don * xhat, axis=0, keepdims=True)
            dxhat = don * ng
            do = r * (dxhat - xhat * jnp.mean(dxhat * xhat, axis=-1, keepdims=True))
            qf = _bsplit(qi[:, sl], n)
            kf = _bsplit(ki[:, sl], n)
            kdf = _bsplit(kd[:, sl], n)
            qh, kh, kdh = qf.astype(MXU_DTYPE), kf.astype(MXU_DTYPE), kdf.astype(MXU_DTYPE)
            vh = _bsplit(v_ref[:, sv], n).astype(MXU_DTYPE)
            doh = _bsplit(do, n).astype(MXU_DTYPE)
            spf = st_ref[:, sl, :]
            sp = spf.astype(MXU_DTYPE)
            att = jnp.where(tril, _bdot('ncd,nsd->ncs', qh, kh), 0.0).astype(MXU_DTYPE)
            datt = jnp.where(tril, _bdot('nce,nse->ncs', doh, vh), 0.0).astype(MXU_DTYPE)
            dv = _bdot('ncs,nce->nse', att, doh)
            dqi = _bdot('ncs,nsd->ncd', datt, kh) + _bdot('nce,nde->ncd', doh, sp)
            dki = _bdot('ncs,ncd->nsd', datt, qh)
            wgt = _bdot('ncd,nce->nde', qh, doh)
            dect = jnp.exp(sum(_bdot('ncd,nce->nde', _bsplit(p[:, sl], n), ones) for p in la3))
            g = gs_ref[sl, :]
            for c in reversed(range(n)):
                gn_ref[c] = g
                g = wgt[c] + dect[c] * g
            gs_ref[sl, :] = g
            gnf = gn_ref[...]
            gn = gnf.astype(MXU_DTYPE)
            dkd = _bdot('nce,nde->ncd', vh, gn)
            dv = dv + _bdot('ncd,nde->nce', kdh, gn)
            dp_ref[:, 2 * QK_W + h * DV:2 * QK_W + (h + 1) * DV] = dv.reshape(R, DV).astype(dp_ref.dtype)
            dbl = sum(_bdot('nce,nde->ncd', ones, p) for p in _split3(gnf * spf * dect))
            pk = dkd * kdf
            dbl = dbl + jnp.sum(pk, axis=1, keepdims=True)
            dbl_ref[:, sl] = dbl.reshape(R, DK)
            db_ref[:, sl] = (dqi * qf - dki * kf - pk).reshape(R, DK)
            dp_ref[:, sl] = ((dqi.reshape(R, DK) * (DK ** -0.5)) * eb[:, sl]).astype(dp_ref.dtype)
            dp_ref[:, QK_W + h * DK:QK_W + (h + 1) * DK] = (
                dki.reshape(R, DK) * enb[:, sl] + dkd.reshape(R, DK) * ek[:, sl]).astype(dp_ref.dtype)
        dng_ref[...] += dng
        row = lax.broadcasted_iota(jnp.int32, (R, R), 0)
        col = lax.broadcasted_iota(jnp.int32, (R, R), 1)
        m_rev = (_same_chunk(row, col) & (col >= row)).astype(BF16)
        dla = sum(jnp.dot(m_rev, p, preferred_element_type=F32) for p in _split3(db_ref[...])) + dbl_ref[...]
        dgl = (dla * (1.0 / 16.0)) * jax.nn.sigmoid(-gl)
        dp_ref[:, 2 * QK_W + 2 * GLA_W:GLA_COLS] = _dot_nt(dgl, w2_ref[...]).astype(dp_ref.dtype)
        dw2_ref[...] += _dot_tn(glr, dgl)
        db2_ref[...] += jnp.sum(dgl, axis=0, keepdims=True)

    cb = lambda w, j: pl.BlockSpec((R, w), lambda i: (nb - 1 - i, j))
    full = lambda s: pl.BlockSpec(s, lambda i: (0,) * len(s))
    return pl.pallas_call(
        body, name=name, grid=(nb,),
        in_specs=[cb(GLA_W, 0), cb(QK_W, 0), cb(QK_W, 1), cb(GLA_W, 1), cb(GLA_W, 2), cb(GATE_PAD, 12),
                  cb(GLA_W, 0), pl.BlockSpec((n, QK_W, DV), lambda i: (nb - 1 - i, 0, 0)),
                  full((GATE_PAD, QK_W)), full((1, QK_W)), full((1, DV))],
        out_specs=[pl.BlockSpec((R, GLA_COLS), lambda i: (nb - 1 - i, 0)),
                   full((GATE_PAD, QK_W)), full((1, QK_W)), full((1, DV))],
        out_shape=[jax.ShapeDtypeStruct((T, GLA_COLS), BF16), jax.ShapeDtypeStruct((GATE_PAD, QK_W), F32),
                   jax.ShapeDtypeStruct((1, QK_W), F32), jax.ShapeDtypeStruct((1, DV), F32)],
        scratch_shapes=[pltpu.VMEM((QK_W, DV), F32), pltpu.VMEM((n, DK, DV), F32),
                        pltpu.VMEM((R, QK_W), F32), pltpu.VMEM((R, QK_W), F32)],
        compiler_params=_cp("arbitrary"),
    )(dy, proj, proj, proj, proj, proj, o_st, s_st, w2p, b2.reshape(1, QK_W), norm_g.reshape(1, DV))


def _scan_scratch(R, W):
    return [pltpu.VMEM((W // LANES, R, LANES), F32), pltpu.VMEM((W // LANES, R, LANES), F32),
            pltpu.VMEM((W // LANES, R // SUBLANES, LANES), F32)]


def _scan_rows(a, u, c0, a_ref, u_ref, c_ref, out_ref, reverse):
    R, W = a.shape
    nt = R // SUBLANES
    shift = _shift_up if reverse else _shift_down
    a, u = a.reshape(nt, SUBLANES, W), u.reshape(nt, SUBLANES, W)
    sub = lax.broadcasted_iota(jnp.int32, (nt, SUBLANES, W), 1)
    for k in (1, 2, 4):
        inside = (sub < SUBLANES - k) if reverse else (sub >= k)
        turn = SUBLANES - k if reverse else k
        u = u + a * jnp.where(inside, pltpu.roll(u, turn, axis=1), 0.0)
        a = a * jnp.where(inside, pltpu.roll(a, turn, axis=1), 1.0)
    a, u = a.reshape(R, W), u.reshape(R, W)
    end = 0 if reverse else SUBLANES - 1
    edge = nt - 1 if reverse else 0
    for j in range(W // LANES):
        cols = slice(j * LANES, (j + 1) * LANES)
        a_ref[j] = a[:, cols]
        u_ref[j] = u[:, cols]
        at = a_ref.at[j][pl.ds(end, nt, stride=SUBLANES), :]
        ut = u_ref.at[j][pl.ds(end, nt, stride=SUBLANES), :]
        k = 1
        while k < nt:
            ut = ut + at * shift(ut, k, 0.0)
            at = at * shift(at, k, 1.0)
            k *= 2
        c_ref[j] = shift(ut + at * c0[:, cols], 1, 0.0)
        c_ref[j, edge:edge + 1, :] = c0[:, cols]
        for r in range(nt):
            rows = pl.ds(r * SUBLANES, SUBLANES)
            out_ref[rows, cols] = u_ref[j, rows, :] + a_ref[j, rows, :] * c_ref[j, r:r + 1, :]


def _lru_conv(ext_ref, cw_ref, cb_ref, R):
    xc = cb_ref[...] + ext_ref[pl.ds(SUBLANES - 3, R), :] * cw_ref[0:1, :]
    xc = xc + ext_ref[pl.ds(SUBLANES - 2, R), :] * cw_ref[1:2, :]
    xc = xc + ext_ref[pl.ds(SUBLANES - 1, R), :] * cw_ref[2:3, :]
    return xc + ext_ref[pl.ds(SUBLANES, R), :] * cw_ref[3:4, :]


def _lru_gates(xc, wa, ba, wx, bx, lam, first):
    r = jax.nn.sigmoid(_dot(xc, wa) + ba)
    ig = jax.nn.sigmoid(_dot(xc, wx) + bx)
    sp = jax.nn.softplus(-lam)
    la = (-LRU_C * r) * sp
    a = jnp.exp(la)
    mult = jnp.where(first, 1.0, jnp.sqrt(-_expm1(2.0 * la)))
    return r, ig, sp, a, mult


def _lru_fwd(proj, y_mix, cw, cb, wa, ba, wx, bx, lam, name):
    T = proj.shape[0]
    R = min(T, ROWS)
    W = LRU_W

    def body(xr_ref, xh_ref, xg_ref, cw_ref, cb_ref, wa_ref, ba_ref, wx_ref, bx_ref, lam_ref, mix_ref,
             y_ref, hs_ref, ext_ref, hc_ref, sa_ref, su_ref, sc_ref):
        del mix_ref
        i = pl.program_id(0)

        @pl.when(i == 0)
        def _():
            hc_ref[...] = jnp.zeros_like(hc_ref)

        ext_ref[0:SUBLANES, :] = jnp.where(i > 0, xh_ref[...], 0.0)
        ext_ref[pl.ds(SUBLANES, R), :] = xr_ref[...]
        xc = _lru_conv(ext_ref, cw_ref, cb_ref, R)
        row = lax.broadcasted_iota(jnp.int32, (R, W), 0)
        first = (row == 0) & (i == 0)
        _, ig, _, a, mult = _lru_gates(xc, wa_ref[...], ba_ref[...], wx_ref[...], bx_ref[...], lam_ref[...], first)
        _scan_rows(a, mult * (ig * xc), hc_ref[0:1, :], sa_ref, su_ref, sc_ref, hs_ref, reverse=False)
        hc_ref[0:1, :] = hs_ref[R - 1:R, :]
        y_ref[...] = (hs_ref[...] * _gelu(xg_ref[...])).astype(y_ref.dtype)

    rb = R // SUBLANES
    full = lambda s: pl.BlockSpec(s, lambda i: (0,) * len(s))
    return pl.pallas_call(
        body, name=name, grid=(T // R,),
        in_specs=[pl.BlockSpec((R, W), lambda i: (i, 0)),
                  pl.BlockSpec((SUBLANES, W), lambda i: (jnp.maximum(i * rb - 1, 0), 0)),
                  pl.BlockSpec((R, W), lambda i: (i, 1)),
                  full((SUBLANES, W)), full((1, W)), full((W, W)), full((1, W)), full((W, W)), full((1, W)),
                  full((1, W)), pl.BlockSpec(memory_space=pl.ANY)],
        out_specs=[pl.BlockSpec((R, W), lambda i: (i, 1)), pl.BlockSpec((R, W), lambda i: (i, 0))],
        out_shape=[jax.ShapeDtypeStruct(y_mix.shape, y_mix.dtype), jax.ShapeDtypeStruct((T, W), F32)],
        scratch_shapes=[pltpu.VMEM((R + SUBLANES, W), F32), pltpu.VMEM((SUBLANES, W), F32),
                        *_scan_scratch(R, W)],
        input_output_aliases={10: 0}, compiler_params=_cp("arbitrary"),
    )(proj, proj, proj, cw, cb, wa, ba, wx, bx, lam, y_mix)


def _lru_bwd(dy, proj, hs, cw, cb, wa, ba, wx, bx, lam, name):
    T = proj.shape[0]
    R = min(T, ROWS)
    W = LRU_W
    nb = T // R

    def body(dy_ref, xr_ref, xh_ref, xg_ref, hs_ref, hh_ref, cw_ref, cb_ref, wa_ref, ba_ref, wx_ref, bx_ref, lam_ref,
             dp_ref, dcw_ref, dvec_ref, dwa_ref, dwx_ref, ext_ref, ext2_ref, lc_ref, sa_ref, su_ref, sc_ref, adj_ref):
        ib = pl.program_id(0)
        i = nb - 1 - ib

        @pl.when(ib == 0)
        def _():
            lc_ref[...] = jnp.zeros_like(lc_ref)
            ext2_ref[pl.ds(R, SUBLANES), :] = jnp.zeros((SUBLANES, W), F32)
            dcw_ref[...] = jnp.zeros_like(dcw_ref)
            dvec_ref[...] = jnp.zeros_like(dvec_ref)
            dwa_ref[...] = jnp.zeros_like(dwa_ref)
            dwx_ref[...] = jnp.zeros_like(dwx_ref)

        ext_ref[0:SUBLANES, :] = jnp.where(i > 0, xh_ref[...], 0.0)
        ext_ref[pl.ds(SUBLANES, R), :] = xr_ref[...]
        xc = _lru_conv(ext_ref, cw_ref, cb_ref, R)
        row = lax.broadcasted_iota(jnp.int32, (R, W), 0)
        first = (row == 0) & (i == 0)
        lam = lam_ref[...]
        r, ig, sp, a, mult = _lru_gates(xc, wa_ref[...], ba_ref[...], wx_ref[...], bx_ref[...], lam, first)
        h = hs_ref[...]
        gel, dgel = _gelu_and_grad(xg_ref[...])
        dy = dy_ref[...].astype(F32)
        dp_ref[:, W:2 * W] = (dy * h * dgel).astype(dp_ref.dtype)
        _scan_rows(_shift_up(a, 1, 1.0), dy * gel, lc_ref[0:1, :], sa_ref, su_ref, sc_ref, adj_ref, reverse=True)
        v = adj_ref[...]
        lc_ref[...] = (a * v)[0:SUBLANES, :]
        hprev = _shift_down(h, 1, 0.0) + jnp.where((row == 0) & (i > 0), hh_ref[SUBLANES - 1:SUBLANES, :], 0.0)
        da = v * hprev
        dmult = jnp.where(first, 0.0, v * (ig * xc))
        dig = v * (mult * xc)
        dxc = v * (mult * ig)
        dla = da * a - dmult * ((a * a) / mult)
        dra = (dla * (-LRU_C * sp)) * (r * (1.0 - r))
        drx = dig * (ig * (1.0 - ig))
        dxc = dxc + _dot_nt(dra, wa_ref[...]) + _dot_nt(drx, wx_ref[...])
        dwa_ref[...] += _dot_tn(xc, dra)
        dwx_ref[...] += _dot_tn(xc, drx)
        dvec_ref[0:1, :] += jnp.sum(dxc, axis=0, keepdims=True)
        dvec_ref[1:2, :] += jnp.sum(dra, axis=0, keepdims=True)
        dvec_ref[2:3, :] += jnp.sum(drx, axis=0, keepdims=True)
        dvec_ref[3:4, :] += jnp.sum(dla * (-LRU_C * r), axis=0, keepdims=True) * (-jax.nn.sigmoid(-lam))
        ext2_ref[pl.ds(0, R), :] = dxc
        dxr = ext2_ref[pl.ds(0, R), :] * cw_ref[3:4, :]
        dxr = dxr + ext2_ref[pl.ds(1, R), :] * cw_ref[2:3, :]
        dxr = dxr + ext2_ref[pl.ds(2, R), :] * cw_ref[1:2, :]
        dxr = dxr + ext2_ref[pl.ds(3, R), :] * cw_ref[0:1, :]
        dp_ref[:, 0:W] = dxr.astype(dp_ref.dtype)
        for j in range(LRU_CONV):
            dcw_ref[j:j + 1, :] += jnp.sum(dxc * ext_ref[pl.ds(SUBLANES - 3 + j, R), :], axis=0, keepdims=True)
        ext2_ref[pl.ds(R, SUBLANES), :] = dxc[0:SUBLANES, :]

    rb = R // SUBLANES
    full = lambda s: pl.BlockSpec(s, lambda i: (0,) * len(s))
    blk = lambda j: pl.BlockSpec((R, W), lambda i: (nb - 1 - i, j))
    halo = pl.BlockSpec((SUBLANES, W), lambda i: (jnp.maximum((nb - 1 - i) * rb - 1, 0), 0))
    return pl.pallas_call(
        body, name=name, grid=(nb,),
        in_specs=[blk(1), blk(0), halo, blk(1), blk(0), halo,
                  full((SUBLANES, W)), full((1, W)), full((W, W)), full((1, W)), full((W, W)), full((1, W)),
                  full((1, W))],
        out_specs=[pl.BlockSpec((R, 2 * W), lambda i: (nb - 1 - i, 0)), full((SUBLANES, W)), full((SUBLANES, W)),
                   full((W, W)), full((W, W))],
        out_shape=[jax.ShapeDtypeStruct((T, 2 * W), BF16), jax.ShapeDtypeStruct((SUBLANES, W), F32),
                   jax.ShapeDtypeStruct((SUBLANES, W), F32), jax.ShapeDtypeStruct((W, W), F32),
                   jax.ShapeDtypeStruct((W, W), F32)],
        scratch_shapes=[pltpu.VMEM((R + SUBLANES, W), F32), pltpu.VMEM((R + SUBLANES, W), F32),
                        pltpu.VMEM((SUBLANES, W), F32), *_scan_scratch(R, W), pltpu.VMEM((R, W), F32)],
        compiler_params=_cp("arbitrary"),
    )(dy, proj, proj, proj, hs, hs, cw, cb, wa, ba, wx, bx, lam)


def _conv3_window(src, start, cs, w, b):
    win = src[pl.ds(start, FFN_RC + SUBLANES), cs]
    x2 = pltpu.roll(win, 2, axis=0)[SUBLANES:]
    x1 = pltpu.roll(win, 1, axis=0)[SUBLANES:]
    x0 = win[SUBLANES:]
    return ((b + x2 * w[0]) + x1 * w[1]) + x0 * w[2], (x2, x1, x0)


def _ffn_up_fwd(u2, fa, fg, cwa, cwg, cba, cbg, name):
    T, D = u2.shape
    Fh = fa.shape[1]
    tm = min(T, ROWS)
    CW, SB, RC = FFN_CW, FFN_SB, FFN_RC
    ns = CW // SB

    def body(u_ref, fa_ref, fg_ref, cwa_ref, cwg_ref, cba_ref, cbg_ref, xa_ref, xg_ref, p_ref, q_ref, act_ref,
             ka_ref, kg_ref, ea_ref, eg_ref, za_ref, zg_ref):
        @pl.when(pl.program_id(1) == 0)
        def _():
            ka_ref[...] = jnp.zeros_like(ka_ref)
            kg_ref[...] = jnp.zeros_like(kg_ref)

        def gate(s):
            cs = pl.ds(s * SB, SB)
            wa = [cwa_ref[j:j + 1, cs] for j in range(FFN_CONV)]
            wg = [cwg_ref[j:j + 1, cs] for j in range(FFN_CONV)]
            ba, bg = cba_ref[:, cs], cbg_ref[:, cs]
            ea_ref[0:SUBLANES, cs] = ka_ref[:, cs]
            ea_ref[pl.ds(SUBLANES, RC), cs] = za_ref[0:RC, cs]
            eg_ref[0:SUBLANES, cs] = kg_ref[:, cs]
            eg_ref[pl.ds(SUBLANES, RC), cs] = zg_ref[0:RC, cs]
            for c in range(tm // RC):
                sa, sg, start = (ea_ref, eg_ref, 0) if c == 0 else (za_ref, zg_ref, c * RC - SUBLANES)
                rows = pl.ds(c * RC, RC)
                a_c, xa = _conv3_window(sa, start, cs, wa, ba)
                g_c, xg = _conv3_window(sg, start, cs, wg, bg)
                gel, dgel = _gelu_and_grad(a_c)
                xa_ref[rows, cs] = xa[2].astype(xa_ref.dtype)
                xg_ref[rows, cs] = xg[2].astype(xg_ref.dtype)
                p_ref[rows, cs] = (g_c * dgel).astype(p_ref.dtype)
                q_ref[rows, cs] = gel.astype(q_ref.dtype)
                act_ref[rows, cs] = (gel * g_c).astype(act_ref.dtype)
            ka_ref[:, cs] = za_ref[tm - SUBLANES:tm, cs]
            kg_ref[:, cs] = zg_ref[tm - SUBLANES:tm, cs]

        za_ref[...] = _dot(u_ref[...], fa_ref[...])
        zg_ref[...] = _dot(u_ref[...], fg_ref[...])
        for s in range(ns):
            gate(s)

    blk = pl.BlockSpec((tm, CW), lambda j, i: (i, j))
    wblk = pl.BlockSpec((D, CW), lambda j, i: (0, j))
    w8 = pl.BlockSpec((SUBLANES, CW), lambda j, i: (0, j))
    w1 = pl.BlockSpec((1, CW), lambda j, i: (0, j))
    return pl.pallas_call(
        body, name=name, grid=(Fh // CW, T // tm),
        in_specs=[pl.BlockSpec((tm, D), lambda j, i: (i, 0)), wblk, wblk, w8, w8, w1, w1],
        out_specs=[blk] * 5,
        out_shape=[jax.ShapeDtypeStruct((T, Fh), BF16)] * 5,
        scratch_shapes=[pltpu.VMEM((SUBLANES, CW), F32), pltpu.VMEM((SUBLANES, CW), F32),
                        pltpu.VMEM((RC + SUBLANES, CW), F32), pltpu.VMEM((RC + SUBLANES, CW), F32),
                        pltpu.VMEM((tm, CW), F32), pltpu.VMEM((tm, CW), F32)],
        compiler_params=_cp("parallel", "arbitrary"),
    )(u2, fa, fg, cwa, cwg, cba, cbg)


def _ffn_bwd_core(dhb, xa, xg, p, q, wdT, faT, fgT, cwa, cwg, name):
    T, D = dhb.shape
    Fh = xa.shape[1]
    tm = min(T, ROWS)
    CW, SB, RC = FFN_CW, FFN_SB, FFN_RC
    ns = CW // SB
    nj = Fh // CW
    nb = T // tm
    nc = tm // RC

    def body(dh_ref, xa_ref, xg_ref, p_ref, q_ref, wd_ref, fa_ref, fg_ref, cwa_ref, cwg_ref,
             dza_ref, dzg_ref, du_ref, dca_ref, dcg_ref, d_ref, sa_ref, sg_ref, ka_ref, kg_ref):
        ib, j = pl.program_id(0), pl.program_id(1)
        i = nb - 1 - ib

        @pl.when((ib == 0) & (j == 0))
        def _():
            dca_ref[...] = jnp.zeros_like(dca_ref)
            dcg_ref[...] = jnp.zeros_like(dcg_ref)

        @pl.when(ib == 0)
        def _():
            ka_ref[j] = jnp.zeros((SUBLANES, CW), F32)
            kg_ref[j] = jnp.zeros((SUBLANES, CW), F32)

        @pl.when(j == 0)
        def _():
            du_ref[...] = jnp.zeros_like(du_ref)

        def fold(v):
            return jnp.sum(v.reshape(RC // SUBLANES, SUBLANES, SB), axis=0)

        def gate(s):
            cs = pl.ds(s * SB, SB)
            wa = [cwa_ref[t:t + 1, cs] for t in range(FFN_CONV)]
            wg = [cwg_ref[t:t + 1, cs] for t in range(FFN_CONV)]
            sa_ref[pl.ds(tm, SUBLANES), cs] = ka_ref[j, :, cs]
            sg_ref[pl.ds(tm, SUBLANES), cs] = kg_ref[j, :, cs]
            for c in range(nc):
                rows = pl.ds(c * RC, RC)
                dact = d_ref[rows, cs]
                sa_ref[rows, cs] = dact * p_ref[rows, cs].astype(F32)
                sg_ref[rows, cs] = dact * q_ref[rows, cs].astype(F32)
            n = RC + SUBLANES
            for s_ref, x_ref, o_ref, dc_ref, w in ((sa_ref, xa_ref, dza_ref, dca_ref, wa),
                                                   (sg_ref, xg_ref, dzg_ref, dcg_ref, wg)):
                acc = [jnp.zeros((SUBLANES, SB), F32) for _ in range(FFN_CONV + 1)]
                for c in range(nc):
                    rows = pl.ds(c * RC, RC)
                    win = s_ref[pl.ds(c * RC, n), cs]
                    d0 = win[:RC]
                    d1 = pltpu.roll(win, n - 1, axis=0)[:RC]
                    d2 = pltpu.roll(win, n - 2, axis=0)[:RC]
                    o_ref[rows, cs] = ((d0 * w[2] + d1 * w[1]) + d2 * w[0]).astype(o_ref.dtype)
                    x = x_ref[rows, cs].astype(F32)
                    acc = [acc[0] + fold(d2 * x), acc[1] + fold(d1 * x), acc[2] + fold(d0 * x), acc[3] + fold(d0)]
                for t in range(FFN_CONV + 1):
                    dc_ref[j, t:t + 1, cs] += jnp.sum(acc[t], axis=0, keepdims=True)
            ka_ref[j, :, cs] = sa_ref[0:SUBLANES, cs]
            kg_ref[j, :, cs] = sg_ref[0:SUBLANES, cs]

        d_ref[...] = _dot(dh_ref[...], wd_ref[...])
        for s in range(ns):
            gate(s)
        du_ref[...] += _dot(dza_ref[...], fa_ref[...]) + _dot(dzg_ref[...], fg_ref[...])

    blk = pl.BlockSpec((tm, CW), lambda ib, j: (nb - 1 - ib, j))
    w8 = pl.BlockSpec((SUBLANES, CW), lambda ib, j: (0, j))
    wrow = pl.BlockSpec((CW, D), lambda ib, j: (j, 0))
    acc = pl.BlockSpec((nj, SUBLANES, CW), lambda ib, j: (0, 0, 0))
    return pl.pallas_call(
        body, name=name, grid=(nb, nj),
        in_specs=[pl.BlockSpec((tm, D), lambda ib, j: (nb - 1 - ib, 0)), blk, blk, blk, blk,
                  pl.BlockSpec((D, CW), lambda ib, j: (0, j)), wrow, wrow, w8, w8],
        out_specs=[blk, blk, pl.BlockSpec((tm, D), lambda ib, j: (nb - 1 - ib, 0)), acc, acc],
        out_shape=[jax.ShapeDtypeStruct((T, Fh), BF16), jax.ShapeDtypeStruct((T, Fh), BF16),
                   jax.ShapeDtypeStruct((T, D), F32), jax.ShapeDtypeStruct((nj, SUBLANES, CW), F32),
                   jax.ShapeDtypeStruct((nj, SUBLANES, CW), F32)],
        scratch_shapes=[pltpu.VMEM((tm, CW), F32), pltpu.VMEM((tm + SUBLANES, CW), F32),
                        pltpu.VMEM((tm + SUBLANES, CW), F32), pltpu.VMEM((nj, SUBLANES, CW), F32),
                        pltpu.VMEM((nj, SUBLANES, CW), F32)],
        compiler_params=_cp("arbitrary", "arbitrary"),
    )(dhb, xa, xg, p, q, wdT, faT, fgT, cwa, cwg)


def _adamw(w, grads, m, v, name):
    rows, cols = w.shape
    tr = _row_tile(rows, max(SUBLANES, min(512, TILE_BYTES // (4 * cols)) // SUBLANES * SUBLANES))
    ng = len(grads)

    def body(*refs):
        w_ref, g_refs, m_ref, v_ref = refs[0], refs[1:1 + ng], refs[1 + ng], refs[2 + ng]
        go_ref, d_ref, mo_ref, vo_ref = refs[3 + ng:]
        g = g_refs[0][...]
        for r in g_refs[1:]:
            g = g + r[...]
        mm = ADAM_B1 * m_ref[...] + (1.0 - ADAM_B1) * g
        vv = ADAM_B2 * v_ref[...] + (1.0 - ADAM_B2) * (g * g)
        m_hat = mm / (1.0 - ADAM_B1 ** ADAM_STEP)
        v_hat = vv / (1.0 - ADAM_B2 ** ADAM_STEP)
        go_ref[...] = g
        d_ref[...] = -ADAM_LR * (m_hat / (jnp.sqrt(v_hat) + ADAM_EPS) + ADAM_WD * w_ref[...])
        mo_ref[...] = mm
        vo_ref[...] = vv

    blk = pl.BlockSpec((tr, cols), lambda i: (i, 0))
    return pl.pallas_call(
        body, name=name, grid=(rows // tr,), in_specs=[blk] * (3 + ng), out_specs=[blk] * 4,
        out_shape=[jax.ShapeDtypeStruct((rows, cols), F32)] * 4, compiler_params=_cp("parallel"),
    )(w, *grads, m, v)


def _add_slabs(a, b, out_dtype, name):
    n, rows, cols = a.shape
    tr = _row_tile(rows, max(SUBLANES, min(512, TILE_BYTES // (4 * cols)) // SUBLANES * SUBLANES))

    def body(a_ref, b_ref, o_ref):
        o_ref[...] = (a_ref[...] + b_ref[...]).astype(o_ref.dtype)

    blk = pl.BlockSpec((1, tr, cols), lambda k, i: (k, i, 0))
    return pl.pallas_call(
        body, name=name, grid=(n, rows // tr), in_specs=[blk, blk], out_specs=blk,
        out_shape=jax.ShapeDtypeStruct((n, rows, cols), out_dtype), compiler_params=_cp("parallel", "parallel"),
    )(a, b)


def _sum_leading(parts, name, last=None):
    n, rows, cols = parts.shape
    tr = _row_tile(rows, max(SUBLANES, min(512, TILE_BYTES // (4 * cols)) // SUBLANES * SUBLANES))

    def body(*refs):
        p_ref, o_ref = refs[0], refs[-1]
        acc = p_ref[0].astype(F32)
        for d in range(1, n):
            acc = acc + p_ref[d].astype(F32)
        if last is not None:
            acc = acc + refs[1][...].astype(F32)
        o_ref[...] = acc

    blk = pl.BlockSpec((tr, cols), lambda i: (i, 0))
    return pl.pallas_call(
        body, name=name, grid=(rows // tr,),
        in_specs=[pl.BlockSpec((n, tr, cols), lambda i: (0, i, 0))] + ([] if last is None else [blk]), out_specs=blk,
        out_shape=jax.ShapeDtypeStruct((rows, cols), F32), compiler_params=_cp("parallel"),
    )(*((parts,) if last is None else (parts, last)))


def _row_tile(rows, cap=512):
    if rows <= cap:
        return rows
    return max(t for t in range(SUBLANES, cap + 1, SUBLANES) if rows % t == 0)


def _place():
    return lax.axis_index("x"), lax.axis_index("y"), lax.axis_index("c")


def _gather_shards(arrs, name):
    n = len(arrs)
    hl = DEPTH // 2

    def body(*refs):
        ins, outs = refs[:n], refs[n:2 * n]
        send_sems, recv_sems, pass_send, pass_recv = refs[2 * n:]
        x, y, c = _place()
        chips = [(1 - x, y), (x, 1 - y), (1 - x, 1 - y)]
        mine, theirs = pl.ds(c * hl, hl), pl.ds((1 - c) * hl, hl)

        def send(a, j, shard):
            px, py = chips[j]
            return pltpu.make_async_remote_copy(
                src_ref=ins[a].at[mine], dst_ref=outs[a].at[shard, mine], send_sem=send_sems.at[3 * a + j],
                recv_sem=recv_sems.at[3 * a + j], device_id=(px, py, c), device_id_type=MESH)

        def passed(a, j, half):
            px, py = chips[j]
            blk = outs[a].at[2 * px + py, half]
            return pltpu.make_async_remote_copy(
                src_ref=blk, dst_ref=blk, send_sem=pass_send.at[3 * a + j], recv_sem=pass_recv.at[3 * a + j],
                device_id=(x, y, 1 - c), device_id_type=MESH)

        sends = [send(a, j, 2 * x + y) for a in range(n) for j in range(3)]
        for cp in sends:
            cp.start()
        passes = []
        for a in range(n):
            for j, (px, py) in enumerate(chips):
                send(a, j, 2 * px + py).wait_recv()
                passes.append(passed(a, j, mine))
                passes[-1].start()
        for a in range(n):
            for j in range(3):
                passed(a, j, theirs).wait_recv()
        for cp in sends + passes:
            cp.wait_send()

    hbm = pl.BlockSpec(memory_space=pl.ANY)
    return pl.pallas_call(
        body, name=name, in_specs=[hbm] * n, out_specs=[hbm] * n,
        out_shape=[jax.ShapeDtypeStruct((N_CHIPS,) + a.shape, a.dtype) for a in arrs],
        scratch_shapes=[pltpu.SemaphoreType.DMA((3 * n,)), pltpu.SemaphoreType.DMA((3 * n,)),
                        pltpu.SemaphoreType.DMA((3 * n,)), pltpu.SemaphoreType.DMA((3 * n,))],
        compiler_params=pltpu.CompilerParams(has_side_effects=True),
    )(*arrs)


def _other_half_to_sibling(slabs, name):
    n = len(slabs)

    def body(*refs):
        ins, got = refs[:n], refs[n:2 * n]
        send_sems, recv_sems = refs[2 * n:]
        x, y, c = _place()
        copies = [pltpu.make_async_remote_copy(
            src_ref=ins[a].at[:, 1 - c], dst_ref=got[a], send_sem=send_sems.at[a], recv_sem=recv_sems.at[a],
            device_id=(x, y, 1 - c), device_id_type=MESH) for a in range(n)]
        for cp in copies:
            cp.start()
        for cp in copies:
            cp.wait()

    hbm = pl.BlockSpec(memory_space=pl.ANY)
    return pl.pallas_call(
        body, name=name, in_specs=[hbm] * n, out_specs=[hbm] * n,
        out_shape=[jax.ShapeDtypeStruct((s.shape[0],) + s.shape[2:], s.dtype) for s in slabs],
        scratch_shapes=[pltpu.SemaphoreType.DMA((n,)), pltpu.SemaphoreType.DMA((n,))],
        compiler_params=pltpu.CompilerParams(has_side_effects=True),
    )(*slabs)


def _exchange_grads(slabs, small, name):
    n = len(slabs)

    def body(*refs):
        ins, small_ref = refs[:n], refs[n]
        outs, small_out = refs[n + 1:2 * n + 1], refs[2 * n + 1]
        send_sems, recv_sems, ssend, srecv = refs[2 * n + 2:]
        x, y, c = _place()
        chips = [(1 - x, y), (x, 1 - y), (1 - x, 1 - y)]
        me = 4 * x + 2 * y + c
        flips = [(fx, fy, fc) for fx in (0, 1) for fy in (0, 1) for fc in (0, 1)][1:]

        def copy(a, j):
            px, py = chips[j]
            return pltpu.make_async_remote_copy(
                src_ref=ins[a].at[2 * px + py], dst_ref=outs[a].at[j], send_sem=send_sems.at[3 * a + j],
                recv_sem=recv_sems.at[3 * a + j], device_id=(px, py, c), device_id_type=MESH)

        def scopy(k, row):
            fx, fy, fc = flips[k]
            return pltpu.make_async_remote_copy(
                src_ref=small_ref, dst_ref=small_out.at[row], send_sem=ssend.at[k], recv_sem=srecv.at[k],
                device_id=(x ^ fx, y ^ fy, c ^ fc), device_id_type=MESH)

        sends = [copy(a, j) for a in range(n) for j in range(3)] + [scopy(k, me) for k in range(7)]
        for cp in sends:
            cp.start()
        for k, (fx, fy, fc) in enumerate(flips):
            scopy(k, 4 * (x ^ fx) + 2 * (y ^ fy) + (c ^ fc)).wait_recv()
        for a in range(n):
            for j in range(3):
                copy(a, j).wait_recv()
        for cp in sends:
            cp.wait_send()

    hbm = pl.BlockSpec(memory_space=pl.ANY)
    return pl.pallas_call(
        body, name=name, in_specs=[hbm] * (n + 1), out_specs=[hbm] * (n + 1),
        out_shape=[jax.ShapeDtypeStruct((3,) + s.shape[1:], s.dtype) for s in slabs]
        + [jax.ShapeDtypeStruct((N_DEV,) + small.shape, small.dtype)],
        scratch_shapes=[pltpu.SemaphoreType.DMA((3 * n,)), pltpu.SemaphoreType.DMA((3 * n,)),
                        pltpu.SemaphoreType.DMA((7,)), pltpu.SemaphoreType.DMA((7,))],
        compiler_params=pltpu.CompilerParams(has_side_effects=True),
    )(*slabs, small)


def _swap_with_sibling(arrs, name):
    n = len(arrs)

    def body(*refs):
        ins, outs = refs[:n], refs[n:2 * n]
        send_sems, recv_sems = refs[2 * n:]
        x, y, c = _place()
        copies = [pltpu.make_async_remote_copy(
            src_ref=ins[a], dst_ref=outs[a], send_sem=send_sems.at[a], recv_sem=recv_sems.at[a],
            device_id=(x, y, 1 - c), device_id_type=MESH) for a in range(n)]
        for cp in copies:
            cp.start()
        for cp in copies:
            cp.wait()

    hbm = pl.BlockSpec(memory_space=pl.ANY)
    return pl.pallas_call(
        body, name=name, in_specs=[hbm] * n, out_specs=[hbm] * n,
        out_shape=[jax.ShapeDtypeStruct(a.shape, a.dtype) for a in arrs],
        scratch_shapes=[pltpu.SemaphoreType.DMA((n,)), pltpu.SemaphoreType.DMA((n,))],
        compiler_params=pltpu.CompilerParams(has_side_effects=True),
    )(*arrs)


def _block_diag(w):
    eye = jnp.eye(LRU_BLOCKS, dtype=w.dtype)
    return (eye[:, None, :, None] * w[:, :, None, :]).reshape(LRU_W, LRU_W)


def _diag_blocks(m):
    m4 = m.reshape(LRU_BLOCKS, LRU_BLOCK, LRU_BLOCKS, LRU_BLOCK)
    return jnp.stack([m4[b, :, b, :] for b in range(LRU_BLOCKS)])


def _pad_rows(a, rows):
    return jnp.pad(a, ((0, rows - a.shape[0]), (0, 0)))


def _layer_weights(p, l):
    w_in = p["w_in"][l]
    n_gla = 2 * QK_W + 2 * GLA_W
    gate = jnp.pad(w_in[:, n_gla:n_gla + GATE_RANK], ((0, 0), (0, GATE_PAD - GATE_RANK)))
    wg = jnp.concatenate([w_in[:, :n_gla], gate], axis=1)
    wl = w_in[:, n_gla + GATE_RANK:]
    w_out = p["w_out"][l]
    fa, fg = p["ffn_w_in"][l][:, :FFN_H], p["ffn_w_in"][l][:, FFN_H:]
    wd = p["ffn_w_down"][l]
    return dict(
        wg=wg, wl=wl, wgT=wg.T, wlT=wl.T, wo=w_out, woT=w_out.T,
        fa=fa, fg=fg, faT=fa.T, fgT=fg.T, wd=wd, wdT=wd.T,
        w2p=_pad_rows(p["gla_gate_w2"][l], GATE_PAD).astype(BF16),
        wa=_block_diag(p["lru_wa"][l]).astype(BF16), wx=_block_diag(p["lru_wx"][l]).astype(BF16),
        lcw=_pad_rows(p["lru_conv_w"][l], SUBLANES),
        fcwa=_pad_rows(p["ffn_conv_w"][l][:, :FFN_H], SUBLANES), fcwg=_pad_rows(p["ffn_conv_w"][l][:, FFN_H:], SUBLANES),
    )


def _local_step(x, tgt, p):
    row = lambda v: v.reshape(1, -1)
    h = x
    stash = []
    for l in range(DEPTH):
        w = _layer_weights(p, l)
        s = dict(w=w, h0=h)
        u = _rms_fwd(h, p["ln_mix"][l], f"mix_norm_fwd{l}")
        pg = _mm(u, w["wg"], None, F32, f"proj_gla_fwd{l}")
        plr = _mm(u, w["wl"], None, F32, f"proj_lru_fwd{l}")
        ym, o_st, s_st = _gla_fwd(pg, w["w2p"], p["gla_gate_b"][l], p["gla_norm"][l], f"gla_fwd{l}")
        ym, hs = _lru_fwd(plr, ym, w["lcw"], row(p["lru_conv_b"][l]), w["wa"], row(p["lru_ba"][l]), w["wx"],
                          row(p["lru_bx"][l]), row(p["lru_lambda"][l]), f"lru_fwd{l}")
        h = _mm(ym, w["wo"], h, F32, f"out_fwd{l}")
        s.update(u=u, pg=pg, plr=plr, ym=ym, o_st=o_st, s_st=s_st, hs=hs, h1=h)
        u2 = _rms_fwd(h, p["ln_ffn"][l], f"ffn_norm_fwd{l}")
        cba, cbg = row(p["ffn_conv_b"][l][:FFN_H]), row(p["ffn_conv_b"][l][FFN_H:])
        *kept, act = _ffn_up_fwd(u2, w["fa"], w["fg"], w["fcwa"], w["fcwg"], cba, cbg, f"ffn_up_fwd{l}")
        h = _mm(act, w["wd"], h, F32, f"ffn_down_fwd{l}")
        s.update(u2=u2, ffn_kept=kept, act=act)
        stash.append(s)

    loss, dh, dhb, d_ln_final = _loss_head(h, p["ln_final"], tgt, "loss_head")

    g = {k: [None] * DEPTH for k in ("ln_mix", "w_in", "gla_gate_w2", "gla_gate_b", "gla_norm", "lru_conv_w",
                                     "lru_conv_b", "lru_wa", "lru_ba", "lru_wx", "lru_bx", "lru_lambda",
                                     "ln_ffn", "ffn_conv_w", "ffn_conv_b")}
    slab = dict(w_out=jnp.zeros((N_CHIPS, DEPTH * D_MODEL // N_CHIPS, D_MODEL), F32),
                ffn_w_in=jnp.zeros((N_CHIPS, DEPTH * D_MODEL, 2 * FFN_H // N_CHIPS), F32),
                ffn_w_down=jnp.zeros((N_CHIPS, DEPTH * FFN_H // N_CHIPS, D_MODEL), F32))
    n_gla = 2 * QK_W + 2 * GLA_W
    for l in reversed(range(DEPTH)):
        s = stash[l]
        w = s["w"]
        slab["ffn_w_down"] = _mm_tn_into(slab["ffn_w_down"], s["act"], dhb, l, 0, f"ffn_down_dw{l}",
                                         tk=FFN_H // N_CHIPS, tn=D_MODEL, tm=2048)
        dza, dzg, du2, dca, dcg = _ffn_bwd_core(dhb, *s["ffn_kept"], w["wdT"], w["faT"], w["fgT"], w["fcwa"],
                                                w["fcwg"], f"ffn_bwd_core{l}")
        dca, dcg = (jnp.moveaxis(d, 0, 1).reshape(SUBLANES, FFN_H) for d in (dca, dcg))
        g["ffn_conv_w"][l] = jnp.concatenate([dca[:FFN_CONV], dcg[:FFN_CONV]], axis=1)
        g["ffn_conv_b"][l] = jnp.concatenate([dca[FFN_CONV], dcg[FFN_CONV]])
        for half, dz in enumerate((dza, dzg)):
            slab["ffn_w_in"] = _mm_tn_into(slab["ffn_w_in"], s["u2"], dz, l, 2 * half, f"ffn_in_dw{l}_{half}",
                                           tk=D_MODEL, tn=2 * FFN_H // N_CHIPS)
        dh, dhb, dln = _rms_bwd(s["h1"], p["ln_ffn"][l], du2, dh, f"ffn_norm_bwd{l}")
        g["ln_ffn"][l] = dln[0]
        slab["w_out"] = _mm_tn_into(slab["w_out"], s["ym"], dhb, l, 0, f"out_dw{l}",
                                    tk=D_MODEL // N_CHIPS, tn=D_MODEL, tm=2048)
        dyc = _mm(dhb, w["woT"], None, F32, f"out_dx{l}")
        dpg, dw2, db2, dng = _gla_bwd(dyc, s["pg"], s["o_st"], s["s_st"], w["w2p"], p["gla_gate_b"][l],
                                      p["gla_norm"][l], f"gla_bwd{l}")
        dpl, dcw, dvec, dwa, dwx = _lru_bwd(dyc, s["plr"], s["hs"], w["lcw"], row(p["lru_conv_b"][l]), w["wa"],
                                            row(p["lru_ba"][l]), w["wx"], row(p["lru_bx"][l]),
                                            row(p["lru_lambda"][l]), f"lru_bwd{l}")
        g["gla_gate_w2"][l] = dw2[:GATE_RANK]
        g["gla_gate_b"][l] = db2[0]
        g["gla_norm"][l] = dng[0]
        g["lru_conv_w"][l] = dcw[:LRU_CONV]
        g["lru_conv_b"][l], g["lru_ba"][l], g["lru_bx"][l], g["lru_lambda"][l] = dvec[0], dvec[1], dvec[2], dvec[3]
        g["lru_wa"][l], g["lru_wx"][l] = _diag_blocks(dwa), _diag_blocks(dwx)
        dwg = _mm_tn(s["u"], dpg, f"proj_gla_dw{l}")
        dwl = _mm_tn(s["u"], dpl, f"proj_lru_dw{l}")
        g["w_in"][l] = jnp.concatenate([dwg[:, :n_gla + GATE_RANK], dwl], axis=1)
        du = _mm(dpg, w["wgT"], None, F32, f"proj_dx{l}", more=(dpl, w["wlT"]))
        dh, dhb, dln = _rms_bwd(s["h0"], p["ln_mix"][l], du, dh, f"mix_norm_bwd{l}")
        g["ln_mix"][l] = dln[0]
    grads = {k: jnp.stack(v) for k, v in g.items()}
    grads["w_in"] = _slabs_from_whole("w_in", grads["w_in"])
    grads.update(slab)
    grads["ln_final"] = d_ln_final[0]
    return loss, dh, grads


BIG = ("w_in", "w_out", "ffn_w_in", "ffn_w_down")
COL_SHARDED = ("w_in", "ffn_w_in", "gla_gate_w2", "lru_conv_w", "ffn_conv_w")
SMALL = ("ln_mix", "gla_gate_w2", "gla_gate_b", "gla_norm", "lru_conv_w", "lru_conv_b", "lru_wa", "lru_ba", "lru_wx",
         "lru_bx", "lru_lambda", "ln_ffn", "ffn_conv_w", "ffn_conv_b", "ln_final")
WEIGHTS = ("ln_mix", "w_in", "gla_gate_w2", "gla_gate_b", "gla_norm", "lru_conv_w", "lru_conv_b", "lru_wa", "lru_ba",
           "lru_wx", "lru_bx", "lru_lambda", "w_out", "ln_ffn", "ffn_w_in", "ffn_conv_w", "ffn_conv_b", "ffn_w_down",
           "ln_final")
PACK = SUBLANES * LANES


def _whole_from_shards(name, g):
    if name in COL_SHARDED:
        return jnp.moveaxis(g, 0, -2).reshape(g.shape[1:-1] + (N_CHIPS * g.shape[-1],))
    return jnp.moveaxis(g, 0, 1).reshape((g.shape[1], N_CHIPS * g.shape[2]) + g.shape[3:])


def _slabs_from_whole(name, w):
    L, r, c = w.shape
    if name in COL_SHARDED:
        s = jnp.moveaxis(w.reshape(L, r, N_CHIPS, c // N_CHIPS), 2, 0)
    else:
        s = jnp.moveaxis(w.reshape(L, N_CHIPS, r // N_CHIPS, c), 1, 0)
    return s.reshape(N_CHIPS, -1, s.shape[-1])


def _pack(arrs):
    flat = []
    for a in arrs:
        f = a.reshape(-1)
        flat.append(jnp.pad(f, (0, (-f.shape[0]) % PACK)))
    return jnp.concatenate(flat).reshape(-1, LANES)


def _unpack(packed, shapes):
    out, at = [], 0
    flat = packed.reshape(-1)
    for s in shapes:
        size = math.prod(s)
        out.append(flat[at:at + size].reshape(s))
        at += size + (-size) % PACK
    return out


def kernel(x, ln_mix, w_in, gla_gate_w2, gla_gate_b, gla_norm, lru_conv_w, lru_conv_b, lru_wa, lru_ba, lru_wx, lru_bx, lru_lambda, w_out, ln_ffn, ffn_w_in, ffn_conv_w, ffn_conv_b, ffn_w_down, ln_final, loss_target, m_ln_mix, m_w_in, m_gla_gate_w2, m_gla_gate_b, m_gla_norm, m_lru_conv_w, m_lru_conv_b, m_lru_wa, m_lru_ba, m_lru_wx, m_lru_bx, m_lru_lambda, m_w_out, m_ln_ffn, m_ffn_w_in, m_ffn_conv_w, m_ffn_conv_b, m_ffn_w_down, m_ln_final, v_ln_mix, v_w_in, v_gla_gate_w2, v_gla_gate_b, v_gla_norm, v_lru_conv_w, v_lru_conv_b, v_lru_wa, v_lru_ba, v_lru_wx, v_lru_bx, v_lru_lambda, v_w_out, v_ln_ffn, v_ffn_w_in, v_ffn_conv_w, v_ffn_conv_b, v_ffn_w_down, v_ln_final):
    w = dict(ln_mix=ln_mix, w_in=w_in, gla_gate_w2=gla_gate_w2, gla_gate_b=gla_gate_b, gla_norm=gla_norm,
             lru_conv_w=lru_conv_w, lru_conv_b=lru_conv_b, lru_wa=lru_wa, lru_ba=lru_ba, lru_wx=lru_wx, lru_bx=lru_bx,
             lru_lambda=lru_lambda, w_out=w_out, ln_ffn=ln_ffn, ffn_w_in=ffn_w_in, ffn_conv_w=ffn_conv_w,
             ffn_conv_b=ffn_conv_b, ffn_w_down=ffn_w_down, ln_final=ln_final)
    m = dict(ln_mix=m_ln_mix, w_in=m_w_in, gla_gate_w2=m_gla_gate_w2, gla_gate_b=m_gla_gate_b, gla_norm=m_gla_norm,
             lru_conv_w=m_lru_conv_w, lru_conv_b=m_lru_conv_b, lru_wa=m_lru_wa, lru_ba=m_lru_ba, lru_wx=m_lru_wx,
             lru_bx=m_lru_bx, lru_lambda=m_lru_lambda, w_out=m_w_out, ln_ffn=m_ln_ffn, ffn_w_in=m_ffn_w_in,
             ffn_conv_w=m_ffn_conv_w, ffn_conv_b=m_ffn_conv_b, ffn_w_down=m_ffn_w_down, ln_final=m_ln_final)
    v = dict(ln_mix=v_ln_mix, w_in=v_w_in, gla_gate_w2=v_gla_gate_w2, gla_gate_b=v_gla_gate_b, gla_norm=v_gla_norm,
             lru_conv_w=v_lru_conv_w, lru_conv_b=v_lru_conv_b, lru_wa=v_lru_wa, lru_ba=v_lru_ba, lru_wx=v_lru_wx,
             lru_bx=v_lru_bx, lru_lambda=v_lru_lambda, w_out=v_w_out, ln_ffn=v_ln_ffn, ffn_w_in=v_ffn_w_in,
             ffn_conv_w=v_ffn_conv_w, ffn_conv_b=v_ffn_conv_b, ffn_w_down=v_ffn_w_down, ln_final=v_ln_final)

    sharded = BIG + ("gla_gate_w2", "lru_conv_w", "ffn_conv_w")
    chip = 2 * lax.axis_index("x") + lax.axis_index("y")
    core = lax.axis_index("c")
    shards = [w[k].astype(MXU_DTYPE) if k in BIG else w[k] for k in sharded]
    gathered = _gather_shards(shards, "gather_weights")
    p = dict(w)
    for k, gk, own in zip(sharded, gathered, shards):
        p[k] = _whole_from_shards(k, lax.dynamic_update_index_in_dim(gk, own, chip, 0))

    loss, grad_x, grads = _local_step(x[0], loss_target[0], p)
    loss = lax.psum(loss[0, 0], ("x", "y", "c"))

    slabs = [grads[k].reshape(N_CHIPS, 2, grads[k].shape[1] // 2, grads[k].shape[2]) for k in BIG]
    got = _other_half_to_sibling(slabs, "other_half_to_sibling")
    kept = [lax.dynamic_index_in_dim(s, core, 1, keepdims=False) for s in slabs]
    chip_half = [_add_slabs(a, b, BF16, f"core_sum_{k}") for k, a, b in zip(BIG, kept, got)]
    small = _pack([grads[k] for k in SMALL])
    *recv, small_all = _exchange_grads(chip_half, small, "exchange_grads")
    own = [lax.dynamic_index_in_dim(h, chip, 0, keepdims=False) for h in chip_half]
    mine = [_sum_leading(r, f"chip_sum_{k}", last=o) for k, r, o in zip(BIG, recv, own)]
    theirs = _swap_with_sibling(mine, "swap_core_halves")
    big_g = [jnp.concatenate([jnp.where(core == 0, a, b), jnp.where(core == 0, b, a)]) for a, b in zip(mine, theirs)]
    small_all = lax.dynamic_update_index_in_dim(small_all, small, 2 * chip + core, 0)
    small_sum = _unpack(_sum_leading(small_all, "sum_small_grads"), [grads[k].shape for k in SMALL])

    me = 2 * lax.axis_index("x") + lax.axis_index("y")
    out_g, out_d, out_m, out_v = {}, {}, {}, {}
    for k, gk in zip(BIG, big_g):
        shape = w[k].shape
        cols = shape[-1]
        res = _adamw(w[k].reshape(-1, cols), [gk], m[k].reshape(-1, cols), v[k].reshape(-1, cols), f"adamw_{k}")
        out_g[k], out_d[k], out_m[k], out_v[k] = [r.reshape(shape) for r in res]
    small_g = []
    for k, gk in zip(SMALL, small_sum):
        if k in COL_SHARDED:
            width = w[k].shape[-1]
            gk = lax.dynamic_slice_in_dim(gk, me * width, width, axis=gk.ndim - 1)
        small_g.append(gk)
    shapes = [w[k].shape for k in SMALL]
    res = _adamw(_pack([w[k] for k in SMALL]), [_pack(small_g)], _pack([m[k] for k in SMALL]),
                 _pack([v[k] for k in SMALL]), "adamw_small")
    for out, packed in zip((out_g, out_d, out_m, out_v), res):
        for k, a in zip(SMALL, _unpack(packed, shapes)):
            out[k] = a
    return (loss, grad_x[None], *[out_g[k] for k in WEIGHTS], *[out_d[k] for k in WEIGHTS],
            *[out_m[k] for k in WEIGHTS], *[out_v[k] for k in WEIGHTS])
```

```python
import math

import jax
import jax.numpy as jnp
from jax import lax
from jax.experimental import pallas as pl
from jax.experimental.pallas import tpu as pltpu

F32 = jnp.float32
BF16 = jnp.bfloat16
MXU_DTYPE = BF16

D_MODEL = 1024
DEPTH = 4
HEADS, DK, DV, CHUNK, GATE_RANK = 4, 64, 128, 64, 16
QK_W = HEADS * DK
GLA_W = HEADS * DV
LRU_W = 512
LRU_BLOCKS, LRU_BLOCK, LRU_CONV, LRU_C = 8, 64, 4, 8.0
FFN_H = 3 * D_MODEL
FFN_CONV = 3
EPS = 1e-6
GATE_PAD = 128
GLA_COLS = 2 * QK_W + 2 * GLA_W + GATE_PAD
LRU_COLS = 2 * LRU_W
ADAM_LR, ADAM_B1, ADAM_B2, ADAM_EPS, ADAM_WD, ADAM_STEP = 0.001, 0.9, 0.999, 1e-08, 0.01, 10

LANES = 128
SUBLANES = 8
VMEM_LIMIT = 56 * 1024 * 1024
ROWS = 512
TILE_BYTES = 1 << 20
FFN_CW = 1024
FFN_SB = 256
FFN_RC = 32
N_CHIPS = 4
N_DEV = 8
MESH = pl.DeviceIdType.MESH


def _cp(*sem):
    return pltpu.CompilerParams(dimension_semantics=sem, vmem_limit_bytes=VMEM_LIMIT)


def _dot(a, b):
    return jnp.dot(a.astype(MXU_DTYPE), b.astype(MXU_DTYPE), preferred_element_type=F32)


def _dot_nt(a, b):
    return lax.dot_general(a.astype(MXU_DTYPE), b.astype(MXU_DTYPE), (((1,), (1,)), ((), ())),
                           preferred_element_type=F32)


def _dot_tn(a, b):
    return lax.dot_general(a.astype(MXU_DTYPE), b.astype(MXU_DTYPE), (((0,), (0,)), ((), ())),
                           preferred_element_type=F32)


def _bdot(eq, a, b):
    return jnp.einsum(eq, a, b, preferred_element_type=F32)


def _split3(x):
    x1 = x.astype(BF16)
    r1 = x - x1.astype(F32)
    x2 = r1.astype(BF16)
    x3 = (r1 - x2.astype(F32)).astype(BF16)
    return x1, x2, x3


def _gelu(x):
    c = math.sqrt(2.0 / math.pi)
    return x * (0.5 * (1.0 + jnp.tanh(c * (x + 0.044715 * (x * x * x)))))


def _gelu_and_grad(x):
    c = math.sqrt(2.0 / math.pi)
    t = jnp.tanh(c * (x + 0.044715 * (x * x * x)))
    cdf = 0.5 * (1.0 + t)
    dcdf = 0.5 * (1.0 - t * t) * (c * (1.0 + 3.0 * 0.044715 * (x * x)))
    return x * cdf, cdf + x * dcdf


def _expm1(x):
    small = x * (1.0 + x * (0.5 + x * (1.0 / 6.0 + x * (1.0 / 24.0 + x * (1.0 / 120.0)))))
    return jnp.where(jnp.abs(x) < 0.1, small, jnp.exp(x) - 1.0)


def _shift_down(x, k, fill):
    row = lax.broadcasted_iota(jnp.int32, x.shape, 0)
    return jnp.where(row >= k, pltpu.roll(x, k, axis=0), fill)


def _shift_up(x, k, fill):
    n = x.shape[0]
    row = lax.broadcasted_iota(jnp.int32, x.shape, 0)
    return jnp.where(row < n - k, pltpu.roll(x, n - k, axis=0), fill)


def _rms_fwd(h, g, name):
    T, D = h.shape
    R = min(T, ROWS)

    def body(h_ref, g_ref, o_ref):
        x = h_ref[...]
        r = lax.rsqrt(jnp.mean(x * x, axis=-1, keepdims=True) + EPS)
        o_ref[...] = ((x * r) * g_ref[...]).astype(o_ref.dtype)

    return pl.pallas_call(
        body, name=name, grid=(T // R,),
        in_specs=[pl.BlockSpec((R, D), lambda i: (i, 0)), pl.BlockSpec((1, D), lambda i: (0, 0))],
        out_specs=pl.BlockSpec((R, D), lambda i: (i, 0)),
        out_shape=jax.ShapeDtypeStruct((T, D), BF16), compiler_params=_cp("parallel"),
    )(h, g.reshape(1, D))


def _rms_bwd_tile(x, g, du, dres):
    r = lax.rsqrt(jnp.mean(x * x, axis=-1, keepdims=True) + EPS)
    xhat = x * r
    dxhat = du * g
    dx = r * (dxhat - xhat * jnp.mean(dxhat * xhat, axis=-1, keepdims=True))
    return dres + dx, jnp.sum(du * xhat, axis=0, keepdims=True)


def _mm_rms_bwd(pairs, h, g, dres, name, tm=512):
    M, D = h.shape
    tm = min(tm, M)
    n = len(pairs)

    def body(*refs):
        h_ref, g_ref, dres_ref = refs[2 * n:2 * n + 3]
        dh_ref, dhb_ref, dg_ref = refs[2 * n + 3:]

        @pl.when(pl.program_id(0) == 0)
        def _():
            dg_ref[...] = jnp.zeros_like(dg_ref)

        du = _dot(refs[0][...], refs[1][...])
        for k in range(1, n):
            du = du + _dot(refs[2 * k][...], refs[2 * k + 1][...])
        dh, dg = _rms_bwd_tile(h_ref[...], g_ref[...], du, dres_ref[...])
        dg_ref[...] += dg
        dh_ref[...] = dh
        dhb_ref[...] = dh.astype(dhb_ref.dtype)

    in_specs, args = [], []
    for a, b in pairs:
        in_specs += [pl.BlockSpec((tm, a.shape[1]), lambda i: (i, 0)), pl.BlockSpec(b.shape, lambda i: (0, 0))]
        args += [a, b]
    blk = pl.BlockSpec((tm, D), lambda i: (i, 0))
    vec = pl.BlockSpec((1, D), lambda i: (0, 0))
    return pl.pallas_call(
        body, name=name, grid=(M // tm,), in_specs=in_specs + [blk, vec, blk], out_specs=[blk, blk, vec],
        out_shape=[jax.ShapeDtypeStruct((M, D), F32), jax.ShapeDtypeStruct((M, D), BF16),
                   jax.ShapeDtypeStruct((1, D), F32)],
        compiler_params=_cp("arbitrary"),
    )(*args, h, g.reshape(1, D), dres)


def _loss_head(h, g, tgt, name):
    T, D = h.shape
    R = min(T, ROWS)

    def body(h_ref, g_ref, t_ref, loss_ref, dh_ref, dhb_ref, dg_ref):
        @pl.when(pl.program_id(0) == 0)
        def _():
            dg_ref[...] = jnp.zeros_like(dg_ref)
            loss_ref[...] = jnp.zeros_like(loss_ref)

        x = h_ref[...]
        r = lax.rsqrt(jnp.mean(x * x, axis=-1, keepdims=True) + EPS)
        xhat = x * r
        gg = g_ref[...]
        err = xhat * gg - t_ref[...]
        loss_ref[...] += 0.5 * jnp.sum(jnp.mean(err * err, axis=-1, keepdims=True), axis=0, keepdims=True)
        dy = err * (1.0 / D)
        dg_ref[...] += jnp.sum(dy * xhat, axis=0, keepdims=True)
        dxhat = dy * gg
        dh = r * (dxhat - xhat * jnp.mean(dxhat * xhat, axis=-1, keepdims=True))
        dh_ref[...] = dh
        dhb_ref[...] = dh.astype(dhb_ref.dtype)

    blk = pl.BlockSpec((R, D), lambda i: (i, 0))
    vec = pl.BlockSpec((1, D), lambda i: (0, 0))
    one = pl.BlockSpec((1, LANES), lambda i: (0, 0))
    return pl.pallas_call(
        body, name=name, grid=(T // R,), in_specs=[blk, vec, blk], out_specs=[one, blk, blk, vec],
        out_shape=[jax.ShapeDtypeStruct((1, LANES), F32), jax.ShapeDtypeStruct((T, D), F32),
                   jax.ShapeDtypeStruct((T, D), BF16), jax.ShapeDtypeStruct((1, D), F32)],
        compiler_params=_cp("arbitrary"),
    )(h, g.reshape(1, D), tgt)


def _mm(a, b, res, out_dtype, name, tm=512, tn=None, norm_g=None):
    M, K = a.shape
    N = b.shape[1]
    tm = min(tm, M)
    tn = N if tn is None else tn
    assert norm_g is None or tn == N

    def body(*refs):
        refs = list(refs)
        a_ref, b_ref = refs[:2]
        acc = _dot(a_ref[...], b_ref[...])
        if res is not None:
            acc = refs[2][...].astype(F32) + acc
        if norm_g is None:
            refs[-1][...] = acc.astype(refs[-1].dtype)
        else:
            refs[-2][...] = acc.astype(refs[-2].dtype)
            r = lax.rsqrt(jnp.mean(acc * acc, axis=-1, keepdims=True) + EPS)
            refs[-1][...] = ((acc * r) * refs[-3][...]).astype(refs[-1].dtype)

    blk = pl.BlockSpec((tm, tn), lambda j, i: (i, j))
    in_specs = [pl.BlockSpec((tm, K), lambda j, i: (i, 0)), pl.BlockSpec((K, tn), lambda j, i: (0, j))]
    args = [a, b]
    if res is not None:
        in_specs.append(blk)
        args.append(res)
    out_specs, out_shape = blk, jax.ShapeDtypeStruct((M, N), out_dtype)
    if norm_g is not None:
        in_specs.append(pl.BlockSpec((1, N), lambda j, i: (0, 0)))
        args.append(norm_g.reshape(1, N))
        out_specs, out_shape = [blk, blk], [out_shape, jax.ShapeDtypeStruct((M, N), BF16)]
    return pl.pallas_call(
        body, name=name, grid=(N // tn, M // tm), in_specs=in_specs, out_specs=out_specs, out_shape=out_shape,
        compiler_params=_cp("parallel", "parallel"),
    )(*args)


def _mm_tn_into(slab, a, b, layer, chip0, name, tk, tn, tm=1024):
    M, K = a.shape
    N = b.shape[1]
    tm = min(tm, M)
    assert slab.shape[2] == tn and (K // tk == 1 or N // tn == 1)

    def body(a_ref, b_ref, slab_ref, o_ref):
        del slab_ref

        @pl.when(pl.program_id(2) == 0)
        def _():
            o_ref[...] = jnp.zeros_like(o_ref)

        o_ref[0] += _dot_tn(a_ref[...], b_ref[...])

    return pl.pallas_call(
        body, name=name, grid=(K // tk, N // tn, M // tm),
        in_specs=[pl.BlockSpec((tm, tk), lambda k, n, m: (m, k)), pl.BlockSpec((tm, tn), lambda k, n, m: (m, n)),
                  pl.BlockSpec(memory_space=pl.ANY)],
        out_specs=pl.BlockSpec((1, tk, tn), lambda k, n, m: (chip0 + k + n, layer, 0)),
        out_shape=jax.ShapeDtypeStruct(slab.shape, F32), input_output_aliases={2: 0},
        compiler_params=_cp("parallel", "parallel", "arbitrary"),
    )(a, b, slab)


def _mm_tn(a, b, name, tm=2048, tk=None, tn=None):
    M, K = a.shape
    N = b.shape[1]
    tm = min(tm, M)
    tk = K if tk is None else tk
    tn = N if tn is None else tn

    def body(a_ref, b_ref, o_ref):
        @pl.when(pl.program_id(2) == 0)
        def _():
            o_ref[...] = jnp.zeros_like(o_ref)

        o_ref[...] += _dot_tn(a_ref[...], b_ref[...])

    return pl.pallas_call(
        body, name=name, grid=(K // tk, N // tn, M // tm),
        in_specs=[pl.BlockSpec((tm, tk), lambda k, n, m: (m, k)), pl.BlockSpec((tm, tn), lambda k, n, m: (m, n))],
        out_specs=pl.BlockSpec((tk, tn), lambda k, n, m: (k, n)),
        out_shape=jax.ShapeDtypeStruct((K, N), F32), compiler_params=_cp("parallel", "parallel", "arbitrary"),
    )(a, b)


def _same_chunk(row, col):
    shift = CHUNK.bit_length() - 1
    return jnp.right_shift(row, shift) == jnp.right_shift(col, shift)


def _gla_common(q, k, glr, w2, b2, R):
    gl = _dot(glr, w2) + b2
    la = jax.nn.log_sigmoid(gl) * (1.0 / 16.0)
    row = lax.broadcasted_iota(jnp.int32, (R, R), 0)
    col = lax.broadcasted_iota(jnp.int32, (R, R), 1)
    same = _same_chunk(row, col)
    m_tri = (same & (col <= row)).astype(BF16)
    m_all = same.astype(BF16)
    la3 = _split3(la)
    b = sum(jnp.dot(m_tri, p, preferred_element_type=F32) for p in la3)
    bl = sum(jnp.dot(m_all, p, preferred_element_type=F32) for p in la3)
    eb = jnp.exp(b)
    enb = jnp.exp(-b)
    ek = jnp.exp(bl - b)
    qi = (q * (DK ** -0.5)) * eb
    ki = k * enb
    kd = k * ek
    return gl, la3, eb, enb, ek, qi, ki, kd


def _bsplit(x, n):
    return x.reshape(n, CHUNK, x.shape[-1])


def _tril():
    return (lax.broadcasted_iota(jnp.int32, (CHUNK, CHUNK), 1)
            <= lax.broadcasted_iota(jnp.int32, (CHUNK, CHUNK), 0))[None]


def _gla_fwd(proj, w2p, b2, norm_g, name):
    T = proj.shape[0]
    R = min(T, ROWS)
    n = R // CHUNK

    def body(q_ref, k_ref, v_ref, g_ref, a_ref, w2_ref, b2_ref, ng_ref, y_ref, o_ref, st_ref, s_ref):
        @pl.when(pl.program_id(0) == 0)
        def _():
            s_ref[...] = jnp.zeros_like(s_ref)

        _, la3, _, _, _, qi, ki, kd = _gla_common(q_ref[...], k_ref[...], a_ref[...], w2_ref[...], b2_ref[...], R)
        tril = _tril()
        ones = jnp.ones((n, CHUNK, DV), BF16)
        for h in range(HEADS):
            sl = slice(h * DK, (h + 1) * DK)
            sv = slice(h * DV, (h + 1) * DV)
            qh = _bsplit(qi[:, sl], n).astype(MXU_DTYPE)
            kh = _bsplit(ki[:, sl], n).astype(MXU_DTYPE)
            kdh = _bsplit(kd[:, sl], n).astype(MXU_DTYPE)
            vh = _bsplit(v_ref[:, sv], n).astype(MXU_DTYPE)
            att = jnp.where(tril, _bdot('ncd,nsd->ncs', qh, kh), 0.0)
            upd = _bdot('ncd,nce->nde', kdh, vh)
            dect = jnp.exp(sum(_bdot('ncd,nce->nde', _bsplit(p[:, sl], n), ones) for p in la3))
            s = s_ref[sl, :]
            for c in range(n):
                st_ref[c, sl, :] = s
                s = dect[c] * s + upd[c]
            s_ref[sl, :] = s
            sp = st_ref[:, sl, :].astype(MXU_DTYPE)
            o = (_bdot('ncs,nse->nce', att.astype(MXU_DTYPE), vh) + _bdot('ncd,nde->nce', qh, sp)).reshape(R, DV)
            o_ref[:, sv] = o
            r = lax.rsqrt(jnp.mean(o * o, axis=-1, keepdims=True) + EPS)
            gate = g_ref[:, sv]
            y_ref[:, sv] = (((o * r) * ng_ref[...]) * (gate * jax.nn.sigmoid(gate))).astype(y_ref.dtype)

    cb = lambda w, j: pl.BlockSpec((R, w), lambda i: (i, j))
    full = lambda s: pl.BlockSpec(s, lambda i: (0,) * len(s))
    return pl.pallas_call(
        body, name=name, grid=(T // R,),
        in_specs=[cb(QK_W, 0), cb(QK_W, 1), cb(GLA_W, 1), cb(GLA_W, 2), cb(GATE_PAD, 12),
                  full((GATE_PAD, QK_W)), full((1, QK_W)), full((1, DV))],
        out_specs=[pl.BlockSpec((R, GLA_W), lambda i: (i, 0)), pl.BlockSpec((R, GLA_W), lambda i: (i, 0)),
                   pl.BlockSpec((n, QK_W, DV), lambda i: (i, 0, 0))],
        out_shape=[jax.ShapeDtypeStruct((T, GLA_W + LRU_W), BF16), jax.ShapeDtypeStruct((T, GLA_W), F32),
                   jax.ShapeDtypeStruct((T // CHUNK, QK_W, DV), F32)],
        scratch_shapes=[pltpu.VMEM((QK_W, DV), F32)],
        compiler_params=_cp("arbitrary"),
    )(proj, proj, proj, proj, proj, w2p, b2.reshape(1, QK_W), norm_g.reshape(1, DV))


def _gla_bwd(dy, proj, o_st, s_st, w2p, b2, norm_g, name):
    T = proj.shape[0]
    R = min(T, ROWS)
    n = R // CHUNK
    nb = T // R

    def body(dy_ref, q_ref, k_ref, v_ref, g_ref, a_ref, o_ref, st_ref, w2_ref, b2_ref, ng_ref,
             dp_ref, dw2_ref, db2_ref, dng_ref, gs_ref, gn_ref, db_ref, dbl_ref):
        @pl.when(pl.program_id(0) == 0)
        def _():
            gs_ref[...] = jnp.zeros_like(gs_ref)
            dw2_ref[...] = jnp.zeros_like(dw2_ref)
            db2_ref[...] = jnp.zeros_like(db2_ref)
            dng_ref[...] = jnp.zeros_like(dng_ref)

        glr = a_ref[...]
        gl, la3, eb, enb, ek, qi, ki, kd = _gla_common(q_ref[...], k_ref[...], glr, w2_ref[...], b2_ref[...], R)
        tril = _tril()
        ones = jnp.ones((n, CHUNK, DV), BF16)
        ng = ng_ref[...]
        dng = jnp.zeros((1, DV), F32)
        for h in range(HEADS):
            sl = slice(h * DK, (h + 1) * DK)
            sv = slice(h * DV, (h + 1) * DV)
            o = o_ref[:, sv]
            r = lax.rsqrt(jnp.mean(o * o, axis=-1, keepdims=True) + EPS)
            xhat = o * r
            gate = g_ref[:, sv]
            sg = jax.nn.sigmoid(gate)
            dyh = dy_ref[:, sv].astype(F32)
            dp_ref[:, 2 * QK_W + GLA_W + h * DV:2 * QK_W + GLA_W + (h + 1) * DV] = (
                dyh * (xhat * ng) * (sg * (1.0 + gate * (1.0 - sg)))).astype(dp_ref.dtype)
            don = dyh * (gate * sg)
            dng = dng + jnp.sum(don * xhat, axis=0, keepdims=True)
            dxhat = don * ng
            do = r * (dxhat - xhat * jnp.mean(dxhat * xhat, axis=-1, keepdims=True))
            qf = _bsplit(qi[:, sl], n)
            kf = _bsplit(ki[:, sl], n)
            kdf = _bsplit(kd[:, sl], n)
            qh, kh, kdh = qf.astype(MXU_DTYPE), kf.astype(MXU_DTYPE), kdf.astype(MXU_DTYPE)
            vh = _bsplit(v_ref[:, sv], n).astype(MXU_DTYPE)
            doh = _bsplit(do, n).astype(MXU_DTYPE)
            spf = st_ref[:, sl, :]
            sp = spf.astype(MXU_DTYPE)
            att = jnp.where(tril, _bdot('ncd,nsd->ncs', qh, kh), 0.0).astype(MXU_DTYPE)
            datt = jnp.where(tril, _bdot('nce,nse->ncs', doh, vh), 0.0).astype(MXU_DTYPE)
            dv = _bdot('ncs,nce->nse', att, doh)
            dqi = _bdot('ncs,nsd->ncd', datt, kh) + _bdot('nce,nde->ncd', doh, sp)
            dki = _bdot('ncs,ncd->nsd', datt, qh)
            wgt = _bdot('ncd,nce->nde', qh, doh)
            dect = jnp.exp(sum(_bdot('ncd,nce->nde', _bsplit(p[:, sl], n), ones) for p in la3))
            g = gs_ref[sl, :]
            for c in reversed(range(n)):
                gn_ref[c] = g
                g = wgt[c] + dect[c] * g
            gs_ref[sl, :] = g
            gnf = gn_ref[...]
            gn = gnf.astype(MXU_DTYPE)
            dkd = _bdot('nce,nde->ncd', vh, gn)
            dv = dv + _bdot('ncd,nde->nce', kdh, gn)
            dp_ref[:, 2 * QK_W + h * DV:2 * QK_W + (h + 1) * DV] = dv.reshape(R, DV).astype(dp_ref.dtype)
            dbl = sum(_bdot('nce,nde->ncd', ones, p) for p in _split3(gnf * spf * dect))
            pk = dkd * kdf
            dbl = dbl + jnp.sum(pk, axis=1, keepdims=True)
            dbl_ref[:, sl] = dbl.reshape(R, DK)
            db_ref[:, sl] = (dqi * qf - dki * kf - pk).reshape(R, DK)
            dp_ref[:, sl] = ((dqi.reshape(R, DK) * (DK ** -0.5)) * eb[:, sl]).astype(dp_ref.dtype)
            dp_ref[:, QK_W + h * DK:QK_W + (h + 1) * DK] = (
                dki.reshape(R, DK) * enb[:, sl] + dkd.reshape(R, DK) * ek[:, sl]).astype(dp_ref.dtype)
        dng_ref[...] += dng
        row = lax.broadcasted_iota(jnp.int32, (R, R), 0)
        col = lax.broadcasted_iota(jnp.int32, (R, R), 1)
        m_rev = (_same_chunk(row, col) & (col >= row)).astype(BF16)
        dla = sum(jnp.dot(m_rev, p, preferred_element_type=F32) for p in _split3(db_ref[...])) + dbl_ref[...]
        dgl = (dla * (1.0 / 16.0)) * jax.nn.sigmoid(-gl)
        dp_ref[:, 2 * QK_W + 2 * GLA_W:GLA_COLS] = _dot_nt(dgl, w2_ref[...]).astype(dp_ref.dtype)
        dw2_ref[...] += _dot_tn(glr, dgl)
        db2_ref[...] += jnp.sum(dgl, axis=0, keepdims=True)

    cb = lambda w, j: pl.BlockSpec((R, w), lambda i: (nb - 1 - i, j))
    full = lambda s: pl.BlockSpec(s, lambda i: (0,) * len(s))
    return pl.pallas_call(
        body, name=name, grid=(nb,),
        in_specs=[cb(GLA_W, 0), cb(QK_W, 0), cb(QK_W, 1), cb(GLA_W, 1), cb(GLA_W, 2), cb(GATE_PAD, 12),
                  cb(GLA_W, 0), pl.BlockSpec((n, QK_W, DV), lambda i: (nb - 1 - i, 0, 0)),
                  full((GATE_PAD, QK_W)), full((1, QK_W)), full((1, DV))],
        out_specs=[pl.BlockSpec((R, GLA_COLS), lambda i: (nb - 1 - i, 0)),
                   full((GATE_PAD, QK_W)), full((1, QK_W)), full((1, DV))],
        out_shape=[jax.ShapeDtypeStruct((T, GLA_COLS), BF16), jax.ShapeDtypeStruct((GATE_PAD, QK_W), F32),
                   jax.ShapeDtypeStruct((1, QK_W), F32), jax.ShapeDtypeStruct((1, DV), F32)],
        scratch_shapes=[pltpu.VMEM((QK_W, DV), F32), pltpu.VMEM((n, DK, DV), F32),
                        pltpu.VMEM((R, QK_W), F32), pltpu.VMEM((R, QK_W), F32)],
        compiler_params=_cp("arbitrary"),
    )(dy, proj, proj, proj, proj, proj, o_st, s_st, w2p, b2.reshape(1, QK_W), norm_g.reshape(1, DV))


def _scan_scratch(R, W):
    return [pltpu.VMEM((W // LANES, R, LANES), F32), pltpu.VMEM((W // LANES, R, LANES), F32),
            pltpu.VMEM((W // LANES, R // SUBLANES, LANES), F32)]


def _scan_rows(a, u, c0, a_ref, u_ref, c_ref, out_ref, reverse):
    R, W = a.shape
    nt = R // SUBLANES
    shift = _shift_up if reverse else _shift_down
    a, u = a.reshape(nt, SUBLANES, W), u.reshape(nt, SUBLANES, W)
    sub = lax.broadcasted_iota(jnp.int32, (nt, SUBLANES, W), 1)
    for k in (1, 2, 4):
        inside = (sub < SUBLANES - k) if reverse else (sub >= k)
        turn = SUBLANES - k if reverse else k
        u = u + a * jnp.where(inside, pltpu.roll(u, turn, axis=1), 0.0)
        a = a * jnp.where(inside, pltpu.roll(a, turn, axis=1), 1.0)
    a, u = a.reshape(R, W), u.reshape(R, W)
    end = 0 if reverse else SUBLANES - 1
    edge = nt - 1 if reverse else 0
    for j in range(W // LANES):
        cols = slice(j * LANES, (j + 1) * LANES)
        a_ref[j] = a[:, cols]
        u_ref[j] = u[:, cols]
        at = a_ref.at[j][pl.ds(end, nt, stride=SUBLANES), :]
        ut = u_ref.at[j][pl.ds(end, nt, stride=SUBLANES), :]
        k = 1
        while k < nt:
            ut = ut + at * shift(ut, k, 0.0)
            at = at * shift(at, k, 1.0)
            k *= 2
        c_ref[j] = shift(ut + at * c0[:, cols], 1, 0.0)
        c_ref[j, edge:edge + 1, :] = c0[:, cols]
        for r in range(nt):
            rows = pl.ds(r * SUBLANES, SUBLANES)
            out_ref[rows, cols] = u_ref[j, rows, :] + a_ref[j, rows, :] * c_ref[j, r:r + 1, :]


def _lru_conv(ext_ref, cw_ref, cb_ref, R):
    xc = cb_ref[...] + ext_ref[pl.ds(SUBLANES - 3, R), :] * cw_ref[0:1, :]
    xc = xc + ext_ref[pl.ds(SUBLANES - 2, R), :] * cw_ref[1:2, :]
    xc = xc + ext_ref[pl.ds(SUBLANES - 1, R), :] * cw_ref[2:3, :]
    return xc + ext_ref[pl.ds(SUBLANES, R), :] * cw_ref[3:4, :]


def _lru_gates(xc, wa, ba, wx, bx, lam, first):
    r = jax.nn.sigmoid(_dot(xc, wa) + ba)
    ig = jax.nn.sigmoid(_dot(xc, wx) + bx)
    sp = jax.nn.softplus(-lam)
    la = (-LRU_C * r) * sp
    a = jnp.exp(la)
    mult = jnp.where(first, 1.0, jnp.sqrt(-_expm1(2.0 * la)))
    return r, ig, sp, a, mult


def _lru_fwd(proj, y_mix, cw, cb, wa, ba, wx, bx, lam, name):
    T = proj.shape[0]
    R = min(T, ROWS)
    W = LRU_W

    def body(xr_ref, xh_ref, xg_ref, cw_ref, cb_ref, wa_ref, ba_ref, wx_ref, bx_ref, lam_ref, mix_ref,
             y_ref, hs_ref, ext_ref, hc_ref, sa_ref, su_ref, sc_ref):
        del mix_ref
        i = pl.program_id(0)

        @pl.when(i == 0)
        def _():
            hc_ref[...] = jnp.zeros_like(hc_ref)

        ext_ref[0:SUBLANES, :] = jnp.where(i > 0, xh_ref[...], 0.0)
        ext_ref[pl.ds(SUBLANES, R), :] = xr_ref[...]
        xc = _lru_conv(ext_ref, cw_ref, cb_ref, R)
        row = lax.broadcasted_iota(jnp.int32, (R, W), 0)
        first = (row == 0) & (i == 0)
        _, ig, _, a, mult = _lru_gates(xc, wa_ref[...], ba_ref[...], wx_ref[...], bx_ref[...], lam_ref[...], first)
        _scan_rows(a, mult * (ig * xc), hc_ref[0:1, :], sa_ref, su_ref, sc_ref, hs_ref, reverse=False)
        hc_ref[0:1, :] = hs_ref[R - 1:R, :]
        y_ref[...] = (hs_ref[...] * _gelu(xg_ref[...])).astype(y_ref.dtype)

    rb = R // SUBLANES
    full = lambda s: pl.BlockSpec(s, lambda i: (0,) * len(s))
    return pl.pallas_call(
        body, name=name, grid=(T // R,),
        in_specs=[pl.BlockSpec((R, W), lambda i: (i, 0)),
                  pl.BlockSpec((SUBLANES, W), lambda i: (jnp.maximum(i * rb - 1, 0), 0)),
                  pl.BlockSpec((R, W), lambda i: (i, 1)),
                  full((SUBLANES, W)), full((1, W)), full((W, W)), full((1, W)), full((W, W)), full((1, W)),
                  full((1, W)), pl.BlockSpec(memory_space=pl.ANY)],
        out_specs=[pl.BlockSpec((R, W), lambda i: (i, 1)), pl.BlockSpec((R, W), lambda i: (i, 0))],
        out_shape=[jax.ShapeDtypeStruct(y_mix.shape, y_mix.dtype), jax.ShapeDtypeStruct((T, W), F32)],
        scratch_shapes=[pltpu.VMEM((R + SUBLANES, W), F32), pltpu.VMEM((SUBLANES, W), F32),
                        *_scan_scratch(R, W)],
        input_output_aliases={10: 0}, compiler_params=_cp("arbitrary"),
    )(proj, proj, proj, cw, cb, wa, ba, wx, bx, lam, y_mix)


def _lru_bwd(dy, proj, hs, cw, cb, wa, ba, wx, bx, lam, name):
    T = proj.shape[0]
    R = min(T, ROWS)
    W = LRU_W
    nb = T // R

    def body(dy_ref, xr_ref, xh_ref, xg_ref, hs_ref, hh_ref, cw_ref, cb_ref, wa_ref, ba_ref, wx_ref, bx_ref, lam_ref,
             dp_ref, dcw_ref, dvec_ref, dwa_ref, dwx_ref, ext_ref, ext2_ref, lc_ref, sa_ref, su_ref, sc_ref, adj_ref):
        ib = pl.program_id(0)
        i = nb - 1 - ib

        @pl.when(ib == 0)
        def _():
            lc_ref[...] = jnp.zeros_like(lc_ref)
            ext2_ref[pl.ds(R, SUBLANES), :] = jnp.zeros((SUBLANES, W), F32)
            dcw_ref[...] = jnp.zeros_like(dcw_ref)
            dvec_ref[...] = jnp.zeros_like(dvec_ref)
            dwa_ref[...] = jnp.zeros_like(dwa_ref)
            dwx_ref[...] = jnp.zeros_like(dwx_ref)

        ext_ref[0:SUBLANES, :] = jnp.where(i > 0, xh_ref[...], 0.0)
        ext_ref[pl.ds(SUBLANES, R), :] = xr_ref[...]
        xc = _lru_conv(ext_ref, cw_ref, cb_ref, R)
        row = lax.broadcasted_iota(jnp.int32, (R, W), 0)
        first = (row == 0) & (i == 0)
        lam = lam_ref[...]
        r, ig, sp, a, mult = _lru_gates(xc, wa_ref[...], ba_ref[...], wx_ref[...], bx_ref[...], lam, first)
        h = hs_ref[...]
        gel, dgel = _gelu_and_grad(xg_ref[...])
        dy = dy_ref[...].astype(F32)
        dp_ref[:, W:2 * W] = (dy * h * dgel).astype(dp_ref.dtype)
        _scan_rows(_shift_up(a, 1, 1.0), dy * gel, lc_ref[0:1, :], sa_ref, su_ref, sc_ref, adj_ref, reverse=True)
        v = adj_ref[...]
        lc_ref[...] = (a * v)[0:SUBLANES, :]
        hprev = _shift_down(h, 1, 0.0) + jnp.where((row == 0) & (i > 0), hh_ref[SUBLANES - 1:SUBLANES, :], 0.0)
        da = v * hprev
        dmult = jnp.where(first, 0.0, v * (ig * xc))
        dig = v * (mult * xc)
        dxc = v * (mult * ig)
        dla = da * a - dmult * ((a * a) / mult)
        dra = (dla * (-LRU_C * sp)) * (r * (1.0 - r))
        drx = dig * (ig * (1.0 - ig))
        dxc = dxc + _dot_nt(dra, wa_ref[...]) + _dot_nt(drx, wx_ref[...])
        dwa_ref[...] += _dot_tn(xc, dra)
        dwx_ref[...] += _dot_tn(xc, drx)
        dvec_ref[0:1, :] += jnp.sum(dxc, axis=0, keepdims=True)
        dvec_ref[1:2, :] += jnp.sum(dra, axis=0, keepdims=True)
        dvec_ref[2:3, :] += jnp.sum(drx, axis=0, keepdims=True)
        dvec_ref[3:4, :] += jnp.sum(dla * (-LRU_C * r), axis=0, keepdims=True) * (-jax.nn.sigmoid(-lam))
        ext2_ref[pl.ds(0, R), :] = dxc
        dxr = ext2_ref[pl.ds(0, R), :] * cw_ref[3:4, :]
        dxr = dxr + ext2_ref[pl.ds(1, R), :] * cw_ref[2:3, :]
        dxr = dxr + ext2_ref[pl.ds(2, R), :] * cw_ref[1:2, :]
        dxr = dxr + ext2_ref[pl.ds(3, R), :] * cw_ref[0:1, :]
        dp_ref[:, 0:W] = dxr.astype(dp_ref.dtype)
        for j in range(LRU_CONV):
            dcw_ref[j:j + 1, :] += jnp.sum(dxc * ext_ref[pl.ds(SUBLANES - 3 + j, R), :], axis=0, keepdims=True)
        ext2_ref[pl.ds(R, SUBLANES), :] = dxc[0:SUBLANES, :]

    rb = R // SUBLANES
    full = lambda s: pl.BlockSpec(s, lambda i: (0,) * len(s))
    blk = lambda j: pl.BlockSpec((R, W), lambda i: (nb - 1 - i, j))
    halo = pl.BlockSpec((SUBLANES, W), lambda i: (jnp.maximum((nb - 1 - i) * rb - 1, 0), 0))
    return pl.pallas_call(
        body, name=name, grid=(nb,),
        in_specs=[blk(1), blk(0), halo, blk(1), blk(0), halo,
                  full((SUBLANES, W)), full((1, W)), full((W, W)), full((1, W)), full((W, W)), full((1, W)),
                  full((1, W))],
        out_specs=[pl.BlockSpec((R, 2 * W), lambda i: (nb - 1 - i, 0)), full((SUBLANES, W)), full((SUBLANES, W)),
                   full((W, W)), full((W, W))],
        out_shape=[jax.ShapeDtypeStruct((T, 2 * W), BF16), jax.ShapeDtypeStruct((SUBLANES, W), F32),
                   jax.ShapeDtypeStruct((SUBLANES, W), F32), jax.ShapeDtypeStruct((W, W), F32),
                   jax.ShapeDtypeStruct((W, W), F32)],
        scratch_shapes=[pltpu.VMEM((R + SUBLANES, W), F32), pltpu.VMEM((R + SUBLANES, W), F32),
                        pltpu.VMEM((SUBLANES, W), F32), *_scan_scratch(R, W), pltpu.VMEM((R, W), F32)],
        compiler_params=_cp("arbitrary"),
    )(dy, proj, proj, proj, hs, hs, cw, cb, wa, ba, wx, bx, lam)


def _conv3_window(src, start, cs, w, b):
    win = src[pl.ds(start, FFN_RC + SUBLANES), cs]
    x2 = pltpu.roll(win, 2, axis=0)[SUBLANES:]
    x1 = pltpu.roll(win, 1, axis=0)[SUBLANES:]
    x0 = win[SUBLANES:]
    return ((b + x2 * w[0]) + x1 * w[1]) + x0 * w[2], (x2, x1, x0)


def _ffn_up_fwd(u2, fa, fg, cwa, cwg, cba, cbg, name):
    T, D = u2.shape
    Fh = fa.shape[1]
    tm = min(T, ROWS)
    CW, SB, RC = FFN_CW, FFN_SB, FFN_RC
    ns = CW // SB

    def body(u_ref, fa_ref, fg_ref, cwa_ref, cwg_ref, cba_ref, cbg_ref, xa_ref, xg_ref, p_ref, q_ref, act_ref,
             ka_ref, kg_ref, ea_ref, eg_ref, za_ref, zg_ref):
        @pl.when(pl.program_id(1) == 0)
        def _():
            ka_ref[...] = jnp.zeros_like(ka_ref)
            kg_ref[...] = jnp.zeros_like(kg_ref)

        def gate(s):
            cs = pl.ds(s * SB, SB)
            wa = [cwa_ref[j:j + 1, cs] for j in range(FFN_CONV)]
            wg = [cwg_ref[j:j + 1, cs] for j in range(FFN_CONV)]
            ba, bg = cba_ref[:, cs], cbg_ref[:, cs]
            ea_ref[0:SUBLANES, cs] = ka_ref[:, cs]
            ea_ref[pl.ds(SUBLANES, RC), cs] = za_ref[0:RC, cs]
            eg_ref[0:SUBLANES, cs] = kg_ref[:, cs]
            eg_ref[pl.ds(SUBLANES, RC), cs] = zg_ref[0:RC, cs]
            for c in range(tm // RC):
                sa, sg, start = (ea_ref, eg_ref, 0) if c == 0 else (za_ref, zg_ref, c * RC - SUBLANES)
                rows = pl.ds(c * RC, RC)
                a_c, xa = _conv3_window(sa, start, cs, wa, ba)
                g_c, xg = _conv3_window(sg, start, cs, wg, bg)
                gel, dgel = _gelu_and_grad(a_c)
                xa_ref[rows, cs] = xa[2].astype(xa_ref.dtype)
                xg_ref[rows, cs] = xg[2].astype(xg_ref.dtype)
                p_ref[rows, cs] = (g_c * dgel).astype(p_ref.dtype)
                q_ref[rows, cs] = gel.astype(q_ref.dtype)
                act_ref[rows, cs] = (gel * g_c).astype(act_ref.dtype)
            ka_ref[:, cs] = za_ref[tm - SUBLANES:tm, cs]
            kg_ref[:, cs] = zg_ref[tm - SUBLANES:tm, cs]

        za_ref[...] = _dot(u_ref[...], fa_ref[...])
        zg_ref[...] = _dot(u_ref[...], fg_ref[...])
        for s in range(ns):
            gate(s)

    blk = pl.BlockSpec((tm, CW), lambda j, i: (i, j))
    wblk = pl.BlockSpec((D, CW), lambda j, i: (0, j))
    w8 = pl.BlockSpec((SUBLANES, CW), lambda j, i: (0, j))
    w1 = pl.BlockSpec((1, CW), lambda j, i: (0, j))
    return pl.pallas_call(
        body, name=name, grid=(Fh // CW, T // tm),
        in_specs=[pl.BlockSpec((tm, D), lambda j, i: (i, 0)), wblk, wblk, w8, w8, w1, w1],
        out_specs=[blk] * 5,
        out_shape=[jax.ShapeDtypeStruct((T, Fh), BF16)] * 5,
        scratch_shapes=[pltpu.VMEM((SUBLANES, CW), F32), pltpu.VMEM((SUBLANES, CW), F32),
                        pltpu.VMEM((RC + SUBLANES, CW), F32), pltpu.VMEM((RC + SUBLANES, CW), F32),
                        pltpu.VMEM((tm, CW), F32), pltpu.VMEM((tm, CW), F32)],
        compiler_params=_cp("parallel", "arbitrary"),
    )(u2, fa, fg, cwa, cwg, cba, cbg)


def _ffn_bwd_core(dh, dhb, xa, xg, p, q, wdT, faT, fgT, cwa, cwg, h, g, name):
    T, D = dhb.shape
    Fh = xa.shape[1]
    tm = min(T, ROWS)
    CW, SB, RC = FFN_CW, FFN_SB, FFN_RC
    ns = CW // SB
    nj = Fh // CW
    nb = T // tm
    nc = tm // RC

    def body(dh_ref, xa_ref, xg_ref, p_ref, q_ref, wd_ref, fa_ref, fg_ref, cwa_ref, cwg_ref, h_ref, g_ref, res_ref,
             dza_ref, dzg_ref, dca_ref, dcg_ref, dho_ref, dhbo_ref, dg_ref,
             d_ref, sa_ref, sg_ref, ka_ref, kg_ref, du_ref):
        ib, j = pl.program_id(0), pl.program_id(1)

        @pl.when((ib == 0) & (j == 0))
        def _():
            dca_ref[...] = jnp.zeros_like(dca_ref)
            dcg_ref[...] = jnp.zeros_like(dcg_ref)
            dg_ref[...] = jnp.zeros_like(dg_ref)

        @pl.when(ib == 0)
        def _():
            ka_ref[j] = jnp.zeros((SUBLANES, CW), F32)
            kg_ref[j] = jnp.zeros((SUBLANES, CW), F32)

        @pl.when(j == 0)
        def _():
            du_ref[...] = jnp.zeros_like(du_ref)

        def fold(v):
            return jnp.sum(v.reshape(RC // SUBLANES, SUBLANES, SB), axis=0)

        def gate(s):
            cs = pl.ds(s * SB, SB)
            wa = [cwa_ref[t:t + 1, cs] for t in range(FFN_CONV)]
            wg = [cwg_ref[t:t + 1, cs] for t in range(FFN_CONV)]
            sa_ref[pl.ds(tm, SUBLANES), cs] = ka_ref[j, :, cs]
            sg_ref[pl.ds(tm, SUBLANES), cs] = kg_ref[j, :, cs]
            for c in range(nc):
                rows = pl.ds(c * RC, RC)
                dact = d_ref[rows, cs]
                sa_ref[rows, cs] = dact * p_ref[rows, cs].astype(F32)
                sg_ref[rows, cs] = dact * q_ref[rows, cs].astype(F32)
            n = RC + SUBLANES
            for s_ref, x_ref, o_ref, dc_ref, w in ((sa_ref, xa_ref, dza_ref, dca_ref, wa),
                                                   (sg_ref, xg_ref, dzg_ref, dcg_ref, wg)):
                acc = [jnp.zeros((SUBLANES, SB), F32) for _ in range(FFN_CONV + 1)]
                for c in range(nc):
                    rows = pl.ds(c * RC, RC)
                    win = s_ref[pl.ds(c * RC, n), cs]
                    d0 = win[:RC]
                    d1 = pltpu.roll(win, n - 1, axis=0)[:RC]
                    d2 = pltpu.roll(win, n - 2, axis=0)[:RC]
                    o_ref[rows, cs] = ((d0 * w[2] + d1 * w[1]) + d2 * w[0]).astype(o_ref.dtype)
                    x = x_ref[rows, cs].astype(F32)
                    acc = [acc[0] + fold(d2 * x), acc[1] + fold(d1 * x), acc[2] + fold(d0 * x), acc[3] + fold(d0)]
                for t in range(FFN_CONV + 1):
                    dc_ref[j, t:t + 1, cs] += jnp.sum(acc[t], axis=0, keepdims=True)
            ka_ref[j, :, cs] = sa_ref[0:SUBLANES, cs]
            kg_ref[j, :, cs] = sg_ref[0:SUBLANES, cs]

        d_ref[...] = _dot(dh_ref[...], wd_ref[...])
        for s in range(ns):
            gate(s)
        du_ref[...] += _dot(dza_ref[...], fa_ref[...]) + _dot(dzg_ref[...], fg_ref[...])

        @pl.when(j == nj - 1)
        def _():
            dho, dg = _rms_bwd_tile(h_ref[...], g_ref[...], du_ref[...], res_ref[...])
            dg_ref[...] += dg
            dho_ref[...] = dho
            dhbo_ref[...] = dho.astype(dhbo_ref.dtype)

    blk = pl.BlockSpec((tm, CW), lambda ib, j: (nb - 1 - ib, j))
    row = pl.BlockSpec((tm, D), lambda ib, j: (nb - 1 - ib, 0))
    vec = pl.BlockSpec((1, D), lambda ib, j: (0, 0))
    w8 = pl.BlockSpec((SUBLANES, CW), lambda ib, j: (0, j))
    wrow = pl.BlockSpec((CW, D), lambda ib, j: (j, 0))
    acc = pl.BlockSpec((nj, SUBLANES, CW), lambda ib, j: (0, 0, 0))
    return pl.pallas_call(
        body, name=name, grid=(nb, nj),
        in_specs=[row, blk, blk, blk, blk, pl.BlockSpec((D, CW), lambda ib, j: (0, j)), wrow, wrow, w8, w8,
                  row, vec, row],
        out_specs=[blk, blk, acc, acc, row, row, vec],
        out_shape=[jax.ShapeDtypeStruct((T, Fh), BF16), jax.ShapeDtypeStruct((T, Fh), BF16),
                   jax.ShapeDtypeStruct((nj, SUBLANES, CW), F32), jax.ShapeDtypeStruct((nj, SUBLANES, CW), F32),
                   jax.ShapeDtypeStruct((T, D), F32), jax.ShapeDtypeStruct((T, D), BF16),
                   jax.ShapeDtypeStruct((1, D), F32)],
        scratch_shapes=[pltpu.VMEM((tm, CW), F32), pltpu.VMEM((tm + SUBLANES, CW), F32),
                        pltpu.VMEM((tm + SUBLANES, CW), F32), pltpu.VMEM((nj, SUBLANES, CW), F32),
                        pltpu.VMEM((nj, SUBLANES, CW), F32), pltpu.VMEM((tm, D), F32)],
        compiler_params=_cp("arbitrary", "arbitrary"),
    )(dhb, xa, xg, p, q, wdT, faT, fgT, cwa, cwg, h, g.reshape(1, D), dh)


def _adamw(w, grads, m, v, name):
    rows, cols = w.shape
    tr = _row_tile(rows, max(SUBLANES, min(512, TILE_BYTES // (4 * cols)) // SUBLANES * SUBLANES))
    ng = len(grads)

    def body(*refs):
        w_ref, g_refs, m_ref, v_ref = refs[0], refs[1:1 + ng], refs[1 + ng], refs[2 + ng]
        go_ref, d_ref, mo_ref, vo_ref = refs[3 + ng:]
        g = g_refs[0][...]
        for r in g_refs[1:]:
            g = g + r[...]
        mm = ADAM_B1 * m_ref[...] + (1.0 - ADAM_B1) * g
        vv = ADAM_B2 * v_ref[...] + (1.0 - ADAM_B2) * (g * g)
        m_hat = mm / (1.0 - ADAM_B1 ** ADAM_STEP)
        v_hat = vv / (1.0 - ADAM_B2 ** ADAM_STEP)
        go_ref[...] = g
        d_ref[...] = -ADAM_LR * (m_hat / (jnp.sqrt(v_hat) + ADAM_EPS) + ADAM_WD * w_ref[...])
        mo_ref[...] = mm
        vo_ref[...] = vv

    blk = pl.BlockSpec((tr, cols), lambda i: (i, 0))
    return pl.pallas_call(
        body, name=name, grid=(rows // tr,), in_specs=[blk] * (3 + ng), out_specs=[blk] * 4,
        out_shape=[jax.ShapeDtypeStruct((rows, cols), F32)] * 4, compiler_params=_cp("parallel"),
    )(w, *grads, m, v)


def _add_slabs(a, b, out_dtype, name):
    n, rows, cols = a.shape
    tr = _row_tile(rows, max(SUBLANES, min(512, TILE_BYTES // (4 * cols)) // SUBLANES * SUBLANES))

    def body(a_ref, b_ref, o_ref):
        o_ref[...] = (a_ref[...] + b_ref[...]).astype(o_ref.dtype)

    blk = pl.BlockSpec((1, tr, cols), lambda k, i: (k, i, 0))
    return pl.pallas_call(
        body, name=name, grid=(n, rows // tr), in_specs=[blk, blk], out_specs=blk,
        out_shape=jax.ShapeDtypeStruct((n, rows, cols), out_dtype), compiler_params=_cp("parallel", "parallel"),
    )(a, b)


def _sum_leading(parts, name, last=None):
    n, rows, cols = parts.shape
    tr = _row_tile(rows, max(SUBLANES, min(512, TILE_BYTES // (4 * cols)) // SUBLANES * SUBLANES))

    def body(*refs):
        p_ref, o_ref = refs[0], refs[-1]
        acc = p_ref[0].astype(F32)
        for d in range(1, n):
            acc = acc + p_ref[d].astype(F32)
        if last is not None:
            acc = acc + refs[1][...].astype(F32)
        o_ref[...] = acc

    blk = pl.BlockSpec((tr, cols), lambda i: (i, 0))
    return pl.pallas_call(
        body, name=name, grid=(rows // tr,),
        in_specs=[pl.BlockSpec((n, tr, cols), lambda i: (0, i, 0))] + ([] if last is None else [blk]), out_specs=blk,
        out_shape=jax.ShapeDtypeStruct((rows, cols), F32), compiler_params=_cp("parallel"),
    )(*((parts,) if last is None else (parts, last)))


def _row_tile(rows, cap=512):
    if rows <= cap:
        return rows
    return max(t for t in range(SUBLANES, cap + 1, SUBLANES) if rows % t == 0)


def _place():
    return lax.axis_index("x"), lax.axis_index("y"), lax.axis_index("c")


def _gather_shards(arrs, name):
    n = len(arrs)
    hl = DEPTH // 2

    def body(*refs):
        ins, outs = refs[:n], refs[n:2 * n]
        send_sems, recv_sems, pass_send, pass_recv = refs[2 * n:]
        x, y, c = _place()
        chips = [(1 - x, y), (x, 1 - y), (1 - x, 1 - y)]
        mine, theirs = pl.ds(c * hl, hl), pl.ds((1 - c) * hl, hl)

        def send(a, j, shard):
            px, py = chips[j]
            return pltpu.make_async_remote_copy(
                src_ref=ins[a].at[mine], dst_ref=outs[a].at[shard, mine], send_sem=send_sems.at[3 * a + j],
                recv_sem=recv_sems.at[3 * a + j], device_id=(px, py, c), device_id_type=MESH)

        def passed(a, j, half):
            px, py = chips[j]
            blk = outs[a].at[2 * px + py, half]
            return pltpu.make_async_remote_copy(
                src_ref=blk, dst_ref=blk, send_sem=pass_send.at[3 * a + j], recv_sem=pass_recv.at[3 * a + j],
                device_id=(x, y, 1 - c), device_id_type=MESH)

        sends = [send(a, j, 2 * x + y) for a in range(n) for j in range(3)]
        for cp in sends:
            cp.start()
        passes = []
        for a in range(n):
            for j, (px, py) in enumerate(chips):
                send(a, j, 2 * px + py).wait_recv()
                passes.append(passed(a, j, mine))
                passes[-1].start()
        for a in range(n):
            for j in range(3):
                passed(a, j, theirs).wait_recv()
        for cp in sends + passes:
            cp.wait_send()

    hbm = pl.BlockSpec(memory_space=pl.ANY)
    return pl.pallas_call(
        body, name=name, in_specs=[hbm] * n, out_specs=[hbm] * n,
        out_shape=[jax.ShapeDtypeStruct((N_CHIPS,) + a.shape, a.dtype) for a in arrs],
        scratch_shapes=[pltpu.SemaphoreType.DMA((3 * n,)), pltpu.SemaphoreType.DMA((3 * n,)),
                        pltpu.SemaphoreType.DMA((3 * n,)), pltpu.SemaphoreType.DMA((3 * n,))],
        compiler_params=pltpu.CompilerParams(has_side_effects=True),
    )(*arrs)


def _other_half_to_sibling(slabs, name):
    n = len(slabs)

    def body(*refs):
        ins, got = refs[:n], refs[n:2 * n]
        send_sems, recv_sems = refs[2 * n:]
        x, y, c = _place()
        copies = [pltpu.make_async_remote_copy(
            src_ref=ins[a].at[:, 1 - c], dst_ref=got[a], send_sem=send_sems.at[a], recv_sem=recv_sems.at[a],
            device_id=(x, y, 1 - c), device_id_type=MESH) for a in range(n)]
        for cp in copies:
            cp.start()
        for cp in copies:
            cp.wait()

    hbm = pl.BlockSpec(memory_space=pl.ANY)
    return pl.pallas_call(
        body, name=name, in_specs=[hbm] * n, out_specs=[hbm] * n,
        out_shape=[jax.ShapeDtypeStruct((s.shape[0],) + s.shape[2:], s.dtype) for s in slabs],
        scratch_shapes=[pltpu.SemaphoreType.DMA((n,)), pltpu.SemaphoreType.DMA((n,))],
        compiler_params=pltpu.CompilerParams(has_side_effects=True),
    )(*slabs)


def _exchange_grads(slabs, small, name):
    n = len(slabs)

    def body(*refs):
        ins, small_ref = refs[:n], refs[n]
        outs, small_out = refs[n + 1:2 * n + 1], refs[2 * n + 1]
        send_sems, recv_sems, ssend, srecv = refs[2 * n + 2:]
        x, y, c = _place()
        chips = [(1 - x, y), (x, 1 - y), (1 - x, 1 - y)]
        me = 4 * x + 2 * y + c
        flips = [(fx, fy, fc) for fx in (0, 1) for fy in (0, 1) for fc in (0, 1)][1:]

        def copy(a, j):
            px, py = chips[j]
            return pltpu.make_async_remote_copy(
                src_ref=ins[a].at[2 * px + py], dst_ref=outs[a].at[j], send_sem=send_sems.at[3 * a + j],
                recv_sem=recv_sems.at[3 * a + j], device_id=(px, py, c), device_id_type=MESH)

        def scopy(k, row):
            fx, fy, fc = flips[k]
            return pltpu.make_async_remote_copy(
                src_ref=small_ref, dst_ref=small_out.at[row], send_sem=ssend.at[k], recv_sem=srecv.at[k],
                device_id=(x ^ fx, y ^ fy, c ^ fc), device_id_type=MESH)

        sends = [copy(a, j) for a in range(n) for j in range(3)] + [scopy(k, me) for k in range(7)]
        for cp in sends:
            cp.start()
        for k, (fx, fy, fc) in enumerate(flips):
            scopy(k, 4 * (x ^ fx) + 2 * (y ^ fy) + (c ^ fc)).wait_recv()
        for a in range(n):
            for j in range(3):
                copy(a, j).wait_recv()
        for cp in sends:
            cp.wait_send()

    hbm = pl.BlockSpec(memory_space=pl.ANY)
    return pl.pallas_call(
        body, name=name, in_specs=[hbm] * (n + 1), out_specs=[hbm] * (n + 1),
        out_shape=[jax.ShapeDtypeStruct((3,) + s.shape[1:], s.dtype) for s in slabs]
        + [jax.ShapeDtypeStruct((N_DEV,) + small.shape, small.dtype)],
        scratch_shapes=[pltpu.SemaphoreType.DMA((3 * n,)), pltpu.SemaphoreType.DMA((3 * n,)),
                        pltpu.SemaphoreType.DMA((7,)), pltpu.SemaphoreType.DMA((7,))],
        compiler_params=pltpu.CompilerParams(has_side_effects=True),
    )(*slabs, small)


def _swap_with_sibling(arrs, name):
    n = len(arrs)

    def body(*refs):
        ins, outs = refs[:n], refs[n:2 * n]
        send_sems, recv_sems = refs[2 * n:]
        x, y, c = _place()
        copies = [pltpu.make_async_remote_copy(
            src_ref=ins[a], dst_ref=outs[a], send_sem=send_sems.at[a], recv_sem=recv_sems.at[a],
            device_id=(x, y, 1 - c), device_id_type=MESH) for a in range(n)]
        for cp in copies:
            cp.start()
        for cp in copies:
            cp.wait()

    hbm = pl.BlockSpec(memory_space=pl.ANY)
    return pl.pallas_call(
        body, name=name, in_specs=[hbm] * n, out_specs=[hbm] * n,
        out_shape=[jax.ShapeDtypeStruct(a.shape, a.dtype) for a in arrs],
        scratch_shapes=[pltpu.SemaphoreType.DMA((n,)), pltpu.SemaphoreType.DMA((n,))],
        compiler_params=pltpu.CompilerParams(has_side_effects=True),
    )(*arrs)


def _block_diag(w):
    eye = jnp.eye(LRU_BLOCKS, dtype=w.dtype)
    return (eye[:, None, :, None] * w[:, :, None, :]).reshape(LRU_W, LRU_W)


def _diag_blocks(m):
    m4 = m.reshape(LRU_BLOCKS, LRU_BLOCK, LRU_BLOCKS, LRU_BLOCK)
    return jnp.stack([m4[b, :, b, :] for b in range(LRU_BLOCKS)])


def _pad_rows(a, rows):
    return jnp.pad(a, ((0, rows - a.shape[0]), (0, 0)))


def _layer_weights(p, l):
    w_in = p["w_in"][l]
    n_gla = 2 * QK_W + 2 * GLA_W
    gate = jnp.pad(w_in[:, n_gla:n_gla + GATE_RANK], ((0, 0), (0, GATE_PAD - GATE_RANK)))
    wg = jnp.concatenate([w_in[:, :n_gla], gate], axis=1)
    wl = w_in[:, n_gla + GATE_RANK:]
    w_out = p["w_out"][l]
    fa, fg = p["ffn_w_in"][l][:, :FFN_H], p["ffn_w_in"][l][:, FFN_H:]
    wd = p["ffn_w_down"][l]
    return dict(
        wg=wg, wl=wl, wgT=wg.T, wlT=wl.T, wo=w_out, woT=w_out.T,
        fa=fa, fg=fg, faT=fa.T, fgT=fg.T, wd=wd, wdT=wd.T,
        w2p=_pad_rows(p["gla_gate_w2"][l], GATE_PAD).astype(BF16),
        wa=_block_diag(p["lru_wa"][l]).astype(BF16), wx=_block_diag(p["lru_wx"][l]).astype(BF16),
        lcw=_pad_rows(p["lru_conv_w"][l], SUBLANES),
        fcwa=_pad_rows(p["ffn_conv_w"][l][:, :FFN_H], SUBLANES), fcwg=_pad_rows(p["ffn_conv_w"][l][:, FFN_H:], SUBLANES),
    )


def _local_step(x, tgt, p):
    row = lambda v: v.reshape(1, -1)
    h = x
    u = _rms_fwd(h, p["ln_mix"][0], "mix_norm_fwd0")
    stash = []
    for l in range(DEPTH):
        w = _layer_weights(p, l)
        s = dict(w=w, h0=h)
        pg = _mm(u, w["wg"], None, F32, f"proj_gla_fwd{l}")
        plr = _mm(u, w["wl"], None, F32, f"proj_lru_fwd{l}")
        ym, o_st, s_st = _gla_fwd(pg, w["w2p"], p["gla_gate_b"][l], p["gla_norm"][l], f"gla_fwd{l}")
        ym, hs = _lru_fwd(plr, ym, w["lcw"], row(p["lru_conv_b"][l]), w["wa"], row(p["lru_ba"][l]), w["wx"],
                          row(p["lru_bx"][l]), row(p["lru_lambda"][l]), f"lru_fwd{l}")
        h, u2 = _mm(ym, w["wo"], h, F32, f"out_fwd{l}", norm_g=p["ln_ffn"][l])
        s.update(u=u, pg=pg, plr=plr, ym=ym, o_st=o_st, s_st=s_st, hs=hs, h1=h)
        cba, cbg = row(p["ffn_conv_b"][l][:FFN_H]), row(p["ffn_conv_b"][l][FFN_H:])
        *kept, act = _ffn_up_fwd(u2, w["fa"], w["fg"], w["fcwa"], w["fcwg"], cba, cbg, f"ffn_up_fwd{l}")
        if l + 1 < DEPTH:
            h, u = _mm(act, w["wd"], h, F32, f"ffn_down_fwd{l}", norm_g=p["ln_mix"][l + 1])
        else:
            h = _mm(act, w["wd"], h, F32, f"ffn_down_fwd{l}")
        s.update(u2=u2, ffn_kept=kept, act=act)
        stash.append(s)

    loss, dh, dhb, d_ln_final = _loss_head(h, p["ln_final"], tgt, "loss_head")

    g = {k: [None] * DEPTH for k in ("ln_mix", "w_in", "gla_gate_w2", "gla_gate_b", "gla_norm", "lru_conv_w",
                                     "lru_conv_b", "lru_wa", "lru_ba", "lru_wx", "lru_bx", "lru_lambda",
                                     "ln_ffn", "ffn_conv_w", "ffn_conv_b")}
    slab = dict(w_out=jnp.zeros((N_CHIPS, DEPTH * D_MODEL // N_CHIPS, D_MODEL), F32),
                ffn_w_in=jnp.zeros((N_CHIPS, DEPTH * D_MODEL, 2 * FFN_H // N_CHIPS), F32),
                ffn_w_down=jnp.zeros((N_CHIPS, DEPTH * FFN_H // N_CHIPS, D_MODEL), F32))
    n_gla = 2 * QK_W + 2 * GLA_W
    for l in reversed(range(DEPTH)):
        s = stash[l]
        w = s["w"]
        slab["ffn_w_down"] = _mm_tn_into(slab["ffn_w_down"], s["act"], dhb, l, 0, f"ffn_down_dw{l}",
                                         tk=FFN_H // N_CHIPS, tn=D_MODEL, tm=2048)
        dza, dzg, dca, dcg, dh, dhb, dln = _ffn_bwd_core(
            dh, dhb, *s["ffn_kept"], w["wdT"], w["faT"], w["fgT"], w["fcwa"], w["fcwg"], s["h1"], p["ln_ffn"][l],
            f"ffn_bwd_core{l}")
        dca, dcg = (jnp.moveaxis(d, 0, 1).reshape(SUBLANES, FFN_H) for d in (dca, dcg))
        g["ffn_conv_w"][l] = jnp.concatenate([dca[:FFN_CONV], dcg[:FFN_CONV]], axis=1)
        g["ffn_conv_b"][l] = jnp.concatenate([dca[FFN_CONV], dcg[FFN_CONV]])
        for half, dz in enumerate((dza, dzg)):
            slab["ffn_w_in"] = _mm_tn_into(slab["ffn_w_in"], s["u2"], dz, l, 2 * half, f"ffn_in_dw{l}_{half}",
                                           tk=D_MODEL, tn=2 * FFN_H // N_CHIPS)
        g["ln_ffn"][l] = dln[0]
        slab["w_out"] = _mm_tn_into(slab["w_out"], s["ym"], dhb, l, 0, f"out_dw{l}",
                                    tk=D_MODEL // N_CHIPS, tn=D_MODEL, tm=2048)
        dyc = _mm(dhb, w["woT"], None, F32, f"out_dx{l}")
        dpg, dw2, db2, dng = _gla_bwd(dyc, s["pg"], s["o_st"], s["s_st"], w["w2p"], p["gla_gate_b"][l],
                                      p["gla_norm"][l], f"gla_bwd{l}")
        dpl, dcw, dvec, dwa, dwx = _lru_bwd(dyc, s["plr"], s["hs"], w["lcw"], row(p["lru_conv_b"][l]), w["wa"],
                                            row(p["lru_ba"][l]), w["wx"], row(p["lru_bx"][l]),
                                            row(p["lru_lambda"][l]), f"lru_bwd{l}")
        g["gla_gate_w2"][l] = dw2[:GATE_RANK]
        g["gla_gate_b"][l] = db2[0]
        g["gla_norm"][l] = dng[0]
        g["lru_conv_w"][l] = dcw[:LRU_CONV]
        g["lru_conv_b"][l], g["lru_ba"][l], g["lru_bx"][l], g["lru_lambda"][l] = dvec[0], dvec[1], dvec[2], dvec[3]
        g["lru_wa"][l], g["lru_wx"][l] = _diag_blocks(dwa), _diag_blocks(dwx)
        dwg = _mm_tn(s["u"], dpg, f"proj_gla_dw{l}")
        dwl = _mm_tn(s["u"], dpl, f"proj_lru_dw{l}")
        g["w_in"][l] = jnp.concatenate([dwg[:, :n_gla + GATE_RANK], dwl], axis=1)
        dh, dhb, dln = _mm_rms_bwd([(dpg, w["wgT"]), (dpl, w["wlT"])], s["h0"], p["ln_mix"][l], dh, f"proj_dx{l}")
        g["ln_mix"][l] = dln[0]
    grads = {k: jnp.stack(v) for k, v in g.items()}
    grads["w_in"] = _slabs_from_whole("w_in", grads["w_in"])
    grads.update(slab)
    grads["ln_final"] = d_ln_final[0]
    return loss, dh, grads


BIG = ("w_in", "w_out", "ffn_w_in", "ffn_w_down")
COL_SHARDED = ("w_in", "ffn_w_in", "gla_gate_w2", "lru_conv_w", "ffn_conv_w")
SMALL = ("ln_mix", "gla_gate_w2", "gla_gate_b", "gla_norm", "lru_conv_w", "lru_conv_b", "lru_wa", "lru_ba", "lru_wx",
         "lru_bx", "lru_lambda", "ln_ffn", "ffn_conv_w", "ffn_conv_b", "ln_final")
WEIGHTS = ("ln_mix", "w_in", "gla_gate_w2", "gla_gate_b", "gla_norm", "lru_conv_w", "lru_conv_b", "lru_wa", "lru_ba",
           "lru_wx", "lru_bx", "lru_lambda", "w_out", "ln_ffn", "ffn_w_in", "ffn_conv_w", "ffn_conv_b", "ffn_w_down",
           "ln_final")
PACK = SUBLANES * LANES


def _whole_from_shards(name, g):
    if name in COL_SHARDED:
        return jnp.moveaxis(g, 0, -2).reshape(g.shape[1:-1] + (N_CHIPS * g.shape[-1],))
    return jnp.moveaxis(g, 0, 1).reshape((g.shape[1], N_CHIPS * g.shape[2]) + g.shape[3:])


def _slabs_from_whole(name, w):
    L, r, c = w.shape
    if name in COL_SHARDED:
        s = jnp.moveaxis(w.reshape(L, r, N_CHIPS, c // N_CHIPS), 2, 0)
    else:
        s = jnp.moveaxis(w.reshape(L, N_CHIPS, r // N_CHIPS, c), 1, 0)
    return s.reshape(N_CHIPS, -1, s.shape[-1])


def _pack(arrs):
    flat = []
    for a in arrs:
        f = a.reshape(-1)
        flat.append(jnp.pad(f, (0, (-f.shape[0]) % PACK)))
    return jnp.concatenate(flat).reshape(-1, LANES)


def _unpack(packed, shapes):
    out, at = [], 0
    flat = packed.reshape(-1)
    for s in shapes:
        size = math.prod(s)
        out.append(flat[at:at + size].reshape(s))
        at += size + (-size) % PACK
    return out


def kernel(x, ln_mix, w_in, gla_gate_w2, gla_gate_b, gla_norm, lru_conv_w, lru_conv_b, lru_wa, lru_ba, lru_wx, lru_bx, lru_lambda, w_out, ln_ffn, ffn_w_in, ffn_conv_w, ffn_conv_b, ffn_w_down, ln_final, loss_target, m_ln_mix, m_w_in, m_gla_gate_w2, m_gla_gate_b, m_gla_norm, m_lru_conv_w, m_lru_conv_b, m_lru_wa, m_lru_ba, m_lru_wx, m_lru_bx, m_lru_lambda, m_w_out, m_ln_ffn, m_ffn_w_in, m_ffn_conv_w, m_ffn_conv_b, m_ffn_w_down, m_ln_final, v_ln_mix, v_w_in, v_gla_gate_w2, v_gla_gate_b, v_gla_norm, v_lru_conv_w, v_lru_conv_b, v_lru_wa, v_lru_ba, v_lru_wx, v_lru_bx, v_lru_lambda, v_w_out, v_ln_ffn, v_ffn_w_in, v_ffn_conv_w, v_ffn_conv_b, v_ffn_w_down, v_ln_final):
    w = dict(ln_mix=ln_mix, w_in=w_in, gla_gate_w2=gla_gate_w2, gla_gate_b=gla_gate_b, gla_norm=gla_norm,
             lru_conv_w=lru_conv_w, lru_conv_b=lru_conv_b, lru_wa=lru_wa, lru_ba=lru_ba, lru_wx=lru_wx, lru_bx=lru_bx,
             lru_lambda=lru_lambda, w_out=w_out, ln_ffn=ln_ffn, ffn_w_in=ffn_w_in, ffn_conv_w=ffn_conv_w,
             ffn_conv_b=ffn_conv_b, ffn_w_down=ffn_w_down, ln_final=ln_final)
    m = dict(ln_mix=m_ln_mix, w_in=m_w_in, gla_gate_w2=m_gla_gate_w2, gla_gate_b=m_gla_gate_b, gla_norm=m_gla_norm,
             lru_conv_w=m_lru_conv_w, lru_conv_b=m_lru_conv_b, lru_wa=m_lru_wa, lru_ba=m_lru_ba, lru_wx=m_lru_wx,
             lru_bx=m_lru_bx, lru_lambda=m_lru_lambda, w_out=m_w_out, ln_ffn=m_ln_ffn, ffn_w_in=m_ffn_w_in,
             ffn_conv_w=m_ffn_conv_w, ffn_conv_b=m_ffn_conv_b, ffn_w_down=m_ffn_w_down, ln_final=m_ln_final)
    v = dict(ln_mix=v_ln_mix, w_in=v_w_in, gla_gate_w2=v_gla_gate_w2, gla_gate_b=v_gla_gate_b, gla_norm=v_gla_norm,
             lru_conv_w=v_lru_conv_w, lru_conv_b=v_lru_conv_b, lru_wa=v_lru_wa, lru_ba=v_lru_ba, lru_wx=v_lru_wx,
             lru_bx=v_lru_bx, lru_lambda=v_lru_lambda, w_out=v_w_out, ln_ffn=v_ln_ffn, ffn_w_in=v_ffn_w_in,
             ffn_conv_w=v_ffn_conv_w, ffn_conv_b=v_ffn_conv_b, ffn_w_down=v_ffn_w_down, ln_final=v_ln_final)

    sharded = BIG + ("gla_gate_w2", "lru_conv_w", "ffn_conv_w")
    chip = 2 * lax.axis_index("x") + lax.axis_index("y")
    core = lax.axis_index("c")
    shards = [w[k].astype(MXU_DTYPE) if k in BIG else w[k] for k in sharded]
    gathered = _gather_shards(shards, "gather_weights")
    p = dict(w)
    for k, gk, own in zip(sharded, gathered, shards):
        p[k] = _whole_from_shards(k, lax.dynamic_update_index_in_dim(gk, own, chip, 0))

    loss, grad_x, grads = _local_step(x[0], loss_target[0], p)
    loss = lax.psum(loss[0, 0], ("x", "y", "c"))

    slabs = [grads[k].reshape(N_CHIPS, 2, grads[k].shape[1] // 2, grads[k].shape[2]) for k in BIG]
    got = _other_half_to_sibling(slabs, "other_half_to_sibling")
    kept = [lax.dynamic_index_in_dim(s, core, 1, keepdims=False) for s in slabs]
    chip_half = [_add_slabs(a, b, BF16, f"core_sum_{k}") for k, a, b in zip(BIG, kept, got)]
    small = _pack([grads[k] for k in SMALL])
    *recv, small_all = _exchange_grads(chip_half, small, "exchange_grads")
    own = [lax.dynamic_index_in_dim(h, chip, 0, keepdims=False) for h in chip_half]
    mine = [_sum_leading(r, f"chip_sum_{k}", last=o) for k, r, o in zip(BIG, recv, own)]
    theirs = _swap_with_sibling(mine, "swap_core_halves")
    big_g = [jnp.concatenate([jnp.where(core == 0, a, b), jnp.where(core == 0, b, a)]) for a, b in zip(mine, theirs)]
    small_all = lax.dynamic_update_index_in_dim(small_all, small, 2 * chip + core, 0)
    small_sum = _unpack(_sum_leading(small_all, "sum_small_grads"), [grads[k].shape for k in SMALL])

    me = 2 * lax.axis_index("x") + lax.axis_index("y")
    out_g, out_d, out_m, out_v = {}, {}, {}, {}
    for k, gk in zip(BIG, big_g):
        shape = w[k].shape
        cols = shape[-1]
        res = _adamw(w[k].reshape(-1, cols), [gk], m[k].reshape(-1, cols), v[k].reshape(-1, cols), f"adamw_{k}")
        out_g[k], out_d[k], out_m[k], out_v[k] = [r.reshape(shape) for r in res]
    small_g = []
    for k, gk in zip(SMALL, small_sum):
        if k in COL_SHARDED:
            width = w[k].shape[-1]
            gk = lax.dynamic_slice_in_dim(gk, me * width, width, axis=gk.ndim - 1)
        small_g.append(gk)
    shapes = [w[k].shape for k in SMALL]
    res = _adamw(_pack([w[k] for k in SMALL]), [_pack(small_g)], _pack([m[k] for k in SMALL]),
                 _pack([v[k] for k in SMALL]), "adamw_small")
    for out, packed in zip((out_g, out_d, out_m, out_v), res):
        for k, a in zip(SMALL, _unpack(packed, shapes)):
            out[k] = a
    return (loss, grad_x[None], *[out_g[k] for k in WEIGHTS], *[out_d[k] for k in WEIGHTS],
            *[out_m[k] for k in WEIGHTS], *[out_v[k] for k in WEIGHTS])
```

```python
import math

import jax
import jax.numpy as jnp
from jax import lax
from jax.experimental import pallas as pl
from jax.experimental.pallas import tpu as pltpu

F32 = jnp.float32
BF16 = jnp.bfloat16
MXU_DTYPE = BF16

D_MODEL = 1024
DEPTH = 4
HEADS, DK, DV, CHUNK, GATE_RANK = 4, 64, 128, 64, 16
QK_W = HEADS * DK
GLA_W = HEADS * DV
LRU_W = 512
LRU_BLOCKS, LRU_BLOCK, LRU_CONV, LRU_C = 8, 64, 4, 8.0
FFN_H = 3 * D_MODEL
FFN_CONV = 3
EPS = 1e-6
GATE_PAD = 128
GLA_COLS = 2 * QK_W + 2 * GLA_W + GATE_PAD
LRU_COLS = 2 * LRU_W
ADAM_LR, ADAM_B1, ADAM_B2, ADAM_EPS, ADAM_WD, ADAM_STEP = 0.001, 0.9, 0.999, 1e-08, 0.01, 10

LANES = 128
SUBLANES = 8
VMEM_LIMIT = 56 * 1024 * 1024
ROWS = 512
TILE_BYTES = 1 << 20
FFN_CW = 1024
FFN_SB = 256
FFN_RC = 32
N_CHIPS = 4
N_DEV = 8
MESH = pl.DeviceIdType.MESH


def _cp(*sem):
    return pltpu.CompilerParams(dimension_semantics=sem, vmem_limit_bytes=VMEM_LIMIT)


def _dot(a, b):
    return jnp.dot(a.astype(MXU_DTYPE), b.astype(MXU_DTYPE), preferred_element_type=F32)


def _dot_nt(a, b):
    return lax.dot_general(a.astype(MXU_DTYPE), b.astype(MXU_DTYPE), (((1,), (1,)), ((), ())),
                           preferred_element_type=F32)


def _dot_tn(a, b):
    return lax.dot_general(a.astype(MXU_DTYPE), b.astype(MXU_DTYPE), (((0,), (0,)), ((), ())),
                           preferred_element_type=F32)


def _bdot(eq, a, b):
    return jnp.einsum(eq, a, b, preferred_element_type=F32)


def _split3(x):
    x1 = x.astype(BF16)
    r1 = x - x1.astype(F32)
    x2 = r1.astype(BF16)
    x3 = (r1 - x2.astype(F32)).astype(BF16)
    return x1, x2, x3


GELU_C = math.sqrt(2.0 / math.pi)
GELU_A = 0.044715


def _gelu(x):
    return x * (0.5 * (1.0 + jnp.tanh(GELU_C * (x + GELU_A * (x * x * x)))))


def _gelu_and_grad(x):
    x2 = x * x
    t = jnp.tanh(x * (GELU_C + (GELU_C * GELU_A) * x2))
    cdf = 0.5 * t + 0.5
    half_sech2 = 0.5 - 0.5 * (t * t)
    return x * cdf, cdf + (x * half_sech2) * (GELU_C + (3.0 * GELU_C * GELU_A) * x2)


def _expm1(x):
    small = x * (1.0 + x * (0.5 + x * (1.0 / 6.0 + x * (1.0 / 24.0 + x * (1.0 / 120.0)))))
    return jnp.where(jnp.abs(x) < 0.1, small, jnp.exp(x) - 1.0)


def _shift_down(x, k, fill):
    row = lax.broadcasted_iota(jnp.int32, x.shape, 0)
    return jnp.where(row >= k, pltpu.roll(x, k, axis=0), fill)


def _shift_up(x, k, fill):
    n = x.shape[0]
    row = lax.broadcasted_iota(jnp.int32, x.shape, 0)
    return jnp.where(row < n - k, pltpu.roll(x, n - k, axis=0), fill)


def _rms_fwd(h, g, name):
    T, D = h.shape
    R = min(T, ROWS)

    def body(h_ref, g_ref, o_ref):
        x = h_ref[...]
        r = lax.rsqrt(jnp.mean(x * x, axis=-1, keepdims=True) + EPS)
        o_ref[...] = ((x * r) * g_ref[...]).astype(o_ref.dtype)

    return pl.pallas_call(
        body, name=name, grid=(T // R,),
        in_specs=[pl.BlockSpec((R, D), lambda i: (i, 0)), pl.BlockSpec((1, D), lambda i: (0, 0))],
        out_specs=pl.BlockSpec((R, D), lambda i: (i, 0)),
        out_shape=jax.ShapeDtypeStruct((T, D), BF16), compiler_params=_cp("parallel"),
    )(h, g.reshape(1, D))


def _rms_bwd_tile(x, g, du, dres):
    r = lax.rsqrt(jnp.mean(x * x, axis=-1, keepdims=True) + EPS)
    xhat = x * r
    dxhat = du * g
    dx = r * (dxhat - xhat * jnp.mean(dxhat * xhat, axis=-1, keepdims=True))
    return dres + dx, jnp.sum(du * xhat, axis=0, keepdims=True)


def _mm_rms_bwd(pairs, h, g, dres, name, tm=512):
    M, D = h.shape
    tm = min(tm, M)
    n = len(pairs)

    def body(*refs):
        h_ref, g_ref, dres_ref = refs[2 * n:2 * n + 3]
        dh_ref, dhb_ref, dg_ref = refs[2 * n + 3:]

        @pl.when(pl.program_id(0) == 0)
        def _():
            dg_ref[...] = jnp.zeros_like(dg_ref)

        du = _dot(refs[0][...], refs[1][...])
        for k in range(1, n):
            du = du + _dot(refs[2 * k][...], refs[2 * k + 1][...])
        dh, dg = _rms_bwd_tile(h_ref[...], g_ref[...], du, dres_ref[...])
        dg_ref[...] += dg
        dh_ref[...] = dh
        dhb_ref[...] = dh.astype(dhb_ref.dtype)

    in_specs, args = [], []
    for a, b in pairs:
        in_specs += [pl.BlockSpec((tm, a.shape[1]), lambda i: (i, 0)), pl.BlockSpec(b.shape, lambda i: (0, 0))]
        args += [a, b]
    blk = pl.BlockSpec((tm, D), lambda i: (i, 0))
    vec = pl.BlockSpec((1, D), lambda i: (0, 0))
    return pl.pallas_call(
        body, name=name, grid=(M // tm,), in_specs=in_specs + [blk, vec, blk], out_specs=[blk, blk, vec],
        out_shape=[jax.ShapeDtypeStruct((M, D), F32), jax.ShapeDtypeStruct((M, D), BF16),
                   jax.ShapeDtypeStruct((1, D), F32)],
        compiler_params=_cp("arbitrary"),
    )(*args, h, g.reshape(1, D), dres)


def _loss_head(h, g, tgt, name):
    T, D = h.shape
    R = min(T, ROWS)

    def body(h_ref, g_ref, t_ref, loss_ref, dh_ref, dhb_ref, dg_ref):
        @pl.when(pl.program_id(0) == 0)
        def _():
            dg_ref[...] = jnp.zeros_like(dg_ref)
            loss_ref[...] = jnp.zeros_like(loss_ref)

        x = h_ref[...]
        r = lax.rsqrt(jnp.mean(x * x, axis=-1, keepdims=True) + EPS)
        xhat = x * r
        gg = g_ref[...]
        err = xhat * gg - t_ref[...]
        loss_ref[...] += 0.5 * jnp.sum(jnp.mean(err * err, axis=-1, keepdims=True), axis=0, keepdims=True)
        dy = err * (1.0 / D)
        dg_ref[...] += jnp.sum(dy * xhat, axis=0, keepdims=True)
        dxhat = dy * gg
        dh = r * (dxhat - xhat * jnp.mean(dxhat * xhat, axis=-1, keepdims=True))
        dh_ref[...] = dh
        dhb_ref[...] = dh.astype(dhb_ref.dtype)

    blk = pl.BlockSpec((R, D), lambda i: (i, 0))
    vec = pl.BlockSpec((1, D), lambda i: (0, 0))
    one = pl.BlockSpec((1, LANES), lambda i: (0, 0))
    return pl.pallas_call(
        body, name=name, grid=(T // R,), in_specs=[blk, vec, blk], out_specs=[one, blk, blk, vec],
        out_shape=[jax.ShapeDtypeStruct((1, LANES), F32), jax.ShapeDtypeStruct((T, D), F32),
                   jax.ShapeDtypeStruct((T, D), BF16), jax.ShapeDtypeStruct((1, D), F32)],
        compiler_params=_cp("arbitrary"),
    )(h, g.reshape(1, D), tgt)


def _mm(a, b, res, out_dtype, name, tm=512, tn=None, norm_g=None):
    M, K = a.shape
    N = b.shape[1]
    tm = min(tm, M)
    tn = N if tn is None else tn
    assert norm_g is None or tn == N

    def body(*refs):
        refs = list(refs)
        a_ref, b_ref = refs[:2]
        acc = _dot(a_ref[...], b_ref[...])
        if res is not None:
            acc = refs[2][...].astype(F32) + acc
        if norm_g is None:
            refs[-1][...] = acc.astype(refs[-1].dtype)
        else:
            refs[-2][...] = acc.astype(refs[-2].dtype)
            r = lax.rsqrt(jnp.mean(acc * acc, axis=-1, keepdims=True) + EPS)
            refs[-1][...] = ((acc * r) * refs[-3][...]).astype(refs[-1].dtype)

    blk = pl.BlockSpec((tm, tn), lambda j, i: (i, j))
    in_specs = [pl.BlockSpec((tm, K), lambda j, i: (i, 0)), pl.BlockSpec((K, tn), lambda j, i: (0, j))]
    args = [a, b]
    if res is not None:
        in_specs.append(blk)
        args.append(res)
    out_specs, out_shape = blk, jax.ShapeDtypeStruct((M, N), out_dtype)
    if norm_g is not None:
        in_specs.append(pl.BlockSpec((1, N), lambda j, i: (0, 0)))
        args.append(norm_g.reshape(1, N))
        out_specs, out_shape = [blk, blk], [out_shape, jax.ShapeDtypeStruct((M, N), BF16)]
    return pl.pallas_call(
        body, name=name, grid=(N // tn, M // tm), in_specs=in_specs, out_specs=out_specs, out_shape=out_shape,
        compiler_params=_cp("parallel", "parallel"),
    )(*args)


def _mm_tn_into(slab, a, b, layer, chip0, name, tk, tn, tm=1024):
    M, K = a.shape
    N = b.shape[1]
    tm = min(tm, M)
    assert slab.shape[2] == tn and (K // tk == 1 or N // tn == 1)

    def body(a_ref, b_ref, slab_ref, o_ref):
        del slab_ref

        @pl.when(pl.program_id(2) == 0)
        def _():
            o_ref[...] = jnp.zeros_like(o_ref)

        o_ref[0] += _dot_tn(a_ref[...], b_ref[...])

    return pl.pallas_call(
        body, name=name, grid=(K // tk, N // tn, M // tm),
        in_specs=[pl.BlockSpec((tm, tk), lambda k, n, m: (m, k)), pl.BlockSpec((tm, tn), lambda k, n, m: (m, n)),
                  pl.BlockSpec(memory_space=pl.ANY)],
        out_specs=pl.BlockSpec((1, tk, tn), lambda k, n, m: (chip0 + k + n, layer, 0)),
        out_shape=jax.ShapeDtypeStruct(slab.shape, F32), input_output_aliases={2: 0},
        compiler_params=_cp("parallel", "parallel", "arbitrary"),
    )(a, b, slab)


def _mm_tn(a, b, name, tm=2048, tk=None, tn=None):
    M, K = a.shape
    N = b.shape[1]
    tm = min(tm, M)
    tk = K if tk is None else tk
    tn = N if tn is None else tn

    def body(a_ref, b_ref, o_ref):
        @pl.when(pl.program_id(2) == 0)
        def _():
            o_ref[...] = jnp.zeros_like(o_ref)

        o_ref[...] += _dot_tn(a_ref[...], b_ref[...])

    return pl.pallas_call(
        body, name=name, grid=(K // tk, N // tn, M // tm),
        in_specs=[pl.BlockSpec((tm, tk), lambda k, n, m: (m, k)), pl.BlockSpec((tm, tn), lambda k, n, m: (m, n))],
        out_specs=pl.BlockSpec((tk, tn), lambda k, n, m: (k, n)),
        out_shape=jax.ShapeDtypeStruct((K, N), F32), compiler_params=_cp("parallel", "parallel", "arbitrary"),
    )(a, b)


def _same_chunk(row, col):
    shift = CHUNK.bit_length() - 1
    return jnp.right_shift(row, shift) == jnp.right_shift(col, shift)


def _gla_common(q, k, glr, w2, b2, R):
    gl = _dot(glr, w2) + b2
    la = jax.nn.log_sigmoid(gl) * (1.0 / 16.0)
    row = lax.broadcasted_iota(jnp.int32, (R, R), 0)
    col = lax.broadcasted_iota(jnp.int32, (R, R), 1)
    same = _same_chunk(row, col)
    m_tri = (same & (col <= row)).astype(BF16)
    m_all = same.astype(BF16)
    la3 = _split3(la)
    b = sum(jnp.dot(m_tri, p, preferred_element_type=F32) for p in la3)
    bl = sum(jnp.dot(m_all, p, preferred_element_type=F32) for p in la3)
    eb = jnp.exp(b)
    enb = jnp.exp(-b)
    ek = jnp.exp(bl - b)
    qi = (q * (DK ** -0.5)) * eb
    ki = k * enb
    kd = k * ek
    return gl, la3, eb, enb, ek, qi, ki, kd


def _bsplit(x, n):
    return x.reshape(n, CHUNK, x.shape[-1])


def _tril():
    return (lax.broadcasted_iota(jnp.int32, (CHUNK, CHUNK), 1)
            <= lax.broadcasted_iota(jnp.int32, (CHUNK, CHUNK), 0))[None]


def _gla_fwd(proj, w2p, b2, norm_g, name):
    T = proj.shape[0]
    R = min(T, ROWS)
    n = R // CHUNK

    def body(q_ref, k_ref, v_ref, g_ref, a_ref, w2_ref, b2_ref, ng_ref, y_ref, o_ref, st_ref, s_ref):
        @pl.when(pl.program_id(0) == 0)
        def _():
            s_ref[...] = jnp.zeros_like(s_ref)

        _, la3, _, _, _, qi, ki, kd = _gla_common(q_ref[...], k_ref[...], a_ref[...], w2_ref[...], b2_ref[...], R)
        tril = _tril()
        ones = jnp.ones((n, CHUNK, DV), BF16)
        for h in range(HEADS):
            sl = slice(h * DK, (h + 1) * DK)
            sv = slice(h * DV, (h + 1) * DV)
            qh = _bsplit(qi[:, sl], n).astype(MXU_DTYPE)
            kh = _bsplit(ki[:, sl], n).astype(MXU_DTYPE)
            kdh = _bsplit(kd[:, sl], n).astype(MXU_DTYPE)
            vh = _bsplit(v_ref[:, sv], n).astype(MXU_DTYPE)
            att = jnp.where(tril, _bdot('ncd,nsd->ncs', qh, kh), 0.0)
            upd = _bdot('ncd,nce->nde', kdh, vh)
            dect = jnp.exp(sum(_bdot('ncd,nce->nde', _bsplit(p[:, sl], n), ones) for p in la3))
            s = s_ref[sl, :]
            for c in range(n):
                st_ref[c, sl, :] = s
                s = dect[c] * s + upd[c]
            s_ref[sl, :] = s
            sp = st_ref[:, sl, :].astype(MXU_DTYPE)
            o = (_bdot('ncs,nse->nce', att.astype(MXU_DTYPE), vh) + _bdot('ncd,nde->nce', qh, sp)).reshape(R, DV)
            o_ref[:, sv] = o
            r = lax.rsqrt(jnp.mean(o * o, axis=-1, keepdims=True) + EPS)
            gate = g_ref[:, sv]
            y_ref[:, sv] = (((o * r) * ng_ref[...]) * (gate * jax.nn.sigmoid(gate))).astype(y_ref.dtype)

    cb = lambda w, j: pl.BlockSpec((R, w), lambda i: (i, j))
    full = lambda s: pl.BlockSpec(s, lambda i: (0,) * len(s))
    return pl.pallas_call(
        body, name=name, grid=(T // R,),
        in_specs=[cb(QK_W, 0), cb(QK_W, 1), cb(GLA_W, 1), cb(GLA_W, 2), cb(GATE_PAD, 12),
                  full((GATE_PAD, QK_W)), full((1, QK_W)), full((1, DV))],
        out_specs=[pl.BlockSpec((R, GLA_W), lambda i: (i, 0)), pl.BlockSpec((R, GLA_W), lambda i: (i, 0)),
                   pl.BlockSpec((n, QK_W, DV), lambda i: (i, 0, 0))],
        out_shape=[jax.ShapeDtypeStruct((T, GLA_W + LRU_W), BF16), jax.ShapeDtypeStruct((T, GLA_W), F32),
                   jax.ShapeDtypeStruct((T // CHUNK, QK_W, DV), F32)],
        scratch_shapes=[pltpu.VMEM((QK_W, DV), F32)],
        compiler_params=_cp("arbitrary"),
    )(proj, proj, proj, proj, proj, w2p, b2.reshape(1, QK_W), norm_g.reshape(1, DV))


def _gla_bwd(dy, proj, o_st, s_st, w2p, b2, norm_g, name):
    T = proj.shape[0]
    R = min(T, ROWS)
    n = R // CHUNK
    nb = T // R

    def body(dy_ref, q_ref, k_ref, v_ref, g_ref, a_ref, o_ref, st_ref, w2_ref, b2_ref, ng_ref,
             dp_ref, dw2_ref, db2_ref, dng_ref, gs_ref, gn_ref, db_ref, dbl_ref):
        @pl.when(pl.program_id(0) == 0)
        def _():
            gs_ref[...] = jnp.zeros_like(gs_ref)
            dw2_ref[...] = jnp.zeros_like(dw2_ref)
            db2_ref[...] = jnp.zeros_like(db2_ref)
            dng_ref[...] = jnp.zeros_like(dng_ref)

        glr = a_ref[...]
        gl, la3, eb, enb, ek, qi, ki, kd = _gla_common(q_ref[...], k_ref[...], glr, w2_ref[...], b2_ref[...], R)
        tril = _tril()
        ones = jnp.ones((n, CHUNK, DV), BF16)
        ng = ng_ref[...]
        dng = jnp.zeros((1, DV), F32)
        for h in range(HEADS):
            sl = slice(h * DK, (h + 1) * DK)
            sv = slice(h * DV, (h + 1) * DV)
            o = o_ref[:, sv]
            r = lax.rsqrt(jnp.mean(o * o, axis=-1, keepdims=True) + EPS)
            xhat = o * r
            gate = g_ref[:, sv]
            sg = jax.nn.sigmoid(gate)
            dyh = dy_ref[:, sv].astype(F32)
            dp_ref[:, 2 * QK_W + GLA_W + h * DV:2 * QK_W + GLA_W + (h + 1) * DV] = (
                dyh * (xhat * ng) * (sg * (1.0 + gate * (1.0 - sg)))).astype(dp_ref.dtype)
            don = dyh * (gate * sg)
            dng = dng + jnp.sum(don * xhat, axis=0, keepdims=True)
            dxhat = don * ng
            do = r * (dxhat - xhat * jnp.mean(dxhat * xhat, axis=-1, keepdims=True))
            qf = _bsplit(qi[:, sl], n)
            kf = _bsplit(ki[:, sl], n)
            kdf = _bsplit(kd[:, sl], n)
            qh, kh, kdh = qf.astype(MXU_DTYPE), kf.astype(MXU_DTYPE), kdf.astype(MXU_DTYPE)
            vh = _bsplit(v_ref[:, sv], n).astype(MXU_DTYPE)
            doh = _bsplit(do, n).astype(MXU_DTYPE)
            spf = st_ref[:, sl, :]
            sp = spf.astype(MXU_DTYPE)
            att = jnp.where(tril, _bdot('ncd,nsd->ncs', qh, kh), 0.0).astype(MXU_DTYPE)
            datt = jnp.where(tril, _bdot('nce,nse->ncs', doh, vh), 0.0).astype(MXU_DTYPE)
            dv = _bdot('ncs,nce->nse', att, doh)
            dqi = _bdot('ncs,nsd->ncd', datt, kh) + _bdot('nce,nde->ncd', doh, sp)
            dki = _bdot('ncs,ncd->nsd', datt, qh)
            wgt = _bdot('ncd,nce->nde', qh, doh)
            dect = jnp.exp(sum(_bdot('ncd,nce->nde', _bsplit(p[:, sl], n), ones) for p in la3))
            g = gs_ref[sl, :]
            for c in reversed(range(n)):
                gn_ref[c] = g
                g = wgt[c] + dect[c] * g
            gs_ref[sl, :] = g
            gnf = gn_ref[...]
            gn = gnf.astype(MXU_DTYPE)
            dkd = _bdot('nce,nde->ncd', vh, gn)
            dv = dv + _bdot('ncd,nde->nce', kdh, gn)
            dp_ref[:, 2 * QK_W + h * DV:2 * QK_W + (h + 1) * DV] = dv.reshape(R, DV).astype(dp_ref.dtype)
            dbl = sum(_bdot('nce,nde->ncd', ones, p) for p in _split3(gnf * spf * dect))
            pk = dkd * kdf
            dbl = dbl + jnp.sum(pk, axis=1, keepdims=True)
            dbl_ref[:, sl] = dbl.reshape(R, DK)
            db_ref[:, sl] = (dqi * qf - dki * kf - pk).reshape(R, DK)
            dp_ref[:, sl] = ((dqi.reshape(R, DK) * (DK ** -0.5)) * eb[:, sl]).astype(dp_ref.dtype)
            dp_ref[:, QK_W + h * DK:QK_W + (h + 1) * DK] = (
                dki.reshape(R, DK) * enb[:, sl] + dkd.reshape(R, DK) * ek[:, sl]).astype(dp_ref.dtype)
        dng_ref[...] += dng
        row = lax.broadcasted_iota(jnp.int32, (R, R), 0)
        col = lax.broadcasted_iota(jnp.int32, (R, R), 1)
        m_rev = (_same_chunk(row, col) & (col >= row)).astype(BF16)
        dla = sum(jnp.dot(m_rev, p, preferred_element_type=F32) for p in _split3(db_ref[...])) + dbl_ref[...]
        dgl = (dla * (1.0 / 16.0)) * jax.nn.sigmoid(-gl)
        dp_ref[:, 2 * QK_W + 2 * GLA_W:GLA_COLS] = _dot_nt(dgl, w2_ref[...]).astype(dp_ref.dtype)
        dw2_ref[...] += _dot_tn(glr, dgl)
        db2_ref[...] += jnp.sum(dgl, axis=0, keepdims=True)

    cb = lambda w, j: pl.BlockSpec((R, w), lambda i: (nb - 1 - i, j))
    full = lambda s: pl.BlockSpec(s, lambda i: (0,) * len(s))
    return pl.pallas_call(
        body, name=name, grid=(nb,),
        in_specs=[cb(GLA_W, 0), cb(QK_W, 0), cb(QK_W, 1), cb(GLA_W, 1), cb(GLA_W, 2), cb(GATE_PAD, 12),
                  cb(GLA_W, 0), pl.BlockSpec((n, QK_W, DV), lambda i: (nb - 1 - i, 0, 0)),
                  full((GATE_PAD, QK_W)), full((1, QK_W)), full((1, DV))],
        out_specs=[pl.BlockSpec((R, GLA_COLS), lambda i: (nb - 1 - i, 0)),
                   full((GATE_PAD, QK_W)), full((1, QK_W)), full((1, DV))],
        out_shape=[jax.ShapeDtypeStruct((T, GLA_COLS), BF16), jax.ShapeDtypeStruct((GATE_PAD, QK_W), F32),
                   jax.ShapeDtypeStruct((1, QK_W), F32), jax.ShapeDtypeStruct((1, DV), F32)],
        scratch_shapes=[pltpu.VMEM((QK_W, DV), F32), pltpu.VMEM((n, DK, DV), F32),
                        pltpu.VMEM((R, QK_W), F32), pltpu.VMEM((R, QK_W), F32)],
        compiler_params=_cp("arbitrary"),
    )(dy, proj, proj, proj, proj, proj, o_st, s_st, w2p, b2.reshape(1, QK_W), norm_g.reshape(1, DV))


def _scan_scratch(R, W):
    return [pltpu.VMEM((W // LANES, R, LANES), F32), pltpu.VMEM((W // LANES, R, LANES), F32),
            pltpu.VMEM((W // LANES, R // SUBLANES, LANES), F32)]


def _scan_rows(a, u, c0, a_ref, u_ref, c_ref, out_ref, reverse):
    R, W = a.shape
    nt = R // SUBLANES
    shift = _shift_up if reverse else _shift_down
    a, u = a.reshape(nt, SUBLANES, W), u.reshape(nt, SUBLANES, W)
    sub = lax.broadcasted_iota(jnp.int32, (nt, SUBLANES, W), 1)
    for k in (1, 2, 4):
        inside = (sub < SUBLANES - k) if reverse else (sub >= k)
        turn = SUBLANES - k if reverse else k
        u = u + a * jnp.where(inside, pltpu.roll(u, turn, axis=1), 0.0)
        a = a * jnp.where(inside, pltpu.roll(a, turn, axis=1), 1.0)
    a, u = a.reshape(R, W), u.reshape(R, W)
    end = 0 if reverse else SUBLANES - 1
    edge = nt - 1 if reverse else 0
    for j in range(W // LANES):
        cols = slice(j * LANES, (j + 1) * LANES)
        a_ref[j] = a[:, cols]
        u_ref[j] = u[:, cols]
        at = a_ref.at[j][pl.ds(end, nt, stride=SUBLANES), :]
        ut = u_ref.at[j][pl.ds(end, nt, stride=SUBLANES), :]
        k = 1
        while k < nt:
            ut = ut + at * shift(ut, k, 0.0)
            at = at * shift(at, k, 1.0)
            k *= 2
        c_ref[j] = shift(ut + at * c0[:, cols], 1, 0.0)
        c_ref[j, edge:edge + 1, :] = c0[:, cols]
        for r in range(nt):
            rows = pl.ds(r * SUBLANES, SUBLANES)
            out_ref[rows, cols] = u_ref[j, rows, :] + a_ref[j, rows, :] * c_ref[j, r:r + 1, :]


def _lru_conv(ext_ref, cw_ref, cb_ref, R):
    xc = cb_ref[...] + ext_ref[pl.ds(SUBLANES - 3, R), :] * cw_ref[0:1, :]
    xc = xc + ext_ref[pl.ds(SUBLANES - 2, R), :] * cw_ref[1:2, :]
    xc = xc + ext_ref[pl.ds(SUBLANES - 1, R), :] * cw_ref[2:3, :]
    return xc + ext_ref[pl.ds(SUBLANES, R), :] * cw_ref[3:4, :]


def _lru_gates(xc, wa, ba, wx, bx, lam, first):
    r = jax.nn.sigmoid(_dot(xc, wa) + ba)
    ig = jax.nn.sigmoid(_dot(xc, wx) + bx)
    sp = jax.nn.softplus(-lam)
    la = (-LRU_C * r) * sp
    a = jnp.exp(la)
    mult = jnp.where(first, 1.0, jnp.sqrt(-_expm1(2.0 * la)))
    return r, ig, sp, a, mult


def _lru_fwd(proj, y_mix, cw, cb, wa, ba, wx, bx, lam, name):
    T = proj.shape[0]
    R = min(T, ROWS)
    W = LRU_W

    def body(xr_ref, xh_ref, xg_ref, cw_ref, cb_ref, wa_ref, ba_ref, wx_ref, bx_ref, lam_ref, mix_ref,
             y_ref, hs_ref, ext_ref, hc_ref, sa_ref, su_ref, sc_ref):
        del mix_ref
        i = pl.program_id(0)

        @pl.when(i == 0)
        def _():
            hc_ref[...] = jnp.zeros_like(hc_ref)

        ext_ref[0:SUBLANES, :] = jnp.where(i > 0, xh_ref[...], 0.0)
        ext_ref[pl.ds(SUBLANES, R), :] = xr_ref[...]
        xc = _lru_conv(ext_ref, cw_ref, cb_ref, R)
        row = lax.broadcasted_iota(jnp.int32, (R, W), 0)
        first = (row == 0) & (i == 0)
        _, ig, _, a, mult = _lru_gates(xc, wa_ref[...], ba_ref[...], wx_ref[...], bx_ref[...], lam_ref[...], first)
        _scan_rows(a, mult * (ig * xc), hc_ref[0:1, :], sa_ref, su_ref, sc_ref, hs_ref, reverse=False)
        hc_ref[0:1, :] = hs_ref[R - 1:R, :]
        y_ref[...] = (hs_ref[...] * _gelu(xg_ref[...])).astype(y_ref.dtype)

    rb = R // SUBLANES
    full = lambda s: pl.BlockSpec(s, lambda i: (0,) * len(s))
    return pl.pallas_call(
        body, name=name, grid=(T // R,),
        in_specs=[pl.BlockSpec((R, W), lambda i: (i, 0)),
                  pl.BlockSpec((SUBLANES, W), lambda i: (jnp.maximum(i * rb - 1, 0), 0)),
                  pl.BlockSpec((R, W), lambda i: (i, 1)),
                  full((SUBLANES, W)), full((1, W)), full((W, W)), full((1, W)), full((W, W)), full((1, W)),
                  full((1, W)), pl.BlockSpec(memory_space=pl.ANY)],
        out_specs=[pl.BlockSpec((R, W), lambda i: (i, 1)), pl.BlockSpec((R, W), lambda i: (i, 0))],
        out_shape=[jax.ShapeDtypeStruct(y_mix.shape, y_mix.dtype), jax.ShapeDtypeStruct((T, W), F32)],
        scratch_shapes=[pltpu.VMEM((R + SUBLANES, W), F32), pltpu.VMEM((SUBLANES, W), F32),
                        *_scan_scratch(R, W)],
        input_output_aliases={10: 0}, compiler_params=_cp("arbitrary"),
    )(proj, proj, proj, cw, cb, wa, ba, wx, bx, lam, y_mix)


def _lru_bwd(dy, proj, hs, cw, cb, wa, ba, wx, bx, lam, name):
    T = proj.shape[0]
    R = min(T, ROWS)
    W = LRU_W
    nb = T // R

    def body(dy_ref, xr_ref, xh_ref, xg_ref, hs_ref, hh_ref, cw_ref, cb_ref, wa_ref, ba_ref, wx_ref, bx_ref, lam_ref,
             dp_ref, dcw_ref, dvec_ref, dwa_ref, dwx_ref, ext_ref, ext2_ref, lc_ref, sa_ref, su_ref, sc_ref, adj_ref):
        ib = pl.program_id(0)
        i = nb - 1 - ib

        @pl.when(ib == 0)
        def _():
            lc_ref[...] = jnp.zeros_like(lc_ref)
            ext2_ref[pl.ds(R, SUBLANES), :] = jnp.zeros((SUBLANES, W), F32)
            dcw_ref[...] = jnp.zeros_like(dcw_ref)
            dvec_ref[...] = jnp.zeros_like(dvec_ref)
            dwa_ref[...] = jnp.zeros_like(dwa_ref)
            dwx_ref[...] = jnp.zeros_like(dwx_ref)

        ext_ref[0:SUBLANES, :] = jnp.where(i > 0, xh_ref[...], 0.0)
        ext_ref[pl.ds(SUBLANES, R), :] = xr_ref[...]
        xc = _lru_conv(ext_ref, cw_ref, cb_ref, R)
        row = lax.broadcasted_iota(jnp.int32, (R, W), 0)
        first = (row == 0) & (i == 0)
        lam = lam_ref[...]
        r, ig, sp, a, mult = _lru_gates(xc, wa_ref[...], ba_ref[...], wx_ref[...], bx_ref[...], lam, first)
        h = hs_ref[...]
        gel, dgel = _gelu_and_grad(xg_ref[...])
        dy = dy_ref[...].astype(F32)
        dp_ref[:, W:2 * W] = (dy * h * dgel).astype(dp_ref.dtype)
        _scan_rows(_shift_up(a, 1, 1.0), dy * gel, lc_ref[0:1, :], sa_ref, su_ref, sc_ref, adj_ref, reverse=True)
        v = adj_ref[...]
        lc_ref[...] = (a * v)[0:SUBLANES, :]
        hprev = _shift_down(h, 1, 0.0) + jnp.where((row == 0) & (i > 0), hh_ref[SUBLANES - 1:SUBLANES, :], 0.0)
        da = v * hprev
        dmult = jnp.where(first, 0.0, v * (ig * xc))
        dig = v * (mult * xc)
        dxc = v * (mult * ig)
        dla = da * a - dmult * ((a * a) / mult)
        dra = (dla * (-LRU_C * sp)) * (r * (1.0 - r))
        drx = dig * (ig * (1.0 - ig))
        dxc = dxc + _dot_nt(dra, wa_ref[...]) + _dot_nt(drx, wx_ref[...])
        dwa_ref[...] += _dot_tn(xc, dra)
        dwx_ref[...] += _dot_tn(xc, drx)
        dvec_ref[0:1, :] += jnp.sum(dxc, axis=0, keepdims=True)
        dvec_ref[1:2, :] += jnp.sum(dra, axis=0, keepdims=True)
        dvec_ref[2:3, :] += jnp.sum(drx, axis=0, keepdims=True)
        dvec_ref[3:4, :] += jnp.sum(dla * (-LRU_C * r), axis=0, keepdims=True) * (-jax.nn.sigmoid(-lam))
        ext2_ref[pl.ds(0, R), :] = dxc
        dxr = ext2_ref[pl.ds(0, R), :] * cw_ref[3:4, :]
        dxr = dxr + ext2_ref[pl.ds(1, R), :] * cw_ref[2:3, :]
        dxr = dxr + ext2_ref[pl.ds(2, R), :] * cw_ref[1:2, :]
        dxr = dxr + ext2_ref[pl.ds(3, R), :] * cw_ref[0:1, :]
        dp_ref[:, 0:W] = dxr.astype(dp_ref.dtype)
        for j in range(LRU_CONV):
            dcw_ref[j:j + 1, :] += jnp.sum(dxc * ext_ref[pl.ds(SUBLANES - 3 + j, R), :], axis=0, keepdims=True)
        ext2_ref[pl.ds(R, SUBLANES), :] = dxc[0:SUBLANES, :]

    rb = R // SUBLANES
    full = lambda s: pl.BlockSpec(s, lambda i: (0,) * len(s))
    blk = lambda j: pl.BlockSpec((R, W), lambda i: (nb - 1 - i, j))
    halo = pl.BlockSpec((SUBLANES, W), lambda i: (jnp.maximum((nb - 1 - i) * rb - 1, 0), 0))
    return pl.pallas_call(
        body, name=name, grid=(nb,),
        in_specs=[blk(1), blk(0), halo, blk(1), blk(0), halo,
                  full((SUBLANES, W)), full((1, W)), full((W, W)), full((1, W)), full((W, W)), full((1, W)),
                  full((1, W))],
        out_specs=[pl.BlockSpec((R, 2 * W), lambda i: (nb - 1 - i, 0)), full((SUBLANES, W)), full((SUBLANES, W)),
                   full((W, W)), full((W, W))],
        out_shape=[jax.ShapeDtypeStruct((T, 2 * W), BF16), jax.ShapeDtypeStruct((SUBLANES, W), F32),
                   jax.ShapeDtypeStruct((SUBLANES, W), F32), jax.ShapeDtypeStruct((W, W), F32),
                   jax.ShapeDtypeStruct((W, W), F32)],
        scratch_shapes=[pltpu.VMEM((R + SUBLANES, W), F32), pltpu.VMEM((R + SUBLANES, W), F32),
                        pltpu.VMEM((SUBLANES, W), F32), *_scan_scratch(R, W), pltpu.VMEM((R, W), F32)],
        compiler_params=_cp("arbitrary"),
    )(dy, proj, proj, proj, hs, hs, cw, cb, wa, ba, wx, bx, lam)


def _conv3_window(src, start, cs, w, b):
    win = src[pl.ds(start, FFN_RC + SUBLANES), cs]
    x2 = pltpu.roll(win, 2, axis=0)[SUBLANES:]
    x1 = pltpu.roll(win, 1, axis=0)[SUBLANES:]
    x0 = win[SUBLANES:]
    return ((b + x2 * w[0]) + x1 * w[1]) + x0 * w[2], (x2, x1, x0)


def _ffn_up_fwd(u2, fa, fg, cwa, cwg, cba, cbg, name):
    T, D = u2.shape
    Fh = fa.shape[1]
    tm = min(T, ROWS)
    CW, SB, RC = FFN_CW, FFN_SB, FFN_RC
    ns = CW // SB

    def body(u_ref, fa_ref, fg_ref, cwa_ref, cwg_ref, cba_ref, cbg_ref, xa_ref, xg_ref, p_ref, q_ref, act_ref,
             ka_ref, kg_ref, ea_ref, eg_ref, za_ref, zg_ref):
        @pl.when(pl.program_id(1) == 0)
        def _():
            ka_ref[...] = jnp.zeros_like(ka_ref)
            kg_ref[...] = jnp.zeros_like(kg_ref)

        def gate(s):
            cs = pl.ds(s * SB, SB)
            wa = [cwa_ref[j:j + 1, cs] for j in range(FFN_CONV)]
            wg = [cwg_ref[j:j + 1, cs] for j in range(FFN_CONV)]
            ba, bg = cba_ref[:, cs], cbg_ref[:, cs]
            ea_ref[0:SUBLANES, cs] = ka_ref[:, cs]
            ea_ref[pl.ds(SUBLANES, RC), cs] = za_ref[0:RC, cs]
            eg_ref[0:SUBLANES, cs] = kg_ref[:, cs]
            eg_ref[pl.ds(SUBLANES, RC), cs] = zg_ref[0:RC, cs]
            for c in range(tm // RC):
                sa, sg, start = (ea_ref, eg_ref, 0) if c == 0 else (za_ref, zg_ref, c * RC - SUBLANES)
                rows = pl.ds(c * RC, RC)
                a_c, xa = _conv3_window(sa, start, cs, wa, ba)
                g_c, xg = _conv3_window(sg, start, cs, wg, bg)
                gel, dgel = _gelu_and_grad(a_c)
                xa_ref[rows, cs] = xa[2].astype(xa_ref.dtype)
                xg_ref[rows, cs] = xg[2].astype(xg_ref.dtype)
                p_ref[rows, cs] = (g_c * dgel).astype(p_ref.dtype)
                q_ref[rows, cs] = gel.astype(q_ref.dtype)
                act_ref[rows, cs] = (gel * g_c).astype(act_ref.dtype)
            ka_ref[:, cs] = za_ref[tm - SUBLANES:tm, cs]
            kg_ref[:, cs] = zg_ref[tm - SUBLANES:tm, cs]

        za_ref[...] = _dot(u_ref[...], fa_ref[...])
        zg_ref[...] = _dot(u_ref[...], fg_ref[...])
        for s in range(ns):
            gate(s)

    blk = pl.BlockSpec((tm, CW), lambda j, i: (i, j))
    wblk = pl.BlockSpec((D, CW), lambda j, i: (0, j))
    w8 = pl.BlockSpec((SUBLANES, CW), lambda j, i: (0, j))
    w1 = pl.BlockSpec((1, CW), lambda j, i: (0, j))
    return pl.pallas_call(
        body, name=name, grid=(Fh // CW, T // tm),
        in_specs=[pl.BlockSpec((tm, D), lambda j, i: (i, 0)), wblk, wblk, w8, w8, w1, w1],
        out_specs=[blk] * 5,
        out_shape=[jax.ShapeDtypeStruct((T, Fh), BF16)] * 5,
        scratch_shapes=[pltpu.VMEM((SUBLANES, CW), F32), pltpu.VMEM((SUBLANES, CW), F32),
                        pltpu.VMEM((RC + SUBLANES, CW), F32), pltpu.VMEM((RC + SUBLANES, CW), F32),
                        pltpu.VMEM((tm, CW), F32), pltpu.VMEM((tm, CW), F32)],
        compiler_params=_cp("parallel", "arbitrary"),
    )(u2, fa, fg, cwa, cwg, cba, cbg)


def _ffn_bwd_core(dh, dhb, xa, xg, p, q, wdT, faT, fgT, cwa, cwg, h, g, name):
    T, D = dhb.shape
    Fh = xa.shape[1]
    tm = min(T, ROWS)
    CW, SB, RC = FFN_CW, FFN_SB, FFN_RC
    ns = CW // SB
    nj = Fh // CW
    nb = T // tm
    nc = tm // RC

    def body(dh_ref, xa_ref, xg_ref, p_ref, q_ref, wd_ref, fa_ref, fg_ref, cwa_ref, cwg_ref, h_ref, g_ref, res_ref,
             dza_ref, dzg_ref, dca_ref, dcg_ref, dho_ref, dhbo_ref, dg_ref,
             d_ref, sa_ref, sg_ref, ka_ref, kg_ref, du_ref):
        ib, j = pl.program_id(0), pl.program_id(1)

        @pl.when((ib == 0) & (j == 0))
        def _():
            dca_ref[...] = jnp.zeros_like(dca_ref)
            dcg_ref[...] = jnp.zeros_like(dcg_ref)
            dg_ref[...] = jnp.zeros_like(dg_ref)

        @pl.when(ib == 0)
        def _():
            ka_ref[j] = jnp.zeros((SUBLANES, CW), F32)
            kg_ref[j] = jnp.zeros((SUBLANES, CW), F32)

        @pl.when(j == 0)
        def _():
            du_ref[...] = jnp.zeros_like(du_ref)

        def fold(v):
            return jnp.sum(v.reshape(RC // SUBLANES, SUBLANES, SB), axis=0)

        def gate(s):
            cs = pl.ds(s * SB, SB)
            wa = [cwa_ref[t:t + 1, cs] for t in range(FFN_CONV)]
            wg = [cwg_ref[t:t + 1, cs] for t in range(FFN_CONV)]
            sa_ref[pl.ds(tm, SUBLANES), cs] = ka_ref[j, :, cs]
            sg_ref[pl.ds(tm, SUBLANES), cs] = kg_ref[j, :, cs]
            for c in range(nc):
                rows = pl.ds(c * RC, RC)
                dact = d_ref[rows, cs]
                sa_ref[rows, cs] = dact * p_ref[rows, cs].astype(F32)
                sg_ref[rows, cs] = dact * q_ref[rows, cs].astype(F32)
            n = RC + SUBLANES
            for s_ref, x_ref, o_ref, dc_ref, w in ((sa_ref, xa_ref, dza_ref, dca_ref, wa),
                                                   (sg_ref, xg_ref, dzg_ref, dcg_ref, wg)):
                acc = [jnp.zeros((SUBLANES, SB), F32) for _ in range(FFN_CONV + 1)]
                for c in range(nc):
                    rows = pl.ds(c * RC, RC)
                    win = s_ref[pl.ds(c * RC, n), cs]
                    d0 = win[:RC]
                    d1 = pltpu.roll(win, n - 1, axis=0)[:RC]
                    d2 = pltpu.roll(win, n - 2, axis=0)[:RC]
                    o_ref[rows, cs] = ((d0 * w[2] + d1 * w[1]) + d2 * w[0]).astype(o_ref.dtype)
                    x = x_ref[rows, cs].astype(F32)
                    acc = [acc[0] + fold(d2 * x), acc[1] + fold(d1 * x), acc[2] + fold(d0 * x), acc[3] + fold(d0)]
                for t in range(FFN_CONV + 1):
                    dc_ref[j, t:t + 1, cs] += jnp.sum(acc[t], axis=0, keepdims=True)
            ka_ref[j, :, cs] = sa_ref[0:SUBLANES, cs]
            kg_ref[j, :, cs] = sg_ref[0:SUBLANES, cs]

        d_ref[...] = _dot(dh_ref[...], wd_ref[...])
        for s in range(ns):
            gate(s)
        du_ref[...] += _dot(dza_ref[...], fa_ref[...]) + _dot(dzg_ref[...], fg_ref[...])

        @pl.when(j == nj - 1)
        def _():
            dho, dg = _rms_bwd_tile(h_ref[...], g_ref[...], du_ref[...], res_ref[...])
            dg_ref[...] += dg
            dho_ref[...] = dho
            dhbo_ref[...] = dho.astype(dhbo_ref.dtype)

    blk = pl.BlockSpec((tm, CW), lambda ib, j: (nb - 1 - ib, j))
    row = pl.BlockSpec((tm, D), lambda ib, j: (nb - 1 - ib, 0))
    vec = pl.BlockSpec((1, D), lambda ib, j: (0, 0))
    w8 = pl.BlockSpec((SUBLANES, CW), lambda ib, j: (0, j))
    wrow = pl.BlockSpec((CW, D), lambda ib, j: (j, 0))
    acc = pl.BlockSpec((nj, SUBLANES, CW), lambda ib, j: (0, 0, 0))
    return pl.pallas_call(
        body, name=name, grid=(nb, nj),
        in_specs=[row, blk, blk, blk, blk, pl.BlockSpec((D, CW), lambda ib, j: (0, j)), wrow, wrow, w8, w8,
                  row, vec, row],
        out_specs=[blk, blk, acc, acc, row, row, vec],
        out_shape=[jax.ShapeDtypeStruct((T, Fh), BF16), jax.ShapeDtypeStruct((T, Fh), BF16),
                   jax.ShapeDtypeStruct((nj, SUBLANES, CW), F32), jax.ShapeDtypeStruct((nj, SUBLANES, CW), F32),
                   jax.ShapeDtypeStruct((T, D), F32), jax.ShapeDtypeStruct((T, D), BF16),
                   jax.ShapeDtypeStruct((1, D), F32)],
        scratch_shapes=[pltpu.VMEM((tm, CW), F32), pltpu.VMEM((tm + SUBLANES, CW), F32),
                        pltpu.VMEM((tm + SUBLANES, CW), F32), pltpu.VMEM((nj, SUBLANES, CW), F32),
                        pltpu.VMEM((nj, SUBLANES, CW), F32), pltpu.VMEM((tm, D), F32)],
        compiler_params=_cp("arbitrary", "arbitrary"),
    )(dhb, xa, xg, p, q, wdT, faT, fgT, cwa, cwg, h, g.reshape(1, D), dh)


def _adamw(w, grads, m, v, name):
    rows, cols = w.shape
    tr = _row_tile(rows, max(SUBLANES, min(512, TILE_BYTES // (4 * cols)) // SUBLANES * SUBLANES))
    ng = len(grads)

    def body(*refs):
        w_ref, g_refs, m_ref, v_ref = refs[0], refs[1:1 + ng], refs[1 + ng], refs[2 + ng]
        go_ref, d_ref, mo_ref, vo_ref = refs[3 + ng:]
        g = g_refs[0][...]
        for r in g_refs[1:]:
            g = g + r[...]
        mm = ADAM_B1 * m_ref[...] + (1.0 - ADAM_B1) * g
        vv = ADAM_B2 * v_ref[...] + (1.0 - ADAM_B2) * (g * g)
        m_hat = mm / (1.0 - ADAM_B1 ** ADAM_STEP)
        v_hat = vv / (1.0 - ADAM_B2 ** ADAM_STEP)
        go_ref[...] = g
        d_ref[...] = -ADAM_LR * (m_hat / (jnp.sqrt(v_hat) + ADAM_EPS) + ADAM_WD * w_ref[...])
        mo_ref[...] = mm
        vo_ref[...] = vv

    blk = pl.BlockSpec((tr, cols), lambda i: (i, 0))
    return pl.pallas_call(
        body, name=name, grid=(rows // tr,), in_specs=[blk] * (3 + ng), out_specs=[blk] * 4,
        out_shape=[jax.ShapeDtypeStruct((rows, cols), F32)] * 4, compiler_params=_cp("parallel"),
    )(w, *grads, m, v)


def _add_slabs(a, b, out_dtype, name):
    n, rows, cols = a.shape
    tr = _row_tile(rows, max(SUBLANES, min(512, TILE_BYTES // (4 * cols)) // SUBLANES * SUBLANES))

    def body(a_ref, b_ref, o_ref):
        o_ref[...] = (a_ref[...] + b_ref[...]).astype(o_ref.dtype)

    blk = pl.BlockSpec((1, tr, cols), lambda k, i: (k, i, 0))
    return pl.pallas_call(
        body, name=name, grid=(n, rows // tr), in_specs=[blk, blk], out_specs=blk,
        out_shape=jax.ShapeDtypeStruct((n, rows, cols), out_dtype), compiler_params=_cp("parallel", "parallel"),
    )(a, b)


def _sum_leading(parts, name, last=None):
    n, rows, cols = parts.shape
    tr = _row_tile(rows, max(SUBLANES, min(512, TILE_BYTES // (4 * cols)) // SUBLANES * SUBLANES))

    def body(*refs):
        p_ref, o_ref = refs[0], refs[-1]
        acc = p_ref[0].astype(F32)
        for d in range(1, n):
            acc = acc + p_ref[d].astype(F32)
        if last is not None:
            acc = acc + refs[1][...].astype(F32)
        o_ref[...] = acc

    blk = pl.BlockSpec((tr, cols), lambda i: (i, 0))
    return pl.pallas_call(
        body, name=name, grid=(rows // tr,),
        in_specs=[pl.BlockSpec((n, tr, cols), lambda i: (0, i, 0))] + ([] if last is None else [blk]), out_specs=blk,
        out_shape=jax.ShapeDtypeStruct((rows, cols), F32), compiler_params=_cp("parallel"),
    )(*((parts,) if last is None else (parts, last)))


def _row_tile(rows, cap=512):
    if rows <= cap:
        return rows
    return max(t for t in range(SUBLANES, cap + 1, SUBLANES) if rows % t == 0)


def _place():
    return lax.axis_index("x"), lax.axis_index("y"), lax.axis_index("c")


def _gather_shards(arrs, name):
    n = len(arrs)

    def body(*refs):
        ins, outs = refs[:n], refs[n:2 * n]
        send_sems, recv_sems, pass_send, pass_recv = refs[2 * n:]
        x, y, c = _place()
        chips = [(1 - x, y), (x, 1 - y), (1 - x, 1 - y)]
        mine, theirs = c, 1 - c

        def half(a, which):
            hl = ins[a].shape[0] // 2
            return pl.ds(which * hl, hl)

        def send(a, j, shard):
            px, py = chips[j]
            return pltpu.make_async_remote_copy(
                src_ref=ins[a].at[half(a, mine)], dst_ref=outs[a].at[shard, half(a, mine)],
                send_sem=send_sems.at[3 * a + j], recv_sem=recv_sems.at[3 * a + j], device_id=(px, py, c),
                device_id_type=MESH)

        def passed(a, j, which):
            px, py = chips[j]
            blk = outs[a].at[2 * px + py, half(a, which)]
            return pltpu.make_async_remote_copy(
                src_ref=blk, dst_ref=blk, send_sem=pass_send.at[3 * a + j], recv_sem=pass_recv.at[3 * a + j],
                device_id=(x, y, 1 - c), device_id_type=MESH)

        sends = [send(a, j, 2 * x + y) for a in range(n) for j in range(3)]
        for cp in sends:
            cp.start()
        passes = []
        for a in range(n):
            for j, (px, py) in enumerate(chips):
                send(a, j, 2 * px + py).wait_recv()
                passes.append(passed(a, j, mine))
                passes[-1].start()
        for a in range(n):
            for j in range(3):
                passed(a, j, theirs).wait_recv()
        for cp in sends + passes:
            cp.wait_send()

    hbm = pl.BlockSpec(memory_space=pl.ANY)
    return pl.pallas_call(
        body, name=name, in_specs=[hbm] * n, out_specs=[hbm] * n,
        out_shape=[jax.ShapeDtypeStruct((N_CHIPS,) + a.shape, a.dtype) for a in arrs],
        scratch_shapes=[pltpu.SemaphoreType.DMA((3 * n,)), pltpu.SemaphoreType.DMA((3 * n,)),
                        pltpu.SemaphoreType.DMA((3 * n,)), pltpu.SemaphoreType.DMA((3 * n,))],
        compiler_params=pltpu.CompilerParams(has_side_effects=True),
    )(*arrs)


def _other_half_to_sibling(slabs, name):
    n = len(slabs)

    def body(*refs):
        ins, got = refs[:n], refs[n:2 * n]
        send_sems, recv_sems = refs[2 * n:]
        x, y, c = _place()
        copies = [pltpu.make_async_remote_copy(
            src_ref=ins[a].at[:, 1 - c], dst_ref=got[a], send_sem=send_sems.at[a], recv_sem=recv_sems.at[a],
            device_id=(x, y, 1 - c), device_id_type=MESH) for a in range(n)]
        for cp in copies:
            cp.start()
        for cp in copies:
            cp.wait()

    hbm = pl.BlockSpec(memory_space=pl.ANY)
    return pl.pallas_call(
        body, name=name, in_specs=[hbm] * n, out_specs=[hbm] * n,
        out_shape=[jax.ShapeDtypeStruct((s.shape[0],) + s.shape[2:], s.dtype) for s in slabs],
        scratch_shapes=[pltpu.SemaphoreType.DMA((n,)), pltpu.SemaphoreType.DMA((n,))],
        compiler_params=pltpu.CompilerParams(has_side_effects=True),
    )(*slabs)


def _exchange_grads(slabs, small, name):
    n = len(slabs)

    def body(*refs):
        ins, small_ref = refs[:n], refs[n]
        outs, small_out = refs[n + 1:2 * n + 1], refs[2 * n + 1]
        send_sems, recv_sems, ssend, srecv = refs[2 * n + 2:]
        x, y, c = _place()
        chips = [(1 - x, y), (x, 1 - y), (1 - x, 1 - y)]
        me = 4 * x + 2 * y + c
        flips = [(fx, fy, fc) for fx in (0, 1) for fy in (0, 1) for fc in (0, 1)][1:]

        def copy(a, j):
            px, py = chips[j]
            return pltpu.make_async_remote_copy(
                src_ref=ins[a].at[2 * px + py], dst_ref=outs[a].at[j], send_sem=send_sems.at[3 * a + j],
                recv_sem=recv_sems.at[3 * a + j], device_id=(px, py, c), device_id_type=MESH)

        def scopy(k, row):
            fx, fy, fc = flips[k]
            return pltpu.make_async_remote_copy(
                src_ref=small_ref, dst_ref=small_out.at[row], send_sem=ssend.at[k], recv_sem=srecv.at[k],
                device_id=(x ^ fx, y ^ fy, c ^ fc), device_id_type=MESH)

        sends = [copy(a, j) for a in range(n) for j in range(3)] + [scopy(k, me) for k in range(7)]
        for cp in sends:
            cp.start()
        for k, (fx, fy, fc) in enumerate(flips):
            scopy(k, 4 * (x ^ fx) + 2 * (y ^ fy) + (c ^ fc)).wait_recv()
        for a in range(n):
            for j in range(3):
                copy(a, j).wait_recv()
        for cp in sends:
            cp.wait_send()

    hbm = pl.BlockSpec(memory_space=pl.ANY)
    return pl.pallas_call(
        body, name=name, in_specs=[hbm] * (n + 1), out_specs=[hbm] * (n + 1),
        out_shape=[jax.ShapeDtypeStruct((3,) + s.shape[1:], s.dtype) for s in slabs]
        + [jax.ShapeDtypeStruct((N_DEV,) + small.shape, small.dtype)],
        scratch_shapes=[pltpu.SemaphoreType.DMA((3 * n,)), pltpu.SemaphoreType.DMA((3 * n,)),
                        pltpu.SemaphoreType.DMA((7,)), pltpu.SemaphoreType.DMA((7,))],
        compiler_params=pltpu.CompilerParams(has_side_effects=True),
    )(*slabs, small)


def _swap_with_sibling(arrs, name):
    n = len(arrs)

    def body(*refs):
        ins, outs = refs[:n], refs[n:2 * n]
        send_sems, recv_sems = refs[2 * n:]
        x, y, c = _place()
        copies = [pltpu.make_async_remote_copy(
            src_ref=ins[a], dst_ref=outs[a], send_sem=send_sems.at[a], recv_sem=recv_sems.at[a],
            device_id=(x, y, 1 - c), device_id_type=MESH) for a in range(n)]
        for cp in copies:
            cp.start()
        for cp in copies:
            cp.wait()

    hbm = pl.BlockSpec(memory_space=pl.ANY)
    return pl.pallas_call(
        body, name=name, in_specs=[hbm] * n, out_specs=[hbm] * n,
        out_shape=[jax.ShapeDtypeStruct(a.shape, a.dtype) for a in arrs],
        scratch_shapes=[pltpu.SemaphoreType.DMA((n,)), pltpu.SemaphoreType.DMA((n,))],
        compiler_params=pltpu.CompilerParams(has_side_effects=True),
    )(*arrs)


def _block_diag(w):
    eye = jnp.eye(LRU_BLOCKS, dtype=w.dtype)
    return (eye[:, None, :, None] * w[:, :, None, :]).reshape(LRU_W, LRU_W)


def _diag_blocks(m):
    m4 = m.reshape(LRU_BLOCKS, LRU_BLOCK, LRU_BLOCKS, LRU_BLOCK)
    return jnp.stack([m4[b, :, b, :] for b in range(LRU_BLOCKS)])


def _pad_rows(a, rows):
    return jnp.pad(a, ((0, rows - a.shape[0]), (0, 0)))


def _layer_weights(p, l):
    w_in = p["w_in"][l]
    n_gla = 2 * QK_W + 2 * GLA_W
    gate = jnp.pad(w_in[:, n_gla:n_gla + GATE_RANK], ((0, 0), (0, GATE_PAD - GATE_RANK)))
    wg = jnp.concatenate([w_in[:, :n_gla], gate], axis=1)
    wl = w_in[:, n_gla + GATE_RANK:]
    w_out = p["w_out"][l]
    fa, fg = p["ffn_w_in"][l][:, :FFN_H], p["ffn_w_in"][l][:, FFN_H:]
    wd = p["ffn_w_down"][l]
    return dict(
        wg=wg, wl=wl, wgT=wg.T, wlT=wl.T, wo=w_out, woT=w_out.T,
        fa=fa, fg=fg, faT=fa.T, fgT=fg.T, wd=wd, wdT=wd.T,
        w2p=_pad_rows(p["gla_gate_w2"][l], GATE_PAD).astype(BF16),
        wa=_block_diag(p["lru_wa"][l]).astype(BF16), wx=_block_diag(p["lru_wx"][l]).astype(BF16),
        lcw=_pad_rows(p["lru_conv_w"][l], SUBLANES),
        fcwa=_pad_rows(p["ffn_conv_w"][l][:, :FFN_H], SUBLANES), fcwg=_pad_rows(p["ffn_conv_w"][l][:, FFN_H:], SUBLANES),
    )


def _local_step(x, tgt, p):
    row = lambda v: v.reshape(1, -1)
    h = x
    u = _rms_fwd(h, p["ln_mix"][0], "mix_norm_fwd0")
    stash = []
    for l in range(DEPTH):
        w = _layer_weights(p, l)
        s = dict(w=w, h0=h)
        pg = _mm(u, w["wg"], None, F32, f"proj_gla_fwd{l}")
        plr = _mm(u, w["wl"], None, F32, f"proj_lru_fwd{l}")
        ym, o_st, s_st = _gla_fwd(pg, w["w2p"], p["gla_gate_b"][l], p["gla_norm"][l], f"gla_fwd{l}")
        ym, hs = _lru_fwd(plr, ym, w["lcw"], row(p["lru_conv_b"][l]), w["wa"], row(p["lru_ba"][l]), w["wx"],
                          row(p["lru_bx"][l]), row(p["lru_lambda"][l]), f"lru_fwd{l}")
        h, u2 = _mm(ym, w["wo"], h, F32, f"out_fwd{l}", norm_g=p["ln_ffn"][l])
        s.update(u=u, pg=pg, plr=plr, ym=ym, o_st=o_st, s_st=s_st, hs=hs, h1=h)
        cba, cbg = row(p["ffn_conv_b"][l][:FFN_H]), row(p["ffn_conv_b"][l][FFN_H:])
        *kept, act = _ffn_up_fwd(u2, w["fa"], w["fg"], w["fcwa"], w["fcwg"], cba, cbg, f"ffn_up_fwd{l}")
        if l + 1 < DEPTH:
            h, u = _mm(act, w["wd"], h, F32, f"ffn_down_fwd{l}", norm_g=p["ln_mix"][l + 1])
        else:
            h = _mm(act, w["wd"], h, F32, f"ffn_down_fwd{l}")
        s.update(u2=u2, ffn_kept=kept, act=act)
        stash.append(s)

    loss, dh, dhb, d_ln_final = _loss_head(h, p["ln_final"], tgt, "loss_head")

    g = {k: [None] * DEPTH for k in ("ln_mix", "w_in", "gla_gate_w2", "gla_gate_b", "gla_norm", "lru_conv_w",
                                     "lru_conv_b", "lru_wa", "lru_ba", "lru_wx", "lru_bx", "lru_lambda",
                                     "ln_ffn", "ffn_conv_w", "ffn_conv_b")}
    slab = dict(w_out=jnp.zeros((N_CHIPS, DEPTH * D_MODEL // N_CHIPS, D_MODEL), F32),
                ffn_w_in=jnp.zeros((N_CHIPS, DEPTH * D_MODEL, 2 * FFN_H // N_CHIPS), F32),
                ffn_w_down=jnp.zeros((N_CHIPS, DEPTH * FFN_H // N_CHIPS, D_MODEL), F32))
    n_gla = 2 * QK_W + 2 * GLA_W
    for l in reversed(range(DEPTH)):
        s = stash[l]
        w = s["w"]
        slab["ffn_w_down"] = _mm_tn_into(slab["ffn_w_down"], s["act"], dhb, l, 0, f"ffn_down_dw{l}",
                                         tk=FFN_H // N_CHIPS, tn=D_MODEL, tm=2048)
        dza, dzg, dca, dcg, dh, dhb, dln = _ffn_bwd_core(
            dh, dhb, *s["ffn_kept"], w["wdT"], w["faT"], w["fgT"], w["fcwa"], w["fcwg"], s["h1"], p["ln_ffn"][l],
            f"ffn_bwd_core{l}")
        dca, dcg = (jnp.moveaxis(d, 0, 1).reshape(SUBLANES, FFN_H) for d in (dca, dcg))
        g["ffn_conv_w"][l] = jnp.concatenate([dca[:FFN_CONV], dcg[:FFN_CONV]], axis=1)
        g["ffn_conv_b"][l] = jnp.concatenate([dca[FFN_CONV], dcg[FFN_CONV]])
        for half, dz in enumerate((dza, dzg)):
            slab["ffn_w_in"] = _mm_tn_into(slab["ffn_w_in"], s["u2"], dz, l, 2 * half, f"ffn_in_dw{l}_{half}",
                                           tk=D_MODEL, tn=2 * FFN_H // N_CHIPS)
        g["ln_ffn"][l] = dln[0]
        slab["w_out"] = _mm_tn_into(slab["w_out"], s["ym"], dhb, l, 0, f"out_dw{l}",
                                    tk=D_MODEL // N_CHIPS, tn=D_MODEL, tm=2048)
        dyc = _mm(dhb, w["woT"], None, F32, f"out_dx{l}")
        dpg, dw2, db2, dng = _gla_bwd(dyc, s["pg"], s["o_st"], s["s_st"], w["w2p"], p["gla_gate_b"][l],
                                      p["gla_norm"][l], f"gla_bwd{l}")
        dpl, dcw, dvec, dwa, dwx = _lru_bwd(dyc, s["plr"], s["hs"], w["lcw"], row(p["lru_conv_b"][l]), w["wa"],
                                            row(p["lru_ba"][l]), w["wx"], row(p["lru_bx"][l]),
                                            row(p["lru_lambda"][l]), f"lru_bwd{l}")
        g["gla_gate_w2"][l] = dw2[:GATE_RANK]
        g["gla_gate_b"][l] = db2[0]
        g["gla_norm"][l] = dng[0]
        g["lru_conv_w"][l] = dcw[:LRU_CONV]
        g["lru_conv_b"][l], g["lru_ba"][l], g["lru_bx"][l], g["lru_lambda"][l] = dvec[0], dvec[1], dvec[2], dvec[3]
        g["lru_wa"][l], g["lru_wx"][l] = _diag_blocks(dwa), _diag_blocks(dwx)
        dwg = _mm_tn(s["u"], dpg, f"proj_gla_dw{l}")
        dwl = _mm_tn(s["u"], dpl, f"proj_lru_dw{l}")
        g["w_in"][l] = jnp.concatenate([dwg[:, :n_gla + GATE_RANK], dwl], axis=1)
        dh, dhb, dln = _mm_rms_bwd([(dpg, w["wgT"]), (dpl, w["wlT"])], s["h0"], p["ln_mix"][l], dh, f"proj_dx{l}")
        g["ln_mix"][l] = dln[0]
    grads = {k: jnp.stack(v) for k, v in g.items()}
    grads["w_in"] = _slabs_from_whole("w_in", grads["w_in"])
    grads.update(slab)
    grads["ln_final"] = d_ln_final[0]
    return loss, dh, grads


BIG = ("w_in", "w_out", "ffn_w_in", "ffn_w_down")
COL_SHARDED = ("w_in", "ffn_w_in", "gla_gate_w2", "lru_conv_w", "ffn_conv_w")
SMALL = ("ln_mix", "gla_gate_w2", "gla_gate_b", "gla_norm", "lru_conv_w", "lru_conv_b", "lru_wa", "lru_ba", "lru_wx",
         "lru_bx", "lru_lambda", "ln_ffn", "ffn_conv_w", "ffn_conv_b", "ln_final")
WEIGHTS = ("ln_mix", "w_in", "gla_gate_w2", "gla_gate_b", "gla_norm", "lru_conv_w", "lru_conv_b", "lru_wa", "lru_ba",
           "lru_wx", "lru_bx", "lru_lambda", "w_out", "ln_ffn", "ffn_w_in", "ffn_conv_w", "ffn_conv_b", "ffn_w_down",
           "ln_final")
PACK = SUBLANES * LANES


def _whole_from_shards(name, g):
    if name in COL_SHARDED:
        return jnp.moveaxis(g, 0, -2).reshape(g.shape[1:-1] + (N_CHIPS * g.shape[-1],))
    return jnp.moveaxis(g, 0, 1).reshape((g.shape[1], N_CHIPS * g.shape[2]) + g.shape[3:])


def _slabs_from_whole(name, w):
    L, r, c = w.shape
    if name in COL_SHARDED:
        s = jnp.moveaxis(w.reshape(L, r, N_CHIPS, c // N_CHIPS), 2, 0)
    else:
        s = jnp.moveaxis(w.reshape(L, N_CHIPS, r // N_CHIPS, c), 1, 0)
    return s.reshape(N_CHIPS, -1, s.shape[-1])


def _pack(arrs):
    flat = []
    for a in arrs:
        f = a.reshape(-1)
        flat.append(jnp.pad(f, (0, (-f.shape[0]) % PACK)))
    return jnp.concatenate(flat).reshape(-1, LANES)


def _unpack(packed, shapes):
    out, at = [], 0
    flat = packed.reshape(-1)
    for s in shapes:
        size = math.prod(s)
        out.append(flat[at:at + size].reshape(s))
        at += size + (-size) % PACK
    return out


def kernel(x, ln_mix, w_in, gla_gate_w2, gla_gate_b, gla_norm, lru_conv_w, lru_conv_b, lru_wa, lru_ba, lru_wx, lru_bx, lru_lambda, w_out, ln_ffn, ffn_w_in, ffn_conv_w, ffn_conv_b, ffn_w_down, ln_final, loss_target, m_ln_mix, m_w_in, m_gla_gate_w2, m_gla_gate_b, m_gla_norm, m_lru_conv_w, m_lru_conv_b, m_lru_wa, m_lru_ba, m_lru_wx, m_lru_bx, m_lru_lambda, m_w_out, m_ln_ffn, m_ffn_w_in, m_ffn_conv_w, m_ffn_conv_b, m_ffn_w_down, m_ln_final, v_ln_mix, v_w_in, v_gla_gate_w2, v_gla_gate_b, v_gla_norm, v_lru_conv_w, v_lru_conv_b, v_lru_wa, v_lru_ba, v_lru_wx, v_lru_bx, v_lru_lambda, v_w_out, v_ln_ffn, v_ffn_w_in, v_ffn_conv_w, v_ffn_conv_b, v_ffn_w_down, v_ln_final):
    w = dict(ln_mix=ln_mix, w_in=w_in, gla_gate_w2=gla_gate_w2, gla_gate_b=gla_gate_b, gla_norm=gla_norm,
             lru_conv_w=lru_conv_w, lru_conv_b=lru_conv_b, lru_wa=lru_wa, lru_ba=lru_ba, lru_wx=lru_wx, lru_bx=lru_bx,
             lru_lambda=lru_lambda, w_out=w_out, ln_ffn=ln_ffn, ffn_w_in=ffn_w_in, ffn_conv_w=ffn_conv_w,
             ffn_conv_b=ffn_conv_b, ffn_w_down=ffn_w_down, ln_final=ln_final)
    m = dict(ln_mix=m_ln_mix, w_in=m_w_in, gla_gate_w2=m_gla_gate_w2, gla_gate_b=m_gla_gate_b, gla_norm=m_gla_norm,
             lru_conv_w=m_lru_conv_w, lru_conv_b=m_lru_conv_b, lru_wa=m_lru_wa, lru_ba=m_lru_ba, lru_wx=m_lru_wx,
             lru_bx=m_lru_bx, lru_lambda=m_lru_lambda, w_out=m_w_out, ln_ffn=m_ln_ffn, ffn_w_in=m_ffn_w_in,
             ffn_conv_w=m_ffn_conv_w, ffn_conv_b=m_ffn_conv_b, ffn_w_down=m_ffn_w_down, ln_final=m_ln_final)
    v = dict(ln_mix=v_ln_mix, w_in=v_w_in, gla_gate_w2=v_gla_gate_w2, gla_gate_b=v_gla_gate_b, gla_norm=v_gla_norm,
             lru_conv_w=v_lru_conv_w, lru_conv_b=v_lru_conv_b, lru_wa=v_lru_wa, lru_ba=v_lru_ba, lru_wx=v_lru_wx,
             lru_bx=v_lru_bx, lru_lambda=v_lru_lambda, w_out=v_w_out, ln_ffn=v_ln_ffn, ffn_w_in=v_ffn_w_in,
             ffn_conv_w=v_ffn_conv_w, ffn_conv_b=v_ffn_conv_b, ffn_w_down=v_ffn_w_down, ln_final=v_ln_final)

    sharded = BIG + ("gla_gate_w2", "lru_conv_w", "ffn_conv_w")
    chip = 2 * lax.axis_index("x") + lax.axis_index("y")
    core = lax.axis_index("c")
    shards = [w[k].astype(MXU_DTYPE) if k in BIG else w[k] for k in sharded]
    gathered = _gather_shards(shards, "gather_weights")
    p = dict(w)
    for k, gk, own in zip(sharded, gathered, shards):
        p[k] = _whole_from_shards(k, lax.dynamic_update_index_in_dim(gk, own, chip, 0))

    loss, grad_x, grads = _local_step(x[0], loss_target[0], p)
    loss = lax.psum(loss[0, 0], ("x", "y", "c"))

    slabs = [grads[k].reshape(N_CHIPS, 2, grads[k].shape[1] // 2, grads[k].shape[2]) for k in BIG]
    got = _other_half_to_sibling(slabs, "other_half_to_sibling")
    kept = [lax.dynamic_index_in_dim(s, core, 1, keepdims=False) for s in slabs]
    chip_half = [_add_slabs(a, b, BF16, f"core_sum_{k}") for k, a, b in zip(BIG, kept, got)]
    small = _pack([grads[k] for k in SMALL])
    *recv, small_all = _exchange_grads(chip_half, small, "exchange_grads")
    own = [lax.dynamic_index_in_dim(h, chip, 0, keepdims=False) for h in chip_half]
    mine = [_sum_leading(r, f"chip_sum_{k}", last=o) for k, r, o in zip(BIG, recv, own)]
    theirs = _swap_with_sibling(mine, "swap_core_halves")
    big_g = [jnp.concatenate([jnp.where(core == 0, a, b), jnp.where(core == 0, b, a)]) for a, b in zip(mine, theirs)]
    small_all = lax.dynamic_update_index_in_dim(small_all, small, 2 * chip + core, 0)
    small_sum = _unpack(_sum_leading(small_all, "sum_small_grads"), [grads[k].shape for k in SMALL])

    me = 2 * lax.axis_index("x") + lax.axis_index("y")
    out_g, out_d, out_m, out_v = {}, {}, {}, {}
    for k, gk in zip(BIG, big_g):
        shape = w[k].shape
        cols = shape[-1]
        res = _adamw(w[k].reshape(-1, cols), [gk], m[k].reshape(-1, cols), v[k].reshape(-1, cols), f"adamw_{k}")
        out_g[k], out_d[k], out_m[k], out_v[k] = [r.reshape(shape) for r in res]
    small_g = []
    for k, gk in zip(SMALL, small_sum):
        if k in COL_SHARDED:
            width = w[k].shape[-1]
            gk = lax.dynamic_slice_in_dim(gk, me * width, width, axis=gk.ndim - 1)
        small_g.append(gk)
    shapes = [w[k].shape for k in SMALL]
    res = _adamw(_pack([w[k] for k in SMALL]), [_pack(small_g)], _pack([m[k] for k in SMALL]),
                 _pack([v[k] for k in SMALL]), "adamw_small")
    for out, packed in zip((out_g, out_d, out_m, out_v), res):
        for k, a in zip(SMALL, _unpack(packed, shapes)):
            out[k] = a
    return (loss, grad_x[None], *[out_g[k] for k in WEIGHTS], *[out_d[k] for k in WEIGHTS],
            *[out_m[k] for k in WEIGHTS], *[out_v[k] for k in WEIGHTS])
```

```python
import math

import jax
import jax.numpy as jnp
from jax import lax
from jax.experimental import pallas as pl
from jax.experimental.pallas import tpu as pltpu

F32 = jnp.float32
BF16 = jnp.bfloat16
MXU_DTYPE = BF16

D_MODEL = 1024
DEPTH = 4
HEADS, DK, DV, CHUNK, GATE_RANK = 4, 64, 128, 64, 16
QK_W = HEADS * DK
GLA_W = HEADS * DV
LRU_W = 512
LRU_BLOCKS, LRU_BLOCK, LRU_CONV, LRU_C = 8, 64, 4, 8.0
FFN_H = 3 * D_MODEL
FFN_CONV = 3
EPS = 1e-6
GATE_PAD = 128
GLA_COLS = 2 * QK_W + 2 * GLA_W + GATE_PAD
LRU_COLS = 2 * LRU_W
ADAM_LR, ADAM_B1, ADAM_B2, ADAM_EPS, ADAM_WD, ADAM_STEP = 0.001, 0.9, 0.999, 1e-08, 0.01, 10

LANES = 128
SUBLANES = 8
VMEM_LIMIT = 56 * 1024 * 1024
ROWS = 512
TILE_BYTES = 1 << 20
FFN_CW = 1024
FFN_SB = 256
FFN_RC = 32
N_CHIPS = 4
N_DEV = 8
MESH = pl.DeviceIdType.MESH


def _cp(*sem):
    return pltpu.CompilerParams(dimension_semantics=sem, vmem_limit_bytes=VMEM_LIMIT)


def _dot(a, b):
    return jnp.dot(a.astype(MXU_DTYPE), b.astype(MXU_DTYPE), preferred_element_type=F32)


def _dot_nt(a, b):
    return lax.dot_general(a.astype(MXU_DTYPE), b.astype(MXU_DTYPE), (((1,), (1,)), ((), ())),
                           preferred_element_type=F32)


def _dot_tn(a, b):
    return lax.dot_general(a.astype(MXU_DTYPE), b.astype(MXU_DTYPE), (((0,), (0,)), ((), ())),
                           preferred_element_type=F32)


def _bdot(eq, a, b):
    return jnp.einsum(eq, a, b, preferred_element_type=F32)


def _split3(x):
    x1 = x.astype(BF16)
    r1 = x - x1.astype(F32)
    x2 = r1.astype(BF16)
    x3 = (r1 - x2.astype(F32)).astype(BF16)
    return x1, x2, x3


GELU_C = math.sqrt(2.0 / math.pi)
GELU_A = 0.044715


def _gelu(x):
    return x * (0.5 * (1.0 + jnp.tanh(GELU_C * (x + GELU_A * (x * x * x)))))


def _gelu_and_grad(x):
    x2 = x * x
    t = jnp.tanh(x * (GELU_C + (GELU_C * GELU_A) * x2))
    cdf = 0.5 * t + 0.5
    half_sech2 = 0.5 - 0.5 * (t * t)
    return x * cdf, cdf + (x * half_sech2) * (GELU_C + (3.0 * GELU_C * GELU_A) * x2)


EXPM1_SERIES_BELOW = 0.1


def _expm1(x, exp_x):
    small = x * (1.0 + x * (0.5 + x * (1.0 / 6.0 + x * (1.0 / 24.0 + x * (1.0 / 120.0)))))
    return jnp.where(jnp.abs(x) < EXPM1_SERIES_BELOW, small, exp_x - 1.0)


def _shift_down(x, k, fill):
    row = lax.broadcasted_iota(jnp.int32, x.shape, 0)
    return jnp.where(row >= k, pltpu.roll(x, k, axis=0), fill)


def _shift_up(x, k, fill):
    n = x.shape[0]
    row = lax.broadcasted_iota(jnp.int32, x.shape, 0)
    return jnp.where(row < n - k, pltpu.roll(x, n - k, axis=0), fill)


def _rms_fwd(h, g, name):
    T, D = h.shape
    R = min(T, ROWS)

    def body(h_ref, g_ref, o_ref):
        x = h_ref[...]
        r = lax.rsqrt(jnp.mean(x * x, axis=-1, keepdims=True) + EPS)
        o_ref[...] = ((x * r) * g_ref[...]).astype(o_ref.dtype)

    return pl.pallas_call(
        body, name=name, grid=(T // R,),
        in_specs=[pl.BlockSpec((R, D), lambda i: (i, 0)), pl.BlockSpec((1, D), lambda i: (0, 0))],
        out_specs=pl.BlockSpec((R, D), lambda i: (i, 0)),
        out_shape=jax.ShapeDtypeStruct((T, D), BF16), compiler_params=_cp("parallel"),
    )(h, g.reshape(1, D))


def _rms_bwd_tile(x, g, du, dres):
    r = lax.rsqrt(jnp.mean(x * x, axis=-1, keepdims=True) + EPS)
    xhat = x * r
    dxhat = du * g
    dx = r * (dxhat - xhat * jnp.mean(dxhat * xhat, axis=-1, keepdims=True))
    return dres + dx, jnp.sum(du * xhat, axis=0, keepdims=True)


def _mm_rms_bwd(pairs, h, g, dres, name, tm=512):
    M, D = h.shape
    tm = min(tm, M)
    n = len(pairs)

    def body(*refs):
        h_ref, g_ref, dres_ref = refs[2 * n:2 * n + 3]
        dh_ref, dhb_ref, dg_ref = refs[2 * n + 3:]

        @pl.when(pl.program_id(0) == 0)
        def _():
            dg_ref[...] = jnp.zeros_like(dg_ref)

        du = _dot(refs[0][...], refs[1][...])
        for k in range(1, n):
            du = du + _dot(refs[2 * k][...], refs[2 * k + 1][...])
        dh, dg = _rms_bwd_tile(h_ref[...], g_ref[...], du, dres_ref[...])
        dg_ref[...] += dg
        dh_ref[...] = dh
        dhb_ref[...] = dh.astype(dhb_ref.dtype)

    in_specs, args = [], []
    for a, b in pairs:
        in_specs += [pl.BlockSpec((tm, a.shape[1]), lambda i: (i, 0)), pl.BlockSpec(b.shape, lambda i: (0, 0))]
        args += [a, b]
    blk = pl.BlockSpec((tm, D), lambda i: (i, 0))
    vec = pl.BlockSpec((1, D), lambda i: (0, 0))
    return pl.pallas_call(
        body, name=name, grid=(M // tm,), in_specs=in_specs + [blk, vec, blk], out_specs=[blk, blk, vec],
        out_shape=[jax.ShapeDtypeStruct((M, D), F32), jax.ShapeDtypeStruct((M, D), BF16),
                   jax.ShapeDtypeStruct((1, D), F32)],
        compiler_params=_cp("arbitrary"),
    )(*args, h, g.reshape(1, D), dres)


def _loss_head(h, g, tgt, name):
    T, D = h.shape
    R = min(T, ROWS)

    def body(h_ref, g_ref, t_ref, loss_ref, dh_ref, dhb_ref, dg_ref):
        @pl.when(pl.program_id(0) == 0)
        def _():
            dg_ref[...] = jnp.zeros_like(dg_ref)
            loss_ref[...] = jnp.zeros_like(loss_ref)

        x = h_ref[...]
        r = lax.rsqrt(jnp.mean(x * x, axis=-1, keepdims=True) + EPS)
        xhat = x * r
        gg = g_ref[...]
        err = xhat * gg - t_ref[...]
        loss_ref[...] += 0.5 * jnp.sum(jnp.mean(err * err, axis=-1, keepdims=True), axis=0, keepdims=True)
        dy = err * (1.0 / D)
        dg_ref[...] += jnp.sum(dy * xhat, axis=0, keepdims=True)
        dxhat = dy * gg
        dh = r * (dxhat - xhat * jnp.mean(dxhat * xhat, axis=-1, keepdims=True))
        dh_ref[...] = dh
        dhb_ref[...] = dh.astype(dhb_ref.dtype)

    blk = pl.BlockSpec((R, D), lambda i: (i, 0))
    vec = pl.BlockSpec((1, D), lambda i: (0, 0))
    one = pl.BlockSpec((1, LANES), lambda i: (0, 0))
    return pl.pallas_call(
        body, name=name, grid=(T // R,), in_specs=[blk, vec, blk], out_specs=[one, blk, blk, vec],
        out_shape=[jax.ShapeDtypeStruct((1, LANES), F32), jax.ShapeDtypeStruct((T, D), F32),
                   jax.ShapeDtypeStruct((T, D), BF16), jax.ShapeDtypeStruct((1, D), F32)],
        compiler_params=_cp("arbitrary"),
    )(h, g.reshape(1, D), tgt)


def _mm(a, b, res, out_dtype, name, tm=512, tn=None, norm_g=None):
    M, K = a.shape
    N = b.shape[1]
    tm = min(tm, M)
    tn = N if tn is None else tn
    assert norm_g is None or tn == N

    def body(*refs):
        refs = list(refs)
        a_ref, b_ref = refs[:2]
        acc = _dot(a_ref[...], b_ref[...])
        if res is not None:
            acc = refs[2][...].astype(F32) + acc
        if norm_g is None:
            refs[-1][...] = acc.astype(refs[-1].dtype)
        else:
            refs[-2][...] = acc.astype(refs[-2].dtype)
            r = lax.rsqrt(jnp.mean(acc * acc, axis=-1, keepdims=True) + EPS)
            refs[-1][...] = ((acc * r) * refs[-3][...]).astype(refs[-1].dtype)

    blk = pl.BlockSpec((tm, tn), lambda j, i: (i, j))
    in_specs = [pl.BlockSpec((tm, K), lambda j, i: (i, 0)), pl.BlockSpec((K, tn), lambda j, i: (0, j))]
    args = [a, b]
    if res is not None:
        in_specs.append(blk)
        args.append(res)
    out_specs, out_shape = blk, jax.ShapeDtypeStruct((M, N), out_dtype)
    if norm_g is not None:
        in_specs.append(pl.BlockSpec((1, N), lambda j, i: (0, 0)))
        args.append(norm_g.reshape(1, N))
        out_specs, out_shape = [blk, blk], [out_shape, jax.ShapeDtypeStruct((M, N), BF16)]
    return pl.pallas_call(
        body, name=name, grid=(N // tn, M // tm), in_specs=in_specs, out_specs=out_specs, out_shape=out_shape,
        compiler_params=_cp("parallel", "parallel"),
    )(*args)


def _mm_tn_into(slab, a, b, layer, chip0, name, tk, tn, tm=1024):
    M, K = a.shape
    N = b.shape[1]
    tm = min(tm, M)
    assert slab.shape[2] == tn and (K // tk == 1 or N // tn == 1)

    def body(a_ref, b_ref, slab_ref, o_ref):
        del slab_ref

        @pl.when(pl.program_id(2) == 0)
        def _():
            o_ref[...] = jnp.zeros_like(o_ref)

        o_ref[0] += _dot_tn(a_ref[...], b_ref[...])

    return pl.pallas_call(
        body, name=name, grid=(K // tk, N // tn, M // tm),
        in_specs=[pl.BlockSpec((tm, tk), lambda k, n, m: (m, k)), pl.BlockSpec((tm, tn), lambda k, n, m: (m, n)),
                  pl.BlockSpec(memory_space=pl.ANY)],
        out_specs=pl.BlockSpec((1, tk, tn), lambda k, n, m: (chip0 + k + n, layer, 0)),
        out_shape=jax.ShapeDtypeStruct(slab.shape, F32), input_output_aliases={2: 0},
        compiler_params=_cp("parallel", "parallel", "arbitrary"),
    )(a, b, slab)


def _mm_tn(a, b, name, tm=2048, tk=None, tn=None):
    M, K = a.shape
    N = b.shape[1]
    tm = min(tm, M)
    tk = K if tk is None else tk
    tn = N if tn is None else tn

    def body(a_ref, b_ref, o_ref):
        @pl.when(pl.program_id(2) == 0)
        def _():
            o_ref[...] = jnp.zeros_like(o_ref)

        o_ref[...] += _dot_tn(a_ref[...], b_ref[...])

    return pl.pallas_call(
        body, name=name, grid=(K // tk, N // tn, M // tm),
        in_specs=[pl.BlockSpec((tm, tk), lambda k, n, m: (m, k)), pl.BlockSpec((tm, tn), lambda k, n, m: (m, n))],
        out_specs=pl.BlockSpec((tk, tn), lambda k, n, m: (k, n)),
        out_shape=jax.ShapeDtypeStruct((K, N), F32), compiler_params=_cp("parallel", "parallel", "arbitrary"),
    )(a, b)


def _same_chunk(row, col):
    shift = CHUNK.bit_length() - 1
    return jnp.right_shift(row, shift) == jnp.right_shift(col, shift)


def _gla_common(q, k, glr, w2, b2, R):
    gl = _dot(glr, w2) + b2
    la = jax.nn.log_sigmoid(gl) * (1.0 / 16.0)
    row = lax.broadcasted_iota(jnp.int32, (R, R), 0)
    col = lax.broadcasted_iota(jnp.int32, (R, R), 1)
    same = _same_chunk(row, col)
    m_tri = (same & (col <= row)).astype(BF16)
    m_all = same.astype(BF16)
    la3 = _split3(la)
    b = sum(jnp.dot(m_tri, p, preferred_element_type=F32) for p in la3)
    bl = sum(jnp.dot(m_all, p, preferred_element_type=F32) for p in la3)
    eb = jnp.exp(b)
    enb = jnp.exp(-b)
    ek = jnp.exp(bl - b)
    qi = (q * (DK ** -0.5)) * eb
    ki = k * enb
    kd = k * ek
    return gl, la3, eb, enb, ek, qi, ki, kd


def _bsplit(x, n):
    return x.reshape(n, CHUNK, x.shape[-1])


def _tril():
    return (lax.broadcasted_iota(jnp.int32, (CHUNK, CHUNK), 1)
            <= lax.broadcasted_iota(jnp.int32, (CHUNK, CHUNK), 0))[None]


def _gla_fwd(proj, w2p, b2, norm_g, name):
    T = proj.shape[0]
    R = min(T, ROWS)
    n = R // CHUNK

    def body(q_ref, k_ref, v_ref, g_ref, a_ref, w2_ref, b2_ref, ng_ref, y_ref, o_ref, st_ref, s_ref):
        @pl.when(pl.program_id(0) == 0)
        def _():
            s_ref[...] = jnp.zeros_like(s_ref)

        _, la3, _, _, _, qi, ki, kd = _gla_common(q_ref[...], k_ref[...], a_ref[...], w2_ref[...], b2_ref[...], R)
        tril = _tril()
        ones = jnp.ones((n, CHUNK, DV), BF16)
        for h in range(HEADS):
            sl = slice(h * DK, (h + 1) * DK)
            sv = slice(h * DV, (h + 1) * DV)
            qh = _bsplit(qi[:, sl], n).astype(MXU_DTYPE)
            kh = _bsplit(ki[:, sl], n).astype(MXU_DTYPE)
            kdh = _bsplit(kd[:, sl], n).astype(MXU_DTYPE)
            vh = _bsplit(v_ref[:, sv], n).astype(MXU_DTYPE)
            att = jnp.where(tril, _bdot('ncd,nsd->ncs', qh, kh), 0.0)
            upd = _bdot('ncd,nce->nde', kdh, vh)
            dect = jnp.exp(sum(_bdot('ncd,nce->nde', _bsplit(p[:, sl], n), ones) for p in la3))
            s = s_ref[sl, :]
            for c in range(n):
                st_ref[c, sl, :] = s
                s = dect[c] * s + upd[c]
            s_ref[sl, :] = s
            sp = st_ref[:, sl, :].astype(MXU_DTYPE)
            o = (_bdot('ncs,nse->nce', att.astype(MXU_DTYPE), vh) + _bdot('ncd,nde->nce', qh, sp)).reshape(R, DV)
            o_ref[:, sv] = o
            r = lax.rsqrt(jnp.mean(o * o, axis=-1, keepdims=True) + EPS)
            gate = g_ref[:, sv]
            y_ref[:, sv] = (((o * r) * ng_ref[...]) * (gate * jax.nn.sigmoid(gate))).astype(y_ref.dtype)

    cb = lambda w, j: pl.BlockSpec((R, w), lambda i: (i, j))
    full = lambda s: pl.BlockSpec(s, lambda i: (0,) * len(s))
    return pl.pallas_call(
        body, name=name, grid=(T // R,),
        in_specs=[cb(QK_W, 0), cb(QK_W, 1), cb(GLA_W, 1), cb(GLA_W, 2), cb(GATE_PAD, 12),
                  full((GATE_PAD, QK_W)), full((1, QK_W)), full((1, DV))],
        out_specs=[pl.BlockSpec((R, GLA_W), lambda i: (i, 0)), pl.BlockSpec((R, GLA_W), lambda i: (i, 0)),
                   pl.BlockSpec((n, QK_W, DV), lambda i: (i, 0, 0))],
        out_shape=[jax.ShapeDtypeStruct((T, GLA_W + LRU_W), BF16), jax.ShapeDtypeStruct((T, GLA_W), F32),
                   jax.ShapeDtypeStruct((T // CHUNK, QK_W, DV), F32)],
        scratch_shapes=[pltpu.VMEM((QK_W, DV), F32)],
        compiler_params=_cp("arbitrary"),
    )(proj, proj, proj, proj, proj, w2p, b2.reshape(1, QK_W), norm_g.reshape(1, DV))


def _gla_bwd(dy, proj, o_st, s_st, w2p, b2, norm_g, name):
    T = proj.shape[0]
    R = min(T, ROWS)
    n = R // CHUNK
    nb = T // R

    def body(dy_ref, q_ref, k_ref, v_ref, g_ref, a_ref, o_ref, st_ref, w2_ref, b2_ref, ng_ref,
             dp_ref, dw2_ref, db2_ref, dng_ref, gs_ref, gn_ref, db_ref, dbl_ref):
        @pl.when(pl.program_id(0) == 0)
        def _():
            gs_ref[...] = jnp.zeros_like(gs_ref)
            dw2_ref[...] = jnp.zeros_like(dw2_ref)
            db2_ref[...] = jnp.zeros_like(db2_ref)
            dng_ref[...] = jnp.zeros_like(dng_ref)

        glr = a_ref[...]
        gl, la3, eb, enb, ek, qi, ki, kd = _gla_common(q_ref[...], k_ref[...], glr, w2_ref[...], b2_ref[...], R)
        tril = _tril()
        ones = jnp.ones((n, CHUNK, DV), BF16)
        ng = ng_ref[...]
        dng = jnp.zeros((1, DV), F32)
        for h in range(HEADS):
            sl = slice(h * DK, (h + 1) * DK)
            sv = slice(h * DV, (h + 1) * DV)
            o = o_ref[:, sv]
            r = lax.rsqrt(jnp.mean(o * o, axis=-1, keepdims=True) + EPS)
            xhat = o * r
            gate = g_ref[:, sv]
            sg = jax.nn.sigmoid(gate)
            dyh = dy_ref[:, sv].astype(F32)
            dp_ref[:, 2 * QK_W + GLA_W + h * DV:2 * QK_W + GLA_W + (h + 1) * DV] = (
                dyh * (xhat * ng) * (sg * (1.0 + gate * (1.0 - sg)))).astype(dp_ref.dtype)
            don = dyh * (gate * sg)
            dng = dng + jnp.sum(don * xhat, axis=0, keepdims=True)
            dxhat = don * ng
            do = r * (dxhat - xhat * jnp.mean(dxhat * xhat, axis=-1, keepdims=True))
            qf = _bsplit(qi[:, sl], n)
            kf = _bsplit(ki[:, sl], n)
            kdf = _bsplit(kd[:, sl], n)
            qh, kh, kdh = qf.astype(MXU_DTYPE), kf.astype(MXU_DTYPE), kdf.astype(MXU_DTYPE)
            vh = _bsplit(v_ref[:, sv], n).astype(MXU_DTYPE)
            doh = _bsplit(do, n).astype(MXU_DTYPE)
            spf = st_ref[:, sl, :]
            sp = spf.astype(MXU_DTYPE)
            att = jnp.where(tril, _bdot('ncd,nsd->ncs', qh, kh), 0.0).astype(MXU_DTYPE)
            datt = jnp.where(tril, _bdot('nce,nse->ncs', doh, vh), 0.0).astype(MXU_DTYPE)
            dv = _bdot('ncs,nce->nse', att, doh)
            dqi = _bdot('ncs,nsd->ncd', datt, kh) + _bdot('nce,nde->ncd', doh, sp)
            dki = _bdot('ncs,ncd->nsd', datt, qh)
            wgt = _bdot('ncd,nce->nde', qh, doh)
            dect = jnp.exp(sum(_bdot('ncd,nce->nde', _bsplit(p[:, sl], n), ones) for p in la3))
            g = gs_ref[sl, :]
            for c in reversed(range(n)):
                gn_ref[c] = g
                g = wgt[c] + dect[c] * g
            gs_ref[sl, :] = g
            gnf = gn_ref[...]
            gn = gnf.astype(MXU_DTYPE)
            dkd = _bdot('nce,nde->ncd', vh, gn)
            dv = dv + _bdot('ncd,nde->nce', kdh, gn)
            dp_ref[:, 2 * QK_W + h * DV:2 * QK_W + (h + 1) * DV] = dv.reshape(R, DV).astype(dp_ref.dtype)
            dbl = sum(_bdot('nce,nde->ncd', ones, p) for p in _split3(gnf * spf * dect))
            pk = dkd * kdf
            dbl = dbl + jnp.sum(pk, axis=1, keepdims=True)
            dbl_ref[:, sl] = dbl.reshape(R, DK)
            db_ref[:, sl] = (dqi * qf - dki * kf - pk).reshape(R, DK)
            dp_ref[:, sl] = ((dqi.reshape(R, DK) * (DK ** -0.5)) * eb[:, sl]).astype(dp_ref.dtype)
            dp_ref[:, QK_W + h * DK:QK_W + (h + 1) * DK] = (
                dki.reshape(R, DK) * enb[:, sl] + dkd.reshape(R, DK) * ek[:, sl]).astype(dp_ref.dtype)
        dng_ref[...] += dng
        row = lax.broadcasted_iota(jnp.int32, (R, R), 0)
        col = lax.broadcasted_iota(jnp.int32, (R, R), 1)
        m_rev = (_same_chunk(row, col) & (col >= row)).astype(BF16)
        dla = sum(jnp.dot(m_rev, p, preferred_element_type=F32) for p in _split3(db_ref[...])) + dbl_ref[...]
        dgl = (dla * (1.0 / 16.0)) * jax.nn.sigmoid(-gl)
        dp_ref[:, 2 * QK_W + 2 * GLA_W:GLA_COLS] = _dot_nt(dgl, w2_ref[...]).astype(dp_ref.dtype)
        dw2_ref[...] += _dot_tn(glr, dgl)
        db2_ref[...] += jnp.sum(dgl, axis=0, keepdims=True)

    cb = lambda w, j: pl.BlockSpec((R, w), lambda i: (nb - 1 - i, j))
    full = lambda s: pl.BlockSpec(s, lambda i: (0,) * len(s))
    return pl.pallas_call(
        body, name=name, grid=(nb,),
        in_specs=[cb(GLA_W, 0), cb(QK_W, 0), cb(QK_W, 1), cb(GLA_W, 1), cb(GLA_W, 2), cb(GATE_PAD, 12),
                  cb(GLA_W, 0), pl.BlockSpec((n, QK_W, DV), lambda i: (nb - 1 - i, 0, 0)),
                  full((GATE_PAD, QK_W)), full((1, QK_W)), full((1, DV))],
        out_specs=[pl.BlockSpec((R, GLA_COLS), lambda i: (nb - 1 - i, 0)),
                   full((GATE_PAD, QK_W)), full((1, QK_W)), full((1, DV))],
        out_shape=[jax.ShapeDtypeStruct((T, GLA_COLS), BF16), jax.ShapeDtypeStruct((GATE_PAD, QK_W), F32),
                   jax.ShapeDtypeStruct((1, QK_W), F32), jax.ShapeDtypeStruct((1, DV), F32)],
        scratch_shapes=[pltpu.VMEM((QK_W, DV), F32), pltpu.VMEM((n, DK, DV), F32),
                        pltpu.VMEM((R, QK_W), F32), pltpu.VMEM((R, QK_W), F32)],
        compiler_params=_cp("arbitrary"),
    )(dy, proj, proj, proj, proj, proj, o_st, s_st, w2p, b2.reshape(1, QK_W), norm_g.reshape(1, DV))


def _scan_scratch(R, W):
    return [pltpu.VMEM((W // LANES, R, LANES), F32), pltpu.VMEM((W // LANES, R, LANES), F32),
            pltpu.VMEM((W // LANES, R // SUBLANES, LANES), F32)]


def _scan_rows(a, u, c0, a_ref, u_ref, c_ref, out_ref, reverse):
    R, W = a.shape
    nt = R // SUBLANES
    shift = _shift_up if reverse else _shift_down
    a, u = a.reshape(nt, SUBLANES, W), u.reshape(nt, SUBLANES, W)
    sub = lax.broadcasted_iota(jnp.int32, (nt, SUBLANES, W), 1)
    for k in (1, 2, 4):
        inside = (sub < SUBLANES - k) if reverse else (sub >= k)
        turn = SUBLANES - k if reverse else k
        u = u + a * jnp.where(inside, pltpu.roll(u, turn, axis=1), 0.0)
        a = a * jnp.where(inside, pltpu.roll(a, turn, axis=1), 1.0)
    a, u = a.reshape(R, W), u.reshape(R, W)
    end = 0 if reverse else SUBLANES - 1
    edge = nt - 1 if reverse else 0
    for j in range(W // LANES):
        cols = slice(j * LANES, (j + 1) * LANES)
        a_ref[j] = a[:, cols]
        u_ref[j] = u[:, cols]
        at = a_ref.at[j][pl.ds(end, nt, stride=SUBLANES), :]
        ut = u_ref.at[j][pl.ds(end, nt, stride=SUBLANES), :]
        k = 1
        while k < nt:
            ut = ut + at * shift(ut, k, 0.0)
            at = at * shift(at, k, 1.0)
            k *= 2
        c_ref[j] = shift(ut + at * c0[:, cols], 1, 0.0)
        c_ref[j, edge:edge + 1, :] = c0[:, cols]
        for r in range(nt):
            rows = pl.ds(r * SUBLANES, SUBLANES)
            out_ref[rows, cols] = u_ref[j, rows, :] + a_ref[j, rows, :] * c_ref[j, r:r + 1, :]


def _lru_conv(ext_ref, cw_ref, cb_ref, R):
    xc = cb_ref[...] + ext_ref[pl.ds(SUBLANES - 3, R), :] * cw_ref[0:1, :]
    xc = xc + ext_ref[pl.ds(SUBLANES - 2, R), :] * cw_ref[1:2, :]
    xc = xc + ext_ref[pl.ds(SUBLANES - 1, R), :] * cw_ref[2:3, :]
    return xc + ext_ref[pl.ds(SUBLANES, R), :] * cw_ref[3:4, :]


def _lru_gates(xc, wa, ba, wx, bx, lam, first):
    r = jax.nn.sigmoid(_dot(xc, wa) + ba)
    ig = jax.nn.sigmoid(_dot(xc, wx) + bx)
    sp = jax.nn.softplus(-lam)
    la = (-LRU_C * r) * sp
    a = jnp.exp(la)
    mult = jnp.where(first, 1.0, jnp.sqrt(-_expm1(2.0 * la, a * a)))
    return r, ig, sp, a, mult


def _lru_fwd(proj, y_mix, cw, cb, wa, ba, wx, bx, lam, name):
    T = proj.shape[0]
    R = min(T, ROWS)
    W = LRU_W

    def body(xr_ref, xh_ref, xg_ref, cw_ref, cb_ref, wa_ref, ba_ref, wx_ref, bx_ref, lam_ref, mix_ref,
             y_ref, hs_ref, ext_ref, hc_ref, sa_ref, su_ref, sc_ref):
        del mix_ref
        i = pl.program_id(0)

        @pl.when(i == 0)
        def _():
            hc_ref[...] = jnp.zeros_like(hc_ref)

        ext_ref[0:SUBLANES, :] = jnp.where(i > 0, xh_ref[...], 0.0)
        ext_ref[pl.ds(SUBLANES, R), :] = xr_ref[...]
        xc = _lru_conv(ext_ref, cw_ref, cb_ref, R)
        row = lax.broadcasted_iota(jnp.int32, (R, W), 0)
        first = (row == 0) & (i == 0)
        _, ig, _, a, mult = _lru_gates(xc, wa_ref[...], ba_ref[...], wx_ref[...], bx_ref[...], lam_ref[...], first)
        _scan_rows(a, mult * (ig * xc), hc_ref[0:1, :], sa_ref, su_ref, sc_ref, hs_ref, reverse=False)
        hc_ref[0:1, :] = hs_ref[R - 1:R, :]
        y_ref[...] = (hs_ref[...] * _gelu(xg_ref[...])).astype(y_ref.dtype)

    rb = R // SUBLANES
    full = lambda s: pl.BlockSpec(s, lambda i: (0,) * len(s))
    return pl.pallas_call(
        body, name=name, grid=(T // R,),
        in_specs=[pl.BlockSpec((R, W), lambda i: (i, 0)),
                  pl.BlockSpec((SUBLANES, W), lambda i: (jnp.maximum(i * rb - 1, 0), 0)),
                  pl.BlockSpec((R, W), lambda i: (i, 1)),
                  full((SUBLANES, W)), full((1, W)), full((W, W)), full((1, W)), full((W, W)), full((1, W)),
                  full((1, W)), pl.BlockSpec(memory_space=pl.ANY)],
        out_specs=[pl.BlockSpec((R, W), lambda i: (i, 1)), pl.BlockSpec((R, W), lambda i: (i, 0))],
        out_shape=[jax.ShapeDtypeStruct(y_mix.shape, y_mix.dtype), jax.ShapeDtypeStruct((T, W), F32)],
        scratch_shapes=[pltpu.VMEM((R + SUBLANES, W), F32), pltpu.VMEM((SUBLANES, W), F32),
                        *_scan_scratch(R, W)],
        input_output_aliases={10: 0}, compiler_params=_cp("arbitrary"),
    )(proj, proj, proj, cw, cb, wa, ba, wx, bx, lam, y_mix)


def _lru_bwd(dy, proj, hs, cw, cb, wa, ba, wx, bx, lam, name):
    T = proj.shape[0]
    R = min(T, ROWS)
    W = LRU_W
    nb = T // R

    def body(dy_ref, xr_ref, xh_ref, xg_ref, hs_ref, hh_ref, cw_ref, cb_ref, wa_ref, ba_ref, wx_ref, bx_ref, lam_ref,
             dp_ref, dcw_ref, dvec_ref, dwa_ref, dwx_ref, ext_ref, ext2_ref, lc_ref, sa_ref, su_ref, sc_ref, adj_ref):
        ib = pl.program_id(0)
        i = nb - 1 - ib

        @pl.when(ib == 0)
        def _():
            lc_ref[...] = jnp.zeros_like(lc_ref)
            ext2_ref[pl.ds(R, SUBLANES), :] = jnp.zeros((SUBLANES, W), F32)
            dcw_ref[...] = jnp.zeros_like(dcw_ref)
            dvec_ref[...] = jnp.zeros_like(dvec_ref)
            dwa_ref[...] = jnp.zeros_like(dwa_ref)
            dwx_ref[...] = jnp.zeros_like(dwx_ref)

        ext_ref[0:SUBLANES, :] = jnp.where(i > 0, xh_ref[...], 0.0)
        ext_ref[pl.ds(SUBLANES, R), :] = xr_ref[...]
        xc = _lru_conv(ext_ref, cw_ref, cb_ref, R)
        row = lax.broadcasted_iota(jnp.int32, (R, W), 0)
        first = (row == 0) & (i == 0)
        lam = lam_ref[...]
        r, ig, sp, a, mult = _lru_gates(xc, wa_ref[...], ba_ref[...], wx_ref[...], bx_ref[...], lam, first)
        h = hs_ref[...]
        gel, dgel = _gelu_and_grad(xg_ref[...])
        dy = dy_ref[...].astype(F32)
        dp_ref[:, W:2 * W] = (dy * h * dgel).astype(dp_ref.dtype)
        _scan_rows(_shift_up(a, 1, 1.0), dy * gel, lc_ref[0:1, :], sa_ref, su_ref, sc_ref, adj_ref, reverse=True)
        v = adj_ref[...]
        lc_ref[...] = (a * v)[0:SUBLANES, :]
        hprev = _shift_down(h, 1, 0.0) + jnp.where((row == 0) & (i > 0), hh_ref[SUBLANES - 1:SUBLANES, :], 0.0)
        da = v * hprev
        dmult = jnp.where(first, 0.0, v * (ig * xc))
        dig = v * (mult * xc)
        dxc = v * (mult * ig)
        dla = da * a - dmult * ((a * a) / mult)
        dra = (dla * (-LRU_C * sp)) * (r * (1.0 - r))
        drx = dig * (ig * (1.0 - ig))
        dxc = dxc + _dot_nt(dra, wa_ref[...]) + _dot_nt(drx, wx_ref[...])
        dwa_ref[...] += _dot_tn(xc, dra)
        dwx_ref[...] += _dot_tn(xc, drx)
        dvec_ref[0:1, :] += jnp.sum(dxc, axis=0, keepdims=True)
        dvec_ref[1:2, :] += jnp.sum(dra, axis=0, keepdims=True)
        dvec_ref[2:3, :] += jnp.sum(drx, axis=0, keepdims=True)
        dvec_ref[3:4, :] += jnp.sum(dla * (-LRU_C * r), axis=0, keepdims=True) * (-jax.nn.sigmoid(-lam))
        ext2_ref[pl.ds(0, R), :] = dxc
        dxr = ext2_ref[pl.ds(0, R), :] * cw_ref[3:4, :]
        dxr = dxr + ext2_ref[pl.ds(1, R), :] * cw_ref[2:3, :]
        dxr = dxr + ext2_ref[pl.ds(2, R), :] * cw_ref[1:2, :]
        dxr = dxr + ext2_ref[pl.ds(3, R), :] * cw_ref[0:1, :]
        dp_ref[:, 0:W] = dxr.astype(dp_ref.dtype)
        for j in range(LRU_CONV):
            dcw_ref[j:j + 1, :] += jnp.sum(dxc * ext_ref[pl.ds(SUBLANES - 3 + j, R), :], axis=0, keepdims=True)
        ext2_ref[pl.ds(R, SUBLANES), :] = dxc[0:SUBLANES, :]

    rb = R // SUBLANES
    full = lambda s: pl.BlockSpec(s, lambda i: (0,) * len(s))
    blk = lambda j: pl.BlockSpec((R, W), lambda i: (nb - 1 - i, j))
    halo = pl.BlockSpec((SUBLANES, W), lambda i: (jnp.maximum((nb - 1 - i) * rb - 1, 0), 0))
    return pl.pallas_call(
        body, name=name, grid=(nb,),
        in_specs=[blk(1), blk(0), halo, blk(1), blk(0), halo,
                  full((SUBLANES, W)), full((1, W)), full((W, W)), full((1, W)), full((W, W)), full((1, W)),
                  full((1, W))],
        out_specs=[pl.BlockSpec((R, 2 * W), lambda i: (nb - 1 - i, 0)), full((SUBLANES, W)), full((SUBLANES, W)),
                   full((W, W)), full((W, W))],
        out_shape=[jax.ShapeDtypeStruct((T, 2 * W), BF16), jax.ShapeDtypeStruct((SUBLANES, W), F32),
                   jax.ShapeDtypeStruct((SUBLANES, W), F32), jax.ShapeDtypeStruct((W, W), F32),
                   jax.ShapeDtypeStruct((W, W), F32)],
        scratch_shapes=[pltpu.VMEM((R + SUBLANES, W), F32), pltpu.VMEM((R + SUBLANES, W), F32),
                        pltpu.VMEM((SUBLANES, W), F32), *_scan_scratch(R, W), pltpu.VMEM((R, W), F32)],
        compiler_params=_cp("arbitrary"),
    )(dy, proj, proj, proj, hs, hs, cw, cb, wa, ba, wx, bx, lam)


def _conv3_window(src, start, cs, w, b):
    win = src[pl.ds(start, FFN_RC + SUBLANES), cs]
    x2 = pltpu.roll(win, 2, axis=0)[SUBLANES:]
    x1 = pltpu.roll(win, 1, axis=0)[SUBLANES:]
    x0 = win[SUBLANES:]
    return ((b + x2 * w[0]) + x1 * w[1]) + x0 * w[2], (x2, x1, x0)


def _ffn_up_fwd(u2, fa, fg, cwa, cwg, cba, cbg, name):
    T, D = u2.shape
    Fh = fa.shape[1]
    tm = min(T, ROWS)
    CW, SB, RC = FFN_CW, FFN_SB, FFN_RC
    ns = CW // SB

    def body(u_ref, fa_ref, fg_ref, cwa_ref, cwg_ref, cba_ref, cbg_ref, xa_ref, xg_ref, p_ref, q_ref, act_ref,
             ka_ref, kg_ref, ea_ref, eg_ref, za_ref, zg_ref):
        @pl.when(pl.program_id(1) == 0)
        def _():
            ka_ref[...] = jnp.zeros_like(ka_ref)
            kg_ref[...] = jnp.zeros_like(kg_ref)

        def gate(s):
            cs = pl.ds(s * SB, SB)
            wa = [cwa_ref[j:j + 1, cs] for j in range(FFN_CONV)]
            wg = [cwg_ref[j:j + 1, cs] for j in range(FFN_CONV)]
            ba, bg = cba_ref[:, cs], cbg_ref[:, cs]
            ea_ref[0:SUBLANES, cs] = ka_ref[:, cs]
            ea_ref[pl.ds(SUBLANES, RC), cs] = za_ref[0:RC, cs]
            eg_ref[0:SUBLANES, cs] = kg_ref[:, cs]
            eg_ref[pl.ds(SUBLANES, RC), cs] = zg_ref[0:RC, cs]
            for c in range(tm // RC):
                sa, sg, start = (ea_ref, eg_ref, 0) if c == 0 else (za_ref, zg_ref, c * RC - SUBLANES)
                rows = pl.ds(c * RC, RC)
                a_c, xa = _conv3_window(sa, start, cs, wa, ba)
                g_c, xg = _conv3_window(sg, start, cs, wg, bg)
                gel, dgel = _gelu_and_grad(a_c)
                xa_ref[rows, cs] = xa[2].astype(xa_ref.dtype)
                xg_ref[rows, cs] = xg[2].astype(xg_ref.dtype)
                p_ref[rows, cs] = (g_c * dgel).astype(p_ref.dtype)
                q_ref[rows, cs] = gel.astype(q_ref.dtype)
                act_ref[rows, cs] = (gel * g_c).astype(act_ref.dtype)
            ka_ref[:, cs] = za_ref[tm - SUBLANES:tm, cs]
            kg_ref[:, cs] = zg_ref[tm - SUBLANES:tm, cs]

        za_ref[...] = _dot(u_ref[...], fa_ref[...])
        zg_ref[...] = _dot(u_ref[...], fg_ref[...])
        for s in range(ns):
            gate(s)

    blk = pl.BlockSpec((tm, CW), lambda j, i: (i, j))
    wblk = pl.BlockSpec((D, CW), lambda j, i: (0, j))
    w8 = pl.BlockSpec((SUBLANES, CW), lambda j, i: (0, j))
    w1 = pl.BlockSpec((1, CW), lambda j, i: (0, j))
    return pl.pallas_call(
        body, name=name, grid=(Fh // CW, T // tm),
        in_specs=[pl.BlockSpec((tm, D), lambda j, i: (i, 0)), wblk, wblk, w8, w8, w1, w1],
        out_specs=[blk] * 5,
        out_shape=[jax.ShapeDtypeStruct((T, Fh), BF16)] * 5,
        scratch_shapes=[pltpu.VMEM((SUBLANES, CW), F32), pltpu.VMEM((SUBLANES, CW), F32),
                        pltpu.VMEM((RC + SUBLANES, CW), F32), pltpu.VMEM((RC + SUBLANES, CW), F32),
                        pltpu.VMEM((tm, CW), F32), pltpu.VMEM((tm, CW), F32)],
        compiler_params=_cp("parallel", "arbitrary"),
    )(u2, fa, fg, cwa, cwg, cba, cbg)


def _ffn_bwd_core(dh, dhb, xa, xg, p, q, wdT, faT, fgT, cwa, cwg, h, g, name):
    T, D = dhb.shape
    Fh = xa.shape[1]
    tm = min(T, ROWS)
    CW, SB, RC = FFN_CW, FFN_SB, FFN_RC
    ns = CW // SB
    nj = Fh // CW
    nb = T // tm
    nc = tm // RC

    def body(dh_ref, xa_ref, xg_ref, p_ref, q_ref, wd_ref, fa_ref, fg_ref, cwa_ref, cwg_ref, h_ref, g_ref, res_ref,
             dza_ref, dzg_ref, dca_ref, dcg_ref, dho_ref, dhbo_ref, dg_ref,
             d_ref, sa_ref, sg_ref, ka_ref, kg_ref, du_ref):
        ib, j = pl.program_id(0), pl.program_id(1)

        @pl.when((ib == 0) & (j == 0))
        def _():
            dca_ref[...] = jnp.zeros_like(dca_ref)
            dcg_ref[...] = jnp.zeros_like(dcg_ref)
            dg_ref[...] = jnp.zeros_like(dg_ref)

        @pl.when(ib == 0)
        def _():
            ka_ref[j] = jnp.zeros((SUBLANES, CW), F32)
            kg_ref[j] = jnp.zeros((SUBLANES, CW), F32)

        @pl.when(j == 0)
        def _():
            du_ref[...] = jnp.zeros_like(du_ref)

        def fold(v):
            return jnp.sum(v.reshape(RC // SUBLANES, SUBLANES, SB), axis=0)

        def gate(s):
            cs = pl.ds(s * SB, SB)
            wa = [cwa_ref[t:t + 1, cs] for t in range(FFN_CONV)]
            wg = [cwg_ref[t:t + 1, cs] for t in range(FFN_CONV)]
            sa_ref[pl.ds(tm, SUBLANES), cs] = ka_ref[j, :, cs]
            sg_ref[pl.ds(tm, SUBLANES), cs] = kg_ref[j, :, cs]
            for c in range(nc):
                rows = pl.ds(c * RC, RC)
                dact = d_ref[rows, cs]
                sa_ref[rows, cs] = dact * p_ref[rows, cs].astype(F32)
                sg_ref[rows, cs] = dact * q_ref[rows, cs].astype(F32)
            n = RC + SUBLANES
            for s_ref, x_ref, o_ref, dc_ref, w in ((sa_ref, xa_ref, dza_ref, dca_ref, wa),
                                                   (sg_ref, xg_ref, dzg_ref, dcg_ref, wg)):
                acc = [jnp.zeros((SUBLANES, SB), F32) for _ in range(FFN_CONV + 1)]
                for c in range(nc):
                    rows = pl.ds(c * RC, RC)
                    win = s_ref[pl.ds(c * RC, n), cs]
                    d0 = win[:RC]
                    d1 = pltpu.roll(win, n - 1, axis=0)[:RC]
                    d2 = pltpu.roll(win, n - 2, axis=0)[:RC]
                    o_ref[rows, cs] = ((d0 * w[2] + d1 * w[1]) + d2 * w[0]).astype(o_ref.dtype)
                    x = x_ref[rows, cs].astype(F32)
                    acc = [acc[0] + fold(d2 * x), acc[1] + fold(d1 * x), acc[2] + fold(d0 * x), acc[3] + fold(d0)]
                for t in range(FFN_CONV + 1):
                    dc_ref[j, t:t + 1, cs] += jnp.sum(acc[t], axis=0, keepdims=True)
            ka_ref[j, :, cs] = sa_ref[0:SUBLANES, cs]
            kg_ref[j, :, cs] = sg_ref[0:SUBLANES, cs]

        d_ref[...] = _dot(dh_ref[...], wd_ref[...])
        for s in range(ns):
            gate(s)
        du_ref[...] += _dot(dza_ref[...], fa_ref[...]) + _dot(dzg_ref[...], fg_ref[...])

        @pl.when(j == nj - 1)
        def _():
            dho, dg = _rms_bwd_tile(h_ref[...], g_ref[...], du_ref[...], res_ref[...])
            dg_ref[...] += dg
            dho_ref[...] = dho
            dhbo_ref[...] = dho.astype(dhbo_ref.dtype)

    blk = pl.BlockSpec((tm, CW), lambda ib, j: (nb - 1 - ib, j))
    row = pl.BlockSpec((tm, D), lambda ib, j: (nb - 1 - ib, 0))
    vec = pl.BlockSpec((1, D), lambda ib, j: (0, 0))
    w8 = pl.BlockSpec((SUBLANES, CW), lambda ib, j: (0, j))
    wrow = pl.BlockSpec((CW, D), lambda ib, j: (j, 0))
    acc = pl.BlockSpec((nj, SUBLANES, CW), lambda ib, j: (0, 0, 0))
    return pl.pallas_call(
        body, name=name, grid=(nb, nj),
        in_specs=[row, blk, blk, blk, blk, pl.BlockSpec((D, CW), lambda ib, j: (0, j)), wrow, wrow, w8, w8,
                  row, vec, row],
        out_specs=[blk, blk, acc, acc, row, row, vec],
        out_shape=[jax.ShapeDtypeStruct((T, Fh), BF16), jax.ShapeDtypeStruct((T, Fh), BF16),
                   jax.ShapeDtypeStruct((nj, SUBLANES, CW), F32), jax.ShapeDtypeStruct((nj, SUBLANES, CW), F32),
                   jax.ShapeDtypeStruct((T, D), F32), jax.ShapeDtypeStruct((T, D), BF16),
                   jax.ShapeDtypeStruct((1, D), F32)],
        scratch_shapes=[pltpu.VMEM((tm, CW), F32), pltpu.VMEM((tm + SUBLANES, CW), F32),
                        pltpu.VMEM((tm + SUBLANES, CW), F32), pltpu.VMEM((nj, SUBLANES, CW), F32),
                        pltpu.VMEM((nj, SUBLANES, CW), F32), pltpu.VMEM((tm, D), F32)],
        compiler_params=_cp("arbitrary", "arbitrary"),
    )(dhb, xa, xg, p, q, wdT, faT, fgT, cwa, cwg, h, g.reshape(1, D), dh)


def _adamw(w, g, m, v, name):
    rows, cols = w.shape
    tr = _row_tile(rows, max(SUBLANES, min(512, TILE_BYTES // (4 * cols)) // SUBLANES * SUBLANES))

    def body(w_ref, g_ref, m_ref, v_ref, d_ref, mo_ref, vo_ref):
        g = g_ref[...]
        mm = ADAM_B1 * m_ref[...] + (1.0 - ADAM_B1) * g
        vv = ADAM_B2 * v_ref[...] + (1.0 - ADAM_B2) * (g * g)
        m_hat = mm / (1.0 - ADAM_B1 ** ADAM_STEP)
        v_hat = vv / (1.0 - ADAM_B2 ** ADAM_STEP)
        d_ref[...] = -ADAM_LR * (m_hat / (jnp.sqrt(v_hat) + ADAM_EPS) + ADAM_WD * w_ref[...])
        mo_ref[...] = mm
        vo_ref[...] = vv

    blk = pl.BlockSpec((tr, cols), lambda i: (i, 0))
    return pl.pallas_call(
        body, name=name, grid=(rows // tr,), in_specs=[blk] * 4, out_specs=[blk] * 3,
        out_shape=[jax.ShapeDtypeStruct((rows, cols), F32)] * 3, compiler_params=_cp("parallel"),
    )(w, g, m, v)


def _add_slabs(a, b, out_dtype, name):
    n, rows, cols = a.shape
    tr = _row_tile(rows, max(SUBLANES, min(512, TILE_BYTES // (4 * cols)) // SUBLANES * SUBLANES))

    def body(a_ref, b_ref, o_ref):
        o_ref[...] = (a_ref[...] + b_ref[...]).astype(o_ref.dtype)

    blk = pl.BlockSpec((1, tr, cols), lambda k, i: (k, i, 0))
    return pl.pallas_call(
        body, name=name, grid=(n, rows // tr), in_specs=[blk, blk], out_specs=blk,
        out_shape=jax.ShapeDtypeStruct((n, rows, cols), out_dtype), compiler_params=_cp("parallel", "parallel"),
    )(a, b)


def _sum_leading(parts, name, last=None):
    n, rows, cols = parts.shape
    tr = _row_tile(rows, max(SUBLANES, min(512, TILE_BYTES // (4 * cols)) // SUBLANES * SUBLANES))

    def body(*refs):
        p_ref, o_ref = refs[0], refs[-1]
        acc = p_ref[0].astype(F32)
        for d in range(1, n):
            acc = acc + p_ref[d].astype(F32)
        if last is not None:
            acc = acc + refs[1][...].astype(F32)
        o_ref[...] = acc

    blk = pl.BlockSpec((tr, cols), lambda i: (i, 0))
    return pl.pallas_call(
        body, name=name, grid=(rows // tr,),
        in_specs=[pl.BlockSpec((n, tr, cols), lambda i: (0, i, 0))] + ([] if last is None else [blk]), out_specs=blk,
        out_shape=jax.ShapeDtypeStruct((rows, cols), F32), compiler_params=_cp("parallel"),
    )(*((parts,) if last is None else (parts, last)))


def _row_tile(rows, cap=512):
    if rows <= cap:
        return rows
    return max(t for t in range(SUBLANES, cap + 1, SUBLANES) if rows % t == 0)


def _place():
    return lax.axis_index("x"), lax.axis_index("y"), lax.axis_index("c")


def _gather_shards(arrs, name):
    n = len(arrs)

    def body(*refs):
        ins, outs = refs[:n], refs[n:2 * n]
        send_sems, recv_sems, pass_send, pass_recv = refs[2 * n:]
        x, y, c = _place()
        chips = [(1 - x, y), (x, 1 - y), (1 - x, 1 - y)]
        mine, theirs = c, 1 - c

        def half(a, which):
            hl = ins[a].shape[0] // 2
            return pl.ds(which * hl, hl)

        def send(a, j, shard):
            px, py = chips[j]
            return pltpu.make_async_remote_copy(
                src_ref=ins[a].at[half(a, mine)], dst_ref=outs[a].at[shard, half(a, mine)],
                send_sem=send_sems.at[3 * a + j], recv_sem=recv_sems.at[3 * a + j], device_id=(px, py, c),
                device_id_type=MESH)

        def passed(a, j, which):
            px, py = chips[j]
            blk = outs[a].at[2 * px + py, half(a, which)]
            return pltpu.make_async_remote_copy(
                src_ref=blk, dst_ref=blk, send_sem=pass_send.at[3 * a + j], recv_sem=pass_recv.at[3 * a + j],
                device_id=(x, y, 1 - c), device_id_type=MESH)

        sends = [send(a, j, 2 * x + y) for a in range(n) for j in range(3)]
        for cp in sends:
            cp.start()
        passes = []
        for a in range(n):
            for j, (px, py) in enumerate(chips):
                send(a, j, 2 * px + py).wait_recv()
                passes.append(passed(a, j, mine))
                passes[-1].start()
        for a in range(n):
            for j in range(3):
                passed(a, j, theirs).wait_recv()
        for cp in sends + passes:
            cp.wait_send()

    hbm = pl.BlockSpec(memory_space=pl.ANY)
    return pl.pallas_call(
        body, name=name, in_specs=[hbm] * n, out_specs=[hbm] * n,
        out_shape=[jax.ShapeDtypeStruct((N_CHIPS,) + a.shape, a.dtype) for a in arrs],
        scratch_shapes=[pltpu.SemaphoreType.DMA((3 * n,)), pltpu.SemaphoreType.DMA((3 * n,)),
                        pltpu.SemaphoreType.DMA((3 * n,)), pltpu.SemaphoreType.DMA((3 * n,))],
        compiler_params=pltpu.CompilerParams(has_side_effects=True),
    )(*arrs)


def _other_half_to_sibling(slabs, name):
    n = len(slabs)

    def body(*refs):
        ins, got = refs[:n], refs[n:2 * n]
        send_sems, recv_sems = refs[2 * n:]
        x, y, c = _place()
        copies = [pltpu.make_async_remote_copy(
            src_ref=ins[a].at[:, 1 - c], dst_ref=got[a], send_sem=send_sems.at[a], recv_sem=recv_sems.at[a],
            device_id=(x, y, 1 - c), device_id_type=MESH) for a in range(n)]
        for cp in copies:
            cp.start()
        for cp in copies:
            cp.wait()

    hbm = pl.BlockSpec(memory_space=pl.ANY)
    return pl.pallas_call(
        body, name=name, in_specs=[hbm] * n, out_specs=[hbm] * n,
        out_shape=[jax.ShapeDtypeStruct((s.shape[0],) + s.shape[2:], s.dtype) for s in slabs],
        scratch_shapes=[pltpu.SemaphoreType.DMA((n,)), pltpu.SemaphoreType.DMA((n,))],
        compiler_params=pltpu.CompilerParams(has_side_effects=True),
    )(*slabs)


def _exchange_grads(slabs, small, name):
    n = len(slabs)

    def body(*refs):
        ins, small_ref = refs[:n], refs[n]
        outs, small_out = refs[n + 1:2 * n + 1], refs[2 * n + 1]
        send_sems, recv_sems, ssend, srecv = refs[2 * n + 2:]
        x, y, c = _place()
        chips = [(1 - x, y), (x, 1 - y), (1 - x, 1 - y)]
        me = 4 * x + 2 * y + c
        flips = [(fx, fy, fc) for fx in (0, 1) for fy in (0, 1) for fc in (0, 1)][1:]

        def copy(a, j):
            px, py = chips[j]
            return pltpu.make_async_remote_copy(
                src_ref=ins[a].at[2 * px + py], dst_ref=outs[a].at[j], send_sem=send_sems.at[3 * a + j],
                recv_sem=recv_sems.at[3 * a + j], device_id=(px, py, c), device_id_type=MESH)

        def scopy(k, row):
            fx, fy, fc = flips[k]
            return pltpu.make_async_remote_copy(
                src_ref=small_ref, dst_ref=small_out.at[row], send_sem=ssend.at[k], recv_sem=srecv.at[k],
                device_id=(x ^ fx, y ^ fy, c ^ fc), device_id_type=MESH)

        sends = [copy(a, j) for a in range(n) for j in range(3)] + [scopy(k, me) for k in range(7)]
        for cp in sends:
            cp.start()
        for k, (fx, fy, fc) in enumerate(flips):
            scopy(k, 4 * (x ^ fx) + 2 * (y ^ fy) + (c ^ fc)).wait_recv()
        for a in range(n):
            for j in range(3):
                copy(a, j).wait_recv()
        for cp in sends:
            cp.wait_send()

    hbm = pl.BlockSpec(memory_space=pl.ANY)
    return pl.pallas_call(
        body, name=name, in_specs=[hbm] * (n + 1), out_specs=[hbm] * (n + 1),
        out_shape=[jax.ShapeDtypeStruct((3,) + s.shape[1:], s.dtype) for s in slabs]
        + [jax.ShapeDtypeStruct((N_DEV,) + small.shape, small.dtype)],
        scratch_shapes=[pltpu.SemaphoreType.DMA((3 * n,)), pltpu.SemaphoreType.DMA((3 * n,)),
                        pltpu.SemaphoreType.DMA((7,)), pltpu.SemaphoreType.DMA((7,))],
        compiler_params=pltpu.CompilerParams(has_side_effects=True),
    )(*slabs, small)


def _swap_with_sibling(arrs, name):
    n = len(arrs)

    def body(*refs):
        ins, outs = refs[:n], refs[n:2 * n]
        send_sems, recv_sems = refs[2 * n:]
        x, y, c = _place()
        copies = [pltpu.make_async_remote_copy(
            src_ref=ins[a], dst_ref=outs[a], send_sem=send_sems.at[a], recv_sem=recv_sems.at[a],
            device_id=(x, y, 1 - c), device_id_type=MESH) for a in range(n)]
        for cp in copies:
            cp.start()
        for cp in copies:
            cp.wait()

    hbm = pl.BlockSpec(memory_space=pl.ANY)
    return pl.pallas_call(
        body, name=name, in_specs=[hbm] * n, out_specs=[hbm] * n,
        out_shape=[jax.ShapeDtypeStruct(a.shape, a.dtype) for a in arrs],
        scratch_shapes=[pltpu.SemaphoreType.DMA((n,)), pltpu.SemaphoreType.DMA((n,))],
        compiler_params=pltpu.CompilerParams(has_side_effects=True),
    )(*arrs)


def _block_diag(w):
    eye = jnp.eye(LRU_BLOCKS, dtype=w.dtype)
    return (eye[:, None, :, None] * w[:, :, None, :]).reshape(LRU_W, LRU_W)


def _diag_blocks(m):
    m4 = m.reshape(LRU_BLOCKS, LRU_BLOCK, LRU_BLOCKS, LRU_BLOCK)
    return jnp.stack([m4[b, :, b, :] for b in range(LRU_BLOCKS)])


def _pad_rows(a, rows):
    return jnp.pad(a, ((0, rows - a.shape[0]), (0, 0)))


def _layer_weights(p, l):
    w_in = p["w_in"][l]
    n_gla = 2 * QK_W + 2 * GLA_W
    gate = jnp.pad(w_in[:, n_gla:n_gla + GATE_RANK], ((0, 0), (0, GATE_PAD - GATE_RANK)))
    wg = jnp.concatenate([w_in[:, :n_gla], gate], axis=1)
    wl = w_in[:, n_gla + GATE_RANK:]
    w_out = p["w_out"][l]
    fa, fg = p["ffn_w_in"][l][:, :FFN_H], p["ffn_w_in"][l][:, FFN_H:]
    wd = p["ffn_w_down"][l]
    return dict(
        wg=wg, wl=wl, wgT=wg.T, wlT=wl.T, wo=w_out, woT=w_out.T,
        fa=fa, fg=fg, faT=fa.T, fgT=fg.T, wd=wd, wdT=wd.T,
        w2p=_pad_rows(p["gla_gate_w2"][l], GATE_PAD).astype(BF16),
        wa=_block_diag(p["lru_wa"][l]).astype(BF16), wx=_block_diag(p["lru_wx"][l]).astype(BF16),
        lcw=_pad_rows(p["lru_conv_w"][l], SUBLANES),
        fcwa=_pad_rows(p["ffn_conv_w"][l][:, :FFN_H], SUBLANES), fcwg=_pad_rows(p["ffn_conv_w"][l][:, FFN_H:], SUBLANES),
    )


def _local_step(x, tgt, p):
    row = lambda v: v.reshape(1, -1)
    h = x
    u = _rms_fwd(h, p["ln_mix"][0], "mix_norm_fwd0")
    stash = []
    for l in range(DEPTH):
        w = _layer_weights(p, l)
        s = dict(w=w, h0=h)
        pg = _mm(u, w["wg"], None, F32, f"proj_gla_fwd{l}")
        plr = _mm(u, w["wl"], None, F32, f"proj_lru_fwd{l}")
        ym, o_st, s_st = _gla_fwd(pg, w["w2p"], p["gla_gate_b"][l], p["gla_norm"][l], f"gla_fwd{l}")
        ym, hs = _lru_fwd(plr, ym, w["lcw"], row(p["lru_conv_b"][l]), w["wa"], row(p["lru_ba"][l]), w["wx"],
                          row(p["lru_bx"][l]), row(p["lru_lambda"][l]), f"lru_fwd{l}")
        h, u2 = _mm(ym, w["wo"], h, F32, f"out_fwd{l}", norm_g=p["ln_ffn"][l])
        s.update(u=u, pg=pg, plr=plr, ym=ym, o_st=o_st, s_st=s_st, hs=hs, h1=h)
        cba, cbg = row(p["ffn_conv_b"][l][:FFN_H]), row(p["ffn_conv_b"][l][FFN_H:])
        *kept, act = _ffn_up_fwd(u2, w["fa"], w["fg"], w["fcwa"], w["fcwg"], cba, cbg, f"ffn_up_fwd{l}")
        if l + 1 < DEPTH:
            h, u = _mm(act, w["wd"], h, F32, f"ffn_down_fwd{l}", norm_g=p["ln_mix"][l + 1])
        else:
            h = _mm(act, w["wd"], h, F32, f"ffn_down_fwd{l}")
        s.update(u2=u2, ffn_kept=kept, act=act)
        stash.append(s)

    loss, dh, dhb, d_ln_final = _loss_head(h, p["ln_final"], tgt, "loss_head")

    g = {k: [None] * DEPTH for k in ("ln_mix", "w_in", "gla_gate_w2", "gla_gate_b", "gla_norm", "lru_conv_w",
                                     "lru_conv_b", "lru_wa", "lru_ba", "lru_wx", "lru_bx", "lru_lambda",
                                     "ln_ffn", "ffn_conv_w", "ffn_conv_b")}
    slab = dict(w_out=lax.empty((N_CHIPS, DEPTH * D_MODEL // N_CHIPS, D_MODEL), F32),
                ffn_w_in=lax.empty((N_CHIPS, DEPTH * D_MODEL, 2 * FFN_H // N_CHIPS), F32),
                ffn_w_down=lax.empty((N_CHIPS, DEPTH * FFN_H // N_CHIPS, D_MODEL), F32))
    n_gla = 2 * QK_W + 2 * GLA_W
    for l in reversed(range(DEPTH)):
        s = stash[l]
        w = s["w"]
        slab["ffn_w_down"] = _mm_tn_into(slab["ffn_w_down"], s["act"], dhb, l, 0, f"ffn_down_dw{l}",
                                         tk=FFN_H // N_CHIPS, tn=D_MODEL, tm=2048)
        dza, dzg, dca, dcg, dh, dhb, dln = _ffn_bwd_core(
            dh, dhb, *s["ffn_kept"], w["wdT"], w["faT"], w["fgT"], w["fcwa"], w["fcwg"], s["h1"], p["ln_ffn"][l],
            f"ffn_bwd_core{l}")
        dca, dcg = (jnp.moveaxis(d, 0, 1).reshape(SUBLANES, FFN_H) for d in (dca, dcg))
        g["ffn_conv_w"][l] = jnp.concatenate([dca[:FFN_CONV], dcg[:FFN_CONV]], axis=1)
        g["ffn_conv_b"][l] = jnp.concatenate([dca[FFN_CONV], dcg[FFN_CONV]])
        for half, dz in enumerate((dza, dzg)):
            slab["ffn_w_in"] = _mm_tn_into(slab["ffn_w_in"], s["u2"], dz, l, 2 * half, f"ffn_in_dw{l}_{half}",
                                           tk=D_MODEL, tn=2 * FFN_H // N_CHIPS)
        g["ln_ffn"][l] = dln[0]
        slab["w_out"] = _mm_tn_into(slab["w_out"], s["ym"], dhb, l, 0, f"out_dw{l}",
                                    tk=D_MODEL // N_CHIPS, tn=D_MODEL, tm=2048)
        dyc = _mm(dhb, w["woT"], None, F32, f"out_dx{l}")
        dpg, dw2, db2, dng = _gla_bwd(dyc, s["pg"], s["o_st"], s["s_st"], w["w2p"], p["gla_gate_b"][l],
                                      p["gla_norm"][l], f"gla_bwd{l}")
        dpl, dcw, dvec, dwa, dwx = _lru_bwd(dyc, s["plr"], s["hs"], w["lcw"], row(p["lru_conv_b"][l]), w["wa"],
                                            row(p["lru_ba"][l]), w["wx"], row(p["lru_bx"][l]),
                                            row(p["lru_lambda"][l]), f"lru_bwd{l}")
        g["gla_gate_w2"][l] = dw2[:GATE_RANK]
        g["gla_gate_b"][l] = db2[0]
        g["gla_norm"][l] = dng[0]
        g["lru_conv_w"][l] = dcw[:LRU_CONV]
        g["lru_conv_b"][l], g["lru_ba"][l], g["lru_bx"][l], g["lru_lambda"][l] = dvec[0], dvec[1], dvec[2], dvec[3]
        g["lru_wa"][l], g["lru_wx"][l] = _diag_blocks(dwa), _diag_blocks(dwx)
        dwg = _mm_tn(s["u"], dpg, f"proj_gla_dw{l}")
        dwl = _mm_tn(s["u"], dpl, f"proj_lru_dw{l}")
        g["w_in"][l] = jnp.concatenate([dwg[:, :n_gla + GATE_RANK], dwl], axis=1)
        dh, dhb, dln = _mm_rms_bwd([(dpg, w["wgT"]), (dpl, w["wlT"])], s["h0"], p["ln_mix"][l], dh, f"proj_dx{l}")
        g["ln_mix"][l] = dln[0]
    grads = {k: jnp.stack(v) for k, v in g.items()}
    grads["w_in"] = _slabs_from_whole("w_in", grads["w_in"])
    grads.update(slab)
    grads["ln_final"] = d_ln_final[0]
    return loss, dh, grads


BIG = ("w_in", "w_out", "ffn_w_in", "ffn_w_down")
COL_SHARDED = ("w_in", "ffn_w_in", "gla_gate_w2", "lru_conv_w", "ffn_conv_w")
SMALL = ("ln_mix", "gla_gate_w2", "gla_gate_b", "gla_norm", "lru_conv_w", "lru_conv_b", "lru_wa", "lru_ba", "lru_wx",
         "lru_bx", "lru_lambda", "ln_ffn", "ffn_conv_w", "ffn_conv_b", "ln_final")
WEIGHTS = ("ln_mix", "w_in", "gla_gate_w2", "gla_gate_b", "gla_norm", "lru_conv_w", "lru_conv_b", "lru_wa", "lru_ba",
           "lru_wx", "lru_bx", "lru_lambda", "w_out", "ln_ffn", "ffn_w_in", "ffn_conv_w", "ffn_conv_b", "ffn_w_down",
           "ln_final")
PACK = SUBLANES * LANES


def _whole_from_shards(name, g):
    if name in COL_SHARDED:
        return jnp.moveaxis(g, 0, -2).reshape(g.shape[1:-1] + (N_CHIPS * g.shape[-1],))
    return jnp.moveaxis(g, 0, 1).reshape((g.shape[1], N_CHIPS * g.shape[2]) + g.shape[3:])


def _slabs_from_whole(name, w):
    L, r, c = w.shape
    if name in COL_SHARDED:
        s = jnp.moveaxis(w.reshape(L, r, N_CHIPS, c // N_CHIPS), 2, 0)
    else:
        s = jnp.moveaxis(w.reshape(L, N_CHIPS, r // N_CHIPS, c), 1, 0)
    return s.reshape(N_CHIPS, -1, s.shape[-1])


def _pack(arrs):
    flat = []
    for a in arrs:
        f = a.reshape(-1)
        flat.append(jnp.pad(f, (0, (-f.shape[0]) % PACK)))
    return jnp.concatenate(flat).reshape(-1, LANES)


def _unpack(packed, shapes):
    out, at = [], 0
    flat = packed.reshape(-1)
    for s in shapes:
        size = math.prod(s)
        out.append(flat[at:at + size].reshape(s))
        at += size + (-size) % PACK
    return out


def kernel(x, ln_mix, w_in, gla_gate_w2, gla_gate_b, gla_norm, lru_conv_w, lru_conv_b, lru_wa, lru_ba, lru_wx, lru_bx, lru_lambda, w_out, ln_ffn, ffn_w_in, ffn_conv_w, ffn_conv_b, ffn_w_down, ln_final, loss_target, m_ln_mix, m_w_in, m_gla_gate_w2, m_gla_gate_b, m_gla_norm, m_lru_conv_w, m_lru_conv_b, m_lru_wa, m_lru_ba, m_lru_wx, m_lru_bx, m_lru_lambda, m_w_out, m_ln_ffn, m_ffn_w_in, m_ffn_conv_w, m_ffn_conv_b, m_ffn_w_down, m_ln_final, v_ln_mix, v_w_in, v_gla_gate_w2, v_gla_gate_b, v_gla_norm, v_lru_conv_w, v_lru_conv_b, v_lru_wa, v_lru_ba, v_lru_wx, v_lru_bx, v_lru_lambda, v_w_out, v_ln_ffn, v_ffn_w_in, v_ffn_conv_w, v_ffn_conv_b, v_ffn_w_down, v_ln_final):
    w = dict(ln_mix=ln_mix, w_in=w_in, gla_gate_w2=gla_gate_w2, gla_gate_b=gla_gate_b, gla_norm=gla_norm,
             lru_conv_w=lru_conv_w, lru_conv_b=lru_conv_b, lru_wa=lru_wa, lru_ba=lru_ba, lru_wx=lru_wx, lru_bx=lru_bx,
             lru_lambda=lru_lambda, w_out=w_out, ln_ffn=ln_ffn, ffn_w_in=ffn_w_in, ffn_conv_w=ffn_conv_w,
             ffn_conv_b=ffn_conv_b, ffn_w_down=ffn_w_down, ln_final=ln_final)
    m = dict(ln_mix=m_ln_mix, w_in=m_w_in, gla_gate_w2=m_gla_gate_w2, gla_gate_b=m_gla_gate_b, gla_norm=m_gla_norm,
             lru_conv_w=m_lru_conv_w, lru_conv_b=m_lru_conv_b, lru_wa=m_lru_wa, lru_ba=m_lru_ba, lru_wx=m_lru_wx,
             lru_bx=m_lru_bx, lru_lambda=m_lru_lambda, w_out=m_w_out, ln_ffn=m_ln_ffn, ffn_w_in=m_ffn_w_in,
             ffn_conv_w=m_ffn_conv_w, ffn_conv_b=m_ffn_conv_b, ffn_w_down=m_ffn_w_down, ln_final=m_ln_final)
    v = dict(ln_mix=v_ln_mix, w_in=v_w_in, gla_gate_w2=v_gla_gate_w2, gla_gate_b=v_gla_gate_b, gla_norm=v_gla_norm,
             lru_conv_w=v_lru_conv_w, lru_conv_b=v_lru_conv_b, lru_wa=v_lru_wa, lru_ba=v_lru_ba, lru_wx=v_lru_wx,
             lru_bx=v_lru_bx, lru_lambda=v_lru_lambda, w_out=v_w_out, ln_ffn=v_ln_ffn, ffn_w_in=v_ffn_w_in,
             ffn_conv_w=v_ffn_conv_w, ffn_conv_b=v_ffn_conv_b, ffn_w_down=v_ffn_w_down, ln_final=v_ln_final)

    sharded = BIG + ("gla_gate_w2", "lru_conv_w", "ffn_conv_w")
    chip = 2 * lax.axis_index("x") + lax.axis_index("y")
    core = lax.axis_index("c")
    shards = [w[k].astype(MXU_DTYPE) if k in BIG else w[k] for k in sharded]
    gathered = _gather_shards(shards, "gather_weights")
    p = dict(w)
    for k, gk, own in zip(sharded, gathered, shards):
        p[k] = _whole_from_shards(k, lax.dynamic_update_index_in_dim(gk, own, chip, 0))

    loss, grad_x, grads = _local_step(x[0], loss_target[0], p)
    loss = lax.psum(loss[0, 0], ("x", "y", "c"))

    slabs = [grads[k].reshape(N_CHIPS, 2, grads[k].shape[1] // 2, grads[k].shape[2]) for k in BIG]
    got = _other_half_to_sibling(slabs, "other_half_to_sibling")
    kept = [lax.dynamic_index_in_dim(s, core, 1, keepdims=False) for s in slabs]
    chip_half = [_add_slabs(a, b, BF16, f"core_sum_{k}") for k, a, b in zip(BIG, kept, got)]
    small = _pack([grads[k] for k in SMALL])
    *recv, small_all = _exchange_grads(chip_half, small, "exchange_grads")
    own = [lax.dynamic_index_in_dim(h, chip, 0, keepdims=False) for h in chip_half]
    mine = [_sum_leading(r, f"chip_sum_{k}", last=o) for k, r, o in zip(BIG, recv, own)]
    theirs = _swap_with_sibling(mine, "swap_core_halves")
    big_g = [jnp.concatenate([jnp.where(core == 0, a, b), jnp.where(core == 0, b, a)]) for a, b in zip(mine, theirs)]
    small_all = lax.dynamic_update_index_in_dim(small_all, small, 2 * chip + core, 0)
    small_sum = _unpack(_sum_leading(small_all, "sum_small_grads"), [grads[k].shape for k in SMALL])

    me = 2 * lax.axis_index("x") + lax.axis_index("y")
    out_g, out_d, out_m, out_v = {}, {}, {}, {}
    for k, gk in zip(BIG, big_g):
        shape = w[k].shape
        cols = shape[-1]
        res = _adamw(w[k].reshape(-1, cols), gk, m[k].reshape(-1, cols), v[k].reshape(-1, cols), f"adamw_{k}")
        out_g[k] = gk.reshape(shape)
        out_d[k], out_m[k], out_v[k] = [r.reshape(shape) for r in res]
    small_g = []
    for k, gk in zip(SMALL, small_sum):
        if k in COL_SHARDED:
            width = w[k].shape[-1]
            gk = lax.dynamic_slice_in_dim(gk, me * width, width, axis=gk.ndim - 1)
        small_g.append(gk)
    shapes = [w[k].shape for k in SMALL]
    res = _adamw(_pack([w[k] for k in SMALL]), _pack(small_g), _pack([m[k] for k in SMALL]),
                 _pack([v[k] for k in SMALL]), "adamw_small")
    out_g.update(zip(SMALL, small_g))
    for out, packed in zip((out_d, out_m, out_v), res):
        for k, a in zip(SMALL, _unpack(packed, shapes)):
            out[k] = a
    return (loss, grad_x[None], *[out_g[k] for k in WEIGHTS], *[out_d[k] for k in WEIGHTS],
            *[out_m[k] for k in WEIGHTS], *[out_v[k] for k in WEIGHTS])
```

```python
import math

import jax
import jax.numpy as jnp
from jax import lax
from jax.experimental import pallas as pl
from jax.experimental.pallas import tpu as pltpu

F32 = jnp.float32
BF16 = jnp.bfloat16
MXU_DTYPE = BF16

D_MODEL = 1024
DEPTH = 4
HEADS, DK, DV, CHUNK, GATE_RANK = 4, 64, 128, 64, 16
QK_W = HEADS * DK
GLA_W = HEADS * DV
LRU_W = 512
LRU_BLOCKS, LRU_BLOCK, LRU_CONV, LRU_C = 8, 64, 4, 8.0
FFN_H = 3 * D_MODEL
FFN_CONV = 3
EPS = 1e-6
GATE_PAD = 128
GLA_COLS = 2 * QK_W + 2 * GLA_W + GATE_PAD
LRU_COLS = 2 * LRU_W
ADAM_LR, ADAM_B1, ADAM_B2, ADAM_EPS, ADAM_WD, ADAM_STEP = 0.001, 0.9, 0.999, 1e-08, 0.01, 10

LANES = 128
SUBLANES = 8
VMEM_LIMIT = 56 * 1024 * 1024
ROWS = 512
TILE_BYTES = 1 << 20
FFN_CW = 1024
FFN_SB = 256
FFN_RC = 32
N_CHIPS = 4
N_DEV = 8
MESH = pl.DeviceIdType.MESH


def _cp(*sem):
    return pltpu.CompilerParams(dimension_semantics=sem, vmem_limit_bytes=VMEM_LIMIT)


def _dot(a, b):
    return jnp.dot(a.astype(MXU_DTYPE), b.astype(MXU_DTYPE), preferred_element_type=F32)


def _dot_nt(a, b):
    return lax.dot_general(a.astype(MXU_DTYPE), b.astype(MXU_DTYPE), (((1,), (1,)), ((), ())),
                           preferred_element_type=F32)


def _dot_tn(a, b):
    return lax.dot_general(a.astype(MXU_DTYPE), b.astype(MXU_DTYPE), (((0,), (0,)), ((), ())),
                           preferred_element_type=F32)


def _bdot(eq, a, b):
    return jnp.einsum(eq, a, b, preferred_element_type=F32)


def _split3(x):
    x1 = x.astype(BF16)
    r1 = x - x1.astype(F32)
    x2 = r1.astype(BF16)
    x3 = (r1 - x2.astype(F32)).astype(BF16)
    return x1, x2, x3


GELU_C = math.sqrt(2.0 / math.pi)
GELU_A = 0.044715


def _gelu(x):
    return x * (0.5 * (1.0 + jnp.tanh(GELU_C * (x + GELU_A * (x * x * x)))))


def _gelu_and_grad(x):
    x2 = x * x
    t = jnp.tanh(x * (GELU_C + (GELU_C * GELU_A) * x2))
    cdf = 0.5 * t + 0.5
    half_sech2 = 0.5 - 0.5 * (t * t)
    return x * cdf, cdf + (x * half_sech2) * (GELU_C + (3.0 * GELU_C * GELU_A) * x2)


EXPM1_SERIES_BELOW = 0.1


def _expm1(x, exp_x):
    small = x * (1.0 + x * (0.5 + x * (1.0 / 6.0 + x * (1.0 / 24.0 + x * (1.0 / 120.0)))))
    return jnp.where(jnp.abs(x) < EXPM1_SERIES_BELOW, small, exp_x - 1.0)


def _shift_down(x, k, fill):
    row = lax.broadcasted_iota(jnp.int32, x.shape, 0)
    return jnp.where(row >= k, pltpu.roll(x, k, axis=0), fill)


def _shift_up(x, k, fill):
    n = x.shape[0]
    row = lax.broadcasted_iota(jnp.int32, x.shape, 0)
    return jnp.where(row < n - k, pltpu.roll(x, n - k, axis=0), fill)


def _rms_fwd(h, g, name):
    T, D = h.shape
    R = min(T, ROWS)

    def body(h_ref, g_ref, o_ref):
        x = h_ref[...]
        r = lax.rsqrt(jnp.mean(x * x, axis=-1, keepdims=True) + EPS)
        o_ref[...] = ((x * r) * g_ref[...]).astype(o_ref.dtype)

    return pl.pallas_call(
        body, name=name, grid=(T // R,),
        in_specs=[pl.BlockSpec((R, D), lambda i: (i, 0)), pl.BlockSpec((1, D), lambda i: (0, 0))],
        out_specs=pl.BlockSpec((R, D), lambda i: (i, 0)),
        out_shape=jax.ShapeDtypeStruct((T, D), BF16), compiler_params=_cp("parallel"),
    )(h, g.reshape(1, D))


def _rms_bwd_tile(x, g, du, dres):
    r = lax.rsqrt(jnp.mean(x * x, axis=-1, keepdims=True) + EPS)
    xhat = x * r
    dxhat = du * g
    dx = r * (dxhat - xhat * jnp.mean(dxhat * xhat, axis=-1, keepdims=True))
    return dres + dx, jnp.sum(du * xhat, axis=0, keepdims=True)


def _mm_rms_bwd(pairs, h, g, dres, name, tm=512):
    M, D = h.shape
    tm = min(tm, M)
    n = len(pairs)

    def body(*refs):
        h_ref, g_ref, dres_ref = refs[2 * n:2 * n + 3]
        dh_ref, dhb_ref, dg_ref = refs[2 * n + 3:]

        @pl.when(pl.program_id(0) == 0)
        def _():
            dg_ref[...] = jnp.zeros_like(dg_ref)

        du = _dot(refs[0][...], refs[1][...])
        for k in range(1, n):
            du = du + _dot(refs[2 * k][...], refs[2 * k + 1][...])
        dh, dg = _rms_bwd_tile(h_ref[...], g_ref[...], du, dres_ref[...])
        dg_ref[...] += dg
        dh_ref[...] = dh
        dhb_ref[...] = dh.astype(dhb_ref.dtype)

    in_specs, args = [], []
    for a, b in pairs:
        in_specs += [pl.BlockSpec((tm, a.shape[1]), lambda i: (i, 0)), pl.BlockSpec(b.shape, lambda i: (0, 0))]
        args += [a, b]
    blk = pl.BlockSpec((tm, D), lambda i: (i, 0))
    vec = pl.BlockSpec((1, D), lambda i: (0, 0))
    return pl.pallas_call(
        body, name=name, grid=(M // tm,), in_specs=in_specs + [blk, vec, blk], out_specs=[blk, blk, vec],
        out_shape=[jax.ShapeDtypeStruct((M, D), F32), jax.ShapeDtypeStruct((M, D), BF16),
                   jax.ShapeDtypeStruct((1, D), F32)],
        compiler_params=_cp("arbitrary"),
    )(*args, h, g.reshape(1, D), dres)


def _loss_head(h, g, tgt, name):
    T, D = h.shape
    R = min(T, ROWS)

    def body(h_ref, g_ref, t_ref, loss_ref, dh_ref, dhb_ref, dg_ref):
        @pl.when(pl.program_id(0) == 0)
        def _():
            dg_ref[...] = jnp.zeros_like(dg_ref)
            loss_ref[...] = jnp.zeros_like(loss_ref)

        x = h_ref[...]
        r = lax.rsqrt(jnp.mean(x * x, axis=-1, keepdims=True) + EPS)
        xhat = x * r
        gg = g_ref[...]
        err = xhat * gg - t_ref[...]
        loss_ref[...] += 0.5 * jnp.sum(jnp.mean(err * err, axis=-1, keepdims=True), axis=0, keepdims=True)
        dy = err * (1.0 / D)
        dg_ref[...] += jnp.sum(dy * xhat, axis=0, keepdims=True)
        dxhat = dy * gg
        dh = r * (dxhat - xhat * jnp.mean(dxhat * xhat, axis=-1, keepdims=True))
        dh_ref[...] = dh
        dhb_ref[...] = dh.astype(dhb_ref.dtype)

    blk = pl.BlockSpec((R, D), lambda i: (i, 0))
    vec = pl.BlockSpec((1, D), lambda i: (0, 0))
    one = pl.BlockSpec((1, LANES), lambda i: (0, 0))
    return pl.pallas_call(
        body, name=name, grid=(T // R,), in_specs=[blk, vec, blk], out_specs=[one, blk, blk, vec],
        out_shape=[jax.ShapeDtypeStruct((1, LANES), F32), jax.ShapeDtypeStruct((T, D), F32),
                   jax.ShapeDtypeStruct((T, D), BF16), jax.ShapeDtypeStruct((1, D), F32)],
        compiler_params=_cp("arbitrary"),
    )(h, g.reshape(1, D), tgt)


def _mm(a, b, res, out_dtype, name, tm=512, tn=None, norm_g=None):
    M, K = a.shape
    N = b.shape[1]
    tm = min(tm, M)
    tn = N if tn is None else tn
    assert norm_g is None or tn == N

    def body(*refs):
        refs = list(refs)
        a_ref, b_ref = refs[:2]
        acc = _dot(a_ref[...], b_ref[...])
        if res is not None:
            acc = refs[2][...].astype(F32) + acc
        if norm_g is None:
            refs[-1][...] = acc.astype(refs[-1].dtype)
        else:
            refs[-2][...] = acc.astype(refs[-2].dtype)
            r = lax.rsqrt(jnp.mean(acc * acc, axis=-1, keepdims=True) + EPS)
            refs[-1][...] = ((acc * r) * refs[-3][...]).astype(refs[-1].dtype)

    blk = pl.BlockSpec((tm, tn), lambda j, i: (i, j))
    in_specs = [pl.BlockSpec((tm, K), lambda j, i: (i, 0)), pl.BlockSpec((K, tn), lambda j, i: (0, j))]
    args = [a, b]
    if res is not None:
        in_specs.append(blk)
        args.append(res)
    out_specs, out_shape = blk, jax.ShapeDtypeStruct((M, N), out_dtype)
    if norm_g is not None:
        in_specs.append(pl.BlockSpec((1, N), lambda j, i: (0, 0)))
        args.append(norm_g.reshape(1, N))
        out_specs, out_shape = [blk, blk], [out_shape, jax.ShapeDtypeStruct((M, N), BF16)]
    return pl.pallas_call(
        body, name=name, grid=(N // tn, M // tm), in_specs=in_specs, out_specs=out_specs, out_shape=out_shape,
        compiler_params=_cp("parallel", "parallel"),
    )(*args)


def _mm_tn_into(slab, a, b, layer, chip0, name, tk, tn, tm=1024):
    M, K = a.shape
    N = b.shape[1]
    tm = min(tm, M)
    assert slab.shape[2] == tn and (K // tk == 1 or N // tn == 1)

    def body(a_ref, b_ref, slab_ref, o_ref):
        del slab_ref

        @pl.when(pl.program_id(2) == 0)
        def _():
            o_ref[...] = jnp.zeros_like(o_ref)

        o_ref[0] += _dot_tn(a_ref[...], b_ref[...])

    return pl.pallas_call(
        body, name=name, grid=(K // tk, N // tn, M // tm),
        in_specs=[pl.BlockSpec((tm, tk), lambda k, n, m: (m, k)), pl.BlockSpec((tm, tn), lambda k, n, m: (m, n)),
                  pl.BlockSpec(memory_space=pl.ANY)],
        out_specs=pl.BlockSpec((1, tk, tn), lambda k, n, m: (chip0 + k + n, layer, 0)),
        out_shape=jax.ShapeDtypeStruct(slab.shape, F32), input_output_aliases={2: 0},
        compiler_params=_cp("parallel", "parallel", "arbitrary"),
    )(a, b, slab)


def _mm_tn(a, b, name, tm=2048, tk=None, tn=None):
    M, K = a.shape
    N = b.shape[1]
    tm = min(tm, M)
    tk = K if tk is None else tk
    tn = N if tn is None else tn

    def body(a_ref, b_ref, o_ref):
        @pl.when(pl.program_id(2) == 0)
        def _():
            o_ref[...] = jnp.zeros_like(o_ref)

        o_ref[...] += _dot_tn(a_ref[...], b_ref[...])

    return pl.pallas_call(
        body, name=name, grid=(K // tk, N // tn, M // tm),
        in_specs=[pl.BlockSpec((tm, tk), lambda k, n, m: (m, k)), pl.BlockSpec((tm, tn), lambda k, n, m: (m, n))],
        out_specs=pl.BlockSpec((tk, tn), lambda k, n, m: (k, n)),
        out_shape=jax.ShapeDtypeStruct((K, N), F32), compiler_params=_cp("parallel", "parallel", "arbitrary"),
    )(a, b)


def _same_chunk(row, col):
    shift = CHUNK.bit_length() - 1
    return jnp.right_shift(row, shift) == jnp.right_shift(col, shift)


def _gla_common(q, k, glr, w2, b2, R):
    gl = _dot(glr, w2) + b2
    la = jax.nn.log_sigmoid(gl) * (1.0 / 16.0)
    row = lax.broadcasted_iota(jnp.int32, (R, R), 0)
    col = lax.broadcasted_iota(jnp.int32, (R, R), 1)
    same = _same_chunk(row, col)
    m_tri = (same & (col <= row)).astype(BF16)
    m_all = same.astype(BF16)
    la3 = _split3(la)
    b = sum(jnp.dot(m_tri, p, preferred_element_type=F32) for p in la3)
    bl = sum(jnp.dot(m_all, p, preferred_element_type=F32) for p in la3)
    eb = jnp.exp(b)
    enb = jnp.exp(-b)
    ek = jnp.exp(bl - b)
    qi = (q * (DK ** -0.5)) * eb
    ki = k * enb
    kd = k * ek
    return gl, la3, eb, enb, ek, qi, ki, kd


def _bsplit(x, n):
    return x.reshape(n, CHUNK, x.shape[-1])


def _tril():
    return (lax.broadcasted_iota(jnp.int32, (CHUNK, CHUNK), 1)
            <= lax.broadcasted_iota(jnp.int32, (CHUNK, CHUNK), 0))[None]


def _gla_fwd(proj, w2p, b2, norm_g, name):
    T = proj.shape[0]
    R = min(T, ROWS)
    n = R // CHUNK

    def body(q_ref, k_ref, v_ref, g_ref, a_ref, w2_ref, b2_ref, ng_ref, y_ref, o_ref, st_ref, s_ref):
        @pl.when(pl.program_id(0) == 0)
        def _():
            s_ref[...] = jnp.zeros_like(s_ref)

        _, la3, _, _, _, qi, ki, kd = _gla_common(q_ref[...], k_ref[...], a_ref[...], w2_ref[...], b2_ref[...], R)
        tril = _tril()
        ones = jnp.ones((n, CHUNK, DV), BF16)
        for h in range(HEADS):
            sl = slice(h * DK, (h + 1) * DK)
            sv = slice(h * DV, (h + 1) * DV)
            qh = _bsplit(qi[:, sl], n).astype(MXU_DTYPE)
            kh = _bsplit(ki[:, sl], n).astype(MXU_DTYPE)
            kdh = _bsplit(kd[:, sl], n).astype(MXU_DTYPE)
            vh = _bsplit(v_ref[:, sv], n).astype(MXU_DTYPE)
            att = jnp.where(tril, _bdot('ncd,nsd->ncs', qh, kh), 0.0)
            upd = _bdot('ncd,nce->nde', kdh, vh)
            dect = jnp.exp(sum(_bdot('ncd,nce->nde', _bsplit(p[:, sl], n), ones) for p in la3))
            s = s_ref[sl, :]
            for c in range(n):
                st_ref[c, sl, :] = s
                s = dect[c] * s + upd[c]
            s_ref[sl, :] = s
            sp = st_ref[:, sl, :].astype(MXU_DTYPE)
            o = (_bdot('ncs,nse->nce', att.astype(MXU_DTYPE), vh) + _bdot('ncd,nde->nce', qh, sp)).reshape(R, DV)
            o_ref[:, sv] = o
            r = lax.rsqrt(jnp.mean(o * o, axis=-1, keepdims=True) + EPS)
            gate = g_ref[:, sv]
            y_ref[:, sv] = (((o * r) * ng_ref[...]) * (gate * jax.nn.sigmoid(gate))).astype(y_ref.dtype)

    cb = lambda w, j: pl.BlockSpec((R, w), lambda i: (i, j))
    full = lambda s: pl.BlockSpec(s, lambda i: (0,) * len(s))
    return pl.pallas_call(
        body, name=name, grid=(T // R,),
        in_specs=[cb(QK_W, 0), cb(QK_W, 1), cb(GLA_W, 1), cb(GLA_W, 2), cb(GATE_PAD, 12),
                  full((GATE_PAD, QK_W)), full((1, QK_W)), full((1, DV))],
        out_specs=[pl.BlockSpec((R, GLA_W), lambda i: (i, 0)), pl.BlockSpec((R, GLA_W), lambda i: (i, 0)),
                   pl.BlockSpec((n, QK_W, DV), lambda i: (i, 0, 0))],
        out_shape=[jax.ShapeDtypeStruct((T, GLA_W + LRU_W), BF16), jax.ShapeDtypeStruct((T, GLA_W), F32),
                   jax.ShapeDtypeStruct((T // CHUNK, QK_W, DV), F32)],
        scratch_shapes=[pltpu.VMEM((QK_W, DV), F32)],
        compiler_params=_cp("arbitrary"),
    )(proj, proj, proj, proj, proj, w2p, b2.reshape(1, QK_W), norm_g.reshape(1, DV))


def _gla_bwd(dy, proj, o_st, s_st, w2p, b2, norm_g, name):
    T = proj.shape[0]
    R = min(T, ROWS)
    n = R // CHUNK
    nb = T // R

    def body(dy_ref, q_ref, k_ref, v_ref, g_ref, a_ref, o_ref, st_ref, w2_ref, b2_ref, ng_ref,
             dp_ref, dw2_ref, db2_ref, dng_ref, gs_ref, gn_ref, db_ref, dbl_ref):
        @pl.when(pl.program_id(0) == 0)
        def _():
            gs_ref[...] = jnp.zeros_like(gs_ref)
            dw2_ref[...] = jnp.zeros_like(dw2_ref)
            db2_ref[...] = jnp.zeros_like(db2_ref)
            dng_ref[...] = jnp.zeros_like(dng_ref)

        glr = a_ref[...]
        gl, la3, eb, enb, ek, qi, ki, kd = _gla_common(q_ref[...], k_ref[...], glr, w2_ref[...], b2_ref[...], R)
        tril = _tril()
        ones = jnp.ones((n, CHUNK, DV), BF16)
        ng = ng_ref[...]
        dng = jnp.zeros((1, DV), F32)
        for h in range(HEADS):
            sl = slice(h * DK, (h + 1) * DK)
            sv = slice(h * DV, (h + 1) * DV)
            o = o_ref[:, sv]
            r = lax.rsqrt(jnp.mean(o * o, axis=-1, keepdims=True) + EPS)
            xhat = o * r
            gate = g_ref[:, sv]
            sg = jax.nn.sigmoid(gate)
            dyh = dy_ref[:, sv].astype(F32)
            dp_ref[:, 2 * QK_W + GLA_W + h * DV:2 * QK_W + GLA_W + (h + 1) * DV] = (
                dyh * (xhat * ng) * (sg * (1.0 + gate * (1.0 - sg)))).astype(dp_ref.dtype)
            don = dyh * (gate * sg)
            dng = dng + jnp.sum(don * xhat, axis=0, keepdims=True)
            dxhat = don * ng
            do = r * (dxhat - xhat * jnp.mean(dxhat * xhat, axis=-1, keepdims=True))
            qf = _bsplit(qi[:, sl], n)
            kf = _bsplit(ki[:, sl], n)
            kdf = _bsplit(kd[:, sl], n)
            qh, kh, kdh = qf.astype(MXU_DTYPE), kf.astype(MXU_DTYPE), kdf.astype(MXU_DTYPE)
            vh = _bsplit(v_ref[:, sv], n).astype(MXU_DTYPE)
            doh = _bsplit(do, n).astype(MXU_DTYPE)
            spf = st_ref[:, sl, :]
            sp = spf.astype(MXU_DTYPE)
            att = jnp.where(tril, _bdot('ncd,nsd->ncs', qh, kh), 0.0).astype(MXU_DTYPE)
            datt = jnp.where(tril, _bdot('nce,nse->ncs', doh, vh), 0.0).astype(MXU_DTYPE)
            dv = _bdot('ncs,nce->nse', att, doh)
            dqi = _bdot('ncs,nsd->ncd', datt, kh) + _bdot('nce,nde->ncd', doh, sp)
            dki = _bdot('ncs,ncd->nsd', datt, qh)
            wgt = _bdot('ncd,nce->nde', qh, doh)
            dect = jnp.exp(sum(_bdot('ncd,nce->nde', _bsplit(p[:, sl], n), ones) for p in la3))
            g = gs_ref[sl, :]
            for c in reversed(range(n)):
                gn_ref[c] = g
                g = wgt[c] + dect[c] * g
            gs_ref[sl, :] = g
            gnf = gn_ref[...]
            gn = gnf.astype(MXU_DTYPE)
            dkd = _bdot('nce,nde->ncd', vh, gn)
            dv = dv + _bdot('ncd,nde->nce', kdh, gn)
            dp_ref[:, 2 * QK_W + h * DV:2 * QK_W + (h + 1) * DV] = dv.reshape(R, DV).astype(dp_ref.dtype)
            dbl = sum(_bdot('nce,nde->ncd', ones, p) for p in _split3(gnf * spf * dect))
            pk = dkd * kdf
            dbl = dbl + jnp.sum(pk, axis=1, keepdims=True)
            dbl_ref[:, sl] = dbl.reshape(R, DK)
            db_ref[:, sl] = (dqi * qf - dki * kf - pk).reshape(R, DK)
            dp_ref[:, sl] = ((dqi.reshape(R, DK) * (DK ** -0.5)) * eb[:, sl]).astype(dp_ref.dtype)
            dp_ref[:, QK_W + h * DK:QK_W + (h + 1) * DK] = (
                dki.reshape(R, DK) * enb[:, sl] + dkd.reshape(R, DK) * ek[:, sl]).astype(dp_ref.dtype)
        dng_ref[...] += dng
        row = lax.broadcasted_iota(jnp.int32, (R, R), 0)
        col = lax.broadcasted_iota(jnp.int32, (R, R), 1)
        m_rev = (_same_chunk(row, col) & (col >= row)).astype(BF16)
        dla = sum(jnp.dot(m_rev, p, preferred_element_type=F32) for p in _split3(db_ref[...])) + dbl_ref[...]
        dgl = (dla * (1.0 / 16.0)) * jax.nn.sigmoid(-gl)
        dp_ref[:, 2 * QK_W + 2 * GLA_W:GLA_COLS] = _dot_nt(dgl, w2_ref[...]).astype(dp_ref.dtype)
        dw2_ref[...] += _dot_tn(glr, dgl)
        db2_ref[...] += jnp.sum(dgl, axis=0, keepdims=True)

    cb = lambda w, j: pl.BlockSpec((R, w), lambda i: (nb - 1 - i, j))
    full = lambda s: pl.BlockSpec(s, lambda i: (0,) * len(s))
    return pl.pallas_call(
        body, name=name, grid=(nb,),
        in_specs=[cb(GLA_W, 0), cb(QK_W, 0), cb(QK_W, 1), cb(GLA_W, 1), cb(GLA_W, 2), cb(GATE_PAD, 12),
                  cb(GLA_W, 0), pl.BlockSpec((n, QK_W, DV), lambda i: (nb - 1 - i, 0, 0)),
                  full((GATE_PAD, QK_W)), full((1, QK_W)), full((1, DV))],
        out_specs=[pl.BlockSpec((R, GLA_COLS), lambda i: (nb - 1 - i, 0)),
                   full((GATE_PAD, QK_W)), full((1, QK_W)), full((1, DV))],
        out_shape=[jax.ShapeDtypeStruct((T, GLA_COLS), BF16), jax.ShapeDtypeStruct((GATE_PAD, QK_W), F32),
                   jax.ShapeDtypeStruct((1, QK_W), F32), jax.ShapeDtypeStruct((1, DV), F32)],
        scratch_shapes=[pltpu.VMEM((QK_W, DV), F32), pltpu.VMEM((n, DK, DV), F32),
                        pltpu.VMEM((R, QK_W), F32), pltpu.VMEM((R, QK_W), F32)],
        compiler_params=_cp("arbitrary"),
    )(dy, proj, proj, proj, proj, proj, o_st, s_st, w2p, b2.reshape(1, QK_W), norm_g.reshape(1, DV))


def _scan_scratch(R, W):
    return [pltpu.VMEM((W // LANES, R, LANES), F32), pltpu.VMEM((W // LANES, R, LANES), F32),
            pltpu.VMEM((W // LANES, R // SUBLANES, LANES), F32)]


def _scan_rows(a, u, c0, a_ref, u_ref, c_ref, out_ref, reverse):
    R, W = a.shape
    nt = R // SUBLANES
    shift = _shift_up if reverse else _shift_down
    a, u = a.reshape(nt, SUBLANES, W), u.reshape(nt, SUBLANES, W)
    sub = lax.broadcasted_iota(jnp.int32, (nt, SUBLANES, W), 1)
    for k in (1, 2, 4):
        inside = (sub < SUBLANES - k) if reverse else (sub >= k)
        turn = SUBLANES - k if reverse else k
        u = u + a * jnp.where(inside, pltpu.roll(u, turn, axis=1), 0.0)
        a = a * jnp.where(inside, pltpu.roll(a, turn, axis=1), 1.0)
    a, u = a.reshape(R, W), u.reshape(R, W)
    end = 0 if reverse else SUBLANES - 1
    edge = nt - 1 if reverse else 0
    for j in range(W // LANES):
        cols = slice(j * LANES, (j + 1) * LANES)
        a_ref[j] = a[:, cols]
        u_ref[j] = u[:, cols]
        at = a_ref.at[j][pl.ds(end, nt, stride=SUBLANES), :]
        ut = u_ref.at[j][pl.ds(end, nt, stride=SUBLANES), :]
        k = 1
        while k < nt:
            ut = ut + at * shift(ut, k, 0.0)
            at = at * shift(at, k, 1.0)
            k *= 2
        c_ref[j] = shift(ut + at * c0[:, cols], 1, 0.0)
        c_ref[j, edge:edge + 1, :] = c0[:, cols]
        for r in range(nt):
            rows = pl.ds(r * SUBLANES, SUBLANES)
            out_ref[rows, cols] = u_ref[j, rows, :] + a_ref[j, rows, :] * c_ref[j, r:r + 1, :]


def _lru_conv(ext_ref, cw_ref, cb_ref, R):
    xc = cb_ref[...] + ext_ref[pl.ds(SUBLANES - 3, R), :] * cw_ref[0:1, :]
    xc = xc + ext_ref[pl.ds(SUBLANES - 2, R), :] * cw_ref[1:2, :]
    xc = xc + ext_ref[pl.ds(SUBLANES - 1, R), :] * cw_ref[2:3, :]
    return xc + ext_ref[pl.ds(SUBLANES, R), :] * cw_ref[3:4, :]


def _lru_gates(xc, wa, ba, wx, bx, lam, first):
    r = jax.nn.sigmoid(_dot(xc, wa) + ba)
    ig = jax.nn.sigmoid(_dot(xc, wx) + bx)
    sp = jax.nn.softplus(-lam)
    la = (-LRU_C * r) * sp
    a = jnp.exp(la)
    mult = jnp.where(first, 1.0, jnp.sqrt(-_expm1(2.0 * la, a * a)))
    return r, ig, sp, a, mult


def _lru_fwd(proj, y_mix, cw, cb, wa, ba, wx, bx, lam, name):
    T = proj.shape[0]
    R = min(T, ROWS)
    W = LRU_W

    def body(xr_ref, xh_ref, xg_ref, cw_ref, cb_ref, wa_ref, ba_ref, wx_ref, bx_ref, lam_ref, mix_ref,
             y_ref, hs_ref, ext_ref, hc_ref, sa_ref, su_ref, sc_ref):
        del mix_ref
        i = pl.program_id(0)

        @pl.when(i == 0)
        def _():
            hc_ref[...] = jnp.zeros_like(hc_ref)

        ext_ref[0:SUBLANES, :] = jnp.where(i > 0, xh_ref[...], 0.0)
        ext_ref[pl.ds(SUBLANES, R), :] = xr_ref[...]
        xc = _lru_conv(ext_ref, cw_ref, cb_ref, R)
        row = lax.broadcasted_iota(jnp.int32, (R, W), 0)
        first = (row == 0) & (i == 0)
        _, ig, _, a, mult = _lru_gates(xc, wa_ref[...], ba_ref[...], wx_ref[...], bx_ref[...], lam_ref[...], first)
        _scan_rows(a, mult * (ig * xc), hc_ref[0:1, :], sa_ref, su_ref, sc_ref, hs_ref, reverse=False)
        hc_ref[0:1, :] = hs_ref[R - 1:R, :]
        y_ref[...] = (hs_ref[...] * _gelu(xg_ref[...])).astype(y_ref.dtype)

    rb = R // SUBLANES
    full = lambda s: pl.BlockSpec(s, lambda i: (0,) * len(s))
    return pl.pallas_call(
        body, name=name, grid=(T // R,),
        in_specs=[pl.BlockSpec((R, W), lambda i: (i, 0)),
                  pl.BlockSpec((SUBLANES, W), lambda i: (jnp.maximum(i * rb - 1, 0), 0)),
                  pl.BlockSpec((R, W), lambda i: (i, 1)),
                  full((SUBLANES, W)), full((1, W)), full((W, W)), full((1, W)), full((W, W)), full((1, W)),
                  full((1, W)), pl.BlockSpec(memory_space=pl.ANY)],
        out_specs=[pl.BlockSpec((R, W), lambda i: (i, 1)), pl.BlockSpec((R, W), lambda i: (i, 0))],
        out_shape=[jax.ShapeDtypeStruct(y_mix.shape, y_mix.dtype), jax.ShapeDtypeStruct((T, W), F32)],
        scratch_shapes=[pltpu.VMEM((R + SUBLANES, W), F32), pltpu.VMEM((SUBLANES, W), F32),
                        *_scan_scratch(R, W)],
        input_output_aliases={10: 0}, compiler_params=_cp("arbitrary"),
    )(proj, proj, proj, cw, cb, wa, ba, wx, bx, lam, y_mix)


def _lru_bwd(dy, proj, hs, cw, cb, wa, ba, wx, bx, lam, name):
    T = proj.shape[0]
    R = min(T, ROWS)
    W = LRU_W
    nb = T // R

    def body(dy_ref, xr_ref, xh_ref, xg_ref, hs_ref, hh_ref, cw_ref, cb_ref, wa_ref, ba_ref, wx_ref, bx_ref, lam_ref,
             dp_ref, dcw_ref, dvec_ref, dwa_ref, dwx_ref, ext_ref, ext2_ref, lc_ref, sa_ref, su_ref, sc_ref, adj_ref):
        ib = pl.program_id(0)
        i = nb - 1 - ib

        @pl.when(ib == 0)
        def _():
            lc_ref[...] = jnp.zeros_like(lc_ref)
            ext2_ref[pl.ds(R, SUBLANES), :] = jnp.zeros((SUBLANES, W), F32)
            dcw_ref[...] = jnp.zeros_like(dcw_ref)
            dvec_ref[...] = jnp.zeros_like(dvec_ref)
            dwa_ref[...] = jnp.zeros_like(dwa_ref)
            dwx_ref[...] = jnp.zeros_like(dwx_ref)

        ext_ref[0:SUBLANES, :] = jnp.where(i > 0, xh_ref[...], 0.0)
        ext_ref[pl.ds(SUBLANES, R), :] = xr_ref[...]
        xc = _lru_conv(ext_ref, cw_ref, cb_ref, R)
        row = lax.broadcasted_iota(jnp.int32, (R, W), 0)
        first = (row == 0) & (i == 0)
        lam = lam_ref[...]
        r, ig, sp, a, mult = _lru_gates(xc, wa_ref[...], ba_ref[...], wx_ref[...], bx_ref[...], lam, first)
        h = hs_ref[...]
        gel, dgel = _gelu_and_grad(xg_ref[...])
        dy = dy_ref[...].astype(F32)
        dp_ref[:, W:2 * W] = (dy * h * dgel).astype(dp_ref.dtype)
        _scan_rows(_shift_up(a, 1, 1.0), dy * gel, lc_ref[0:1, :], sa_ref, su_ref, sc_ref, adj_ref, reverse=True)
        v = adj_ref[...]
        lc_ref[...] = (a * v)[0:SUBLANES, :]
        hprev = _shift_down(h, 1, 0.0) + jnp.where((row == 0) & (i > 0), hh_ref[SUBLANES - 1:SUBLANES, :], 0.0)
        da = v * hprev
        dmult = jnp.where(first, 0.0, v * (ig * xc))
        dig = v * (mult * xc)
        dxc = v * (mult * ig)
        dla = da * a - dmult * ((a * a) / mult)
        dra = (dla * (-LRU_C * sp)) * (r * (1.0 - r))
        drx = dig * (ig * (1.0 - ig))
        dxc = dxc + _dot_nt(dra, wa_ref[...]) + _dot_nt(drx, wx_ref[...])
        dwa_ref[...] += _dot_tn(xc, dra)
        dwx_ref[...] += _dot_tn(xc, drx)
        dvec_ref[0:1, :] += jnp.sum(dxc, axis=0, keepdims=True)
        dvec_ref[1:2, :] += jnp.sum(dra, axis=0, keepdims=True)
        dvec_ref[2:3, :] += jnp.sum(drx, axis=0, keepdims=True)
        dvec_ref[3:4, :] += jnp.sum(dla * (-LRU_C * r), axis=0, keepdims=True) * (-jax.nn.sigmoid(-lam))
        ext2_ref[pl.ds(0, R), :] = dxc
        dxr = ext2_ref[pl.ds(0, R), :] * cw_ref[3:4, :]
        dxr = dxr + ext2_ref[pl.ds(1, R), :] * cw_ref[2:3, :]
        dxr = dxr + ext2_ref[pl.ds(2, R), :] * cw_ref[1:2, :]
        dxr = dxr + ext2_ref[pl.ds(3, R), :] * cw_ref[0:1, :]
        dp_ref[:, 0:W] = dxr.astype(dp_ref.dtype)
        for j in range(LRU_CONV):
            dcw_ref[j:j + 1, :] += jnp.sum(dxc * ext_ref[pl.ds(SUBLANES - 3 + j, R), :], axis=0, keepdims=True)
        ext2_ref[pl.ds(R, SUBLANES), :] = dxc[0:SUBLANES, :]

    rb = R // SUBLANES
    full = lambda s: pl.BlockSpec(s, lambda i: (0,) * len(s))
    blk = lambda j: pl.BlockSpec((R, W), lambda i: (nb - 1 - i, j))
    halo = pl.BlockSpec((SUBLANES, W), lambda i: (jnp.maximum((nb - 1 - i) * rb - 1, 0), 0))
    return pl.pallas_call(
        body, name=name, grid=(nb,),
        in_specs=[blk(1), blk(0), halo, blk(1), blk(0), halo,
                  full((SUBLANES, W)), full((1, W)), full((W, W)), full((1, W)), full((W, W)), full((1, W)),
                  full((1, W))],
        out_specs=[pl.BlockSpec((R, 2 * W), lambda i: (nb - 1 - i, 0)), full((SUBLANES, W)), full((SUBLANES, W)),
                   full((W, W)), full((W, W))],
        out_shape=[jax.ShapeDtypeStruct((T, 2 * W), BF16), jax.ShapeDtypeStruct((SUBLANES, W), F32),
                   jax.ShapeDtypeStruct((SUBLANES, W), F32), jax.ShapeDtypeStruct((W, W), F32),
                   jax.ShapeDtypeStruct((W, W), F32)],
        scratch_shapes=[pltpu.VMEM((R + SUBLANES, W), F32), pltpu.VMEM((R + SUBLANES, W), F32),
                        pltpu.VMEM((SUBLANES, W), F32), *_scan_scratch(R, W), pltpu.VMEM((R, W), F32)],
        compiler_params=_cp("arbitrary"),
    )(dy, proj, proj, proj, hs, hs, cw, cb, wa, ba, wx, bx, lam)


def _conv3_window(src, start, cs, w, b):
    win = src[pl.ds(start, FFN_RC + SUBLANES), cs]
    x2 = pltpu.roll(win, 2, axis=0)[SUBLANES:]
    x1 = pltpu.roll(win, 1, axis=0)[SUBLANES:]
    x0 = win[SUBLANES:]
    return ((b + x2 * w[0]) + x1 * w[1]) + x0 * w[2], (x2, x1, x0)


def _ffn_up_fwd(u2, fa, fg, cwa, cwg, cba, cbg, name):
    T, D = u2.shape
    Fh = fa.shape[1]
    tm = min(T, ROWS)
    CW, SB, RC = FFN_CW, FFN_SB, FFN_RC
    ns = CW // SB

    def body(u_ref, fa_ref, fg_ref, cwa_ref, cwg_ref, cba_ref, cbg_ref, xa_ref, xg_ref, p_ref, q_ref, act_ref,
             ka_ref, kg_ref, ea_ref, eg_ref, za_ref, zg_ref):
        @pl.when(pl.program_id(1) == 0)
        def _():
            ka_ref[...] = jnp.zeros_like(ka_ref)
            kg_ref[...] = jnp.zeros_like(kg_ref)

        def gate(s):
            cs = pl.ds(s * SB, SB)
            wa = [cwa_ref[j:j + 1, cs] for j in range(FFN_CONV)]
            wg = [cwg_ref[j:j + 1, cs] for j in range(FFN_CONV)]
            ba, bg = cba_ref[:, cs], cbg_ref[:, cs]
            ea_ref[0:SUBLANES, cs] = ka_ref[:, cs]
            ea_ref[pl.ds(SUBLANES, RC), cs] = za_ref[0:RC, cs]
            eg_ref[0:SUBLANES, cs] = kg_ref[:, cs]
            eg_ref[pl.ds(SUBLANES, RC), cs] = zg_ref[0:RC, cs]
            for c in range(tm // RC):
                sa, sg, start = (ea_ref, eg_ref, 0) if c == 0 else (za_ref, zg_ref, c * RC - SUBLANES)
                rows = pl.ds(c * RC, RC)
                a_c, xa = _conv3_window(sa, start, cs, wa, ba)
                g_c, xg = _conv3_window(sg, start, cs, wg, bg)
                gel, dgel = _gelu_and_grad(a_c)
                xa_ref[rows, cs] = xa[2].astype(xa_ref.dtype)
                xg_ref[rows, cs] = xg[2].astype(xg_ref.dtype)
                p_ref[rows, cs] = (g_c * dgel).astype(p_ref.dtype)
                q_ref[rows, cs] = gel.astype(q_ref.dtype)
                act_ref[rows, cs] = (gel * g_c).astype(act_ref.dtype)
            ka_ref[:, cs] = za_ref[tm - SUBLANES:tm, cs]
            kg_ref[:, cs] = zg_ref[tm - SUBLANES:tm, cs]

        za_ref[...] = _dot(u_ref[...], fa_ref[...])
        zg_ref[...] = _dot(u_ref[...], fg_ref[...])
        for s in range(ns):
            gate(s)

    blk = pl.BlockSpec((tm, CW), lambda j, i: (i, j))
    wblk = pl.BlockSpec((D, CW), lambda j, i: (0, j))
    w8 = pl.BlockSpec((SUBLANES, CW), lambda j, i: (0, j))
    w1 = pl.BlockSpec((1, CW), lambda j, i: (0, j))
    return pl.pallas_call(
        body, name=name, grid=(Fh // CW, T // tm),
        in_specs=[pl.BlockSpec((tm, D), lambda j, i: (i, 0)), wblk, wblk, w8, w8, w1, w1],
        out_specs=[blk] * 5,
        out_shape=[jax.ShapeDtypeStruct((T, Fh), BF16)] * 5,
        scratch_shapes=[pltpu.VMEM((SUBLANES, CW), F32), pltpu.VMEM((SUBLANES, CW), F32),
                        pltpu.VMEM((RC + SUBLANES, CW), F32), pltpu.VMEM((RC + SUBLANES, CW), F32),
                        pltpu.VMEM((tm, CW), F32), pltpu.VMEM((tm, CW), F32)],
        compiler_params=_cp("parallel", "arbitrary"),
    )(u2, fa, fg, cwa, cwg, cba, cbg)


def _ffn_bwd_core(dh, dhb, xa, xg, p, q, wdT, faT, fgT, cwa, cwg, h, g, name):
    T, D = dhb.shape
    Fh = xa.shape[1]
    tm = min(T, ROWS)
    CW, SB, RC = FFN_CW, FFN_SB, FFN_RC
    ns = CW // SB
    nj = Fh // CW
    nb = T // tm
    nc = tm // RC

    def body(dh_ref, xa_ref, xg_ref, p_ref, q_ref, wd_ref, fa_ref, fg_ref, cwa_ref, cwg_ref, h_ref, g_ref, res_ref,
             dza_ref, dzg_ref, dca_ref, dcg_ref, dho_ref, dhbo_ref, dg_ref,
             d_ref, sa_ref, sg_ref, ka_ref, kg_ref, du_ref):
        ib, j = pl.program_id(0), pl.program_id(1)

        @pl.when((ib == 0) & (j == 0))
        def _():
            dca_ref[...] = jnp.zeros_like(dca_ref)
            dcg_ref[...] = jnp.zeros_like(dcg_ref)
            dg_ref[...] = jnp.zeros_like(dg_ref)

        @pl.when(ib == 0)
        def _():
            ka_ref[j] = jnp.zeros((SUBLANES, CW), F32)
            kg_ref[j] = jnp.zeros((SUBLANES, CW), F32)

        @pl.when(j == 0)
        def _():
            du_ref[...] = jnp.zeros_like(du_ref)

        def fold(v):
            return jnp.sum(v.reshape(RC // SUBLANES, SUBLANES, SB), axis=0)

        def gate(s):
            cs = pl.ds(s * SB, SB)
            wa = [cwa_ref[t:t + 1, cs] for t in range(FFN_CONV)]
            wg = [cwg_ref[t:t + 1, cs] for t in range(FFN_CONV)]
            sa_ref[pl.ds(tm, SUBLANES), cs] = ka_ref[j, :, cs]
            sg_ref[pl.ds(tm, SUBLANES), cs] = kg_ref[j, :, cs]
            for c in range(nc):
                rows = pl.ds(c * RC, RC)
                dact = d_ref[rows, cs]
                sa_ref[rows, cs] = dact * p_ref[rows, cs].astype(F32)
                sg_ref[rows, cs] = dact * q_ref[rows, cs].astype(F32)
            n = RC + SUBLANES
            for s_ref, x_ref, o_ref, dc_ref, w in ((sa_ref, xa_ref, dza_ref, dca_ref, wa),
                                                   (sg_ref, xg_ref, dzg_ref, dcg_ref, wg)):
                acc = [jnp.zeros((SUBLANES, SB), F32) for _ in range(FFN_CONV + 1)]
                for c in range(nc):
                    rows = pl.ds(c * RC, RC)
                    win = s_ref[pl.ds(c * RC, n), cs]
                    d0 = win[:RC]
                    d1 = pltpu.roll(win, n - 1, axis=0)[:RC]
                    d2 = pltpu.roll(win, n - 2, axis=0)[:RC]
                    o_ref[rows, cs] = ((d0 * w[2] + d1 * w[1]) + d2 * w[0]).astype(o_ref.dtype)
                    x = x_ref[rows, cs].astype(F32)
                    acc = [acc[0] + fold(d2 * x), acc[1] + fold(d1 * x), acc[2] + fold(d0 * x), acc[3] + fold(d0)]
                for t in range(FFN_CONV + 1):
                    dc_ref[j, t:t + 1, cs] += jnp.sum(acc[t], axis=0, keepdims=True)
            ka_ref[j, :, cs] = sa_ref[0:SUBLANES, cs]
            kg_ref[j, :, cs] = sg_ref[0:SUBLANES, cs]

        d_ref[...] = _dot(dh_ref[...], wd_ref[...])
        for s in range(ns):
            gate(s)
        du_ref[...] += _dot(dza_ref[...], fa_ref[...]) + _dot(dzg_ref[...], fg_ref[...])

        @pl.when(j == nj - 1)
        def _():
            dho, dg = _rms_bwd_tile(h_ref[...], g_ref[...], du_ref[...], res_ref[...])
            dg_ref[...] += dg
            dho_ref[...] = dho
            dhbo_ref[...] = dho.astype(dhbo_ref.dtype)

    blk = pl.BlockSpec((tm, CW), lambda ib, j: (nb - 1 - ib, j))
    row = pl.BlockSpec((tm, D), lambda ib, j: (nb - 1 - ib, 0))
    vec = pl.BlockSpec((1, D), lambda ib, j: (0, 0))
    w8 = pl.BlockSpec((SUBLANES, CW), lambda ib, j: (0, j))
    wrow = pl.BlockSpec((CW, D), lambda ib, j: (j, 0))
    acc = pl.BlockSpec((nj, SUBLANES, CW), lambda ib, j: (0, 0, 0))
    return pl.pallas_call(
        body, name=name, grid=(nb, nj),
        in_specs=[row, blk, blk, blk, blk, pl.BlockSpec((D, CW), lambda ib, j: (0, j)), wrow, wrow, w8, w8,
                  row, vec, row],
        out_specs=[blk, blk, acc, acc, row, row, vec],
        out_shape=[jax.ShapeDtypeStruct((T, Fh), BF16), jax.ShapeDtypeStruct((T, Fh), BF16),
                   jax.ShapeDtypeStruct((nj, SUBLANES, CW), F32), jax.ShapeDtypeStruct((nj, SUBLANES, CW), F32),
                   jax.ShapeDtypeStruct((T, D), F32), jax.ShapeDtypeStruct((T, D), BF16),
                   jax.ShapeDtypeStruct((1, D), F32)],
        scratch_shapes=[pltpu.VMEM((tm, CW), F32), pltpu.VMEM((tm + SUBLANES, CW), F32),
                        pltpu.VMEM((tm + SUBLANES, CW), F32), pltpu.VMEM((nj, SUBLANES, CW), F32),
                        pltpu.VMEM((nj, SUBLANES, CW), F32), pltpu.VMEM((tm, D), F32)],
        compiler_params=_cp("arbitrary", "arbitrary"),
    )(dhb, xa, xg, p, q, wdT, faT, fgT, cwa, cwg, h, g.reshape(1, D), dh)


def _adamw(w, g, m, v, name):
    rows, cols = w.shape
    tr = _row_tile(rows, max(SUBLANES, min(512, TILE_BYTES // (4 * cols)) // SUBLANES * SUBLANES))

    def body(w_ref, g_ref, m_ref, v_ref, d_ref, mo_ref, vo_ref):
        g = g_ref[...]
        mm = ADAM_B1 * m_ref[...] + (1.0 - ADAM_B1) * g
        vv = ADAM_B2 * v_ref[...] + (1.0 - ADAM_B2) * (g * g)
        m_hat = mm / (1.0 - ADAM_B1 ** ADAM_STEP)
        v_hat = vv / (1.0 - ADAM_B2 ** ADAM_STEP)
        d_ref[...] = -ADAM_LR * (m_hat / (jnp.sqrt(v_hat) + ADAM_EPS) + ADAM_WD * w_ref[...])
        mo_ref[...] = mm
        vo_ref[...] = vv

    blk = pl.BlockSpec((tr, cols), lambda i: (i, 0))
    return pl.pallas_call(
        body, name=name, grid=(rows // tr,), in_specs=[blk] * 4, out_specs=[blk] * 3,
        out_shape=[jax.ShapeDtypeStruct((rows, cols), F32)] * 3, compiler_params=_cp("parallel"),
    )(w, g, m, v)


def _add_kept_half(slab, got, out_dtype, name):
    n, _, rows, cols = slab.shape
    tr = _row_tile(rows, max(SUBLANES, min(512, TILE_BYTES // (4 * cols)) // SUBLANES * SUBLANES))

    def body(a_ref, b_ref, o_ref):
        o_ref[...] = (a_ref[0] + b_ref[...]).astype(o_ref.dtype)

    blk = pl.BlockSpec((1, tr, cols), lambda k, i: (k, i, 0))
    return pl.pallas_call(
        body, name=name, grid=(n, rows // tr),
        in_specs=[pl.BlockSpec((1, 1, tr, cols), lambda k, i: (k, lax.axis_index("c"), i, 0)), blk], out_specs=blk,
        out_shape=jax.ShapeDtypeStruct((n, rows, cols), out_dtype), compiler_params=_cp("parallel", "parallel"),
    )(slab, got)


def _sum_leading(parts, name, last=None):
    n, rows, cols = parts.shape
    tr = _row_tile(rows, max(SUBLANES, min(512, TILE_BYTES // (4 * cols)) // SUBLANES * SUBLANES))

    def body(*refs):
        p_ref, o_ref = refs[0], refs[-1]
        acc = p_ref[0].astype(F32)
        for d in range(1, n):
            acc = acc + p_ref[d].astype(F32)
        if last is not None:
            acc = acc + refs[1][...].astype(F32)
        o_ref[...] = acc

    blk = pl.BlockSpec((tr, cols), lambda i: (i, 0))
    return pl.pallas_call(
        body, name=name, grid=(rows // tr,),
        in_specs=[pl.BlockSpec((n, tr, cols), lambda i: (0, i, 0))] + ([] if last is None else [blk]), out_specs=blk,
        out_shape=jax.ShapeDtypeStruct((rows, cols), F32), compiler_params=_cp("parallel"),
    )(*((parts,) if last is None else (parts, last)))


def _row_tile(rows, cap=512):
    if rows <= cap:
        return rows
    return max(t for t in range(SUBLANES, cap + 1, SUBLANES) if rows % t == 0)


def _place():
    return lax.axis_index("x"), lax.axis_index("y"), lax.axis_index("c")


def _gather_shards(arrs, name):
    n = len(arrs)

    def body(*refs):
        ins, outs = refs[:n], refs[n:2 * n]
        send_sems, recv_sems, pass_send, pass_recv = refs[2 * n:]
        x, y, c = _place()
        chips = [(1 - x, y), (x, 1 - y), (1 - x, 1 - y)]
        mine, theirs = c, 1 - c

        def half(a, which):
            hl = ins[a].shape[0] // 2
            return pl.ds(which * hl, hl)

        def send(a, j, shard):
            px, py = chips[j]
            return pltpu.make_async_remote_copy(
                src_ref=ins[a].at[half(a, mine)], dst_ref=outs[a].at[shard, half(a, mine)],
                send_sem=send_sems.at[3 * a + j], recv_sem=recv_sems.at[3 * a + j], device_id=(px, py, c),
                device_id_type=MESH)

        def passed(a, j, which):
            px, py = chips[j]
            blk = outs[a].at[2 * px + py, half(a, which)]
            return pltpu.make_async_remote_copy(
                src_ref=blk, dst_ref=blk, send_sem=pass_send.at[3 * a + j], recv_sem=pass_recv.at[3 * a + j],
                device_id=(x, y, 1 - c), device_id_type=MESH)

        sends = [send(a, j, 2 * x + y) for a in range(n) for j in range(3)]
        for cp in sends:
            cp.start()
        passes = []
        for a in range(n):
            for j, (px, py) in enumerate(chips):
                send(a, j, 2 * px + py).wait_recv()
                passes.append(passed(a, j, mine))
                passes[-1].start()
        for a in range(n):
            for j in range(3):
                passed(a, j, theirs).wait_recv()
        for cp in sends + passes:
            cp.wait_send()

    hbm = pl.BlockSpec(memory_space=pl.ANY)
    return pl.pallas_call(
        body, name=name, in_specs=[hbm] * n, out_specs=[hbm] * n,
        out_shape=[jax.ShapeDtypeStruct((N_CHIPS,) + a.shape, a.dtype) for a in arrs],
        scratch_shapes=[pltpu.SemaphoreType.DMA((3 * n,)), pltpu.SemaphoreType.DMA((3 * n,)),
                        pltpu.SemaphoreType.DMA((3 * n,)), pltpu.SemaphoreType.DMA((3 * n,))],
        compiler_params=pltpu.CompilerParams(has_side_effects=True),
    )(*arrs)


def _other_half_to_sibling(slabs, name):
    n = len(slabs)

    def body(*refs):
        ins, got = refs[:n], refs[n:2 * n]
        send_sems, recv_sems = refs[2 * n:]
        x, y, c = _place()
        copies = [pltpu.make_async_remote_copy(
            src_ref=ins[a].at[:, 1 - c], dst_ref=got[a], send_sem=send_sems.at[a], recv_sem=recv_sems.at[a],
            device_id=(x, y, 1 - c), device_id_type=MESH) for a in range(n)]
        for cp in copies:
            cp.start()
        for cp in copies:
            cp.wait()

    hbm = pl.BlockSpec(memory_space=pl.ANY)
    return pl.pallas_call(
        body, name=name, in_specs=[hbm] * n, out_specs=[hbm] * n,
        out_shape=[jax.ShapeDtypeStruct((s.shape[0],) + s.shape[2:], s.dtype) for s in slabs],
        scratch_shapes=[pltpu.SemaphoreType.DMA((n,)), pltpu.SemaphoreType.DMA((n,))],
        compiler_params=pltpu.CompilerParams(has_side_effects=True),
    )(*slabs)


def _exchange_grads(slabs, small, name):
    n = len(slabs)

    def body(*refs):
        ins, small_ref = refs[:n], refs[n]
        outs, small_out = refs[n + 1:2 * n + 1], refs[2 * n + 1]
        send_sems, recv_sems, ssend, srecv = refs[2 * n + 2:]
        x, y, c = _place()
        chips = [(1 - x, y), (x, 1 - y), (1 - x, 1 - y)]
        me = 4 * x + 2 * y + c
        flips = [(fx, fy, fc) for fx in (0, 1) for fy in (0, 1) for fc in (0, 1)][1:]

        def copy(a, j):
            px, py = chips[j]
            return pltpu.make_async_remote_copy(
                src_ref=ins[a].at[2 * px + py], dst_ref=outs[a].at[j], send_sem=send_sems.at[3 * a + j],
                recv_sem=recv_sems.at[3 * a + j], device_id=(px, py, c), device_id_type=MESH)

        def scopy(k, row):
            fx, fy, fc = flips[k]
            return pltpu.make_async_remote_copy(
                src_ref=small_ref, dst_ref=small_out.at[row], send_sem=ssend.at[k], recv_sem=srecv.at[k],
                device_id=(x ^ fx, y ^ fy, c ^ fc), device_id_type=MESH)

        sends = [copy(a, j) for a in range(n) for j in range(3)] + [scopy(k, me) for k in range(7)]
        for cp in sends:
            cp.start()
        for k, (fx, fy, fc) in enumerate(flips):
            scopy(k, 4 * (x ^ fx) + 2 * (y ^ fy) + (c ^ fc)).wait_recv()
        for a in range(n):
            for j in range(3):
                copy(a, j).wait_recv()
        for cp in sends:
            cp.wait_send()

    hbm = pl.BlockSpec(memory_space=pl.ANY)
    return pl.pallas_call(
        body, name=name, in_specs=[hbm] * (n + 1), out_specs=[hbm] * (n + 1),
        out_shape=[jax.ShapeDtypeStruct((3,) + s.shape[1:], s.dtype) for s in slabs]
        + [jax.ShapeDtypeStruct((N_DEV,) + small.shape, small.dtype)],
        scratch_shapes=[pltpu.SemaphoreType.DMA((3 * n,)), pltpu.SemaphoreType.DMA((3 * n,)),
                        pltpu.SemaphoreType.DMA((7,)), pltpu.SemaphoreType.DMA((7,))],
        compiler_params=pltpu.CompilerParams(has_side_effects=True),
    )(*slabs, small)


def _swap_with_sibling(arrs, name):
    n = len(arrs)

    def body(*refs):
        ins, outs = refs[:n], refs[n:2 * n]
        send_sems, recv_sems = refs[2 * n:]
        x, y, c = _place()
        copies = [pltpu.make_async_remote_copy(
            src_ref=ins[a], dst_ref=outs[a], send_sem=send_sems.at[a], recv_sem=recv_sems.at[a],
            device_id=(x, y, 1 - c), device_id_type=MESH) for a in range(n)]
        for cp in copies:
            cp.start()
        for cp in copies:
            cp.wait()

    hbm = pl.BlockSpec(memory_space=pl.ANY)
    return pl.pallas_call(
        body, name=name, in_specs=[hbm] * n, out_specs=[hbm] * n,
        out_shape=[jax.ShapeDtypeStruct(a.shape, a.dtype) for a in arrs],
        scratch_shapes=[pltpu.SemaphoreType.DMA((n,)), pltpu.SemaphoreType.DMA((n,))],
        compiler_params=pltpu.CompilerParams(has_side_effects=True),
    )(*arrs)


def _block_diag(w):
    eye = jnp.eye(LRU_BLOCKS, dtype=w.dtype)
    return (eye[:, None, :, None] * w[:, :, None, :]).reshape(LRU_W, LRU_W)


def _diag_blocks(m):
    m4 = m.reshape(LRU_BLOCKS, LRU_BLOCK, LRU_BLOCKS, LRU_BLOCK)
    return jnp.stack([m4[b, :, b, :] for b in range(LRU_BLOCKS)])


def _pad_rows(a, rows):
    return jnp.pad(a, ((0, rows - a.shape[0]), (0, 0)))


def _layer_weights(p, l):
    w_in = p["w_in"][l]
    n_gla = 2 * QK_W + 2 * GLA_W
    gate = jnp.pad(w_in[:, n_gla:n_gla + GATE_RANK], ((0, 0), (0, GATE_PAD - GATE_RANK)))
    wg = jnp.concatenate([w_in[:, :n_gla], gate], axis=1)
    wl = w_in[:, n_gla + GATE_RANK:]
    w_out = p["w_out"][l]
    fa, fg = p["ffn_w_in"][l][:, :FFN_H], p["ffn_w_in"][l][:, FFN_H:]
    wd = p["ffn_w_down"][l]
    return dict(
        wg=wg, wl=wl, wgT=wg.T, wlT=wl.T, wo=w_out, woT=w_out.T,
        fa=fa, fg=fg, faT=fa.T, fgT=fg.T, wd=wd, wdT=wd.T,
        w2p=_pad_rows(p["gla_gate_w2"][l], GATE_PAD).astype(BF16),
        wa=_block_diag(p["lru_wa"][l]).astype(BF16), wx=_block_diag(p["lru_wx"][l]).astype(BF16),
        lcw=_pad_rows(p["lru_conv_w"][l], SUBLANES),
        fcwa=_pad_rows(p["ffn_conv_w"][l][:, :FFN_H], SUBLANES), fcwg=_pad_rows(p["ffn_conv_w"][l][:, FFN_H:], SUBLANES),
    )


def _local_step(x, tgt, p):
    row = lambda v: v.reshape(1, -1)
    h = x
    u = _rms_fwd(h, p["ln_mix"][0], "mix_norm_fwd0")
    stash = []
    for l in range(DEPTH):
        w = _layer_weights(p, l)
        s = dict(w=w, h0=h)
        pg = _mm(u, w["wg"], None, F32, f"proj_gla_fwd{l}")
        plr = _mm(u, w["wl"], None, F32, f"proj_lru_fwd{l}")
        ym, o_st, s_st = _gla_fwd(pg, w["w2p"], p["gla_gate_b"][l], p["gla_norm"][l], f"gla_fwd{l}")
        ym, hs = _lru_fwd(plr, ym, w["lcw"], row(p["lru_conv_b"][l]), w["wa"], row(p["lru_ba"][l]), w["wx"],
                          row(p["lru_bx"][l]), row(p["lru_lambda"][l]), f"lru_fwd{l}")
        h, u2 = _mm(ym, w["wo"], h, F32, f"out_fwd{l}", norm_g=p["ln_ffn"][l])
        s.update(u=u, pg=pg, plr=plr, ym=ym, o_st=o_st, s_st=s_st, hs=hs, h1=h)
        cba, cbg = row(p["ffn_conv_b"][l][:FFN_H]), row(p["ffn_conv_b"][l][FFN_H:])
        *kept, act = _ffn_up_fwd(u2, w["fa"], w["fg"], w["fcwa"], w["fcwg"], cba, cbg, f"ffn_up_fwd{l}")
        if l + 1 < DEPTH:
            h, u = _mm(act, w["wd"], h, F32, f"ffn_down_fwd{l}", norm_g=p["ln_mix"][l + 1])
        else:
            h = _mm(act, w["wd"], h, F32, f"ffn_down_fwd{l}")
        s.update(u2=u2, ffn_kept=kept, act=act)
        stash.append(s)

    loss, dh, dhb, d_ln_final = _loss_head(h, p["ln_final"], tgt, "loss_head")

    g = {k: [None] * DEPTH for k in ("ln_mix", "w_in", "gla_gate_w2", "gla_gate_b", "gla_norm", "lru_conv_w",
                                     "lru_conv_b", "lru_wa", "lru_ba", "lru_wx", "lru_bx", "lru_lambda",
                                     "ln_ffn", "ffn_conv_w", "ffn_conv_b")}
    slab = dict(w_out=lax.empty((N_CHIPS, DEPTH * D_MODEL // N_CHIPS, D_MODEL), F32),
                ffn_w_in=lax.empty((N_CHIPS, DEPTH * D_MODEL, 2 * FFN_H // N_CHIPS), F32),
                ffn_w_down=lax.empty((N_CHIPS, DEPTH * FFN_H // N_CHIPS, D_MODEL), F32))
    n_gla = 2 * QK_W + 2 * GLA_W
    for l in reversed(range(DEPTH)):
        s = stash[l]
        w = s["w"]
        slab["ffn_w_down"] = _mm_tn_into(slab["ffn_w_down"], s["act"], dhb, l, 0, f"ffn_down_dw{l}",
                                         tk=FFN_H // N_CHIPS, tn=D_MODEL, tm=2048)
        dza, dzg, dca, dcg, dh, dhb, dln = _ffn_bwd_core(
            dh, dhb, *s["ffn_kept"], w["wdT"], w["faT"], w["fgT"], w["fcwa"], w["fcwg"], s["h1"], p["ln_ffn"][l],
            f"ffn_bwd_core{l}")
        dca, dcg = (jnp.moveaxis(d, 0, 1).reshape(SUBLANES, FFN_H) for d in (dca, dcg))
        g["ffn_conv_w"][l] = jnp.concatenate([dca[:FFN_CONV], dcg[:FFN_CONV]], axis=1)
        g["ffn_conv_b"][l] = jnp.concatenate([dca[FFN_CONV], dcg[FFN_CONV]])
        for half, dz in enumerate((dza, dzg)):
            slab["ffn_w_in"] = _mm_tn_into(slab["ffn_w_in"], s["u2"], dz, l, 2 * half, f"ffn_in_dw{l}_{half}",
                                           tk=D_MODEL, tn=2 * FFN_H // N_CHIPS)
        g["ln_ffn"][l] = dln[0]
        slab["w_out"] = _mm_tn_into(slab["w_out"], s["ym"], dhb, l, 0, f"out_dw{l}",
                                    tk=D_MODEL // N_CHIPS, tn=D_MODEL, tm=2048)
        dyc = _mm(dhb, w["woT"], None, F32, f"out_dx{l}")
        dpg, dw2, db2, dng = _gla_bwd(dyc, s["pg"], s["o_st"], s["s_st"], w["w2p"], p["gla_gate_b"][l],
                                      p["gla_norm"][l], f"gla_bwd{l}")
        dpl, dcw, dvec, dwa, dwx = _lru_bwd(dyc, s["plr"], s["hs"], w["lcw"], row(p["lru_conv_b"][l]), w["wa"],
                                            row(p["lru_ba"][l]), w["wx"], row(p["lru_bx"][l]),
                                            row(p["lru_lambda"][l]), f"lru_bwd{l}")
        g["gla_gate_w2"][l] = dw2[:GATE_RANK]
        g["gla_gate_b"][l] = db2[0]
        g["gla_norm"][l] = dng[0]
        g["lru_conv_w"][l] = dcw[:LRU_CONV]
        g["lru_conv_b"][l], g["lru_ba"][l], g["lru_bx"][l], g["lru_lambda"][l] = dvec[0], dvec[1], dvec[2], dvec[3]
        g["lru_wa"][l], g["lru_wx"][l] = _diag_blocks(dwa), _diag_blocks(dwx)
        dwg = _mm_tn(s["u"], dpg, f"proj_gla_dw{l}")
        dwl = _mm_tn(s["u"], dpl, f"proj_lru_dw{l}")
        g["w_in"][l] = jnp.concatenate([dwg[:, :n_gla + GATE_RANK], dwl], axis=1)
        dh, dhb, dln = _mm_rms_bwd([(dpg, w["wgT"]), (dpl, w["wlT"])], s["h0"], p["ln_mix"][l], dh, f"proj_dx{l}")
        g["ln_mix"][l] = dln[0]
    grads = {k: jnp.stack(v) for k, v in g.items()}
    grads["w_in"] = _slabs_from_whole("w_in", grads["w_in"])
    grads.update(slab)
    grads["ln_final"] = d_ln_final[0]
    return loss, dh, grads


BIG = ("w_in", "w_out", "ffn_w_in", "ffn_w_down")
COL_SHARDED = ("w_in", "ffn_w_in", "gla_gate_w2", "lru_conv_w", "ffn_conv_w")
SMALL = ("ln_mix", "gla_gate_w2", "gla_gate_b", "gla_norm", "lru_conv_w", "lru_conv_b", "lru_wa", "lru_ba", "lru_wx",
         "lru_bx", "lru_lambda", "ln_ffn", "ffn_conv_w", "ffn_conv_b", "ln_final")
WEIGHTS = ("ln_mix", "w_in", "gla_gate_w2", "gla_gate_b", "gla_norm", "lru_conv_w", "lru_conv_b", "lru_wa", "lru_ba",
           "lru_wx", "lru_bx", "lru_lambda", "w_out", "ln_ffn", "ffn_w_in", "ffn_conv_w", "ffn_conv_b", "ffn_w_down",
           "ln_final")
PACK = SUBLANES * LANES


def _whole_from_shards(name, g):
    if name in COL_SHARDED:
        return jnp.moveaxis(g, 0, -2).reshape(g.shape[1:-1] + (N_CHIPS * g.shape[-1],))
    return jnp.moveaxis(g, 0, 1).reshape((g.shape[1], N_CHIPS * g.shape[2]) + g.shape[3:])


def _slabs_from_whole(name, w):
    L, r, c = w.shape
    if name in COL_SHARDED:
        s = jnp.moveaxis(w.reshape(L, r, N_CHIPS, c // N_CHIPS), 2, 0)
    else:
        s = jnp.moveaxis(w.reshape(L, N_CHIPS, r // N_CHIPS, c), 1, 0)
    return s.reshape(N_CHIPS, -1, s.shape[-1])


def _pack(arrs):
    flat = []
    for a in arrs:
        f = a.reshape(-1)
        flat.append(jnp.pad(f, (0, (-f.shape[0]) % PACK)))
    return jnp.concatenate(flat).reshape(-1, LANES)


def _unpack(packed, shapes):
    out, at = [], 0
    flat = packed.reshape(-1)
    for s in shapes:
        size = math.prod(s)
        out.append(flat[at:at + size].reshape(s))
        at += size + (-size) % PACK
    return out


def kernel(x, ln_mix, w_in, gla_gate_w2, gla_gate_b, gla_norm, lru_conv_w, lru_conv_b, lru_wa, lru_ba, lru_wx, lru_bx, lru_lambda, w_out, ln_ffn, ffn_w_in, ffn_conv_w, ffn_conv_b, ffn_w_down, ln_final, loss_target, m_ln_mix, m_w_in, m_gla_gate_w2, m_gla_gate_b, m_gla_norm, m_lru_conv_w, m_lru_conv_b, m_lru_wa, m_lru_ba, m_lru_wx, m_lru_bx, m_lru_lambda, m_w_out, m_ln_ffn, m_ffn_w_in, m_ffn_conv_w, m_ffn_conv_b, m_ffn_w_down, m_ln_final, v_ln_mix, v_w_in, v_gla_gate_w2, v_gla_gate_b, v_gla_norm, v_lru_conv_w, v_lru_conv_b, v_lru_wa, v_lru_ba, v_lru_wx, v_lru_bx, v_lru_lambda, v_w_out, v_ln_ffn, v_ffn_w_in, v_ffn_conv_w, v_ffn_conv_b, v_ffn_w_down, v_ln_final):
    w = dict(ln_mix=ln_mix, w_in=w_in, gla_gate_w2=gla_gate_w2, gla_gate_b=gla_gate_b, gla_norm=gla_norm,
             lru_conv_w=lru_conv_w, lru_conv_b=lru_conv_b, lru_wa=lru_wa, lru_ba=lru_ba, lru_wx=lru_wx, lru_bx=lru_bx,
             lru_lambda=lru_lambda, w_out=w_out, ln_ffn=ln_ffn, ffn_w_in=ffn_w_in, ffn_conv_w=ffn_conv_w,
             ffn_conv_b=ffn_conv_b, ffn_w_down=ffn_w_down, ln_final=ln_final)
    m = dict(ln_mix=m_ln_mix, w_in=m_w_in, gla_gate_w2=m_gla_gate_w2, gla_gate_b=m_gla_gate_b, gla_norm=m_gla_norm,
             lru_conv_w=m_lru_conv_w, lru_conv_b=m_lru_conv_b, lru_wa=m_lru_wa, lru_ba=m_lru_ba, lru_wx=m_lru_wx,
             lru_bx=m_lru_bx, lru_lambda=m_lru_lambda, w_out=m_w_out, ln_ffn=m_ln_ffn, ffn_w_in=m_ffn_w_in,
             ffn_conv_w=m_ffn_conv_w, ffn_conv_b=m_ffn_conv_b, ffn_w_down=m_ffn_w_down, ln_final=m_ln_final)
    v = dict(ln_mix=v_ln_mix, w_in=v_w_in, gla_gate_w2=v_gla_gate_w2, gla_gate_b=v_gla_gate_b, gla_norm=v_gla_norm,
             lru_conv_w=v_lru_conv_w, lru_conv_b=v_lru_conv_b, lru_wa=v_lru_wa, lru_ba=v_lru_ba, lru_wx=v_lru_wx,
             lru_bx=v_lru_bx, lru_lambda=v_lru_lambda, w_out=v_w_out, ln_ffn=v_ln_ffn, ffn_w_in=v_ffn_w_in,
             ffn_conv_w=v_ffn_conv_w, ffn_conv_b=v_ffn_conv_b, ffn_w_down=v_ffn_w_down, ln_final=v_ln_final)

    sharded = BIG + ("gla_gate_w2", "lru_conv_w", "ffn_conv_w")
    chip = 2 * lax.axis_index("x") + lax.axis_index("y")
    core = lax.axis_index("c")
    shards = [w[k].astype(MXU_DTYPE) if k in BIG else w[k] for k in sharded]
    gathered = _gather_shards(shards, "gather_weights")
    p = dict(w)
    for k, gk, own in zip(sharded, gathered, shards):
        p[k] = _whole_from_shards(k, lax.dynamic_update_index_in_dim(gk, own, chip, 0))

    loss, grad_x, grads = _local_step(x[0], loss_target[0], p)
    loss = lax.psum(loss[0, 0], ("x", "y", "c"))

    slabs = [grads[k].reshape(N_CHIPS, 2, grads[k].shape[1] // 2, grads[k].shape[2]) for k in BIG]
    got = _other_half_to_sibling(slabs, "other_half_to_sibling")
    chip_half = [_add_kept_half(s, g, BF16, f"core_sum_{k}") for k, s, g in zip(BIG, slabs, got)]
    small = _pack([grads[k] for k in SMALL])
    *recv, small_all = _exchange_grads(chip_half, small, "exchange_grads")
    own = [lax.dynamic_index_in_dim(h, chip, 0, keepdims=False) for h in chip_half]
    mine = [_sum_leading(r, f"chip_sum_{k}", last=o) for k, r, o in zip(BIG, recv, own)]
    theirs = _swap_with_sibling(mine, "swap_core_halves")
    big_g = [jnp.concatenate([jnp.where(core == 0, a, b), jnp.where(core == 0, b, a)]) for a, b in zip(mine, theirs)]
    small_all = lax.dynamic_update_index_in_dim(small_all, small, 2 * chip + core, 0)
    small_sum = _unpack(_sum_leading(small_all, "sum_small_grads"), [grads[k].shape for k in SMALL])

    me = 2 * lax.axis_index("x") + lax.axis_index("y")
    out_g, out_d, out_m, out_v = {}, {}, {}, {}
    for k, gk in zip(BIG, big_g):
        shape = w[k].shape
        cols = shape[-1]
        res = _adamw(w[k].reshape(-1, cols), gk, m[k].reshape(-1, cols), v[k].reshape(-1, cols), f"adamw_{k}")
        out_g[k] = gk.reshape(shape)
        out_d[k], out_m[k], out_v[k] = [r.reshape(shape) for r in res]
    small_g = []
    for k, gk in zip(SMALL, small_sum):
        if k in COL_SHARDED:
            width = w[k].shape[-1]
            gk = lax.dynamic_slice_in_dim(gk, me * width, width, axis=gk.ndim - 1)
        small_g.append(gk)
    shapes = [w[k].shape for k in SMALL]
    res = _adamw(_pack([w[k] for k in SMALL]), _pack(small_g), _pack([m[k] for k in SMALL]),
                 _pack([v[k] for k in SMALL]), "adamw_small")
    out_g.update(zip(SMALL, small_g))
    for out, packed in zip((out_d, out_m, out_v), res):
        for k, a in zip(SMALL, _unpack(packed, shapes)):
            out[k] = a
    return (loss, grad_x[None], *[out_g[k] for k in WEIGHTS], *[out_d[k] for k in WEIGHTS],
            *[out_m[k] for k in WEIGHTS], *[out_v[k] for k in WEIGHTS])
```

```python
import math

import jax
import jax.numpy as jnp
from jax import lax
from jax.experimental import pallas as pl
from jax.experimental.pallas import tpu as pltpu

F32 = jnp.float32
BF16 = jnp.bfloat16
MXU_DTYPE = BF16

D_MODEL = 1024
DEPTH = 4
HEADS, DK, DV, CHUNK, GATE_RANK = 4, 64, 128, 64, 16
QK_W = HEADS * DK
GLA_W = HEADS * DV
LRU_W = 512
LRU_BLOCKS, LRU_BLOCK, LRU_CONV, LRU_C = 8, 64, 4, 8.0
FFN_H = 3 * D_MODEL
FFN_CONV = 3
EPS = 1e-6
GATE_PAD = 128
GLA_COLS = 2 * QK_W + 2 * GLA_W + GATE_PAD
LRU_COLS = 2 * LRU_W
ADAM_LR, ADAM_B1, ADAM_B2, ADAM_EPS, ADAM_WD, ADAM_STEP = 0.001, 0.9, 0.999, 1e-08, 0.01, 10

LANES = 128
SUBLANES = 8
VMEM_LIMIT = 56 * 1024 * 1024
ROWS = 512
TILE_BYTES = 1 << 20
FFN_CW = 1024
FFN_SB = 256
FFN_RC = 32
N_CHIPS = 4
N_DEV = 8
MESH = pl.DeviceIdType.MESH


def _cp(*sem):
    return pltpu.CompilerParams(dimension_semantics=sem, vmem_limit_bytes=VMEM_LIMIT)


def _dot(a, b):
    return jnp.dot(a.astype(MXU_DTYPE), b.astype(MXU_DTYPE), preferred_element_type=F32)


def _dot_nt(a, b):
    return lax.dot_general(a.astype(MXU_DTYPE), b.astype(MXU_DTYPE), (((1,), (1,)), ((), ())),
                           preferred_element_type=F32)


def _dot_tn(a, b):
    return lax.dot_general(a.astype(MXU_DTYPE), b.astype(MXU_DTYPE), (((0,), (0,)), ((), ())),
                           preferred_element_type=F32)


def _bdot(eq, a, b):
    return jnp.einsum(eq, a, b, preferred_element_type=F32)


def _split3(x):
    x1 = x.astype(BF16)
    r1 = x - x1.astype(F32)
    x2 = r1.astype(BF16)
    x3 = (r1 - x2.astype(F32)).astype(BF16)
    return x1, x2, x3


GELU_C = math.sqrt(2.0 / math.pi)
GELU_A = 0.044715


def _gelu(x):
    return x * (0.5 * (1.0 + jnp.tanh(GELU_C * (x + GELU_A * (x * x * x)))))


def _gelu_and_grad(x):
    x2 = x * x
    t = jnp.tanh(x * (GELU_C + (GELU_C * GELU_A) * x2))
    cdf = 0.5 * t + 0.5
    half_sech2 = 0.5 - 0.5 * (t * t)
    return x * cdf, cdf + (x * half_sech2) * (GELU_C + (3.0 * GELU_C * GELU_A) * x2)


EXPM1_SERIES_BELOW = 0.1


def _expm1(x, exp_x):
    small = x * (1.0 + x * (0.5 + x * (1.0 / 6.0 + x * (1.0 / 24.0 + x * (1.0 / 120.0)))))
    return jnp.where(jnp.abs(x) < EXPM1_SERIES_BELOW, small, exp_x - 1.0)


def _shift_down(x, k, fill):
    row = lax.broadcasted_iota(jnp.int32, x.shape, 0)
    return jnp.where(row >= k, pltpu.roll(x, k, axis=0), fill)


def _shift_up(x, k, fill):
    n = x.shape[0]
    row = lax.broadcasted_iota(jnp.int32, x.shape, 0)
    return jnp.where(row < n - k, pltpu.roll(x, n - k, axis=0), fill)


def _rms_fwd(h, g, name):
    T, D = h.shape
    R = min(T, ROWS)

    def body(h_ref, g_ref, o_ref):
        x = h_ref[...]
        r = lax.rsqrt(jnp.mean(x * x, axis=-1, keepdims=True) + EPS)
        o_ref[...] = ((x * r) * g_ref[...]).astype(o_ref.dtype)

    return pl.pallas_call(
        body, name=name, grid=(T // R,),
        in_specs=[pl.BlockSpec((R, D), lambda i: (i, 0)), pl.BlockSpec((1, D), lambda i: (0, 0))],
        out_specs=pl.BlockSpec((R, D), lambda i: (i, 0)),
        out_shape=jax.ShapeDtypeStruct((T, D), BF16), compiler_params=_cp("parallel"),
    )(h, g.reshape(1, D))


def _rms_bwd_tile(x, g, du, dres):
    r = lax.rsqrt(jnp.mean(x * x, axis=-1, keepdims=True) + EPS)
    xhat = x * r
    dxhat = du * g
    dx = r * (dxhat - xhat * jnp.mean(dxhat * xhat, axis=-1, keepdims=True))
    return dres + dx, jnp.sum(du * xhat, axis=0, keepdims=True)


def _mm_rms_bwd(pairs, h, g, dres, name, tm=512):
    M, D = h.shape
    tm = min(tm, M)
    n = len(pairs)

    def body(*refs):
        h_ref, g_ref, dres_ref = refs[2 * n:2 * n + 3]
        dh_ref, dhb_ref, dg_ref = refs[2 * n + 3:]

        @pl.when(pl.program_id(0) == 0)
        def _():
            dg_ref[...] = jnp.zeros_like(dg_ref)

        du = _dot(refs[0][...], refs[1][...])
        for k in range(1, n):
            du = du + _dot(refs[2 * k][...], refs[2 * k + 1][...])
        dh, dg = _rms_bwd_tile(h_ref[...], g_ref[...], du, dres_ref[...])
        dg_ref[...] += dg
        dh_ref[...] = dh
        dhb_ref[...] = dh.astype(dhb_ref.dtype)

    in_specs, args = [], []
    for a, b in pairs:
        in_specs += [pl.BlockSpec((tm, a.shape[1]), lambda i: (i, 0)), pl.BlockSpec(b.shape, lambda i: (0, 0))]
        args += [a, b]
    blk = pl.BlockSpec((tm, D), lambda i: (i, 0))
    vec = pl.BlockSpec((1, D), lambda i: (0, 0))
    return pl.pallas_call(
        body, name=name, grid=(M // tm,), in_specs=in_specs + [blk, vec, blk], out_specs=[blk, blk, vec],
        out_shape=[jax.ShapeDtypeStruct((M, D), F32), jax.ShapeDtypeStruct((M, D), BF16),
                   jax.ShapeDtypeStruct((1, D), F32)],
        compiler_params=_cp("arbitrary"),
    )(*args, h, g.reshape(1, D), dres)


def _loss_head(h, g, tgt, name):
    T, D = h.shape
    R = min(T, ROWS)

    def body(h_ref, g_ref, t_ref, loss_ref, dh_ref, dhb_ref, dg_ref):
        @pl.when(pl.program_id(0) == 0)
        def _():
            dg_ref[...] = jnp.zeros_like(dg_ref)
            loss_ref[...] = jnp.zeros_like(loss_ref)

        x = h_ref[...]
        r = lax.rsqrt(jnp.mean(x * x, axis=-1, keepdims=True) + EPS)
        xhat = x * r
        gg = g_ref[...]
        err = xhat * gg - t_ref[...]
        loss_ref[...] += 0.5 * jnp.sum(jnp.mean(err * err, axis=-1, keepdims=True), axis=0, keepdims=True)
        dy = err * (1.0 / D)
        dg_ref[...] += jnp.sum(dy * xhat, axis=0, keepdims=True)
        dxhat = dy * gg
        dh = r * (dxhat - xhat * jnp.mean(dxhat * xhat, axis=-1, keepdims=True))
        dh_ref[...] = dh
        dhb_ref[...] = dh.astype(dhb_ref.dtype)

    blk = pl.BlockSpec((R, D), lambda i: (i, 0))
    vec = pl.BlockSpec((1, D), lambda i: (0, 0))
    one = pl.BlockSpec((1, LANES), lambda i: (0, 0))
    return pl.pallas_call(
        body, name=name, grid=(T // R,), in_specs=[blk, vec, blk], out_specs=[one, blk, blk, vec],
        out_shape=[jax.ShapeDtypeStruct((1, LANES), F32), jax.ShapeDtypeStruct((T, D), F32),
                   jax.ShapeDtypeStruct((T, D), BF16), jax.ShapeDtypeStruct((1, D), F32)],
        compiler_params=_cp("arbitrary"),
    )(h, g.reshape(1, D), tgt)


def _mm(a, b, res, out_dtype, name, tm=512, tn=None, norm_g=None):
    M, K = a.shape
    N = b.shape[1]
    tm = min(tm, M)
    tn = N if tn is None else tn
    assert norm_g is None or tn == N

    def body(*refs):
        refs = list(refs)
        a_ref, b_ref = refs[:2]
        acc = _dot(a_ref[...], b_ref[...])
        if res is not None:
            acc = refs[2][...].astype(F32) + acc
        if norm_g is None:
            refs[-1][...] = acc.astype(refs[-1].dtype)
        else:
            refs[-2][...] = acc.astype(refs[-2].dtype)
            r = lax.rsqrt(jnp.mean(acc * acc, axis=-1, keepdims=True) + EPS)
            refs[-1][...] = ((acc * r) * refs[-3][...]).astype(refs[-1].dtype)

    blk = pl.BlockSpec((tm, tn), lambda j, i: (i, j))
    in_specs = [pl.BlockSpec((tm, K), lambda j, i: (i, 0)), pl.BlockSpec((K, tn), lambda j, i: (0, j))]
    args = [a, b]
    if res is not None:
        in_specs.append(blk)
        args.append(res)
    out_specs, out_shape = blk, jax.ShapeDtypeStruct((M, N), out_dtype)
    if norm_g is not None:
        in_specs.append(pl.BlockSpec((1, N), lambda j, i: (0, 0)))
        args.append(norm_g.reshape(1, N))
        out_specs, out_shape = [blk, blk], [out_shape, jax.ShapeDtypeStruct((M, N), BF16)]
    return pl.pallas_call(
        body, name=name, grid=(N // tn, M // tm), in_specs=in_specs, out_specs=out_specs, out_shape=out_shape,
        compiler_params=_cp("parallel", "parallel"),
    )(*args)


def _mm_tn_into(slab, a, b, layer, chip0, name, tk, tn, tm=1024):
    M, K = a.shape
    N = b.shape[1]
    tm = min(tm, M)
    assert slab.shape[2] == tn and (K // tk == 1 or N // tn == 1)

    def body(a_ref, b_ref, slab_ref, o_ref):
        del slab_ref

        @pl.when(pl.program_id(2) == 0)
        def _():
            o_ref[...] = jnp.zeros_like(o_ref)

        o_ref[0] += _dot_tn(a_ref[...], b_ref[...])

    return pl.pallas_call(
        body, name=name, grid=(K // tk, N // tn, M // tm),
        in_specs=[pl.BlockSpec((tm, tk), lambda k, n, m: (m, k)), pl.BlockSpec((tm, tn), lambda k, n, m: (m, n)),
                  pl.BlockSpec(memory_space=pl.ANY)],
        out_specs=pl.BlockSpec((1, tk, tn), lambda k, n, m: (chip0 + k + n, layer, 0)),
        out_shape=jax.ShapeDtypeStruct(slab.shape, F32), input_output_aliases={2: 0},
        compiler_params=_cp("parallel", "parallel", "arbitrary"),
    )(a, b, slab)


def _mm_tn(a, b, name, tm=2048, tk=None, tn=None):
    M, K = a.shape
    N = b.shape[1]
    tm = min(tm, M)
    tk = K if tk is None else tk
    tn = N if tn is None else tn

    def body(a_ref, b_ref, o_ref):
        @pl.when(pl.program_id(2) == 0)
        def _():
            o_ref[...] = jnp.zeros_like(o_ref)

        o_ref[...] += _dot_tn(a_ref[...], b_ref[...])

    return pl.pallas_call(
        body, name=name, grid=(K // tk, N // tn, M // tm),
        in_specs=[pl.BlockSpec((tm, tk), lambda k, n, m: (m, k)), pl.BlockSpec((tm, tn), lambda k, n, m: (m, n))],
        out_specs=pl.BlockSpec((tk, tn), lambda k, n, m: (k, n)),
        out_shape=jax.ShapeDtypeStruct((K, N), F32), compiler_params=_cp("parallel", "parallel", "arbitrary"),
    )(a, b)


def _same_chunk(row, col):
    shift = CHUNK.bit_length() - 1
    return jnp.right_shift(row, shift) == jnp.right_shift(col, shift)


def _gla_common(q, k, glr, w2, b2, R):
    gl = _dot(glr, w2) + b2
    la = jax.nn.log_sigmoid(gl) * (1.0 / 16.0)
    row = lax.broadcasted_iota(jnp.int32, (R, R), 0)
    col = lax.broadcasted_iota(jnp.int32, (R, R), 1)
    same = _same_chunk(row, col)
    m_tri = (same & (col <= row)).astype(BF16)
    m_all = same.astype(BF16)
    la3 = _split3(la)
    b = sum(jnp.dot(m_tri, p, preferred_element_type=F32) for p in la3)
    bl = sum(jnp.dot(m_all, p, preferred_element_type=F32) for p in la3)
    eb = jnp.exp(b)
    enb = jnp.exp(-b)
    ek = jnp.exp(bl - b)
    qi = (q * (DK ** -0.5)) * eb
    ki = k * enb
    kd = k * ek
    return gl, la3, eb, enb, ek, qi, ki, kd


def _bsplit(x, n):
    return x.reshape(n, CHUNK, x.shape[-1])


def _tril():
    return (lax.broadcasted_iota(jnp.int32, (CHUNK, CHUNK), 1)
            <= lax.broadcasted_iota(jnp.int32, (CHUNK, CHUNK), 0))[None]


def _gla_fwd(proj, w2p, b2, norm_g, name):
    T = proj.shape[0]
    R = min(T, ROWS)
    n = R // CHUNK

    def body(q_ref, k_ref, v_ref, g_ref, a_ref, w2_ref, b2_ref, ng_ref, y_ref, o_ref, st_ref, s_ref):
        @pl.when(pl.program_id(0) == 0)
        def _():
            s_ref[...] = jnp.zeros_like(s_ref)

        _, la3, _, _, _, qi, ki, kd = _gla_common(q_ref[...], k_ref[...], a_ref[...], w2_ref[...], b2_ref[...], R)
        tril = _tril()
        ones = jnp.ones((n, CHUNK, DV), BF16)
        for h in range(HEADS):
            sl = slice(h * DK, (h + 1) * DK)
            sv = slice(h * DV, (h + 1) * DV)
            qh = _bsplit(qi[:, sl], n).astype(MXU_DTYPE)
            kh = _bsplit(ki[:, sl], n).astype(MXU_DTYPE)
            kdh = _bsplit(kd[:, sl], n).astype(MXU_DTYPE)
            vh = _bsplit(v_ref[:, sv], n).astype(MXU_DTYPE)
            att = jnp.where(tril, _bdot('ncd,nsd->ncs', qh, kh), 0.0)
            upd = _bdot('ncd,nce->nde', kdh, vh)
            dect = jnp.exp(sum(_bdot('ncd,nce->nde', _bsplit(p[:, sl], n), ones) for p in la3))
            s = s_ref[sl, :]
            for c in range(n):
                st_ref[c, sl, :] = s
                s = dect[c] * s + upd[c]
            s_ref[sl, :] = s
            sp = st_ref[:, sl, :].astype(MXU_DTYPE)
            o = (_bdot('ncs,nse->nce', att.astype(MXU_DTYPE), vh) + _bdot('ncd,nde->nce', qh, sp)).reshape(R, DV)
            o_ref[:, sv] = o
            r = lax.rsqrt(jnp.mean(o * o, axis=-1, keepdims=True) + EPS)
            gate = g_ref[:, sv]
            y_ref[:, sv] = (((o * r) * ng_ref[...]) * (gate * jax.nn.sigmoid(gate))).astype(y_ref.dtype)

    cb = lambda w, j: pl.BlockSpec((R, w), lambda i: (i, j))
    full = lambda s: pl.BlockSpec(s, lambda i: (0,) * len(s))
    return pl.pallas_call(
        body, name=name, grid=(T // R,),
        in_specs=[cb(QK_W, 0), cb(QK_W, 1), cb(GLA_W, 1), cb(GLA_W, 2), cb(GATE_PAD, 12),
                  full((GATE_PAD, QK_W)), full((1, QK_W)), full((1, DV))],
        out_specs=[pl.BlockSpec((R, GLA_W), lambda i: (i, 0)), pl.BlockSpec((R, GLA_W), lambda i: (i, 0)),
                   pl.BlockSpec((n, QK_W, DV), lambda i: (i, 0, 0))],
        out_shape=[jax.ShapeDtypeStruct((T, GLA_W + LRU_W), BF16), jax.ShapeDtypeStruct((T, GLA_W), F32),
                   jax.ShapeDtypeStruct((T // CHUNK, QK_W, DV), F32)],
        scratch_shapes=[pltpu.VMEM((QK_W, DV), F32)],
        compiler_params=_cp("arbitrary"),
    )(proj, proj, proj, proj, proj, w2p, b2.reshape(1, QK_W), norm_g.reshape(1, DV))


def _gla_bwd(dy, proj, o_st, s_st, w2p, b2, norm_g, name):
    T = proj.shape[0]
    R = min(T, ROWS)
    n = R // CHUNK
    nb = T // R

    def body(dy_ref, q_ref, k_ref, v_ref, g_ref, a_ref, o_ref, st_ref, w2_ref, b2_ref, ng_ref,
             dp_ref, dw2_ref, db2_ref, dng_ref, gs_ref, gn_ref, db_ref, dbl_ref):
        @pl.when(pl.program_id(0) == 0)
        def _():
            gs_ref[...] = jnp.zeros_like(gs_ref)
            dw2_ref[...] = jnp.zeros_like(dw2_ref)
            db2_ref[...] = jnp.zeros_like(db2_ref)
            dng_ref[...] = jnp.zeros_like(dng_ref)

        glr = a_ref[...]
        gl, la3, eb, enb, ek, qi, ki, kd = _gla_common(q_ref[...], k_ref[...], glr, w2_ref[...], b2_ref[...], R)
        tril = _tril()
        ones = jnp.ones((n, CHUNK, DV), BF16)
        ng = ng_ref[...]
        dng = jnp.zeros((1, DV), F32)
        for h in range(HEADS):
            sl = slice(h * DK, (h + 1) * DK)
            sv = slice(h * DV, (h + 1) * DV)
            o = o_ref[:, sv]
            r = lax.rsqrt(jnp.mean(o * o, axis=-1, keepdims=True) + EPS)
            xhat = o * r
            gate = g_ref[:, sv]
            sg = jax.nn.sigmoid(gate)
            dyh = dy_ref[:, sv].astype(F32)
            dp_ref[:, 2 * QK_W + GLA_W + h * DV:2 * QK_W + GLA_W + (h + 1) * DV] = (
                dyh * (xhat * ng) * (sg * (1.0 + gate * (1.0 - sg)))).astype(dp_ref.dtype)
            don = dyh * (gate * sg)
            dng = dng + jnp.sum(don * xhat, axis=0, keepdims=True)
            dxhat = don * ng
            do = r * (dxhat - xhat * jnp.mean(dxhat * xhat, axis=-1, keepdims=True))
            qf = _bsplit(qi[:, sl], n)
            kf = _bsplit(ki[:, sl], n)
            kdf = _bsplit(kd[:, sl], n)
            qh, kh, kdh = qf.astype(MXU_DTYPE), kf.astype(MXU_DTYPE), kdf.astype(MXU_DTYPE)
            vh = _bsplit(v_ref[:, sv], n).astype(MXU_DTYPE)
            doh = _bsplit(do, n).astype(MXU_DTYPE)
            spf = st_ref[:, sl, :]
            sp = spf.astype(MXU_DTYPE)
            att = jnp.where(tril, _bdot('ncd,nsd->ncs', qh, kh), 0.0).astype(MXU_DTYPE)
            datt = jnp.where(tril, _bdot('nce,nse->ncs', doh, vh), 0.0).astype(MXU_DTYPE)
            dv = _bdot('ncs,nce->nse', att, doh)
            dqi = _bdot('ncs,nsd->ncd', datt, kh) + _bdot('nce,nde->ncd', doh, sp)
            dki = _bdot('ncs,ncd->nsd', datt, qh)
            wgt = _bdot('ncd,nce->nde', qh, doh)
            dect = jnp.exp(sum(_bdot('ncd,nce->nde', _bsplit(p[:, sl], n), ones) for p in la3))
            g = gs_ref[sl, :]
            for c in reversed(range(n)):
                gn_ref[c] = g
                g = wgt[c] + dect[c] * g
            gs_ref[sl, :] = g
            gnf = gn_ref[...]
            gn = gnf.astype(MXU_DTYPE)
            dkd = _bdot('nce,nde->ncd', vh, gn)
            dv = dv + _bdot('ncd,nde->nce', kdh, gn)
            dp_ref[:, 2 * QK_W + h * DV:2 * QK_W + (h + 1) * DV] = dv.reshape(R, DV).astype(dp_ref.dtype)
            dbl = sum(_bdot('nce,nde->ncd', ones, p) for p in _split3(gnf * spf * dect))
            pk = dkd * kdf
            dbl = dbl + jnp.sum(pk, axis=1, keepdims=True)
            dbl_ref[:, sl] = dbl.reshape(R, DK)
            db_ref[:, sl] = (dqi * qf - dki * kf - pk).reshape(R, DK)
            dp_ref[:, sl] = ((dqi.reshape(R, DK) * (DK ** -0.5)) * eb[:, sl]).astype(dp_ref.dtype)
            dp_ref[:, QK_W + h * DK:QK_W + (h + 1) * DK] = (
                dki.reshape(R, DK) * enb[:, sl] + dkd.reshape(R, DK) * ek[:, sl]).astype(dp_ref.dtype)
        dng_ref[...] += dng
        row = lax.broadcasted_iota(jnp.int32, (R, R), 0)
        col = lax.broadcasted_iota(jnp.int32, (R, R), 1)
        m_rev = (_same_chunk(row, col) & (col >= row)).astype(BF16)
        dla = sum(jnp.dot(m_rev, p, preferred_element_type=F32) for p in _split3(db_ref[...])) + dbl_ref[...]
        dgl = (dla * (1.0 / 16.0)) * jax.nn.sigmoid(-gl)
        dp_ref[:, 2 * QK_W + 2 * GLA_W:GLA_COLS] = _dot_nt(dgl, w2_ref[...]).astype(dp_ref.dtype)
        dw2_ref[...] += _dot_tn(glr, dgl)
        db2_ref[...] += jnp.sum(dgl, axis=0, keepdims=True)

    cb = lambda w, j: pl.BlockSpec((R, w), lambda i: (nb - 1 - i, j))
    full = lambda s: pl.BlockSpec(s, lambda i: (0,) * len(s))
    return pl.pallas_call(
        body, name=name, grid=(nb,),
        in_specs=[cb(GLA_W, 0), cb(QK_W, 0), cb(QK_W, 1), cb(GLA_W, 1), cb(GLA_W, 2), cb(GATE_PAD, 12),
                  cb(GLA_W, 0), pl.BlockSpec((n, QK_W, DV), lambda i: (nb - 1 - i, 0, 0)),
                  full((GATE_PAD, QK_W)), full((1, QK_W)), full((1, DV))],
        out_specs=[pl.BlockSpec((R, GLA_COLS), lambda i: (nb - 1 - i, 0)),
                   full((GATE_PAD, QK_W)), full((1, QK_W)), full((1, DV))],
        out_shape=[jax.ShapeDtypeStruct((T, GLA_COLS), BF16), jax.ShapeDtypeStruct((GATE_PAD, QK_W), F32),
                   jax.ShapeDtypeStruct((1, QK_W), F32), jax.ShapeDtypeStruct((1, DV), F32)],
        scratch_shapes=[pltpu.VMEM((QK_W, DV), F32), pltpu.VMEM((n, DK, DV), F32),
                        pltpu.VMEM((R, QK_W), F32), pltpu.VMEM((R, QK_W), F32)],
        compiler_params=_cp("arbitrary"),
    )(dy, proj, proj, proj, proj, proj, o_st, s_st, w2p, b2.reshape(1, QK_W), norm_g.reshape(1, DV))


def _scan_scratch(R, W):
    return [pltpu.VMEM((W // LANES, R, LANES), F32), pltpu.VMEM((W // LANES, R, LANES), F32),
            pltpu.VMEM((W // LANES, R // SUBLANES, LANES), F32)]


def _scan_rows(a, u, c0, a_ref, u_ref, c_ref, out_ref, reverse):
    R, W = a.shape
    nt = R // SUBLANES
    shift = _shift_up if reverse else _shift_down
    a, u = a.reshape(nt, SUBLANES, W), u.reshape(nt, SUBLANES, W)
    sub = lax.broadcasted_iota(jnp.int32, (nt, SUBLANES, W), 1)
    for k in (1, 2, 4):
        inside = (sub < SUBLANES - k) if reverse else (sub >= k)
        turn = SUBLANES - k if reverse else k
        u = u + a * jnp.where(inside, pltpu.roll(u, turn, axis=1), 0.0)
        a = a * jnp.where(inside, pltpu.roll(a, turn, axis=1), 1.0)
    a, u = a.reshape(R, W), u.reshape(R, W)
    end = 0 if reverse else SUBLANES - 1
    edge = nt - 1 if reverse else 0
    for j in range(W // LANES):
        cols = slice(j * LANES, (j + 1) * LANES)
        a_ref[j] = a[:, cols]
        u_ref[j] = u[:, cols]
        at = a_ref.at[j][pl.ds(end, nt, stride=SUBLANES), :]
        ut = u_ref.at[j][pl.ds(end, nt, stride=SUBLANES), :]
        k = 1
        while k < nt:
            ut = ut + at * shift(ut, k, 0.0)
            at = at * shift(at, k, 1.0)
            k *= 2
        c_ref[j] = shift(ut + at * c0[:, cols], 1, 0.0)
        c_ref[j, edge:edge + 1, :] = c0[:, cols]
        for r in range(nt):
            rows = pl.ds(r * SUBLANES, SUBLANES)
            out_ref[rows, cols] = u_ref[j, rows, :] + a_ref[j, rows, :] * c_ref[j, r:r + 1, :]


def _lru_conv(ext_ref, cw_ref, cb_ref, R):
    xc = cb_ref[...] + ext_ref[pl.ds(SUBLANES - 3, R), :] * cw_ref[0:1, :]
    xc = xc + ext_ref[pl.ds(SUBLANES - 2, R), :] * cw_ref[1:2, :]
    xc = xc + ext_ref[pl.ds(SUBLANES - 1, R), :] * cw_ref[2:3, :]
    return xc + ext_ref[pl.ds(SUBLANES, R), :] * cw_ref[3:4, :]


def _lru_gates(xc, wa, ba, wx, bx, lam, first):
    r = jax.nn.sigmoid(_dot(xc, wa) + ba)
    ig = jax.nn.sigmoid(_dot(xc, wx) + bx)
    sp = jax.nn.softplus(-lam)
    la = (-LRU_C * r) * sp
    a = jnp.exp(la)
    mult = jnp.where(first, 1.0, jnp.sqrt(-_expm1(2.0 * la, a * a)))
    return r, ig, sp, a, mult


def _lru_fwd(proj, y_mix, cw, cb, wa, ba, wx, bx, lam, name):
    T = proj.shape[0]
    R = min(T, ROWS)
    W = LRU_W

    def body(xr_ref, xh_ref, xg_ref, cw_ref, cb_ref, wa_ref, ba_ref, wx_ref, bx_ref, lam_ref, mix_ref,
             y_ref, hs_ref, ext_ref, hc_ref, sa_ref, su_ref, sc_ref):
        del mix_ref
        i = pl.program_id(0)

        @pl.when(i == 0)
        def _():
            hc_ref[...] = jnp.zeros_like(hc_ref)

        ext_ref[0:SUBLANES, :] = jnp.where(i > 0, xh_ref[...], 0.0)
        ext_ref[pl.ds(SUBLANES, R), :] = xr_ref[...]
        xc = _lru_conv(ext_ref, cw_ref, cb_ref, R)
        row = lax.broadcasted_iota(jnp.int32, (R, W), 0)
        first = (row == 0) & (i == 0)
        _, ig, _, a, mult = _lru_gates(xc, wa_ref[...], ba_ref[...], wx_ref[...], bx_ref[...], lam_ref[...], first)
        _scan_rows(a, mult * (ig * xc), hc_ref[0:1, :], sa_ref, su_ref, sc_ref, hs_ref, reverse=False)
        hc_ref[0:1, :] = hs_ref[R - 1:R, :]
        y_ref[...] = (hs_ref[...] * _gelu(xg_ref[...])).astype(y_ref.dtype)

    rb = R // SUBLANES
    full = lambda s: pl.BlockSpec(s, lambda i: (0,) * len(s))
    return pl.pallas_call(
        body, name=name, grid=(T // R,),
        in_specs=[pl.BlockSpec((R, W), lambda i: (i, 0)),
                  pl.BlockSpec((SUBLANES, W), lambda i: (jnp.maximum(i * rb - 1, 0), 0)),
                  pl.BlockSpec((R, W), lambda i: (i, 1)),
                  full((SUBLANES, W)), full((1, W)), full((W, W)), full((1, W)), full((W, W)), full((1, W)),
                  full((1, W)), pl.BlockSpec(memory_space=pl.ANY)],
        out_specs=[pl.BlockSpec((R, W), lambda i: (i, 1)), pl.BlockSpec((R, W), lambda i: (i, 0))],
        out_shape=[jax.ShapeDtypeStruct(y_mix.shape, y_mix.dtype), jax.ShapeDtypeStruct((T, W), F32)],
        scratch_shapes=[pltpu.VMEM((R + SUBLANES, W), F32), pltpu.VMEM((SUBLANES, W), F32),
                        *_scan_scratch(R, W)],
        input_output_aliases={10: 0}, compiler_params=_cp("arbitrary"),
    )(proj, proj, proj, cw, cb, wa, ba, wx, bx, lam, y_mix)


def _lru_bwd(dy, proj, hs, cw, cb, wa, ba, wx, bx, lam, name):
    T = proj.shape[0]
    R = min(T, ROWS)
    W = LRU_W
    nb = T // R

    def body(dy_ref, xr_ref, xh_ref, xg_ref, hs_ref, hh_ref, cw_ref, cb_ref, wa_ref, ba_ref, wx_ref, bx_ref, lam_ref,
             dp_ref, dcw_ref, dvec_ref, dwa_ref, dwx_ref, ext_ref, ext2_ref, lc_ref, sa_ref, su_ref, sc_ref, adj_ref):
        ib = pl.program_id(0)
        i = nb - 1 - ib

        @pl.when(ib == 0)
        def _():
            lc_ref[...] = jnp.zeros_like(lc_ref)
            ext2_ref[pl.ds(R, SUBLANES), :] = jnp.zeros((SUBLANES, W), F32)
            dcw_ref[...] = jnp.zeros_like(dcw_ref)
            dvec_ref[...] = jnp.zeros_like(dvec_ref)
            dwa_ref[...] = jnp.zeros_like(dwa_ref)
            dwx_ref[...] = jnp.zeros_like(dwx_ref)

        ext_ref[0:SUBLANES, :] = jnp.where(i > 0, xh_ref[...], 0.0)
        ext_ref[pl.ds(SUBLANES, R), :] = xr_ref[...]
        xc = _lru_conv(ext_ref, cw_ref, cb_ref, R)
        row = lax.broadcasted_iota(jnp.int32, (R, W), 0)
        first = (row == 0) & (i == 0)
        lam = lam_ref[...]
        r, ig, sp, a, mult = _lru_gates(xc, wa_ref[...], ba_ref[...], wx_ref[...], bx_ref[...], lam, first)
        h = hs_ref[...]
        gel, dgel = _gelu_and_grad(xg_ref[...])
        dy = dy_ref[...].astype(F32)
        dp_ref[:, W:2 * W] = (dy * h * dgel).astype(dp_ref.dtype)
        _scan_rows(_shift_up(a, 1, 1.0), dy * gel, lc_ref[0:1, :], sa_ref, su_ref, sc_ref, adj_ref, reverse=True)
        v = adj_ref[...]
        lc_ref[...] = (a * v)[0:SUBLANES, :]
        hprev = _shift_down(h, 1, 0.0) + jnp.where((row == 0) & (i > 0), hh_ref[SUBLANES - 1:SUBLANES, :], 0.0)
        da = v * hprev
        dmult = jnp.where(first, 0.0, v * (ig * xc))
        dig = v * (mult * xc)
        dxc = v * (mult * ig)
        dla = da * a - dmult * ((a * a) / mult)
        dra = (dla * (-LRU_C * sp)) * (r * (1.0 - r))
        drx = dig * (ig * (1.0 - ig))
        dxc = dxc + _dot_nt(dra, wa_ref[...]) + _dot_nt(drx, wx_ref[...])
        dwa_ref[...] += _dot_tn(xc, dra)
        dwx_ref[...] += _dot_tn(xc, drx)
        dvec_ref[0:1, :] += jnp.sum(dxc, axis=0, keepdims=True)
        dvec_ref[1:2, :] += jnp.sum(dra, axis=0, keepdims=True)
        dvec_ref[2:3, :] += jnp.sum(drx, axis=0, keepdims=True)
        dvec_ref[3:4, :] += jnp.sum(dla * (-LRU_C * r), axis=0, keepdims=True) * (-jax.nn.sigmoid(-lam))
        ext2_ref[pl.ds(0, R), :] = dxc
        dxr = ext2_ref[pl.ds(0, R), :] * cw_ref[3:4, :]
        dxr = dxr + ext2_ref[pl.ds(1, R), :] * cw_ref[2:3, :]
        dxr = dxr + ext2_ref[pl.ds(2, R), :] * cw_ref[1:2, :]
        dxr = dxr + ext2_ref[pl.ds(3, R), :] * cw_ref[0:1, :]
        dp_ref[:, 0:W] = dxr.astype(dp_ref.dtype)
        for j in range(LRU_CONV):
            dcw_ref[j:j + 1, :] += jnp.sum(dxc * ext_ref[pl.ds(SUBLANES - 3 + j, R), :], axis=0, keepdims=True)
        ext2_ref[pl.ds(R, SUBLANES), :] = dxc[0:SUBLANES, :]

    rb = R // SUBLANES
    full = lambda s: pl.BlockSpec(s, lambda i: (0,) * len(s))
    blk = lambda j: pl.BlockSpec((R, W), lambda i: (nb - 1 - i, j))
    halo = pl.BlockSpec((SUBLANES, W), lambda i: (jnp.maximum((nb - 1 - i) * rb - 1, 0), 0))
    return pl.pallas_call(
        body, name=name, grid=(nb,),
        in_specs=[blk(1), blk(0), halo, blk(1), blk(0), halo,
                  full((SUBLANES, W)), full((1, W)), full((W, W)), full((1, W)), full((W, W)), full((1, W)),
                  full((1, W))],
        out_specs=[pl.BlockSpec((R, 2 * W), lambda i: (nb - 1 - i, 0)), full((SUBLANES, W)), full((SUBLANES, W)),
                   full((W, W)), full((W, W))],
        out_shape=[jax.ShapeDtypeStruct((T, 2 * W), BF16), jax.ShapeDtypeStruct((SUBLANES, W), F32),
                   jax.ShapeDtypeStruct((SUBLANES, W), F32), jax.ShapeDtypeStruct((W, W), F32),
                   jax.ShapeDtypeStruct((W, W), F32)],
        scratch_shapes=[pltpu.VMEM((R + SUBLANES, W), F32), pltpu.VMEM((R + SUBLANES, W), F32),
                        pltpu.VMEM((SUBLANES, W), F32), *_scan_scratch(R, W), pltpu.VMEM((R, W), F32)],
        compiler_params=_cp("arbitrary"),
    )(dy, proj, proj, proj, hs, hs, cw, cb, wa, ba, wx, bx, lam)


def _conv3_window(src, start, cs, w, b):
    win = src[pl.ds(start, FFN_RC + SUBLANES), cs]
    x2 = pltpu.roll(win, 2, axis=0)[SUBLANES:]
    x1 = pltpu.roll(win, 1, axis=0)[SUBLANES:]
    x0 = win[SUBLANES:]
    return ((b + x2 * w[0]) + x1 * w[1]) + x0 * w[2], (x2, x1, x0)


def _ffn_up_fwd(u2, fa, fg, cwa, cwg, cba, cbg, name):
    T, D = u2.shape
    Fh = fa.shape[1]
    tm = min(T, ROWS)
    CW, SB, RC = FFN_CW, FFN_SB, FFN_RC
    ns = CW // SB

    def body(u_ref, fa_ref, fg_ref, cwa_ref, cwg_ref, cba_ref, cbg_ref, xa_ref, xg_ref, p_ref, q_ref, act_ref,
             ka_ref, kg_ref, ea_ref, eg_ref, za_ref, zg_ref):
        @pl.when(pl.program_id(1) == 0)
        def _():
            ka_ref[...] = jnp.zeros_like(ka_ref)
            kg_ref[...] = jnp.zeros_like(kg_ref)

        def gate(s):
            cs = pl.ds(s * SB, SB)
            wa = [cwa_ref[j:j + 1, cs] for j in range(FFN_CONV)]
            wg = [cwg_ref[j:j + 1, cs] for j in range(FFN_CONV)]
            ba, bg = cba_ref[:, cs], cbg_ref[:, cs]
            ea_ref[0:SUBLANES, cs] = ka_ref[:, cs]
            ea_ref[pl.ds(SUBLANES, RC), cs] = za_ref[0:RC, cs]
            eg_ref[0:SUBLANES, cs] = kg_ref[:, cs]
            eg_ref[pl.ds(SUBLANES, RC), cs] = zg_ref[0:RC, cs]
            for c in range(tm // RC):
                sa, sg, start = (ea_ref, eg_ref, 0) if c == 0 else (za_ref, zg_ref, c * RC - SUBLANES)
                rows = pl.ds(c * RC, RC)
                a_c, xa = _conv3_window(sa, start, cs, wa, ba)
                g_c, xg = _conv3_window(sg, start, cs, wg, bg)
                gel, dgel = _gelu_and_grad(a_c)
                xa_ref[rows, cs] = xa[2].astype(xa_ref.dtype)
                xg_ref[rows, cs] = xg[2].astype(xg_ref.dtype)
                p_ref[rows, cs] = (g_c * dgel).astype(p_ref.dtype)
                q_ref[rows, cs] = gel.astype(q_ref.dtype)
                act_ref[rows, cs] = (gel * g_c).astype(act_ref.dtype)
            ka_ref[:, cs] = za_ref[tm - SUBLANES:tm, cs]
            kg_ref[:, cs] = zg_ref[tm - SUBLANES:tm, cs]

        za_ref[...] = _dot(u_ref[...], fa_ref[...])
        zg_ref[...] = _dot(u_ref[...], fg_ref[...])
        for s in range(ns):
            gate(s)

    blk = pl.BlockSpec((tm, CW), lambda j, i: (i, j))
    wblk = pl.BlockSpec((D, CW), lambda j, i: (0, j))
    w8 = pl.BlockSpec((SUBLANES, CW), lambda j, i: (0, j))
    w1 = pl.BlockSpec((1, CW), lambda j, i: (0, j))
    return pl.pallas_call(
        body, name=name, grid=(Fh // CW, T // tm),
        in_specs=[pl.BlockSpec((tm, D), lambda j, i: (i, 0)), wblk, wblk, w8, w8, w1, w1],
        out_specs=[blk] * 5,
        out_shape=[jax.ShapeDtypeStruct((T, Fh), BF16)] * 5,
        scratch_shapes=[pltpu.VMEM((SUBLANES, CW), F32), pltpu.VMEM((SUBLANES, CW), F32),
                        pltpu.VMEM((RC + SUBLANES, CW), F32), pltpu.VMEM((RC + SUBLANES, CW), F32),
                        pltpu.VMEM((tm, CW), F32), pltpu.VMEM((tm, CW), F32)],
        compiler_params=_cp("parallel", "arbitrary"),
    )(u2, fa, fg, cwa, cwg, cba, cbg)


def _ffn_bwd_core(dh, dhb, xa, xg, p, q, wdT, faT, fgT, cwa, cwg, h, g, name):
    T, D = dhb.shape
    Fh = xa.shape[1]
    tm = min(T, ROWS)
    CW, SB, RC = FFN_CW, FFN_SB, FFN_RC
    ns = CW // SB
    nj = Fh // CW
    nb = T // tm
    nc = tm // RC

    def body(dh_ref, xa_ref, xg_ref, p_ref, q_ref, wd_ref, fa_ref, fg_ref, cwa_ref, cwg_ref, h_ref, g_ref, res_ref,
             dza_ref, dzg_ref, dca_ref, dcg_ref, dho_ref, dhbo_ref, dg_ref,
             d_ref, sa_ref, sg_ref, ka_ref, kg_ref, du_ref):
        ib, j = pl.program_id(0), pl.program_id(1)

        @pl.when((ib == 0) & (j == 0))
        def _():
            dca_ref[...] = jnp.zeros_like(dca_ref)
            dcg_ref[...] = jnp.zeros_like(dcg_ref)
            dg_ref[...] = jnp.zeros_like(dg_ref)

        @pl.when(ib == 0)
        def _():
            ka_ref[j] = jnp.zeros((SUBLANES, CW), F32)
            kg_ref[j] = jnp.zeros((SUBLANES, CW), F32)

        @pl.when(j == 0)
        def _():
            du_ref[...] = jnp.zeros_like(du_ref)

        def fold(v):
            return jnp.sum(v.reshape(RC // SUBLANES, SUBLANES, SB), axis=0)

        def gate(s):
            cs = pl.ds(s * SB, SB)
            wa = [cwa_ref[t:t + 1, cs] for t in range(FFN_CONV)]
            wg = [cwg_ref[t:t + 1, cs] for t in range(FFN_CONV)]
            sa_ref[pl.ds(tm, SUBLANES), cs] = ka_ref[j, :, cs]
            sg_ref[pl.ds(tm, SUBLANES), cs] = kg_ref[j, :, cs]
            for c in range(nc):
                rows = pl.ds(c * RC, RC)
                dact = d_ref[rows, cs]
                sa_ref[rows, cs] = dact * p_ref[rows, cs].astype(F32)
                sg_ref[rows, cs] = dact * q_ref[rows, cs].astype(F32)
            n = RC + SUBLANES
            for s_ref, x_ref, o_ref, dc_ref, w in ((sa_ref, xa_ref, dza_ref, dca_ref, wa),
                                                   (sg_ref, xg_ref, dzg_ref, dcg_ref, wg)):
                acc = [jnp.zeros((SUBLANES, SB), F32) for _ in range(FFN_CONV + 1)]
                for c in range(nc):
                    rows = pl.ds(c * RC, RC)
                    win = s_ref[pl.ds(c * RC, n), cs]
                    d0 = win[:RC]
                    d1 = pltpu.roll(win, n - 1, axis=0)[:RC]
                    d2 = pltpu.roll(win, n - 2, axis=0)[:RC]
                    o_ref[rows, cs] = ((d0 * w[2] + d1 * w[1]) + d2 * w[0]).astype(o_ref.dtype)
                    x = x_ref[rows, cs].astype(F32)
                    acc = [acc[0] + fold(d2 * x), acc[1] + fold(d1 * x), acc[2] + fold(d0 * x), acc[3] + fold(d0)]
                for t in range(FFN_CONV + 1):
                    dc_ref[j, t:t + 1, cs] += jnp.sum(acc[t], axis=0, keepdims=True)
            ka_ref[j, :, cs] = sa_ref[0:SUBLANES, cs]
            kg_ref[j, :, cs] = sg_ref[0:SUBLANES, cs]

        d_ref[...] = _dot(dh_ref[...], wd_ref[...])
        for s in range(ns):
            gate(s)
        du_ref[...] += _dot(dza_ref[...], fa_ref[...]) + _dot(dzg_ref[...], fg_ref[...])

        @pl.when(j == nj - 1)
        def _():
            dho, dg = _rms_bwd_tile(h_ref[...], g_ref[...], du_ref[...], res_ref[...])
            dg_ref[...] += dg
            dho_ref[...] = dho
            dhbo_ref[...] = dho.astype(dhbo_ref.dtype)

    blk = pl.BlockSpec((tm, CW), lambda ib, j: (nb - 1 - ib, j))
    row = pl.BlockSpec((tm, D), lambda ib, j: (nb - 1 - ib, 0))
    vec = pl.BlockSpec((1, D), lambda ib, j: (0, 0))
    w8 = pl.BlockSpec((SUBLANES, CW), lambda ib, j: (0, j))
    wrow = pl.BlockSpec((CW, D), lambda ib, j: (j, 0))
    acc = pl.BlockSpec((nj, SUBLANES, CW), lambda ib, j: (0, 0, 0))
    return pl.pallas_call(
        body, name=name, grid=(nb, nj),
        in_specs=[row, blk, blk, blk, blk, pl.BlockSpec((D, CW), lambda ib, j: (0, j)), wrow, wrow, w8, w8,
                  row, vec, row],
        out_specs=[blk, blk, acc, acc, row, row, vec],
        out_shape=[jax.ShapeDtypeStruct((T, Fh), BF16), jax.ShapeDtypeStruct((T, Fh), BF16),
                   jax.ShapeDtypeStruct((nj, SUBLANES, CW), F32), jax.ShapeDtypeStruct((nj, SUBLANES, CW), F32),
                   jax.ShapeDtypeStruct((T, D), F32), jax.ShapeDtypeStruct((T, D), BF16),
                   jax.ShapeDtypeStruct((1, D), F32)],
        scratch_shapes=[pltpu.VMEM((tm, CW), F32), pltpu.VMEM((tm + SUBLANES, CW), F32),
                        pltpu.VMEM((tm + SUBLANES, CW), F32), pltpu.VMEM((nj, SUBLANES, CW), F32),
                        pltpu.VMEM((nj, SUBLANES, CW), F32), pltpu.VMEM((tm, D), F32)],
        compiler_params=_cp("arbitrary", "arbitrary"),
    )(dhb, xa, xg, p, q, wdT, faT, fgT, cwa, cwg, h, g.reshape(1, D), dh)


def _adamw(w, g, m, v, name):
    rows, cols = w.shape
    tr = _row_tile(rows, max(SUBLANES, min(512, TILE_BYTES // (4 * cols)) // SUBLANES * SUBLANES))

    def body(w_ref, g_ref, m_ref, v_ref, d_ref, mo_ref, vo_ref):
        g = g_ref[...]
        mm = ADAM_B1 * m_ref[...] + (1.0 - ADAM_B1) * g
        vv = ADAM_B2 * v_ref[...] + (1.0 - ADAM_B2) * (g * g)
        m_hat = mm / (1.0 - ADAM_B1 ** ADAM_STEP)
        v_hat = vv / (1.0 - ADAM_B2 ** ADAM_STEP)
        d_ref[...] = -ADAM_LR * (m_hat / (jnp.sqrt(v_hat) + ADAM_EPS) + ADAM_WD * w_ref[...])
        mo_ref[...] = mm
        vo_ref[...] = vv

    blk = pl.BlockSpec((tr, cols), lambda i: (i, 0))
    return pl.pallas_call(
        body, name=name, grid=(rows // tr,), in_specs=[blk] * 4, out_specs=[blk] * 3,
        out_shape=[jax.ShapeDtypeStruct((rows, cols), F32)] * 3, compiler_params=_cp("parallel"),
    )(w, g, m, v)


def _add_slabs(a, b, out_dtype, name):
    n, rows, cols = a.shape
    tr = _row_tile(rows, max(SUBLANES, min(512, TILE_BYTES // (4 * cols)) // SUBLANES * SUBLANES))

    def body(a_ref, b_ref, o_ref):
        o_ref[...] = (a_ref[...] + b_ref[...]).astype(o_ref.dtype)

    blk = pl.BlockSpec((1, tr, cols), lambda k, i: (k, i, 0))
    return pl.pallas_call(
        body, name=name, grid=(n, rows // tr), in_specs=[blk, blk], out_specs=blk,
        out_shape=jax.ShapeDtypeStruct((n, rows, cols), out_dtype), compiler_params=_cp("parallel", "parallel"),
    )(a, b)


def _sum_leading(parts, name, last=None):
    n, rows, cols = parts.shape
    tr = _row_tile(rows, max(SUBLANES, min(512, TILE_BYTES // (4 * cols)) // SUBLANES * SUBLANES))

    def body(*refs):
        p_ref, o_ref = refs[0], refs[-1]
        acc = p_ref[0].astype(F32)
        for d in range(1, n):
            acc = acc + p_ref[d].astype(F32)
        if last is not None:
            acc = acc + refs[1][...].astype(F32)
        o_ref[...] = acc

    blk = pl.BlockSpec((tr, cols), lambda i: (i, 0))
    return pl.pallas_call(
        body, name=name, grid=(rows // tr,),
        in_specs=[pl.BlockSpec((n, tr, cols), lambda i: (0, i, 0))] + ([] if last is None else [blk]), out_specs=blk,
        out_shape=jax.ShapeDtypeStruct((rows, cols), F32), compiler_params=_cp("parallel"),
    )(*((parts,) if last is None else (parts, last)))


def _row_tile(rows, cap=512):
    if rows <= cap:
        return rows
    return max(t for t in range(SUBLANES, cap + 1, SUBLANES) if rows % t == 0)


def _place():
    return lax.axis_index("x"), lax.axis_index("y"), lax.axis_index("c")


def _gather_shards(arrs, name):
    n = len(arrs)

    def body(*refs):
        ins, outs = refs[:n], refs[n:2 * n]
        send_sems, recv_sems, pass_send, pass_recv = refs[2 * n:]
        x, y, c = _place()
        chips = [(1 - x, y), (x, 1 - y), (1 - x, 1 - y)]
        mine, theirs = c, 1 - c

        def half(a, which):
            hl = ins[a].shape[0] // 2
            return pl.ds(which * hl, hl)

        def send(a, j, shard):
            px, py = chips[j]
            return pltpu.make_async_remote_copy(
                src_ref=ins[a].at[half(a, mine)], dst_ref=outs[a].at[shard, half(a, mine)],
                send_sem=send_sems.at[3 * a + j], recv_sem=recv_sems.at[3 * a + j], device_id=(px, py, c),
                device_id_type=MESH)

        def passed(a, j, which):
            px, py = chips[j]
            blk = outs[a].at[2 * px + py, half(a, which)]
            return pltpu.make_async_remote_copy(
                src_ref=blk, dst_ref=blk, send_sem=pass_send.at[3 * a + j], recv_sem=pass_recv.at[3 * a + j],
                device_id=(x, y, 1 - c), device_id_type=MESH)

        sends = [send(a, j, 2 * x + y) for a in range(n) for j in range(3)]
        for cp in sends:
            cp.start()
        passes = []
        for a in range(n):
            for j, (px, py) in enumerate(chips):
                send(a, j, 2 * px + py).wait_recv()
                passes.append(passed(a, j, mine))
                passes[-1].start()
        for a in range(n):
            for j in range(3):
                passed(a, j, theirs).wait_recv()
        for cp in sends + passes:
            cp.wait_send()

    hbm = pl.BlockSpec(memory_space=pl.ANY)
    return pl.pallas_call(
        body, name=name, in_specs=[hbm] * n, out_specs=[hbm] * n,
        out_shape=[jax.ShapeDtypeStruct((N_CHIPS,) + a.shape, a.dtype) for a in arrs],
        scratch_shapes=[pltpu.SemaphoreType.DMA((3 * n,)), pltpu.SemaphoreType.DMA((3 * n,)),
                        pltpu.SemaphoreType.DMA((3 * n,)), pltpu.SemaphoreType.DMA((3 * n,))],
        compiler_params=pltpu.CompilerParams(has_side_effects=True),
    )(*arrs)


def _other_half_to_sibling(slabs, small, name):
    n = len(slabs)

    def body(*refs):
        ins, small_ref = refs[:n], refs[n]
        got, small_out = refs[n + 1:2 * n + 1], refs[2 * n + 1]
        send_sems, recv_sems, ssend, srecv = refs[2 * n + 2:]
        x, y, c = _place()
        me = 4 * x + 2 * y + c
        flips = [(fx, fy, fc) for fx in (0, 1) for fy in (0, 1) for fc in (0, 1)][1:]

        def scopy(k, row):
            fx, fy, fc = flips[k]
            return pltpu.make_async_remote_copy(
                src_ref=small_ref, dst_ref=small_out.at[row], send_sem=ssend.at[k], recv_sem=srecv.at[k],
                device_id=(x ^ fx, y ^ fy, c ^ fc), device_id_type=MESH)

        copies = [pltpu.make_async_remote_copy(
            src_ref=ins[a].at[:, 1 - c], dst_ref=got[a], send_sem=send_sems.at[a], recv_sem=recv_sems.at[a],
            device_id=(x, y, 1 - c), device_id_type=MESH) for a in range(n)]
        smalls = [scopy(k, me) for k in range(7)]
        for cp in smalls + copies:
            cp.start()
        for k, (fx, fy, fc) in enumerate(flips):
            scopy(k, 4 * (x ^ fx) + 2 * (y ^ fy) + (c ^ fc)).wait_recv()
        for cp in copies:
            cp.wait()
        for cp in smalls:
            cp.wait_send()

    hbm = pl.BlockSpec(memory_space=pl.ANY)
    res = pl.pallas_call(
        body, name=name, in_specs=[hbm] * (n + 1), out_specs=[hbm] * (n + 1),
        out_shape=[jax.ShapeDtypeStruct((s.shape[0],) + s.shape[2:], s.dtype) for s in slabs]
        + [jax.ShapeDtypeStruct((N_DEV,) + small.shape, small.dtype)],
        scratch_shapes=[pltpu.SemaphoreType.DMA((n,)), pltpu.SemaphoreType.DMA((n,)),
                        pltpu.SemaphoreType.DMA((7,)), pltpu.SemaphoreType.DMA((7,))],
        compiler_params=pltpu.CompilerParams(has_side_effects=True),
    )(*slabs, small)
    return res[:n], res[n]


def _exchange_grads(slabs, name):
    n = len(slabs)

    def body(*refs):
        ins, outs = refs[:n], refs[n:2 * n]
        send_sems, recv_sems = refs[2 * n:]
        x, y, c = _place()
        chips = [(1 - x, y), (x, 1 - y), (1 - x, 1 - y)]

        def copy(a, j):
            px, py = chips[j]
            return pltpu.make_async_remote_copy(
                src_ref=ins[a].at[2 * px + py], dst_ref=outs[a].at[j], send_sem=send_sems.at[3 * a + j],
                recv_sem=recv_sems.at[3 * a + j], device_id=(px, py, c), device_id_type=MESH)

        sends = [copy(a, j) for a in range(n) for j in range(3)]
        for cp in sends:
            cp.start()
        for a in range(n):
            for j in range(3):
                copy(a, j).wait_recv()
        for cp in sends:
            cp.wait_send()

    hbm = pl.BlockSpec(memory_space=pl.ANY)
    return pl.pallas_call(
        body, name=name, in_specs=[hbm] * n, out_specs=[hbm] * n,
        out_shape=[jax.ShapeDtypeStruct((3,) + s.shape[1:], s.dtype) for s in slabs],
        scratch_shapes=[pltpu.SemaphoreType.DMA((3 * n,)), pltpu.SemaphoreType.DMA((3 * n,))],
        compiler_params=pltpu.CompilerParams(has_side_effects=True),
    )(*slabs)


def _swap_with_sibling(arrs, name):
    n = len(arrs)

    def body(*refs):
        ins, outs = refs[:n], refs[n:2 * n]
        send_sems, recv_sems = refs[2 * n:]
        x, y, c = _place()
        copies = [pltpu.make_async_remote_copy(
            src_ref=ins[a], dst_ref=outs[a], send_sem=send_sems.at[a], recv_sem=recv_sems.at[a],
            device_id=(x, y, 1 - c), device_id_type=MESH) for a in range(n)]
        for cp in copies:
            cp.start()
        for cp in copies:
            cp.wait()

    hbm = pl.BlockSpec(memory_space=pl.ANY)
    return pl.pallas_call(
        body, name=name, in_specs=[hbm] * n, out_specs=[hbm] * n,
        out_shape=[jax.ShapeDtypeStruct(a.shape, a.dtype) for a in arrs],
        scratch_shapes=[pltpu.SemaphoreType.DMA((n,)), pltpu.SemaphoreType.DMA((n,))],
        compiler_params=pltpu.CompilerParams(has_side_effects=True),
    )(*arrs)


def _block_diag(w):
    eye = jnp.eye(LRU_BLOCKS, dtype=w.dtype)
    return (eye[:, None, :, None] * w[:, :, None, :]).reshape(LRU_W, LRU_W)


def _diag_blocks(m):
    m4 = m.reshape(LRU_BLOCKS, LRU_BLOCK, LRU_BLOCKS, LRU_BLOCK)
    return jnp.stack([m4[b, :, b, :] for b in range(LRU_BLOCKS)])


def _pad_rows(a, rows):
    return jnp.pad(a, ((0, rows - a.shape[0]), (0, 0)))


def _layer_weights(p, l):
    w_in = p["w_in"][l]
    n_gla = 2 * QK_W + 2 * GLA_W
    gate = jnp.pad(w_in[:, n_gla:n_gla + GATE_RANK], ((0, 0), (0, GATE_PAD - GATE_RANK)))
    wg = jnp.concatenate([w_in[:, :n_gla], gate], axis=1)
    wl = w_in[:, n_gla + GATE_RANK:]
    w_out = p["w_out"][l]
    fa, fg = p["ffn_w_in"][l][:, :FFN_H], p["ffn_w_in"][l][:, FFN_H:]
    wd = p["ffn_w_down"][l]
    return dict(
        wg=wg, wl=wl, wgT=wg.T, wlT=wl.T, wo=w_out, woT=w_out.T,
        fa=fa, fg=fg, faT=fa.T, fgT=fg.T, wd=wd, wdT=wd.T,
        w2p=_pad_rows(p["gla_gate_w2"][l], GATE_PAD).astype(BF16),
        wa=_block_diag(p["lru_wa"][l]).astype(BF16), wx=_block_diag(p["lru_wx"][l]).astype(BF16),
        lcw=_pad_rows(p["lru_conv_w"][l], SUBLANES),
        fcwa=_pad_rows(p["ffn_conv_w"][l][:, :FFN_H], SUBLANES), fcwg=_pad_rows(p["ffn_conv_w"][l][:, FFN_H:], SUBLANES),
    )


def _local_step(x, tgt, p):
    row = lambda v: v.reshape(1, -1)
    h = x
    u = _rms_fwd(h, p["ln_mix"][0], "mix_norm_fwd0")
    stash = []
    for l in range(DEPTH):
        w = _layer_weights(p, l)
        s = dict(w=w, h0=h)
        pg = _mm(u, w["wg"], None, F32, f"proj_gla_fwd{l}")
        plr = _mm(u, w["wl"], None, F32, f"proj_lru_fwd{l}")
        ym, o_st, s_st = _gla_fwd(pg, w["w2p"], p["gla_gate_b"][l], p["gla_norm"][l], f"gla_fwd{l}")
        ym, hs = _lru_fwd(plr, ym, w["lcw"], row(p["lru_conv_b"][l]), w["wa"], row(p["lru_ba"][l]), w["wx"],
                          row(p["lru_bx"][l]), row(p["lru_lambda"][l]), f"lru_fwd{l}")
        h, u2 = _mm(ym, w["wo"], h, F32, f"out_fwd{l}", norm_g=p["ln_ffn"][l])
        s.update(u=u, pg=pg, plr=plr, ym=ym, o_st=o_st, s_st=s_st, hs=hs, h1=h)
        cba, cbg = row(p["ffn_conv_b"][l][:FFN_H]), row(p["ffn_conv_b"][l][FFN_H:])
        *kept, act = _ffn_up_fwd(u2, w["fa"], w["fg"], w["fcwa"], w["fcwg"], cba, cbg, f"ffn_up_fwd{l}")
        if l + 1 < DEPTH:
            h, u = _mm(act, w["wd"], h, F32, f"ffn_down_fwd{l}", norm_g=p["ln_mix"][l + 1])
        else:
            h = _mm(act, w["wd"], h, F32, f"ffn_down_fwd{l}")
        s.update(u2=u2, ffn_kept=kept, act=act)
        stash.append(s)

    loss, dh, dhb, d_ln_final = _loss_head(h, p["ln_final"], tgt, "loss_head")

    g = {k: [None] * DEPTH for k in ("ln_mix", "w_in", "gla_gate_w2", "gla_gate_b", "gla_norm", "lru_conv_w",
                                     "lru_conv_b", "lru_wa", "lru_ba", "lru_wx", "lru_bx", "lru_lambda",
                                     "ln_ffn", "ffn_conv_w", "ffn_conv_b")}
    slab = dict(w_out=lax.empty((N_CHIPS, DEPTH * D_MODEL // N_CHIPS, D_MODEL), F32),
                ffn_w_in=lax.empty((N_CHIPS, DEPTH * D_MODEL, 2 * FFN_H // N_CHIPS), F32),
                ffn_w_down=lax.empty((N_CHIPS, DEPTH * FFN_H // N_CHIPS, D_MODEL), F32))
    n_gla = 2 * QK_W + 2 * GLA_W
    for l in reversed(range(DEPTH)):
        s = stash[l]
        w = s["w"]
        slab["ffn_w_down"] = _mm_tn_into(slab["ffn_w_down"], s["act"], dhb, l, 0, f"ffn_down_dw{l}",
                                         tk=FFN_H // N_CHIPS, tn=D_MODEL, tm=2048)
        dza, dzg, dca, dcg, dh, dhb, dln = _ffn_bwd_core(
            dh, dhb, *s["ffn_kept"], w["wdT"], w["faT"], w["fgT"], w["fcwa"], w["fcwg"], s["h1"], p["ln_ffn"][l],
            f"ffn_bwd_core{l}")
        dca, dcg = (jnp.moveaxis(d, 0, 1).reshape(SUBLANES, FFN_H) for d in (dca, dcg))
        g["ffn_conv_w"][l] = jnp.concatenate([dca[:FFN_CONV], dcg[:FFN_CONV]], axis=1)
        g["ffn_conv_b"][l] = jnp.concatenate([dca[FFN_CONV], dcg[FFN_CONV]])
        for half, dz in enumerate((dza, dzg)):
            slab["ffn_w_in"] = _mm_tn_into(slab["ffn_w_in"], s["u2"], dz, l, 2 * half, f"ffn_in_dw{l}_{half}",
                                           tk=D_MODEL, tn=2 * FFN_H // N_CHIPS)
        g["ln_ffn"][l] = dln[0]
        slab["w_out"] = _mm_tn_into(slab["w_out"], s["ym"], dhb, l, 0, f"out_dw{l}",
                                    tk=D_MODEL // N_CHIPS, tn=D_MODEL, tm=2048)
        dyc = _mm(dhb, w["woT"], None, F32, f"out_dx{l}")
        dpg, dw2, db2, dng = _gla_bwd(dyc, s["pg"], s["o_st"], s["s_st"], w["w2p"], p["gla_gate_b"][l],
                                      p["gla_norm"][l], f"gla_bwd{l}")
        dpl, dcw, dvec, dwa, dwx = _lru_bwd(dyc, s["plr"], s["hs"], w["lcw"], row(p["lru_conv_b"][l]), w["wa"],
                                            row(p["lru_ba"][l]), w["wx"], row(p["lru_bx"][l]),
                                            row(p["lru_lambda"][l]), f"lru_bwd{l}")
        g["gla_gate_w2"][l] = dw2[:GATE_RANK]
        g["gla_gate_b"][l] = db2[0]
        g["gla_norm"][l] = dng[0]
        g["lru_conv_w"][l] = dcw[:LRU_CONV]
        g["lru_conv_b"][l], g["lru_ba"][l], g["lru_bx"][l], g["lru_lambda"][l] = dvec[0], dvec[1], dvec[2], dvec[3]
        g["lru_wa"][l], g["lru_wx"][l] = _diag_blocks(dwa), _diag_blocks(dwx)
        dwg = _mm_tn(s["u"], dpg, f"proj_gla_dw{l}")
        dwl = _mm_tn(s["u"], dpl, f"proj_lru_dw{l}")
        g["w_in"][l] = jnp.concatenate([dwg[:, :n_gla + GATE_RANK], dwl], axis=1)
        dh, dhb, dln = _mm_rms_bwd([(dpg, w["wgT"]), (dpl, w["wlT"])], s["h0"], p["ln_mix"][l], dh, f"proj_dx{l}")
        g["ln_mix"][l] = dln[0]
    grads = {k: jnp.stack(v) for k, v in g.items()}
    grads["w_in"] = _slabs_from_whole("w_in", grads["w_in"])
    grads.update(slab)
    grads["ln_final"] = d_ln_final[0]
    return loss, dh, grads


BIG = ("w_in", "w_out", "ffn_w_in", "ffn_w_down")
COL_SHARDED = ("w_in", "ffn_w_in", "gla_gate_w2", "lru_conv_w", "ffn_conv_w")
SMALL = ("ln_mix", "gla_gate_w2", "gla_gate_b", "gla_norm", "lru_conv_w", "lru_conv_b", "lru_wa", "lru_ba", "lru_wx",
         "lru_bx", "lru_lambda", "ln_ffn", "ffn_conv_w", "ffn_conv_b", "ln_final")
WEIGHTS = ("ln_mix", "w_in", "gla_gate_w2", "gla_gate_b", "gla_norm", "lru_conv_w", "lru_conv_b", "lru_wa", "lru_ba",
           "lru_wx", "lru_bx", "lru_lambda", "w_out", "ln_ffn", "ffn_w_in", "ffn_conv_w", "ffn_conv_b", "ffn_w_down",
           "ln_final")
PACK = SUBLANES * LANES


def _whole_from_shards(name, g):
    if name in COL_SHARDED:
        return jnp.moveaxis(g, 0, -2).reshape(g.shape[1:-1] + (N_CHIPS * g.shape[-1],))
    return jnp.moveaxis(g, 0, 1).reshape((g.shape[1], N_CHIPS * g.shape[2]) + g.shape[3:])


def _slabs_from_whole(name, w):
    L, r, c = w.shape
    if name in COL_SHARDED:
        s = jnp.moveaxis(w.reshape(L, r, N_CHIPS, c // N_CHIPS), 2, 0)
    else:
        s = jnp.moveaxis(w.reshape(L, N_CHIPS, r // N_CHIPS, c), 1, 0)
    return s.reshape(N_CHIPS, -1, s.shape[-1])


def _pack(arrs):
    flat = []
    for a in arrs:
        f = a.reshape(-1)
        flat.append(jnp.pad(f, (0, (-f.shape[0]) % PACK)))
    return jnp.concatenate(flat).reshape(-1, LANES)


def _unpack(packed, shapes):
    out, at = [], 0
    flat = packed.reshape(-1)
    for s in shapes:
        size = math.prod(s)
        out.append(flat[at:at + size].reshape(s))
        at += size + (-size) % PACK
    return out


def kernel(x, ln_mix, w_in, gla_gate_w2, gla_gate_b, gla_norm, lru_conv_w, lru_conv_b, lru_wa, lru_ba, lru_wx, lru_bx, lru_lambda, w_out, ln_ffn, ffn_w_in, ffn_conv_w, ffn_conv_b, ffn_w_down, ln_final, loss_target, m_ln_mix, m_w_in, m_gla_gate_w2, m_gla_gate_b, m_gla_norm, m_lru_conv_w, m_lru_conv_b, m_lru_wa, m_lru_ba, m_lru_wx, m_lru_bx, m_lru_lambda, m_w_out, m_ln_ffn, m_ffn_w_in, m_ffn_conv_w, m_ffn_conv_b, m_ffn_w_down, m_ln_final, v_ln_mix, v_w_in, v_gla_gate_w2, v_gla_gate_b, v_gla_norm, v_lru_conv_w, v_lru_conv_b, v_lru_wa, v_lru_ba, v_lru_wx, v_lru_bx, v_lru_lambda, v_w_out, v_ln_ffn, v_ffn_w_in, v_ffn_conv_w, v_ffn_conv_b, v_ffn_w_down, v_ln_final):
    w = dict(ln_mix=ln_mix, w_in=w_in, gla_gate_w2=gla_gate_w2, gla_gate_b=gla_gate_b, gla_norm=gla_norm,
             lru_conv_w=lru_conv_w, lru_conv_b=lru_conv_b, lru_wa=lru_wa, lru_ba=lru_ba, lru_wx=lru_wx, lru_bx=lru_bx,
             lru_lambda=lru_lambda, w_out=w_out, ln_ffn=ln_ffn, ffn_w_in=ffn_w_in, ffn_conv_w=ffn_conv_w,
             ffn_conv_b=ffn_conv_b, ffn_w_down=ffn_w_down, ln_final=ln_final)
    m = dict(ln_mix=m_ln_mix, w_in=m_w_in, gla_gate_w2=m_gla_gate_w2, gla_gate_b=m_gla_gate_b, gla_norm=m_gla_norm,
             lru_conv_w=m_lru_conv_w, lru_conv_b=m_lru_conv_b, lru_wa=m_lru_wa, lru_ba=m_lru_ba, lru_wx=m_lru_wx,
             lru_bx=m_lru_bx, lru_lambda=m_lru_lambda, w_out=m_w_out, ln_ffn=m_ln_ffn, ffn_w_in=m_ffn_w_in,
             ffn_conv_w=m_ffn_conv_w, ffn_conv_b=m_ffn_conv_b, ffn_w_down=m_ffn_w_down, ln_final=m_ln_final)
    v = dict(ln_mix=v_ln_mix, w_in=v_w_in, gla_gate_w2=v_gla_gate_w2, gla_gate_b=v_gla_gate_b, gla_norm=v_gla_norm,
             lru_conv_w=v_lru_conv_w, lru_conv_b=v_lru_conv_b, lru_wa=v_lru_wa, lru_ba=v_lru_ba, lru_wx=v_lru_wx,
             lru_bx=v_lru_bx, lru_lambda=v_lru_lambda, w_out=v_w_out, ln_ffn=v_ln_ffn, ffn_w_in=v_ffn_w_in,
             ffn_conv_w=v_ffn_conv_w, ffn_conv_b=v_ffn_conv_b, ffn_w_down=v_ffn_w_down, ln_final=v_ln_final)

    sharded = BIG + ("gla_gate_w2", "lru_conv_w", "ffn_conv_w")
    chip = 2 * lax.axis_index("x") + lax.axis_index("y")
    core = lax.axis_index("c")
    shards = [w[k].astype(MXU_DTYPE) if k in BIG else w[k] for k in sharded]
    gathered = _gather_shards(shards, "gather_weights")
    p = dict(w)
    for k, gk, own in zip(sharded, gathered, shards):
        p[k] = _whole_from_shards(k, lax.dynamic_update_index_in_dim(gk, own, chip, 0))

    loss, grad_x, grads = _local_step(x[0], loss_target[0], p)
    loss = lax.psum(loss[0, 0], ("x", "y", "c"))

    slabs = [grads[k].reshape(N_CHIPS, 2, grads[k].shape[1] // 2, grads[k].shape[2]) for k in BIG]
    small = _pack([grads[k] for k in SMALL])
    got, small_all = _other_half_to_sibling(slabs, small, "other_half_to_sibling")
    kept = [lax.dynamic_index_in_dim(s, core, 1, keepdims=False) for s in slabs]
    chip_half = [_add_slabs(a, b, BF16, f"core_sum_{k}") for k, a, b in zip(BIG, kept, got)]
    recv = _exchange_grads(chip_half, "exchange_grads")
    own = [lax.dynamic_index_in_dim(h, chip, 0, keepdims=False) for h in chip_half]
    mine = [_sum_leading(r, f"chip_sum_{k}", last=o) for k, r, o in zip(BIG, recv, own)]
    theirs = _swap_with_sibling(mine, "swap_core_halves")
    big_g = [jnp.concatenate([jnp.where(core == 0, a, b), jnp.where(core == 0, b, a)]) for a, b in zip(mine, theirs)]
    small_all = lax.dynamic_update_index_in_dim(small_all, small, 2 * chip + core, 0)
    small_sum = _unpack(_sum_leading(small_all, "sum_small_grads"), [grads[k].shape for k in SMALL])

    me = 2 * lax.axis_index("x") + lax.axis_index("y")
    out_g, out_d, out_m, out_v = {}, {}, {}, {}
    for k, gk in zip(BIG, big_g):
        shape = w[k].shape
        cols = shape[-1]
        res = _adamw(w[k].reshape(-1, cols), gk, m[k].reshape(-1, cols), v[k].reshape(-1, cols), f"adamw_{k}")
        out_g[k] = gk.reshape(shape)
        out_d[k], out_m[k], out_v[k] = [r.reshape(shape) for r in res]
    small_g = []
    for k, gk in zip(SMALL, small_sum):
        if k in COL_SHARDED:
            width = w[k].shape[-1]
            gk = lax.dynamic_slice_in_dim(gk, me * width, width, axis=gk.ndim - 1)
        small_g.append(gk)
    shapes = [w[k].shape for k in SMALL]
    res = _adamw(_pack([w[k] for k in SMALL]), _pack(small_g), _pack([m[k] for k in SMALL]),
                 _pack([v[k] for k in SMALL]), "adamw_small")
    out_g.update(zip(SMALL, small_g))
    for out, packed in zip((out_d, out_m, out_v), res):
        for k, a in zip(SMALL, _unpack(packed, shapes)):
            out[k] = a
    return (loss, grad_x[None], *[out_g[k] for k in WEIGHTS], *[out_d[k] for k in WEIGHTS],
            *[out_m[k] for k in WEIGHTS], *[out_v[k] for k in WEIGHTS])
```
